```python
import jax, jax.numpy as jnp
from jax import lax
import numpy as np

D_MODEL = 1024
BATCH = 8
SEQ = 2048
DEPTH = 1

MEM_LEN = 256
MOBA_HEAD_DIM = 64
MOBA_HEADS = (D_MODEL // 2) // MOBA_HEAD_DIM
MOBA_WIDTH = MOBA_HEADS * MOBA_HEAD_DIM
MOBA_BLOCK = 256
MOBA_TOPK = 3
MOBA_Q_CHUNK = 16
GDN_HEAD_DIM = 128
GDN_HEADS = (D_MODEL // 2) // GDN_HEAD_DIM
GDN_WIDTH = GDN_HEADS * GDN_HEAD_DIM
GDN_CONV = 4
GDN_CHUNK = 64
MIX_WIDTH = MOBA_WIDTH + GDN_WIDTH
IN_SPLITS = [MOBA_WIDTH, MOBA_WIDTH, MOBA_WIDTH, 3 * GDN_WIDTH, GDN_WIDTH, GDN_HEADS, GDN_HEADS]
IN_PROJ_WIDTH = sum(IN_SPLITS)
XATTN_HEADS = 4
XATTN_HEAD_DIM = 128
XATTN_WIDTH = XATTN_HEADS * XATTN_HEAD_DIM
N_EXPERTS = 32
TOP_K = 4
D_FF_EXPERT = D_MODEL
SWIGLU_LIMIT = 7.0
SWIGLU_ALPHA = 1.702
MOE_BLOCK = 128
RMS_EPS = 1e-6

kernel_name = 'hybrid_moba_gdn_moe_block'


def rms_norm(x, g):
    xf = x.astype(jnp.float32)
    y = xf * lax.rsqrt(jnp.mean(xf * xf, axis=-1, keepdims=True) + RMS_EPS)
    return (y * g.astype(jnp.float32)).astype(x.dtype)


def l2norm(x):
    xf = x.astype(jnp.float32)
    return (xf * lax.rsqrt(jnp.sum(xf * xf, axis=-1, keepdims=True) + RMS_EPS)).astype(x.dtype)


def split_last(t, sizes):
    return jnp.split(t, np.cumsum(sizes)[:-1].tolist(), axis=-1)


def moba_attention(q, k, v):
    B, S, H, d = q.shape
    s_pad = -(-S // MOBA_BLOCK) * MOBA_BLOCK
    pad = ((0, 0), (0, s_pad - S), (0, 0), (0, 0))
    q, k, v = (jnp.pad(t, pad).transpose(0, 2, 1, 3) for t in (q, k, v))
    nb = s_pad // MOBA_BLOCK
    k_sel_n = min(MOBA_TOPK, nb)
    k_blk = k.reshape(B, H, nb, MOBA_BLOCK, d)
    v_blk = v.reshape(B, H, nb, MOBA_BLOCK, d)
    k_mean = jnp.mean(k_blk.astype(jnp.float32), axis=3)
    gate = jnp.einsum('bhsd,bhnd->bhsn', q.astype(jnp.float32), k_mean)
    q_blk = jnp.arange(s_pad) // MOBA_BLOCK
    past = jnp.arange(nb)[None, :] < q_blk[:, None]
    gate = jnp.where(past[None, None], gate, -jnp.inf)
    _, sel_idx = lax.top_k(gate, k_sel_n)
    sel_valid = jnp.arange(k_sel_n)[None, :] < q_blk[:, None]
    scale = d ** -0.5
    b_ix = jnp.arange(B)[:, None, None, None]
    h_ix = jnp.arange(H)[None, :, None, None]

    def chunk_attend(c):
        start = c * MOBA_Q_CHUNK
        q_c = lax.dynamic_slice_in_dim(q, start, MOBA_Q_CHUNK, axis=2)
        sel_c = lax.dynamic_slice_in_dim(sel_idx, start, MOBA_Q_CHUNK, axis=2)
        valid_c = lax.dynamic_slice_in_dim(sel_valid, start, MOBA_Q_CHUNK, axis=0)
        blk = start // MOBA_BLOCK
        k_own = lax.dynamic_index_in_dim(k_blk, blk, axis=2, keepdims=False)
        v_own = lax.dynamic_index_in_dim(v_blk, blk, axis=2, keepdims=False)
        k_g = k_blk[b_ix, h_ix, sel_c]
        v_g = v_blk[b_ix, h_ix, sel_c]
        s_sel = jnp.einsum('bhqd,bhqkjd->bhqkj', q_c, k_g).astype(jnp.float32) * scale
        s_sel = jnp.where(valid_c[None, None, :, :, None], s_sel, -jnp.inf)
        s_sel = s_sel.reshape(B, H, MOBA_Q_CHUNK, k_sel_n * MOBA_BLOCK)
        q_pos = start + jnp.arange(MOBA_Q_CHUNK)
        k_pos = blk * MOBA_BLOCK + jnp.arange(MOBA_BLOCK)
        s_own = jnp.einsum('bhqd,bhjd->bhqj', q_c, k_own).astype(jnp.float32) * scale
        s_own = jnp.where((k_pos[None, :] <= q_pos[:, None])[None, None], s_own, -jnp.inf)
        p = jax.nn.softmax(jnp.concatenate([s_sel, s_own], axis=-1), axis=-1).astype(v.dtype)
        p_sel = p[..., :k_sel_n * MOBA_BLOCK].reshape(B, H, MOBA_Q_CHUNK, k_sel_n, MOBA_BLOCK)
        p_own = p[..., k_sel_n * MOBA_BLOCK:]
        return (jnp.einsum('bhqkj,bhqkjd->bhqd', p_sel, v_g)
                + jnp.einsum('bhqj,bhjd->bhqd', p_own, v_own))

    out = lax.map(chunk_attend, jnp.arange(s_pad // MOBA_Q_CHUNK))
    out = out.transpose(1, 0, 3, 2, 4).reshape(B, s_pad, H, d)
    return out[:, :S]


def causal_short_conv(x, w):
    S = x.shape[1]
    xp = jnp.pad(x, ((0, 0), (GDN_CONV - 1, 0), (0, 0)))
    return sum(w[j] * xp[:, j:j + S] for j in range(GDN_CONV))


def gated_delta_rule(q, k, v, g, beta):
    B, S, H, dk = q.shape
    dv = v.shape[-1]
    out_dtype = v.dtype
    C = GDN_CHUNK
    N = S // C
    f32 = jnp.float32

    def to_chunks(t):
        t = jnp.moveaxis(t.astype(f32), 2, 1)
        return t.reshape(B, H, N, C, *t.shape[3:])

    q, k, v, g, beta = (to_chunks(t) for t in (q, k, v, g, beta))
    q = q * dk ** -0.5
    g = jnp.cumsum(g, axis=-1)
    idx = jnp.arange(C)
    causal = idx[:, None] >= idx[None, :]
    strict = idx[:, None] > idx[None, :]
    decay = jnp.exp(jnp.where(causal, g[..., :, None] - g[..., None, :], -jnp.inf))
    k_beta = k * beta[..., None]
    a_strict = jnp.where(strict, jnp.einsum('bhnid,bhnjd->bhnij', k_beta, k) * decay, 0.0)
    rhs = jnp.concatenate([v * beta[..., None], k_beta * jnp.exp(g)[..., None]], axis=-1)
    uw = lax.linalg.triangular_solve(a_strict + jnp.eye(C, dtype=f32), rhs,
                                     left_side=True, lower=True, unit_diagonal=True)
    u, w = uw[..., :dv], uw[..., dv:]
    qk = jnp.einsum('bhnid,bhnjd->bhnij', q, k) * decay
    g_last = g[..., -1]
    q_dec = q * jnp.exp(g)[..., None]
    k_dec = k * jnp.exp(g_last[..., None] - g)[..., None]
    xs = tuple(jnp.moveaxis(t, 2, 0) for t in (u, w, qk, q_dec, k_dec, g_last))

    def step(state, inp):
        u_n, w_n, qk_n, qd_n, kd_n, gl_n = inp
        v_new = u_n - jnp.einsum('bhck,bhkv->bhcv', w_n, state)
        o_n = (jnp.einsum('bhck,bhkv->bhcv', qd_n, state)
               + jnp.einsum('bhij,bhjv->bhiv', qk_n, v_new))
        state = state * jnp.exp(gl_n)[..., None, None] + jnp.einsum('bhck,bhcv->bhkv', kd_n, v_new)
        return state, o_n

    _, o = lax.scan(step, jnp.zeros((B, H, dk, dv), f32), xs)
    o = o.transpose(1, 0, 3, 2, 4).reshape(B, S, H, dv)
    return o.astype(out_dtype)


def gated_deltanet(qkv, z, b, a, conv_w, A_log, dt_bias, norm_g):
    B, S, _ = qkv.shape
    qkv = jax.nn.silu(causal_short_conv(qkv, conv_w))
    q, k, v = (t.reshape(B, S, GDN_HEADS, GDN_HEAD_DIM) for t in jnp.split(qkv, 3, axis=-1))
    q, k = l2norm(q), l2norm(k)
    beta = jax.nn.sigmoid(b.astype(jnp.float32))
    g = -jnp.exp(A_log.astype(jnp.float32)) * jax.nn.softplus(a.astype(jnp.float32) + dt_bias.astype(jnp.float32))
    o = gated_delta_rule(q, k, v, g, beta)
    o = rms_norm(o, norm_g) * jax.nn.silu(z).reshape(B, S, GDN_HEADS, GDN_HEAD_DIM)
    return o.reshape(B, S, GDN_WIDTH)


def parallel_mixer(h, w_in, conv_w, A_log, dt_bias, gdn_norm_g, moba_norm_g, w_out):
    B, S, _ = h.shape
    proj = h @ w_in
    mq, mk, mv, gqkv, gz, gb, ga = split_last(proj, IN_SPLITS)
    heads = lambda t: t.reshape(B, S, MOBA_HEADS, MOBA_HEAD_DIM)
    o_moba = moba_attention(heads(mq), heads(mk), heads(mv)).reshape(B, S, MOBA_WIDTH)
    o_moba = rms_norm(o_moba, moba_norm_g)
    o_gdn = gated_deltanet(gqkv, gz, gb, ga, conv_w, A_log, dt_bias, gdn_norm_g)
    return jnp.concatenate([o_moba, o_gdn], axis=-1) @ w_out


def memory_cross_attention(h, m, w_q, w_kv, w_o):
    B, S, _ = h.shape
    M = m.shape[1]
    q = (h @ w_q).reshape(B, S, XATTN_HEADS, XATTN_HEAD_DIM)
    k, v = (t.reshape(B, M, XATTN_HEADS, XATTN_HEAD_DIM) for t in jnp.split(m @ w_kv, 2, axis=-1))
    s = jnp.einsum('bshd,bmhd->bhsm', q, k).astype(jnp.float32) * XATTN_HEAD_DIM ** -0.5
    p = jax.nn.softmax(s, axis=-1).astype(v.dtype)
    o = jnp.einsum('bhsm,bmhd->bshd', p, v).reshape(B, S, XATTN_WIDTH)
    return o @ w_o


def moe_ffn(h, router_w, router_b, w_gate_up, b_gate_up, w_down, b_down):
    B, S, D = h.shape
    T = B * S
    hf = h.reshape(T, D)
    logits = (hf @ router_w + router_b).astype(jnp.float32)
    top_logits, top_idx = lax.top_k(logits, TOP_K)
    gates = jax.nn.softmax(top_logits, axis=-1).astype(h.dtype)
    n_rows = T * TOP_K
    flat_e = top_idx.reshape(-1)
    flat_tok = jnp.arange(n_rows, dtype=jnp.int32) // TOP_K
    flat_gate = gates.reshape(-1)
    order = jnp.argsort(flat_e)
    e_sorted = flat_e[order]
    counts = jnp.bincount(flat_e, length=N_EXPERTS)
    padded = (counts + MOE_BLOCK - 1) // MOE_BLOCK * MOE_BLOCK
    group_start = jnp.cumsum(counts) - counts
    padded_end = jnp.cumsum(padded)
    padded_start = padded_end - padded
    dest = padded_start[e_sorted] + (jnp.arange(n_rows) - group_start[e_sorted])
    n_blocks = -(-n_rows // MOE_BLOCK) + N_EXPERTS
    R = n_blocks * MOE_BLOCK
    row_tok = jnp.zeros((R,), jnp.int32).at[dest].set(flat_tok[order])
    row_gate = jnp.zeros((R,), h.dtype).at[dest].set(flat_gate[order])
    block_expert = jnp.minimum(
        jnp.searchsorted(padded_end, jnp.arange(n_blocks) * MOE_BLOCK, side='right'), N_EXPERTS - 1)
    x_rows = hf[row_tok].reshape(n_blocks, MOE_BLOCK, D)

    def expert_block(args):
        xb, e = args
        gu = xb @ w_gate_up[e] + b_gate_up[e]
        gate, up = gu[..., :D_FF_EXPERT], gu[..., D_FF_EXPERT:]
        gate = jnp.minimum(gate, SWIGLU_LIMIT)
        up = jnp.clip(up, -SWIGLU_LIMIT, SWIGLU_LIMIT)
        act = (up + 1.0) * gate * jax.nn.sigmoid(SWIGLU_ALPHA * gate)
        return act @ w_down[e] + b_down[e]

    y_rows = lax.map(expert_block, (x_rows, block_expert))
    y = jnp.zeros_like(hf).at[row_tok].add(y_rows.reshape(R, D) * row_gate[:, None])
    return y.reshape(B, S, D)


def setup_inputs(seed: int = 0) -> dict:
    key = jax.random.key(seed)
    ks = jax.random.split(key, 32)
    f32 = jnp.float32
    nrm = lambda k, shape, s: jax.random.normal(k, shape, f32) * s
    gain = lambda k, n: 1.0 + 0.01 * jax.random.normal(k, (DEPTH, n), f32)
    L = DEPTH
    dt = jnp.exp(jax.random.uniform(ks[6], (L, GDN_HEADS), f32, np.log(1e-3), np.log(1e-1)))
    return {
        'x': nrm(ks[0], (BATCH, SEQ, D_MODEL), 1.0),
        'mem': nrm(ks[1], (BATCH, MEM_LEN, D_MODEL), 1.0),
        'norm_mix_g': gain(ks[2], D_MODEL),
        'w_in': nrm(ks[3], (L, D_MODEL, IN_PROJ_WIDTH), D_MODEL ** -0.5),
        'gdn_conv_w': nrm(ks[4], (L, GDN_CONV, 3 * GDN_WIDTH), GDN_CONV ** -0.5),
        'gdn_A_log': jnp.log(jax.random.uniform(ks[5], (L, GDN_HEADS), f32, 1.0, 16.0)),
        'gdn_dt_bias': dt + jnp.log(-jnp.expm1(-dt)),
        'gdn_norm_g': gain(ks[7], GDN_HEAD_DIM),
        'moba_norm_g': gain(ks[8], MOBA_WIDTH),
        'w_out': nrm(ks[9], (L, MIX_WIDTH, D_MODEL), MIX_WIDTH ** -0.5),
        'norm_xattn_g': gain(ks[10], D_MODEL),
        'norm_mem_g': gain(ks[11], D_MODEL),
        'xattn_w_q': nrm(ks[12], (L, D_MODEL, XATTN_WIDTH), D_MODEL ** -0.5),
        'xattn_w_kv': nrm(ks[13], (L, D_MODEL, 2 * XATTN_WIDTH), D_MODEL ** -0.5),
        'xattn_w_o': nrm(ks[14], (L, XATTN_WIDTH, D_MODEL), XATTN_WIDTH ** -0.5),
        'norm_ffn_g': gain(ks[15], D_MODEL),
        'router_w': nrm(ks[16], (L, D_MODEL, N_EXPERTS), D_MODEL ** -0.5),
        'router_b': nrm(ks[17], (L, N_EXPERTS), 0.01),
        'w_gate_up': nrm(ks[18], (L, N_EXPERTS, D_MODEL, 2 * D_FF_EXPERT), D_MODEL ** -0.5),
        'b_gate_up': nrm(ks[19], (L, N_EXPERTS, 2 * D_FF_EXPERT), 0.01),
        'w_down': nrm(ks[20], (L, N_EXPERTS, D_FF_EXPERT, D_MODEL), D_FF_EXPERT ** -0.5),
        'b_down': nrm(ks[21], (L, N_EXPERTS, D_MODEL), 0.01),
        'final_norm_g': 1.0 + 0.01 * jax.random.normal(ks[22], (D_MODEL,), f32),
    }


def reference(x, mem, norm_mix_g, w_in, gdn_conv_w, gdn_A_log, gdn_dt_bias, gdn_norm_g,
              moba_norm_g, w_out, norm_xattn_g, norm_mem_g, xattn_w_q, xattn_w_kv, xattn_w_o,
              norm_ffn_g, router_w, router_b, w_gate_up, b_gate_up, w_down, b_down, final_norm_g):
    for l in range(DEPTH):
        x = x + parallel_mixer(rms_norm(x, norm_mix_g[l]), w_in[l], gdn_conv_w[l], gdn_A_log[l],
                               gdn_dt_bias[l], gdn_norm_g[l], moba_norm_g[l], w_out[l])
        x = x + memory_cross_attention(rms_norm(x, norm_xattn_g[l]), rms_norm(mem, norm_mem_g[l]),
                                       xattn_w_q[l], xattn_w_kv[l], xattn_w_o[l])
        x = x + moe_ffn(rms_norm(x, norm_ffn_g[l]), router_w[l], router_b[l], w_gate_up[l],
                        b_gate_up[l], w_down[l], b_down[l])
    return rms_norm(x, final_norm_g)
```

```python
import functools

import jax
import jax.numpy as jnp
from jax import lax
from jax.experimental import pallas as pl
from jax.experimental.pallas import tpu as pltpu

F32 = jnp.float32
BF16 = jnp.bfloat16

RMS_EPS = 1e-6
MOBA_HEAD_DIM = 64
MOBA_BLOCK = 256
MOBA_TOPK = 3
GDN_HEAD_DIM = 128
GDN_HEADS = 4
GDN_CONV = 4
GDN_CHUNK = 64
XATTN_HEAD_DIM = 128
N_EXPERTS = 32
TOP_K = 4
SWIGLU_LIMIT = 7.0
SWIGLU_ALPHA = 1.702

LANES = 128
VMEM_LIMIT = 56 * 1024 * 1024

IN_ROWS = 512
MID_ROWS = 256
MOE_ROWS = 512
GDN_GROUP = 4
NEG_INF = float("-inf")


def _params(*sem):
    return pltpu.CompilerParams(dimension_semantics=sem, vmem_limit_bytes=VMEM_LIMIT)


def _rms(x, g):
    return x * lax.rsqrt(jnp.mean(x * x, axis=-1, keepdims=True) + RMS_EPS) * g


def _dot(a, b):
    return jnp.dot(a, b, preferred_element_type=F32)


def _dot_nt(a, b):
    return lax.dot_general(a, b, (((1,), (1,)), ((), ())), preferred_element_type=F32)


def _sigmoid(x):
    return 1.0 / (1.0 + jnp.exp(-x))


def _in_proj_kernel(x_ref, g_ref, w_ref, mq_ref, mk_ref, mv_ref, gqkv_ref, gz_ref, gba_ref, *, mw, gw):
    hn = _rms(x_ref[...], g_ref[...]).astype(BF16)
    mm = lambda lo, hi: _dot(hn, w_ref[:, lo:hi])
    mq_ref[...] = (mm(0, mw) * (MOBA_HEAD_DIM ** -0.5)).astype(BF16)
    mk_ref[...] = mm(mw, 2 * mw).astype(BF16)
    mv_ref[...] = mm(2 * mw, 3 * mw).astype(BF16)
    o = 3 * mw
    gqkv_ref[...] = mm(o, o + 3 * gw)
    gz_ref[...] = mm(o + 3 * gw, o + 4 * gw)
    gba_ref[...] = mm(o + 4 * gw, o + 4 * gw + LANES)


def _in_proj(x2d, g, w_in, mw, gw):
    t, d = x2d.shape
    n_real = w_in.shape[1]
    n_pad = 3 * mw + 4 * gw + LANES
    w = jnp.pad(w_in, ((0, 0), (0, n_pad - n_real))).astype(BF16)
    row = lambda n: pl.BlockSpec((IN_ROWS, n), lambda i: (i, 0))
    return pl.pallas_call(
        functools.partial(_in_proj_kernel, mw=mw, gw=gw),
        grid=(t // IN_ROWS,),
        in_specs=[row(d), pl.BlockSpec((1, d), lambda i: (0, 0)), pl.BlockSpec((d, n_pad), lambda i: (0, 0))],
        out_specs=[row(mw), row(mw), row(mw), row(3 * gw), row(gw), row(LANES)],
        out_shape=[jax.ShapeDtypeStruct((t, mw), BF16)] * 3
        + [jax.ShapeDtypeStruct((t, 3 * gw), F32), jax.ShapeDtypeStruct((t, gw), F32),
           jax.ShapeDtypeStruct((t, LANES), F32)],
        compiler_params=_params("arbitrary"),
        name="in_proj",
    )(x2d, g.reshape(1, d), w)


def _moba_select(g_t, i):
    nb = g_t.shape[0]
    row = lax.broadcasted_iota(jnp.int32, g_t.shape, 0)
    valid = row < i
    sel = jnp.zeros_like(g_t)
    for j in range(nb):
        gj = g_t[j:j + 1, :]
        beats = valid & ((g_t > gj) | ((g_t == gj) & (row < j)))
        rank = jnp.sum(jnp.where(beats, 1.0, 0.0), axis=0, keepdims=True)
        sel = jnp.where(row == j, jnp.where(rank < MOBA_TOPK, 1.0, 0.0), sel)
    return jnp.where(valid, sel, 0.0)


def _moba_kernel(q_ref, k_ref, v_ref, o_ref, kmean_ref, *, nb):
    i = pl.program_id(2)
    bs = MOBA_BLOCK

    @pl.when(i == 0)
    def _():
        for j in range(nb):
            kb = k_ref[0, j * bs:(j + 1) * bs, :].astype(F32)
            kmean_ref[j:j + 1, :] = jnp.mean(kb, axis=0, keepdims=True)

    q = q_ref[0]
    lane = lax.broadcasted_iota(jnp.int32, (1, LANES), 1)
    low = lane < MOBA_HEAD_DIM
    zero = jnp.zeros_like(q)
    q_heads = (jnp.where(low, q, zero), jnp.where(low, zero, q))

    kmean = kmean_ref[...]
    eye = (lax.broadcasted_iota(jnp.int32, (bs, bs), 0) == lax.broadcasted_iota(jnp.int32, (bs, bs), 1))
    eye = jnp.where(eye, 1.0, 0.0).astype(BF16)
    sels = []
    for qh in q_heads:
        g_t = lax.dot_general(kmean, qh.astype(F32), (((1,), (1,)), ((), ())),
                              preferred_element_type=F32, precision=lax.Precision.HIGHEST)
        sel_t = _moba_select(g_t, i).astype(BF16)
        sels.append(_dot_nt(eye, sel_t))
    col_nb = lax.broadcasted_iota(jnp.int32, (bs, nb), 1)

    r_ix = lax.broadcasted_iota(jnp.int32, (bs, bs), 0)
    c_ix = lax.broadcasted_iota(jnp.int32, (bs, bs), 1)
    causal = c_ix <= r_ix

    start = pl.multiple_of(i * bs, bs)
    k_own = k_ref[0, pl.ds(start, bs), :]
    v_own = v_ref[0, pl.ds(start, bs), :]
    init = []
    for qh in q_heads:
        s = jnp.where(causal, _dot_nt(qh, k_own), NEG_INF)
        m = jnp.max(s, axis=1, keepdims=True)
        p = jnp.exp(s - m)
        init += [m, jnp.sum(p, axis=1, keepdims=True), _dot(p.astype(BF16), v_own)]

    def body(j, carry):
        off = pl.multiple_of(j * bs, bs)
        kj = k_ref[0, pl.ds(off, bs), :]
        vj = v_ref[0, pl.ds(off, bs), :]
        out = []
        for h, qh in enumerate(q_heads):
            m, l, acc = carry[3 * h:3 * h + 3]
            on = jnp.sum(jnp.where(col_nb == j, sels[h], 0.0), axis=1, keepdims=True) > 0.5
            s = jnp.where(on, _dot_nt(qh, kj), NEG_INF)
            m_new = jnp.maximum(m, jnp.max(s, axis=1, keepdims=True))
            alpha = jnp.exp(m - m_new)
            p = jnp.exp(s - m_new)
            out += [m_new, alpha * l + jnp.sum(p, axis=1, keepdims=True),
                    alpha * acc + _dot(p.astype(BF16), vj)]
        return tuple(out)

    m0, l0, a0, m1, l1, a1 = lax.fori_loop(0, i, body, tuple(init))
    o_ref[0] = jnp.where(low, a0 / l0, a1 / l1)


def _moba(mq, mk, mv):
    b, s, w = mq.shape
    nb = s // MOBA_BLOCK
    full = pl.BlockSpec((1, s, LANES), lambda bi, hp, i: (bi, 0, hp))
    blk = pl.BlockSpec((1, MOBA_BLOCK, LANES), lambda bi, hp, i: (bi, i, hp))
    return pl.pallas_call(
        functools.partial(_moba_kernel, nb=nb),
        grid=(b, w // LANES, nb),
        in_specs=[blk, full, full],
        out_specs=blk,
        out_shape=jax.ShapeDtypeStruct((b, s, w), F32),
        scratch_shapes=[pltpu.VMEM((nb, LANES), F32)],
        compiler_params=_params("arbitrary", "arbitrary", "arbitrary"),
        name="moba",
    )(mq, mk, mv)


def _unit_lower_inverse(a):
    c = a.shape[0]
    r = lax.broadcasted_iota(jnp.int32, (c, c), 0)
    cc = lax.broadcasted_iota(jnp.int32, (c, c), 1)
    inv = jnp.where(r == cc, 1.0, 0.0) - jnp.where((r // 2) == (cc // 2), a, 0.0)
    size = 4
    while size <= c:
        off = jnp.where(((r // size) == (cc // size)) & ((r // (size // 2)) != (cc // (size // 2))), a, 0.0)
        inv_b = inv.astype(BF16)
        inv = inv - _dot(_dot(inv_b, off.astype(BF16)).astype(BF16), inv_b)
        size *= 2
    return inv


def _gdn_kernel(xq_ref, xk_ref, xv_ref, z_ref, ba_ref, wq_ref, wk_ref, wv_ref, alog_ref, dtb_ref, ng_ref,
                o_ref, q_s, k_s, v_s, beta_s, gam_s, u_s, w_s, qk_s, *, seq):
    h = pl.program_id(1)
    c = GDN_CHUNK
    n_chunks = seq // c
    row = lax.broadcasted_iota(jnp.int32, (seq, 1), 0)

    def conv_silu(x_ref, w_ref):
        x = x_ref[0]
        y = w_ref[GDN_CONV - 1:GDN_CONV, :] * x
        for sft in range(1, GDN_CONV):
            y = y + w_ref[GDN_CONV - 1 - sft:GDN_CONV - sft, :] * jnp.where(row >= sft, pltpu.roll(x, sft, 0), 0.0)
        return y * _sigmoid(y)

    def l2n(x):
        return x * lax.rsqrt(jnp.sum(x * x, axis=-1, keepdims=True) + RMS_EPS)

    q_s[...] = l2n(conv_silu(xq_ref, wq_ref)) * (GDN_HEAD_DIM ** -0.5)
    k_s[...] = l2n(conv_silu(xk_ref, wk_ref))
    v_s[...] = conv_silu(xv_ref, wv_ref)

    lane = lax.broadcasted_iota(jnp.int32, (1, LANES), 1)
    pick = lambda x, ln: jnp.sum(jnp.where(lane == ln, x, 0.0), axis=1, keepdims=True)
    ba = ba_ref[0]
    b_col = pick(ba, h)
    a_col = pick(ba, GDN_HEADS + h)
    a_log = pick(alog_ref[...], h)
    dt_b = pick(dtb_ref[...], h)
    beta_s[...] = jnp.broadcast_to(_sigmoid(b_col), (seq, LANES))
    xa = a_col + dt_b
    softplus = jnp.maximum(xa, 0.0) + jnp.log(1.0 + jnp.exp(-jnp.abs(xa)))
    g = jnp.broadcast_to(-jnp.exp(a_log) * softplus, (seq, LANES))
    pos = row % c
    sft = 1
    while sft < c:
        g = g + jnp.where(pos >= sft, pltpu.roll(g, sft, 0), 0.0)
        sft *= 2
    gam_s[...] = g

    r_ix = lax.broadcasted_iota(jnp.int32, (c, c), 0)
    c_ix = lax.broadcasted_iota(jnp.int32, (c, c), 1)

    def intra(grp, carry):
        for gi in range(GDN_GROUP):
            r0 = pl.multiple_of((grp * GDN_GROUP + gi) * c, c)
            rows = pl.ds(r0, c)
            q, k, v = q_s[rows, :], k_s[rows, :], v_s[rows, :]
            beta, gam = beta_s[rows, :], gam_s[rows, :]
            gam_c = gam[:, :c]
            gam_r = jnp.sum(jnp.where(r_ix == c_ix, gam_c, 0.0), axis=0, keepdims=True)
            decay = jnp.exp(jnp.where(c_ix <= r_ix, gam_c - gam_r, NEG_INF))
            kb = k * beta
            k_b16 = k.astype(BF16)
            a = jnp.where(c_ix < r_ix, _dot_nt(kb.astype(BF16), k_b16) * decay, 0.0)
            t_inv = _unit_lower_inverse(a).astype(BF16)
            eg = jnp.exp(gam)
            rhs = jnp.concatenate([v * beta, kb * eg], axis=1).astype(BF16)
            uw = _dot(t_inv, rhs)
            u_s[rows, :] = uw[:, :LANES]
            w_s[rows, :] = uw[:, LANES:]
            qk_s[rows, :] = _dot_nt(q.astype(BF16), k_b16) * decay
            g_last = gam[c - 1:c, :]
            q_s[rows, :] = q * eg
            k_s[rows, :] = k * jnp.exp(g_last - gam)
        return carry

    lax.fori_loop(0, n_chunks // GDN_GROUP, intra, 0)

    ng = ng_ref[...]

    def step(n, state):
        r0 = pl.multiple_of(n * c, c)
        rows = pl.ds(r0, c)
        s_b = state.astype(BF16)
        v_new = u_s[rows, :] - _dot(w_s[rows, :].astype(BF16), s_b)
        v_b = v_new.astype(BF16)
        o = _dot(q_s[rows, :].astype(BF16), s_b) + _dot(qk_s[rows, :].astype(BF16), v_b)
        g_last = gam_s[pl.ds(r0 + c - 8, 8), :][7:8, :]
        kd_t = lax.dot_general(k_s[rows, :].astype(BF16), v_b, (((0,), (0,)), ((), ())),
                               preferred_element_type=F32)
        z = z_ref[0, rows, :]
        o_ref[0, rows, :] = _rms(o, ng) * (z * _sigmoid(z))
        return state * jnp.exp(g_last) + kd_t

    lax.fori_loop(0, n_chunks, step, jnp.zeros((GDN_HEAD_DIM, GDN_HEAD_DIM), F32))


def _gdn(gqkv, gz, gba, conv_w, a_log, dt_bias, norm_g):
    b, s, w3 = gqkv.shape
    nh = GDN_HEADS
    pad_row = lambda v: jnp.pad(v.reshape(1, -1), ((0, 0), (0, LANES - v.shape[-1])))
    seq_blk = lambda off: pl.BlockSpec((1, s, LANES), lambda bi, h: (bi, 0, off + h))
    cw_blk = lambda off: pl.BlockSpec((GDN_CONV, LANES), lambda bi, h: (0, off + h))
    const = pl.BlockSpec((1, LANES), lambda bi, h: (0, 0))
    return pl.pallas_call(
        functools.partial(_gdn_kernel, seq=s),
        grid=(b, nh),
        in_specs=[seq_blk(0), seq_blk(nh), seq_blk(2 * nh), seq_blk(0),
                  pl.BlockSpec((1, s, LANES), lambda bi, h: (bi, 0, 0)),
                  cw_blk(0), cw_blk(nh), cw_blk(2 * nh), const, const, const],
        out_specs=seq_blk(0),
        out_shape=jax.ShapeDtypeStruct((b, s, nh * GDN_HEAD_DIM), F32),
        scratch_shapes=[pltpu.VMEM((s, LANES), F32)] * 7 + [pltpu.VMEM((s, GDN_CHUNK), F32)],
        compiler_params=_params("arbitrary", "arbitrary"),
        name="gdn",
    )(gqkv, gqkv, gqkv, gz, gba, conv_w, conv_w, conv_w, pad_row(a_log), pad_row(dt_bias),
      norm_g.reshape(1, -1))


def _mem_kv_kernel(m_ref, g_ref, w_ref, kv_ref):
    kv_ref[...] = _dot(_rms(m_ref[...], g_ref[...]).astype(BF16), w_ref[...]).astype(BF16)


def _mem_kv(mem2d, g, w_kv, rows):
    t, d = mem2d.shape
    n = w_kv.shape[1]
    return pl.pallas_call(
        _mem_kv_kernel,
        grid=(t // rows,),
        in_specs=[pl.BlockSpec((rows, d), lambda i: (i, 0)), pl.BlockSpec((1, d), lambda i: (0, 0)),
                  pl.BlockSpec((d, n), lambda i: (0, 0))],
        out_specs=pl.BlockSpec((rows, n), lambda i: (i, 0)),
        out_shape=jax.ShapeDtypeStruct((t, n), BF16),
        compiler_params=_params("arbitrary"),
        name="mem_kv",
    )(mem2d, g.reshape(1, d), w_kv.astype(BF16))


def _mid_kernel(x_ref, om_ref, og_ref, mg_ref, wout_ref, xg_ref, wq_ref, kv_ref, wo_ref, fg_ref, rw_ref, rb_ref,
                x2_ref, h3_ref, route_ref, *, mw, xw):
    mo = _rms(om_ref[...], mg_ref[...]).astype(BF16)
    x1 = x_ref[...] + _dot(mo, wout_ref[:mw, :]) + _dot(og_ref[...].astype(BF16), wout_ref[mw:, :])

    h2 = _rms(x1, xg_ref[...]).astype(BF16)
    q = (_dot(h2, wq_ref[...]) * (XATTN_HEAD_DIM ** -0.5)).astype(BF16)
    heads = []
    for h in range(xw // XATTN_HEAD_DIM):
        sl = slice(h * XATTN_HEAD_DIM, (h + 1) * XATTN_HEAD_DIM)
        s = _dot_nt(q[:, sl], kv_ref[:, sl])
        p = jnp.exp(s - jnp.max(s, axis=1, keepdims=True))
        l = jnp.sum(p, axis=1, keepdims=True)
        heads.append(_dot(p.astype(BF16), kv_ref[:, xw + h * XATTN_HEAD_DIM:xw + (h + 1) * XATTN_HEAD_DIM]) / l)
    x2 = x1 + _dot(jnp.concatenate(heads, axis=1).astype(BF16), wo_ref[...])
    x2_ref[...] = x2

    h3 = _rms(x2, fg_ref[...])
    h3_ref[...] = h3
    logits = jnp.dot(h3, rw_ref[...], preferred_element_type=F32, precision=lax.Precision.HIGHEST) + rb_ref[...]
    ne = logits.shape[1]
    col = lax.broadcasted_iota(jnp.int32, logits.shape, 1)
    lane = lax.broadcasted_iota(jnp.int32, (1, LANES), 1)
    route = jnp.zeros((logits.shape[0], LANES), F32)
    top, denom = None, None
    for kk in range(TOP_K):
        m = jnp.max(logits, axis=1, keepdims=True)
        idx = jnp.min(jnp.where(logits == m, col, ne), axis=1, keepdims=True)
        logits = jnp.where(col == idx, NEG_INF, logits)
        if kk == 0:
            top = m
        e = jnp.exp(m - top)
        denom = e if kk == 0 else denom + e
        route = jnp.where(lane == kk, e, route)
        route = jnp.where(lane == TOP_K + kk, idx.astype(F32), route)
    route_ref[...] = jnp.where(lane < TOP_K, route / denom, route)


def _mid(x2d, om, og, moba_g, w_out, xattn_g, w_q, kv, w_o, ffn_g, router_w, router_b, seq, mem_len):
    t, d = x2d.shape
    mw, gw, xw, ne = om.shape[1], og.shape[1], w_q.shape[1], router_w.shape[1]
    tiles_per_seq = seq // MID_ROWS
    row = lambda n: pl.BlockSpec((MID_ROWS, n), lambda i: (i, 0))
    const = lambda r, c: pl.BlockSpec((r, c), lambda i: (0, 0))
    return pl.pallas_call(
        functools.partial(_mid_kernel, mw=mw, xw=xw),
        grid=(t // MID_ROWS,),
        in_specs=[row(d), row(mw), row(gw), const(1, mw), const(mw + gw, d), const(1, d), const(d, xw),
                  pl.BlockSpec((mem_len, 2 * xw), lambda i: (i // tiles_per_seq, 0)),
                  const(xw, d), const(1, d), const(d, ne), const(1, ne)],
        out_specs=[row(d), row(d), row(LANES)],
        out_shape=[jax.ShapeDtypeStruct((t, d), F32), jax.ShapeDtypeStruct((t, d), F32),
                   jax.ShapeDtypeStruct((t, LANES), F32)],
        compiler_params=_params("arbitrary"),
        name="mid",
    )(x2d, om, og, moba_g.reshape(1, mw), w_out.astype(BF16), xattn_g.reshape(1, d), w_q.astype(BF16), kv,
      w_o.astype(BF16), ffn_g.reshape(1, d), router_w, router_b.reshape(1, ne))


def _moe_kernel(be_ref, bv_ref, tok_ref, dst_ref, h_hbm, wgu_ref, bgu_ref, wd_ref, bd_ref, y_hbm,
                xbuf, ybuf, sem_in, sem_out, *, dff):
    blk = pl.program_id(0)
    nv = bv_ref[blk]

    def row_in(r, t):
        return pltpu.make_async_copy(h_hbm.at[pl.ds(t, 1), :], xbuf.at[pl.ds(r, 1), :], sem_in)

    def row_out(r, d):
        return pltpu.make_async_copy(ybuf.at[pl.ds(r, 1), :], y_hbm.at[pl.ds(d, 1), :], sem_out)

    @pl.when(nv > 0)
    def _():
        def issue(r, c):
            row_in(r, tok_ref[0, 0, r]).start()
            return c

        lax.fori_loop(0, MOE_ROWS, issue, 0)

        def drain(r, c):
            row_in(r, 0).wait()
            return c

        lax.fori_loop(0, MOE_ROWS, drain, 0)

        x = xbuf[...].astype(BF16)
        gu = _dot(x, wgu_ref[0]) + bgu_ref[0]
        gate = jnp.minimum(gu[:, :dff], SWIGLU_LIMIT)
        up = jnp.clip(gu[:, dff:], -SWIGLU_LIMIT, SWIGLU_LIMIT)
        act = (up + 1.0) * gate * _sigmoid(SWIGLU_ALPHA * gate)
        ybuf[...] = _dot(act.astype(BF16), wd_ref[0]) + bd_ref[0]

        def scatter(r, c):
            row_out(r, dst_ref[0, 0, r]).start()
            return c

        lax.fori_loop(0, nv, scatter, 0)

        def drain_out(r, c):
            row_out(r, 0).wait()
            return c

        lax.fori_loop(0, nv, drain_out, 0)


def _moe(h3, block_expert, block_valid, row_tok, row_dst, w_gu, b_gu, w_d, b_d):
    t, d = h3.shape
    ne, _, n2 = w_gu.shape
    dff = n2 // 2
    nblk = block_expert.shape[0]
    idx_blk = pl.BlockSpec((1, 1, MOE_ROWS), lambda i, be, bv: (i, 0, 0), memory_space=pltpu.SMEM)
    grid_spec = pltpu.PrefetchScalarGridSpec(
        num_scalar_prefetch=2,
        grid=(nblk,),
        in_specs=[idx_blk, idx_blk, pl.BlockSpec(memory_space=pl.ANY),
                  pl.BlockSpec((1, d, n2), lambda i, be, bv: (be[i], 0, 0)),
                  pl.BlockSpec((1, 1, n2), lambda i, be, bv: (be[i], 0, 0)),
                  pl.BlockSpec((1, dff, d), lambda i, be, bv: (be[i], 0, 0)),
                  pl.BlockSpec((1, 1, d), lambda i, be, bv: (be[i], 0, 0))],
        out_specs=pl.BlockSpec(memory_space=pl.ANY),
        scratch_shapes=[pltpu.VMEM((MOE_ROWS, d), F32), pltpu.VMEM((MOE_ROWS, d), F32),
                        pltpu.SemaphoreType.DMA(()), pltpu.SemaphoreType.DMA(())],
    )
    return pl.pallas_call(
        functools.partial(_moe_kernel, dff=dff),
        grid_spec=grid_spec,
        out_shape=jax.ShapeDtypeStruct((TOP_K * t, d), F32),
        compiler_params=_params("arbitrary"),
        name="moe",
    )(block_expert, block_valid, row_tok.reshape(nblk, 1, MOE_ROWS), row_dst.reshape(nblk, 1, MOE_ROWS), h3,
      w_gu.astype(BF16), b_gu.reshape(ne, 1, n2), w_d.astype(BF16), b_d.reshape(ne, 1, d))


def _route_plan(expert, t):
    n_pairs = t * TOP_K
    e_flat = expert.reshape(-1)
    order = jnp.argsort(e_flat, stable=True).astype(jnp.int32)
    counts = jnp.sum((e_flat[:, None] == jnp.arange(N_EXPERTS, dtype=jnp.int32)[None, :]).astype(jnp.int32), axis=0)
    padded = (counts + MOE_ROWS - 1) // MOE_ROWS * MOE_ROWS
    pend = jnp.cumsum(padded)
    pstart = pend - padded
    gstart = jnp.cumsum(counts) - counts
    nblk = n_pairs // MOE_ROWS + N_EXPERTS
    blk_row0 = jnp.arange(nblk, dtype=jnp.int32) * MOE_ROWS
    block_expert = jnp.minimum(jnp.searchsorted(pend, blk_row0, side="right"), N_EXPERTS - 1).astype(jnp.int32)
    in_group = blk_row0 - pstart[block_expert]
    block_valid = jnp.where(blk_row0 < pend[-1], jnp.clip(counts[block_expert] - in_group, 0, MOE_ROWS), 0)
    r = jnp.arange(nblk * MOE_ROWS, dtype=jnp.int32)
    r_e = jnp.repeat(block_expert, MOE_ROWS)
    srt = gstart[r_e] + (r - pstart[r_e])
    pair = order[jnp.clip(srt, 0, n_pairs - 1)]
    row_tok = pair // TOP_K
    row_dst = (pair % TOP_K) * t + row_tok
    return block_expert, block_valid.astype(jnp.int32), row_tok.astype(jnp.int32), row_dst.astype(jnp.int32)


def _combine_kernel(x2_ref, route_ref, y_ref, g_ref, o_ref, *, final):
    acc = x2_ref[...]
    route = route_ref[...]
    for kk in range(TOP_K):
        acc = acc + route[:, kk:kk + 1] * y_ref[kk]
    o_ref[...] = _rms(acc, g_ref[...]) if final else acc


def _combine(x2, route, y, g, final):
    t, d = x2.shape
    return pl.pallas_call(
        functools.partial(_combine_kernel, final=final),
        grid=(t // MID_ROWS,),
        in_specs=[pl.BlockSpec((MID_ROWS, d), lambda i: (i, 0)), pl.BlockSpec((MID_ROWS, LANES), lambda i: (i, 0)),
                  pl.BlockSpec((TOP_K, MID_ROWS, d), lambda i: (0, i, 0)), pl.BlockSpec((1, d), lambda i: (0, 0))],
        out_specs=pl.BlockSpec((MID_ROWS, d), lambda i: (i, 0)),
        out_shape=jax.ShapeDtypeStruct((t, d), F32),
        compiler_params=_params("arbitrary"),
        name="combine",
    )(x2, route, y.reshape(TOP_K, t, d), g.reshape(1, d))


def kernel(x, mem, norm_mix_g, w_in, gdn_conv_w, gdn_A_log, gdn_dt_bias, gdn_norm_g, moba_norm_g, w_out,
           norm_xattn_g, norm_mem_g, xattn_w_q, xattn_w_kv, xattn_w_o, norm_ffn_g, router_w, router_b,
           w_gate_up, b_gate_up, w_down, b_down, final_norm_g):
    b, s, d = x.shape
    t = b * s
    mem_len = mem.shape[1]
    mw = moba_norm_g.shape[1]
    gw = GDN_HEADS * GDN_HEAD_DIM
    xcur = x.reshape(t, d)
    for l in range(w_in.shape[0]):
        mq, mk, mv, gqkv, gz, gba = _in_proj(xcur, norm_mix_g[l], w_in[l], mw, gw)
        o_moba = _moba(mq.reshape(b, s, mw), mk.reshape(b, s, mw), mv.reshape(b, s, mw))
        o_gdn = _gdn(gqkv.reshape(b, s, 3 * gw), gz.reshape(b, s, gw), gba.reshape(b, s, LANES),
                     gdn_conv_w[l], gdn_A_log[l], gdn_dt_bias[l], gdn_norm_g[l])
        kv = _mem_kv(mem.reshape(b * mem_len, d), norm_mem_g[l], xattn_w_kv[l], mem_len)
        x2, h3, route = _mid(xcur, o_moba.reshape(t, mw), o_gdn.reshape(t, gw), moba_norm_g[l], w_out[l],
                             norm_xattn_g[l], xattn_w_q[l], kv, xattn_w_o[l], norm_ffn_g[l], router_w[l],
                             router_b[l], s, mem_len)
        expert = route[:, TOP_K:2 * TOP_K].astype(jnp.int32)
        plan = _route_plan(expert, t)
        y = _moe(h3, *plan, w_gate_up[l], b_gate_up[l], w_down[l], b_down[l])
        xcur = _combine(x2, route, y, final_norm_g, l == w_in.shape[0] - 1)
    return xcur.reshape(b, s, d)
```

```python
import functools

import jax
import jax.numpy as jnp
from jax import lax
from jax.experimental import pallas as pl
from jax.experimental.pallas import tpu as pltpu

F32 = jnp.float32
BF16 = jnp.bfloat16

RMS_EPS = 1e-6
MOBA_HEAD_DIM = 64
MOBA_BLOCK = 256
MOBA_TOPK = 3
GDN_HEAD_DIM = 128
GDN_HEADS = 4
GDN_CONV = 4
GDN_CHUNK = 64
XATTN_HEAD_DIM = 128
N_EXPERTS = 32
TOP_K = 4
SWIGLU_LIMIT = 7.0
SWIGLU_ALPHA = 1.702

LANES = 128
VMEM_LIMIT = 56 * 1024 * 1024

IN_ROWS = 512
MID_ROWS = 256
MOE_ROWS = 512
GDN_GROUP = 8
NEG_INF = float("-inf")


def _params(*sem):
    return pltpu.CompilerParams(dimension_semantics=sem, vmem_limit_bytes=VMEM_LIMIT)


def _rms(x, g):
    return x * lax.rsqrt(jnp.mean(x * x, axis=-1, keepdims=True) + RMS_EPS) * g


def _dot(a, b):
    return jnp.dot(a, b, preferred_element_type=F32)


def _dot_nt(a, b):
    return lax.dot_general(a, b, (((1,), (1,)), ((), ())), preferred_element_type=F32)


def _sigmoid(x):
    return 1.0 / (1.0 + jnp.exp(-x))


def _in_proj_kernel(x_ref, g_ref, w_ref, wt_ref, qt_ref, mk_ref, vt_ref, gqkv_ref, gz_ref, gba_ref, *, mw, gw):
    hn = _rms(x_ref[...], g_ref[...]).astype(BF16)
    mm = lambda lo, hi: _dot(hn, w_ref[:, lo:hi])
    qt = (_dot_nt(wt_ref[:mw, :], hn) * (MOBA_HEAD_DIM ** -0.5)).astype(BF16)
    vt = _dot_nt(wt_ref[mw:, :], hn).astype(BF16)
    for j in range(IN_ROWS // MOBA_BLOCK):
        qt_ref[j] = qt[:, j * MOBA_BLOCK:(j + 1) * MOBA_BLOCK]
        vt_ref[j] = vt[:, j * MOBA_BLOCK:(j + 1) * MOBA_BLOCK]
    mk_ref[...] = mm(0, mw).astype(BF16)
    gqkv_ref[...] = mm(mw, mw + 3 * gw)
    gz_ref[...] = mm(mw + 3 * gw, mw + 4 * gw)
    gba_ref[...] = mm(mw + 4 * gw, mw + 4 * gw + LANES)


def _in_proj(x2d, g, w_in, mw, gw):
    t, d = x2d.shape
    n_real = w_in.shape[1] - 2 * mw
    n_pad = mw + 4 * gw + LANES
    w = jnp.concatenate([w_in[:, mw:2 * mw], w_in[:, 3 * mw:]], axis=1)
    w = jnp.pad(w, ((0, 0), (0, n_pad - n_real))).astype(BF16)
    w_t = jnp.concatenate([w_in[:, :mw], w_in[:, 2 * mw:3 * mw]], axis=1).T.astype(BF16)
    row = lambda n: pl.BlockSpec((IN_ROWS, n), lambda i: (i, 0))
    per_tile = IN_ROWS // MOBA_BLOCK
    slab = pl.BlockSpec((per_tile, mw, MOBA_BLOCK), lambda i: (i, 0, 0))
    slab_shape = jax.ShapeDtypeStruct((t // MOBA_BLOCK, mw, MOBA_BLOCK), BF16)
    return pl.pallas_call(
        functools.partial(_in_proj_kernel, mw=mw, gw=gw),
        grid=(t // IN_ROWS,),
        in_specs=[row(d), pl.BlockSpec((1, d), lambda i: (0, 0)), pl.BlockSpec((d, n_pad), lambda i: (0, 0)),
                  pl.BlockSpec((2 * mw, d), lambda i: (0, 0))],
        out_specs=[slab, row(mw), slab, row(3 * gw), row(gw), row(LANES)],
        out_shape=[slab_shape, jax.ShapeDtypeStruct((t, mw), BF16), slab_shape,
                   jax.ShapeDtypeStruct((t, 3 * gw), F32), jax.ShapeDtypeStruct((t, gw), F32),
                   jax.ShapeDtypeStruct((t, LANES), F32)],
        compiler_params=_params("arbitrary"),
        name="in_proj",
    )(x2d, g.reshape(1, d), w, w_t)


def _moba_select(g_t, i):
    nb = g_t.shape[0]
    row = lax.broadcasted_iota(jnp.int32, g_t.shape, 0)
    valid = row < i
    sel = jnp.zeros_like(g_t)
    for j in range(nb):
        gj = g_t[j:j + 1, :]
        beats = valid & ((g_t > gj) | ((g_t == gj) & (row < j)))
        rank = jnp.sum(jnp.where(beats, 1.0, 0.0), axis=0, keepdims=True)
        sel = jnp.where(row == j, jnp.where(rank < MOBA_TOPK, 1.0, 0.0), sel)
    return jnp.where(valid, sel, 0.0)


def _moba_kernel(qt_ref, k_ref, vt_ref, o_ref, kmean_ref, sel_ref, *, nb):
    i = pl.program_id(2)
    bs = MOBA_BLOCK

    @pl.when(i == 0)
    def _():
        for j in range(nb):
            kb = k_ref[0, j * bs:(j + 1) * bs, :].astype(F32)
            kmean_ref[j:j + 1, :] = jnp.mean(kb, axis=0, keepdims=True)

    qt = qt_ref[0]
    low = lax.broadcasted_iota(jnp.int32, (LANES, 1), 0) < MOBA_HEAD_DIM
    zero = jnp.zeros_like(qt)
    qt_heads = (jnp.where(low, qt, zero), jnp.where(low, zero, qt))
    kmean = kmean_ref[...]
    for h, qh in enumerate(qt_heads):
        g_t = jnp.dot(kmean, qh.astype(F32), preferred_element_type=F32, precision=lax.Precision.HIGHEST)
        sel_ref[h] = _moba_select(g_t, i)

    key_ix = lax.broadcasted_iota(jnp.int32, (bs, bs), 0)
    qry_ix = lax.broadcasted_iota(jnp.int32, (bs, bs), 1)
    causal = key_ix <= qry_ix

    def attend(kj, vtj, mask, carry):
        scores = [_dot(kj, qh) for qh in qt_heads]
        stats = []
        for h, s in enumerate(scores):
            s = jnp.where(mask(h), s, NEG_INF)
            m_new = jnp.max(s, axis=0, keepdims=True)
            if carry is not None:
                m_new = jnp.maximum(carry[3 * h], m_new)
            p = jnp.exp(s - m_new)
            stats.append((m_new, jnp.sum(p, axis=0, keepdims=True), p.astype(BF16)))
        pv = [_dot(vtj, st[2]) for st in stats]
        out = []
        for h, (m_new, p_sum, _) in enumerate(stats):
            if carry is None:
                out += [m_new, p_sum, pv[h]]
            else:
                m, l, acc = carry[3 * h:3 * h + 3]
                alpha = jnp.exp(m - m_new)
                out += [m_new, alpha * l + p_sum, alpha * acc + pv[h]]
        return tuple(out)

    init = attend(k_ref[0, pl.ds(pl.multiple_of(i * bs, bs), bs), :], vt_ref[i], lambda h: causal, None)

    def body(j, carry):
        kj = k_ref[0, pl.ds(pl.multiple_of(j * bs, bs), bs), :]
        return attend(kj, vt_ref[j], lambda h: sel_ref[h, pl.ds(j, 1), :] > 0.5, carry)

    m0, l0, a0, m1, l1, a1 = lax.fori_loop(0, i, body, init)
    o_ref[0] = jnp.where(low, a0 / l0, a1 / l1).T


def _moba(qt, mk, vt, b, s):
    mw = mk.shape[-1]
    nb = s // MOBA_BLOCK
    return pl.pallas_call(
        functools.partial(_moba_kernel, nb=nb),
        grid=(b, mw // LANES, nb),
        in_specs=[pl.BlockSpec((1, LANES, MOBA_BLOCK), lambda bi, hp, i: (bi * nb + i, hp, 0)),
                  pl.BlockSpec((1, s, LANES), lambda bi, hp, i: (bi, 0, hp)),
                  pl.BlockSpec((nb, LANES, MOBA_BLOCK), lambda bi, hp, i: (bi, hp, 0))],
        out_specs=pl.BlockSpec((1, MOBA_BLOCK, LANES), lambda bi, hp, i: (bi, i, hp)),
        out_shape=jax.ShapeDtypeStruct((b, s, mw), F32),
        scratch_shapes=[pltpu.VMEM((nb, LANES), F32), pltpu.VMEM((2, nb, MOBA_BLOCK), F32)],
        compiler_params=_params("arbitrary", "arbitrary", "arbitrary"),
        name="moba",
    )(qt, mk.reshape(b, s, mw), vt)


def _gdn_gate_kernel(ba_ref, alog_ref, dtb_ref, bg_ref, *, seq):
    x = ba_ref[0]
    lane = lax.broadcasted_iota(jnp.int32, (1, LANES), 1)
    xa = x + dtb_ref[...]
    softplus = jnp.maximum(xa, 0.0) + jnp.log(1.0 + jnp.exp(-jnp.abs(xa)))
    g = jnp.where((lane >= GDN_HEADS) & (lane < 2 * GDN_HEADS), -jnp.exp(alog_ref[...]) * softplus, 0.0)
    pos = lax.broadcasted_iota(jnp.int32, (seq, 1), 0) % GDN_CHUNK
    sft = 1
    while sft < GDN_CHUNK:
        g = g + jnp.where(pos >= sft, pltpu.roll(g, sft, 0), 0.0)
        sft *= 2
    bg_ref[0] = jnp.where(lane < GDN_HEADS, _sigmoid(x), g)


def _gdn_prep_kernel(x_ref, w_ref, o_ref, *, seq):
    cb = pl.program_id(1)
    row = lax.broadcasted_iota(jnp.int32, (seq, 1), 0)
    x = x_ref[0]
    y = w_ref[GDN_CONV - 1:GDN_CONV, :] * x
    for sft in range(1, GDN_CONV):
        y = y + w_ref[GDN_CONV - 1 - sft:GDN_CONV - sft, :] * jnp.where(row >= sft, pltpu.roll(x, sft, 0), 0.0)
    y = y * _sigmoid(y)

    @pl.when(cb < 2 * GDN_HEADS)
    def _():
        scale = jnp.where(cb < GDN_HEADS, GDN_HEAD_DIM ** -0.5, 1.0)
        o_ref[0] = (y * (lax.rsqrt(jnp.sum(y * y, axis=-1, keepdims=True) + RMS_EPS) * scale)).astype(BF16)

    @pl.when(cb >= 2 * GDN_HEADS)
    def _():
        o_ref[0] = y.astype(BF16)


def _unit_lower_inverses(mats):
    c = mats[0].shape[0]
    r = lax.broadcasted_iota(jnp.int32, (c, c), 0)
    cc = lax.broadcasted_iota(jnp.int32, (c, c), 1)
    eye = jnp.where(r == cc, 1.0, 0.0)
    pair = (r // 2) == (cc // 2)
    invs = [eye - jnp.where(pair, a, 0.0) for a in mats]
    size = 4
    while size <= c:
        level = ((r // size) == (cc // size)) & ((r // (size // 2)) != (cc // (size // 2)))
        inv_b = [inv.astype(BF16) for inv in invs]
        left = [_dot(ib, jnp.where(level, a, 0.0).astype(BF16)).astype(BF16) for ib, a in zip(inv_b, mats)]
        invs = [inv - _dot(lf, ib) for inv, lf, ib in zip(invs, left, inv_b)]
        size *= 2
    return invs


def _gdn_intra_kernel(q_ref, k_ref, v_ref, bg_ref, u_ref, w_ref, qd_ref, kd_ref, qk_ref):
    h = pl.program_id(1)
    c = GDN_CHUNK
    chunks = range(GDN_GROUP)
    lane = lax.broadcasted_iota(jnp.int32, (1, LANES), 1)
    r_ix = lax.broadcasted_iota(jnp.int32, (c, c), 0)
    c_ix = lax.broadcasted_iota(jnp.int32, (c, c), 1)
    rows = [slice(gi * c, (gi + 1) * c) for gi in chunks]
    k_b16 = [k_ref[0, rw, :] for rw in rows]
    q_b16 = [q_ref[0, rw, :] for rw in rows]
    bgs = [bg_ref[0, rw, :] for rw in rows]
    beta = [jnp.sum(jnp.where(lane == h, bg, 0.0), axis=1, keepdims=True) for bg in bgs]
    gam = [jnp.sum(jnp.where(lane == GDN_HEADS + h, bg, 0.0), axis=1, keepdims=True) for bg in bgs]
    kb = [kk.astype(F32) * bt for kk, bt in zip(k_b16, beta)]
    kk_raw = [_dot_nt(x.astype(BF16), kk) for x, kk in zip(kb, k_b16)]
    qk_raw = [_dot_nt(qq, kk) for qq, kk in zip(q_b16, k_b16)]
    decay = []
    for gm in gam:
        gam_r = jnp.sum(jnp.where(r_ix == c_ix, gm, 0.0), axis=0, keepdims=True)
        decay.append(jnp.exp(jnp.where(c_ix <= r_ix, gm - gam_r, NEG_INF)))
    t_inv = _unit_lower_inverses([jnp.where(c_ix < r_ix, x * dc, 0.0) for x, dc in zip(kk_raw, decay)])
    eg = [jnp.exp(gm) for gm in gam]
    rhs = [jnp.concatenate([v_ref[0, rw, :].astype(F32) * bt, x * e], axis=1).astype(BF16)
           for rw, bt, x, e in zip(rows, beta, kb, eg)]
    uw = [_dot(ti.astype(BF16), rh) for ti, rh in zip(t_inv, rhs)]
    for gi in chunks:
        rw = rows[gi]
        u_ref[0, rw, :] = uw[gi][:, :LANES].astype(BF16)
        w_ref[0, rw, :] = uw[gi][:, LANES:].astype(BF16)
        qk_ref[0, 0, rw, :] = (qk_raw[gi] * decay[gi]).astype(BF16)
        qd_ref[0, rw, :] = (q_b16[gi].astype(F32) * eg[gi]).astype(BF16)
        kd_ref[0, rw, :] = (k_b16[gi].astype(F32) * jnp.exp(gam[gi][c - 1:c, :] - gam[gi])).astype(BF16)


def _gdn_scan_kernel(u_ref, w_ref, qd_ref, kd_ref, qk_ref, z_ref, bg_ref, ng_ref, o_ref, *, seq):
    c = GDN_CHUNK
    ng = ng_ref[...]
    heads = range(GDN_HEADS)
    cols = [slice(h * GDN_HEAD_DIM, (h + 1) * GDN_HEAD_DIM) for h in heads]

    def step(n, states):
        r0 = pl.multiple_of(n * c, c)
        rows = pl.ds(r0, c)
        bg_tail = bg_ref[0, pl.ds(r0 + c - 8, 8), :]
        s_b = [st.astype(BF16) for st in states]
        ws = [_dot(w_ref[0, rows, cols[h]], s_b[h]) for h in heads]
        qs = [_dot(qd_ref[0, rows, cols[h]], s_b[h]) for h in heads]
        v_b = [(u_ref[0, rows, cols[h]].astype(F32) - ws[h]).astype(BF16) for h in heads]
        kd_v = [lax.dot_general(kd_ref[0, rows, cols[h]], v_b[h], (((0,), (0,)), ((), ())),
                                preferred_element_type=F32) for h in heads]
        qkv = [_dot(qk_ref[0, h, rows, :], v_b[h]) for h in heads]
        new_states = []
        for h in heads:
            g_last = bg_tail[7:8, GDN_HEADS + h:GDN_HEADS + h + 1]
            new_states.append(states[h] * jnp.exp(g_last) + kd_v[h])
            z = z_ref[0, rows, cols[h]]
            o_ref[0, rows, cols[h]] = _rms(qs[h] + qkv[h], ng) * (z * _sigmoid(z))
        return tuple(new_states)

    zero = jnp.zeros((GDN_HEAD_DIM, GDN_HEAD_DIM), F32)
    lax.fori_loop(0, seq // c, step, (zero,) * GDN_HEADS)


def _gdn(gqkv, gz, gba, conv_w, a_log, dt_bias, norm_g):
    b, s, w3 = gqkv.shape
    nh = GDN_HEADS
    hw = nh * GDN_HEAD_DIM
    c = GDN_CHUNK
    lane_pad = lambda v: jnp.pad(v.reshape(1, -1), ((0, 0), (nh, LANES - 2 * nh)))
    seq_blk = pl.BlockSpec((1, s, LANES), lambda bi: (bi, 0, 0))
    const = pl.BlockSpec((1, LANES), lambda bi: (0, 0))
    bg = pl.pallas_call(
        functools.partial(_gdn_gate_kernel, seq=s),
        grid=(b,),
        in_specs=[seq_blk, const, const],
        out_specs=seq_blk,
        out_shape=jax.ShapeDtypeStruct((b, s, LANES), F32),
        compiler_params=_params("arbitrary"),
        name="gdn_gate",
    )(gba, lane_pad(a_log), lane_pad(dt_bias))

    qkv = pl.pallas_call(
        functools.partial(_gdn_prep_kernel, seq=s),
        grid=(b, w3 // LANES),
        in_specs=[pl.BlockSpec((1, s, LANES), lambda bi, cb: (bi, 0, cb)),
                  pl.BlockSpec((GDN_CONV, LANES), lambda bi, cb: (0, cb))],
        out_specs=pl.BlockSpec((1, s, LANES), lambda bi, cb: (bi, 0, cb)),
        out_shape=jax.ShapeDtypeStruct((b, s, w3), BF16),
        compiler_params=_params("arbitrary", "arbitrary"),
        name="gdn_prep",
    )(gqkv, conv_w)

    grp = GDN_GROUP * c
    head_blk = lambda off: pl.BlockSpec((1, grp, LANES), lambda bi, h, n: (bi, n, off + h))
    head_shape = jax.ShapeDtypeStruct((b, s, hw), BF16)
    u, w, qd, kd, qk = pl.pallas_call(
        _gdn_intra_kernel,
        grid=(b, nh, s // grp),
        in_specs=[head_blk(0), head_blk(nh), head_blk(2 * nh),
                  pl.BlockSpec((1, grp, LANES), lambda bi, h, n: (bi, n, 0))],
        out_specs=[head_blk(0)] * 4 + [pl.BlockSpec((1, 1, grp, c), lambda bi, h, n: (bi, h, n, 0))],
        out_shape=[head_shape] * 4 + [jax.ShapeDtypeStruct((b, nh, s, c), BF16)],
        compiler_params=_params("arbitrary", "arbitrary", "arbitrary"),
        name="gdn_intra",
    )(qkv, qkv, qkv, bg)

    full = pl.BlockSpec((1, s, hw), lambda bi: (bi, 0, 0))
    return pl.pallas_call(
        functools.partial(_gdn_scan_kernel, seq=s),
        grid=(b,),
        in_specs=[full, full, full, full, pl.BlockSpec((1, nh, s, c), lambda bi: (bi, 0, 0, 0)), full, seq_blk,
                  const],
        out_specs=full,
        out_shape=jax.ShapeDtypeStruct((b, s, hw), F32),
        compiler_params=_params("arbitrary"),
        name="gdn_scan",
    )(u, w, qd, kd, qk, gz, bg, norm_g.reshape(1, -1))


def _mem_kv_kernel(m_ref, g_ref, w_ref, kv_ref):
    kv_ref[...] = _dot(_rms(m_ref[...], g_ref[...]).astype(BF16), w_ref[...]).astype(BF16)


def _mem_kv(mem2d, g, w_kv, rows):
    t, d = mem2d.shape
    n = w_kv.shape[1]
    return pl.pallas_call(
        _mem_kv_kernel,
        grid=(t // rows,),
        in_specs=[pl.BlockSpec((rows, d), lambda i: (i, 0)), pl.BlockSpec((1, d), lambda i: (0, 0)),
                  pl.BlockSpec((d, n), lambda i: (0, 0))],
        out_specs=pl.BlockSpec((rows, n), lambda i: (i, 0)),
        out_shape=jax.ShapeDtypeStruct((t, n), BF16),
        compiler_params=_params("arbitrary"),
        name="mem_kv",
    )(mem2d, g.reshape(1, d), w_kv.astype(BF16))


def _mid_kernel(x_ref, om_ref, og_ref, mg_ref, wout_ref, xg_ref, wq_ref, kv_ref, wo_ref, fg_ref, rw_ref, rb_ref,
                x2_ref, h3_ref, route_ref, *, mw, xw):
    mo = _rms(om_ref[...], mg_ref[...]).astype(BF16)
    x1 = x_ref[...] + _dot(mo, wout_ref[:mw, :]) + _dot(og_ref[...].astype(BF16), wout_ref[mw:, :])

    h2 = _rms(x1, xg_ref[...]).astype(BF16)
    q = (_dot(h2, wq_ref[...]) * (XATTN_HEAD_DIM ** -0.5)).astype(BF16)
    head_cols = [slice(h * XATTN_HEAD_DIM, (h + 1) * XATTN_HEAD_DIM) for h in range(xw // XATTN_HEAD_DIM)]
    scores = [_dot_nt(q[:, sl], kv_ref[:, sl]) for sl in head_cols]
    probs = [jnp.exp(s - jnp.max(s, axis=1, keepdims=True)) for s in scores]
    heads = [_dot(p.astype(BF16), kv_ref[:, xw + sl.start:xw + sl.stop]) / jnp.sum(p, axis=1, keepdims=True)
             for p, sl in zip(probs, head_cols)]
    x2 = x1 + _dot(jnp.concatenate(heads, axis=1).astype(BF16), wo_ref[...])
    x2_ref[...] = x2

    h3 = _rms(x2, fg_ref[...])
    h3_ref[...] = h3
    logits = jnp.dot(h3, rw_ref[...], preferred_element_type=F32, precision=lax.Precision.HIGHEST) + rb_ref[...]
    ne = logits.shape[1]
    col = lax.broadcasted_iota(jnp.int32, logits.shape, 1)
    lane = lax.broadcasted_iota(jnp.int32, (1, LANES), 1)
    route = jnp.zeros((logits.shape[0], LANES), F32)
    top, denom = None, None
    for kk in range(TOP_K):
        m = jnp.max(logits, axis=1, keepdims=True)
        idx = jnp.min(jnp.where(logits == m, col, ne), axis=1, keepdims=True)
        logits = jnp.where(col == idx, NEG_INF, logits)
        if kk == 0:
            top = m
        e = jnp.exp(m - top)
        denom = e if kk == 0 else denom + e
        route = jnp.where(lane == kk, e, route)
        route = jnp.where(lane == TOP_K + kk, idx.astype(F32), route)
    route_ref[...] = jnp.where(lane < TOP_K, route / denom, route)


def _mid(x2d, om, og, moba_g, w_out, xattn_g, w_q, kv, w_o, ffn_g, router_w, router_b, seq, mem_len):
    t, d = x2d.shape
    mw, gw, xw, ne = om.shape[1], og.shape[1], w_q.shape[1], router_w.shape[1]
    tiles_per_seq = seq // MID_ROWS
    row = lambda n: pl.BlockSpec((MID_ROWS, n), lambda i: (i, 0))
    const = lambda r, c: pl.BlockSpec((r, c), lambda i: (0, 0))
    return pl.pallas_call(
        functools.partial(_mid_kernel, mw=mw, xw=xw),
        grid=(t // MID_ROWS,),
        in_specs=[row(d), row(mw), row(gw), const(1, mw), const(mw + gw, d), const(1, d), const(d, xw),
                  pl.BlockSpec((mem_len, 2 * xw), lambda i: (i // tiles_per_seq, 0)),
                  const(xw, d), const(1, d), const(d, ne), const(1, ne)],
        out_specs=[row(d), row(d), row(LANES)],
        out_shape=[jax.ShapeDtypeStruct((t, d), F32), jax.ShapeDtypeStruct((t, d), F32),
                   jax.ShapeDtypeStruct((t, LANES), F32)],
        compiler_params=_params("arbitrary"),
        name="mid",
    )(x2d, om, og, moba_g.reshape(1, mw), w_out.astype(BF16), xattn_g.reshape(1, d), w_q.astype(BF16), kv,
      w_o.astype(BF16), ffn_g.reshape(1, d), router_w, router_b.reshape(1, ne))


def _moe_kernel(be_ref, bv_ref, tok_ref, tokn_ref, dst_ref, h_hbm, wgu_ref, bgu_ref, wd_ref, bd_ref, y_hbm,
                xbuf, ybuf, sem_in, sem_out, *, dff):
    blk = pl.program_id(0)
    nv = bv_ref[blk]
    nv_next = bv_ref[blk + 1]
    slot = blk % 2
    rows = MOE_ROWS

    def row_in(s, r, t):
        return pltpu.make_async_copy(h_hbm.at[pl.ds(t, 1), :], xbuf.at[s, pl.ds(r, 1), :], sem_in.at[s])

    def row_out(s, r, d):
        return pltpu.make_async_copy(ybuf.at[s, pl.ds(r, 1), :], y_hbm.at[pl.ds(d, 1), :], sem_out.at[s])

    def wait_scatter(s, count):
        @pl.when(count == rows)
        def _():
            pltpu.make_async_copy(ybuf.at[s], y_hbm.at[pl.ds(0, rows), :], sem_out.at[s]).wait()

        @pl.when(count < rows)
        def _():
            def one(r, c):
                row_out(s, 0, 0).wait()
                return c
            lax.fori_loop(0, count, one, 0)

    @pl.when((blk == 0) & (nv > 0))
    def _():
        for r in range(rows):
            row_in(0, r, tok_ref[0, 0, r]).start()

    @pl.when(nv > 0)
    def _():
        @pl.when(nv_next > 0)
        def _():
            for r in range(rows):
                row_in(1 - slot, r, tokn_ref[0, 0, r]).start()

        pltpu.make_async_copy(h_hbm.at[pl.ds(0, rows), :], xbuf.at[slot], sem_in.at[slot]).wait()
        x = xbuf[slot].astype(BF16)
        gu = _dot(x, wgu_ref[0]) + bgu_ref[0]
        gate = jnp.minimum(gu[:, :dff], SWIGLU_LIMIT)
        up = jnp.clip(gu[:, dff:], -SWIGLU_LIMIT, SWIGLU_LIMIT)
        act = (up + 1.0) * gate * _sigmoid(SWIGLU_ALPHA * gate)
        y = _dot(act.astype(BF16), wd_ref[0]) + bd_ref[0]

        @pl.when(blk >= 2)
        def _():
            wait_scatter(slot, bv_ref[jnp.maximum(blk - 2, 0)])

        ybuf[slot] = y

        @pl.when(nv == rows)
        def _():
            for r in range(rows):
                row_out(slot, r, dst_ref[0, 0, r]).start()

        @pl.when(nv < rows)
        def _():
            def one(r, c):
                row_out(slot, r, dst_ref[0, 0, r]).start()
                return c
            lax.fori_loop(0, nv, one, 0)

        @pl.when(nv_next == 0)
        def _():
            @pl.when(blk >= 1)
            def _():
                wait_scatter(1 - slot, bv_ref[jnp.maximum(blk - 1, 0)])
            wait_scatter(slot, nv)


def _moe(h3, block_expert, block_valid, row_tok, row_dst, w_gu, b_gu, w_d, b_d):
    t, d = h3.shape
    ne, _, n2 = w_gu.shape
    dff = n2 // 2
    nblk = block_expert.shape[0]
    idx_blk = pl.BlockSpec((1, 1, MOE_ROWS), lambda i, be, bv: (i, 0, 0), memory_space=pltpu.SMEM)
    idx_next = pl.BlockSpec((1, 1, MOE_ROWS), lambda i, be, bv: (jnp.minimum(i + 1, nblk - 1), 0, 0),
                            memory_space=pltpu.SMEM)
    grid_spec = pltpu.PrefetchScalarGridSpec(
        num_scalar_prefetch=2,
        grid=(nblk,),
        in_specs=[idx_blk, idx_next, idx_blk, pl.BlockSpec(memory_space=pl.ANY),
                  pl.BlockSpec((1, d, n2), lambda i, be, bv: (be[i], 0, 0)),
                  pl.BlockSpec((1, 1, n2), lambda i, be, bv: (be[i], 0, 0)),
                  pl.BlockSpec((1, dff, d), lambda i, be, bv: (be[i], 0, 0)),
                  pl.BlockSpec((1, 1, d), lambda i, be, bv: (be[i], 0, 0))],
        out_specs=pl.BlockSpec(memory_space=pl.ANY),
        scratch_shapes=[pltpu.VMEM((2, MOE_ROWS, d), F32), pltpu.VMEM((2, MOE_ROWS, d), F32),
                        pltpu.SemaphoreType.DMA((2,)), pltpu.SemaphoreType.DMA((2,))],
    )
    tok3 = row_tok.reshape(nblk, 1, MOE_ROWS)
    valid_ext = jnp.concatenate([block_valid, jnp.zeros((1,), jnp.int32)])
    return pl.pallas_call(
        functools.partial(_moe_kernel, dff=dff),
        grid_spec=grid_spec,
        out_shape=jax.ShapeDtypeStruct((TOP_K * t, d), F32),
        compiler_params=_params("arbitrary"),
        name="moe",
    )(block_expert, valid_ext, tok3, tok3, row_dst.reshape(nblk, 1, MOE_ROWS), h3,
      w_gu.astype(BF16), b_gu.reshape(ne, 1, n2), w_d.astype(BF16), b_d.reshape(ne, 1, d))


def _route_plan(expert, t):
    n_pairs = t * TOP_K
    e_flat = expert.reshape(-1)
    order = jnp.argsort(e_flat, stable=True).astype(jnp.int32)
    counts = jnp.sum((e_flat[:, None] == jnp.arange(N_EXPERTS, dtype=jnp.int32)[None, :]).astype(jnp.int32), axis=0)
    padded = (counts + MOE_ROWS - 1) // MOE_ROWS * MOE_ROWS
    pend = jnp.cumsum(padded)
    pstart = pend - padded
    gstart = jnp.cumsum(counts) - counts
    nblk = n_pairs // MOE_ROWS + N_EXPERTS
    blk_row0 = jnp.arange(nblk, dtype=jnp.int32) * MOE_ROWS
    block_expert = jnp.minimum(jnp.searchsorted(pend, blk_row0, side="right"), N_EXPERTS - 1).astype(jnp.int32)
    in_group = blk_row0 - pstart[block_expert]
    block_valid = jnp.where(blk_row0 < pend[-1], jnp.clip(counts[block_expert] - in_group, 0, MOE_ROWS), 0)
    r = jnp.arange(nblk * MOE_ROWS, dtype=jnp.int32)
    r_e = jnp.repeat(block_expert, MOE_ROWS)
    srt = gstart[r_e] + (r - pstart[r_e])
    pair = order[jnp.clip(srt, 0, n_pairs - 1)]
    row_tok = pair // TOP_K
    row_dst = (pair % TOP_K) * t + row_tok
    return block_expert, block_valid.astype(jnp.int32), row_tok.astype(jnp.int32), row_dst.astype(jnp.int32)


def _combine_kernel(x2_ref, route_ref, y_ref, g_ref, o_ref, *, final):
    acc = x2_ref[...]
    route = route_ref[...]
    for kk in range(TOP_K):
        acc = acc + route[:, kk:kk + 1] * y_ref[kk]
    o_ref[...] = _rms(acc, g_ref[...]) if final else acc


def _combine(x2, route, y, g, final):
    t, d = x2.shape
    return pl.pallas_call(
        functools.partial(_combine_kernel, final=final),
        grid=(t // MID_ROWS,),
        in_specs=[pl.BlockSpec((MID_ROWS, d), lambda i: (i, 0)), pl.BlockSpec((MID_ROWS, LANES), lambda i: (i, 0)),
                  pl.BlockSpec((TOP_K, MID_ROWS, d), lambda i: (0, i, 0)), pl.BlockSpec((1, d), lambda i: (0, 0))],
        out_specs=pl.BlockSpec((MID_ROWS, d), lambda i: (i, 0)),
        out_shape=jax.ShapeDtypeStruct((t, d), F32),
        compiler_params=_params("arbitrary"),
        name="combine",
    )(x2, route, y.reshape(TOP_K, t, d), g.reshape(1, d))


def kernel(x, mem, norm_mix_g, w_in, gdn_conv_w, gdn_A_log, gdn_dt_bias, gdn_norm_g, moba_norm_g, w_out,
           norm_xattn_g, norm_mem_g, xattn_w_q, xattn_w_kv, xattn_w_o, norm_ffn_g, router_w, router_b,
           w_gate_up, b_gate_up, w_down, b_down, final_norm_g):
    b, s, d = x.shape
    t = b * s
    mem_len = mem.shape[1]
    mw = moba_norm_g.shape[1]
    gw = GDN_HEADS * GDN_HEAD_DIM
    xcur = x.reshape(t, d)
    for l in range(w_in.shape[0]):
        qt, mk, vt, gqkv, gz, gba = _in_proj(xcur, norm_mix_g[l], w_in[l], mw, gw)
        o_moba = _moba(qt, mk, vt, b, s)
        o_gdn = _gdn(gqkv.reshape(b, s, 3 * gw), gz.reshape(b, s, gw), gba.reshape(b, s, LANES),
                     gdn_conv_w[l], gdn_A_log[l], gdn_dt_bias[l], gdn_norm_g[l])
        kv = _mem_kv(mem.reshape(b * mem_len, d), norm_mem_g[l], xattn_w_kv[l], mem_len)
        x2, h3, route = _mid(xcur, o_moba.reshape(t, mw), o_gdn.reshape(t, gw), moba_norm_g[l], w_out[l],
                             norm_xattn_g[l], xattn_w_q[l], kv, xattn_w_o[l], norm_ffn_g[l], router_w[l],
                             router_b[l], s, mem_len)
        expert = route[:, TOP_K:2 * TOP_K].astype(jnp.int32)
        plan = _route_plan(expert, t)
        y = _moe(h3, *plan, w_gate_up[l], b_gate_up[l], w_down[l], b_down[l])
        xcur = _combine(x2, route, y, final_norm_g, l == w_in.shape[0] - 1)
    return xcur.reshape(b, s, d)
```

```python
import functools

import jax
import jax.numpy as jnp
from jax import lax
from jax.experimental import pallas as pl
from jax.experimental.pallas import tpu as pltpu

F32 = jnp.float32
BF16 = jnp.bfloat16

RMS_EPS = 1e-6
MOBA_HEAD_DIM = 64
MOBA_BLOCK = 256
MOBA_TOPK = 3
GDN_HEAD_DIM = 128
GDN_HEADS = 4
GDN_CONV = 4
GDN_CHUNK = 64
XATTN_HEAD_DIM = 128
N_EXPERTS = 32
TOP_K = 4
SWIGLU_LIMIT = 7.0
SWIGLU_ALPHA = 1.702

LANES = 128
VMEM_LIMIT = 56 * 1024 * 1024

IN_ROWS = 512
MID_ROWS = 256
MOE_ROWS = 512
MOE_CHUNK = 128
GDN_GROUP = 8
NEG_INF = float("-inf")


def _params(*sem):
    return pltpu.CompilerParams(dimension_semantics=sem, vmem_limit_bytes=VMEM_LIMIT)


def _rms(x, g):
    return x * lax.rsqrt(jnp.mean(x * x, axis=-1, keepdims=True) + RMS_EPS) * g


def _dot(a, b):
    return jnp.dot(a, b, preferred_element_type=F32)


def _dot_nt(a, b):
    return lax.dot_general(a, b, (((1,), (1,)), ((), ())), preferred_element_type=F32)


def _sigmoid(x):
    return 1.0 / (1.0 + jnp.exp(-x))


def _in_proj_kernel(x_ref, g_ref, w_ref, wt_ref, qt_ref, mk_ref, vt_ref, gqkv_ref, gz_ref, gba_ref, *, mw, gw):
    hn = _rms(x_ref[...], g_ref[...]).astype(BF16)
    mm = lambda lo, hi: _dot(hn, w_ref[:, lo:hi])
    qt = (_dot_nt(wt_ref[:mw, :], hn) * (MOBA_HEAD_DIM ** -0.5)).astype(BF16)
    vt = _dot_nt(wt_ref[mw:, :], hn).astype(BF16)
    for j in range(IN_ROWS // MOBA_BLOCK):
        qt_ref[j] = qt[:, j * MOBA_BLOCK:(j + 1) * MOBA_BLOCK]
        vt_ref[j] = vt[:, j * MOBA_BLOCK:(j + 1) * MOBA_BLOCK]
    mk_ref[...] = mm(0, mw).astype(BF16)
    gqkv_ref[...] = mm(mw, mw + 3 * gw)
    gz_ref[...] = mm(mw + 3 * gw, mw + 4 * gw)
    gba_ref[...] = mm(mw + 4 * gw, mw + 4 * gw + LANES)


def _in_proj(x2d, g, w_in, mw, gw):
    t, d = x2d.shape
    n_real = w_in.shape[1] - 2 * mw
    n_pad = mw + 4 * gw + LANES
    w = jnp.concatenate([w_in[:, mw:2 * mw], w_in[:, 3 * mw:]], axis=1)
    w = jnp.pad(w, ((0, 0), (0, n_pad - n_real))).astype(BF16)
    w_t = jnp.concatenate([w_in[:, :mw], w_in[:, 2 * mw:3 * mw]], axis=1).T.astype(BF16)
    row = lambda n: pl.BlockSpec((IN_ROWS, n), lambda i: (i, 0))
    per_tile = IN_ROWS // MOBA_BLOCK
    slab = pl.BlockSpec((per_tile, mw, MOBA_BLOCK), lambda i: (i, 0, 0))
    slab_shape = jax.ShapeDtypeStruct((t // MOBA_BLOCK, mw, MOBA_BLOCK), BF16)
    return pl.pallas_call(
        functools.partial(_in_proj_kernel, mw=mw, gw=gw),
        grid=(t // IN_ROWS,),
        in_specs=[row(d), pl.BlockSpec((1, d), lambda i: (0, 0)), pl.BlockSpec((d, n_pad), lambda i: (0, 0)),
                  pl.BlockSpec((2 * mw, d), lambda i: (0, 0))],
        out_specs=[slab, row(mw), slab, row(3 * gw), row(gw), row(LANES)],
        out_shape=[slab_shape, jax.ShapeDtypeStruct((t, mw), BF16), slab_shape,
                   jax.ShapeDtypeStruct((t, 3 * gw), F32), jax.ShapeDtypeStruct((t, gw), F32),
                   jax.ShapeDtypeStruct((t, LANES), F32)],
        compiler_params=_params("arbitrary"),
        name="in_proj",
    )(x2d, g.reshape(1, d), w, w_t)


def _moba_select(g_t, i):
    nb = g_t.shape[0]
    row = lax.broadcasted_iota(jnp.int32, g_t.shape, 0)
    valid = row < i
    sel = jnp.zeros_like(g_t)
    for j in range(nb):
        gj = g_t[j:j + 1, :]
        beats = valid & ((g_t > gj) | ((g_t == gj) & (row < j)))
        rank = jnp.sum(jnp.where(beats, 1.0, 0.0), axis=0, keepdims=True)
        sel = jnp.where(row == j, jnp.where(rank < MOBA_TOPK, 1.0, 0.0), sel)
    return jnp.where(valid, sel, 0.0)


def _moba_kernel(qt_ref, k_ref, vt_ref, o_ref, kmean_ref, sel_ref, *, nb):
    i = pl.program_id(2)
    bs = MOBA_BLOCK

    @pl.when(i == 0)
    def _():
        for j in range(nb):
            kb = k_ref[0, j * bs:(j + 1) * bs, :].astype(F32)
            kmean_ref[j:j + 1, :] = jnp.mean(kb, axis=0, keepdims=True)

    qt = qt_ref[0]
    low = lax.broadcasted_iota(jnp.int32, (LANES, 1), 0) < MOBA_HEAD_DIM
    zero = jnp.zeros_like(qt)
    qt_heads = (jnp.where(low, qt, zero), jnp.where(low, zero, qt))
    kmean = kmean_ref[...]
    for h, qh in enumerate(qt_heads):
        g_t = jnp.dot(kmean, qh.astype(F32), preferred_element_type=F32, precision=lax.Precision.HIGHEST)
        sel_ref[h] = _moba_select(g_t, i)

    key_ix = lax.broadcasted_iota(jnp.int32, (bs, bs), 0)
    qry_ix = lax.broadcasted_iota(jnp.int32, (bs, bs), 1)
    causal = key_ix <= qry_ix

    def attend(kj, vtj, mask, carry):
        scores = [_dot(kj, qh) for qh in qt_heads]
        stats = []
        for h, s in enumerate(scores):
            s = jnp.where(mask(h), s, NEG_INF)
            m_new = jnp.max(s, axis=0, keepdims=True)
            if carry is not None:
                m_new = jnp.maximum(carry[3 * h], m_new)
            p = jnp.exp(s - m_new)
            stats.append((m_new, jnp.sum(p, axis=0, keepdims=True), p.astype(BF16)))
        pv = [_dot(vtj, st[2]) for st in stats]
        out = []
        for h, (m_new, p_sum, _) in enumerate(stats):
            if carry is None:
                out += [m_new, p_sum, pv[h]]
            else:
                m, l, acc = carry[3 * h:3 * h + 3]
                alpha = jnp.exp(m - m_new)
                out += [m_new, alpha * l + p_sum, alpha * acc + pv[h]]
        return tuple(out)

    init = attend(k_ref[0, pl.ds(pl.multiple_of(i * bs, bs), bs), :], vt_ref[i], lambda h: causal, None)

    def body(j, carry):
        kj = k_ref[0, pl.ds(pl.multiple_of(j * bs, bs), bs), :]
        return attend(kj, vt_ref[j], lambda h: sel_ref[h, pl.ds(j, 1), :] > 0.5, carry)

    m0, l0, a0, m1, l1, a1 = lax.fori_loop(0, i, body, init)
    o_ref[0] = jnp.where(low, a0 / l0, a1 / l1).T


def _moba(qt, mk, vt, b, s):
    mw = mk.shape[-1]
    nb = s // MOBA_BLOCK
    return pl.pallas_call(
        functools.partial(_moba_kernel, nb=nb),
        grid=(b, mw // LANES, nb),
        in_specs=[pl.BlockSpec((1, LANES, MOBA_BLOCK), lambda bi, hp, i: (bi * nb + i, hp, 0)),
                  pl.BlockSpec((1, s, LANES), lambda bi, hp, i: (bi, 0, hp)),
                  pl.BlockSpec((nb, LANES, MOBA_BLOCK), lambda bi, hp, i: (bi, hp, 0))],
        out_specs=pl.BlockSpec((1, MOBA_BLOCK, LANES), lambda bi, hp, i: (bi, i, hp)),
        out_shape=jax.ShapeDtypeStruct((b, s, mw), F32),
        scratch_shapes=[pltpu.VMEM((nb, LANES), F32), pltpu.VMEM((2, nb, MOBA_BLOCK), F32)],
        compiler_params=_params("arbitrary", "arbitrary", "arbitrary"),
        name="moba",
    )(qt, mk.reshape(b, s, mw), vt)


def _gdn_gate_kernel(ba_ref, alog_ref, dtb_ref, bg_ref, *, seq):
    x = ba_ref[0]
    lane = lax.broadcasted_iota(jnp.int32, (1, LANES), 1)
    xa = x + dtb_ref[...]
    softplus = jnp.maximum(xa, 0.0) + jnp.log(1.0 + jnp.exp(-jnp.abs(xa)))
    g = jnp.where((lane >= GDN_HEADS) & (lane < 2 * GDN_HEADS), -jnp.exp(alog_ref[...]) * softplus, 0.0)
    pos = lax.broadcasted_iota(jnp.int32, (seq, 1), 0) % GDN_CHUNK
    sft = 1
    while sft < GDN_CHUNK:
        g = g + jnp.where(pos >= sft, pltpu.roll(g, sft, 0), 0.0)
        sft *= 2
    bg_ref[0] = jnp.where(lane < GDN_HEADS, _sigmoid(x), g)


def _gdn_prep_kernel(x_ref, w_ref, o_ref, *, seq):
    cb = pl.program_id(1)
    row = lax.broadcasted_iota(jnp.int32, (seq, 1), 0)
    x = x_ref[0]
    y = w_ref[GDN_CONV - 1:GDN_CONV, :] * x
    for sft in range(1, GDN_CONV):
        y = y + w_ref[GDN_CONV - 1 - sft:GDN_CONV - sft, :] * jnp.where(row >= sft, pltpu.roll(x, sft, 0), 0.0)
    y = y * _sigmoid(y)

    @pl.when(cb < 2 * GDN_HEADS)
    def _():
        scale = jnp.where(cb < GDN_HEADS, GDN_HEAD_DIM ** -0.5, 1.0)
        o_ref[0] = (y * (lax.rsqrt(jnp.sum(y * y, axis=-1, keepdims=True) + RMS_EPS) * scale)).astype(BF16)

    @pl.when(cb >= 2 * GDN_HEADS)
    def _():
        o_ref[0] = y.astype(BF16)


def _unit_lower_inverses(mats):
    c = mats[0].shape[0]
    r = lax.broadcasted_iota(jnp.int32, (c, c), 0)
    cc = lax.broadcasted_iota(jnp.int32, (c, c), 1)
    eye = jnp.where(r == cc, 1.0, 0.0)
    pair = (r // 2) == (cc // 2)
    invs = [eye - jnp.where(pair, a, 0.0) for a in mats]
    size = 4
    while size <= c:
        level = ((r // size) == (cc // size)) & ((r // (size // 2)) != (cc // (size // 2)))
        inv_b = [inv.astype(BF16) for inv in invs]
        left = [_dot(ib, jnp.where(level, a, 0.0).astype(BF16)).astype(BF16) for ib, a in zip(inv_b, mats)]
        invs = [inv - _dot(lf, ib) for inv, lf, ib in zip(invs, left, inv_b)]
        size *= 2
    return invs


def _gdn_intra_kernel(q_ref, k_ref, v_ref, bg_ref, u_ref, w_ref, qd_ref, kd_ref, qk_ref):
    h = pl.program_id(1)
    c = GDN_CHUNK
    chunks = range(GDN_GROUP)
    lane = lax.broadcasted_iota(jnp.int32, (1, LANES), 1)
    r_ix = lax.broadcasted_iota(jnp.int32, (c, c), 0)
    c_ix = lax.broadcasted_iota(jnp.int32, (c, c), 1)
    rows = [slice(gi * c, (gi + 1) * c) for gi in chunks]
    k_b16 = [k_ref[0, rw, :] for rw in rows]
    q_b16 = [q_ref[0, rw, :] for rw in rows]
    bgs = [bg_ref[0, rw, :] for rw in rows]
    beta = [jnp.sum(jnp.where(lane == h, bg, 0.0), axis=1, keepdims=True) for bg in bgs]
    gam = [jnp.sum(jnp.where(lane == GDN_HEADS + h, bg, 0.0), axis=1, keepdims=True) for bg in bgs]
    kb = [kk.astype(F32) * bt for kk, bt in zip(k_b16, beta)]
    kk_raw = [_dot_nt(x.astype(BF16), kk) for x, kk in zip(kb, k_b16)]
    qk_raw = [_dot_nt(qq, kk) for qq, kk in zip(q_b16, k_b16)]
    decay = []
    for gm in gam:
        gam_r = jnp.sum(jnp.where(r_ix == c_ix, gm, 0.0), axis=0, keepdims=True)
        decay.append(jnp.exp(jnp.where(c_ix <= r_ix, gm - gam_r, NEG_INF)))
    t_inv = _unit_lower_inverses([jnp.where(c_ix < r_ix, x * dc, 0.0) for x, dc in zip(kk_raw, decay)])
    eg = [jnp.exp(gm) for gm in gam]
    rhs = [jnp.concatenate([v_ref[0, rw, :].astype(F32) * bt, x * e], axis=1).astype(BF16)
           for rw, bt, x, e in zip(rows, beta, kb, eg)]
    uw = [_dot(ti.astype(BF16), rh) for ti, rh in zip(t_inv, rhs)]
    for gi in chunks:
        rw = rows[gi]
        u_ref[0, rw, :] = uw[gi][:, :LANES].astype(BF16)
        w_ref[0, rw, :] = uw[gi][:, LANES:].astype(BF16)
        qk_ref[0, 0, rw, :] = (qk_raw[gi] * decay[gi]).astype(BF16)
        qd_ref[0, rw, :] = (q_b16[gi].astype(F32) * eg[gi]).astype(BF16)
        kd_ref[0, rw, :] = (k_b16[gi].astype(F32) * jnp.exp(gam[gi][c - 1:c, :] - gam[gi])).astype(BF16)


def _gdn_scan_kernel(u_ref, w_ref, qd_ref, kd_ref, qk_ref, z_ref, bg_ref, ng_ref, o_ref, *, seq):
    c = GDN_CHUNK
    ng = ng_ref[...]
    heads = range(GDN_HEADS)
    cols = [slice(h * GDN_HEAD_DIM, (h + 1) * GDN_HEAD_DIM) for h in heads]

    def step(n, states):
        r0 = pl.multiple_of(n * c, c)
        rows = pl.ds(r0, c)
        bg_tail = bg_ref[0, pl.ds(r0 + c - 8, 8), :]
        s_b = [st.astype(BF16) for st in states]
        ws = [_dot(w_ref[0, rows, cols[h]], s_b[h]) for h in heads]
        qs = [_dot(qd_ref[0, rows, cols[h]], s_b[h]) for h in heads]
        v_b = [(u_ref[0, rows, cols[h]].astype(F32) - ws[h]).astype(BF16) for h in heads]
        kd_v = [lax.dot_general(kd_ref[0, rows, cols[h]], v_b[h], (((0,), (0,)), ((), ())),
                                preferred_element_type=F32) for h in heads]
        qkv = [_dot(qk_ref[0, h, rows, :], v_b[h]) for h in heads]
        new_states = []
        for h in heads:
            g_last = bg_tail[7:8, GDN_HEADS + h:GDN_HEADS + h + 1]
            new_states.append(states[h] * jnp.exp(g_last) + kd_v[h])
            z = z_ref[0, rows, cols[h]]
            o_ref[0, rows, cols[h]] = _rms(qs[h] + qkv[h], ng) * (z * _sigmoid(z))
        return tuple(new_states)

    zero = jnp.zeros((GDN_HEAD_DIM, GDN_HEAD_DIM), F32)
    lax.fori_loop(0, seq // c, step, (zero,) * GDN_HEADS)


def _gdn(gqkv, gz, gba, conv_w, a_log, dt_bias, norm_g):
    b, s, w3 = gqkv.shape
    nh = GDN_HEADS
    hw = nh * GDN_HEAD_DIM
    c = GDN_CHUNK
    lane_pad = lambda v: jnp.pad(v.reshape(1, -1), ((0, 0), (nh, LANES - 2 * nh)))
    seq_blk = pl.BlockSpec((1, s, LANES), lambda bi: (bi, 0, 0))
    const = pl.BlockSpec((1, LANES), lambda bi: (0, 0))
    bg = pl.pallas_call(
        functools.partial(_gdn_gate_kernel, seq=s),
        grid=(b,),
        in_specs=[seq_blk, const, const],
        out_specs=seq_blk,
        out_shape=jax.ShapeDtypeStruct((b, s, LANES), F32),
        compiler_params=_params("arbitrary"),
        name="gdn_gate",
    )(gba, lane_pad(a_log), lane_pad(dt_bias))

    qkv = pl.pallas_call(
        functools.partial(_gdn_prep_kernel, seq=s),
        grid=(b, w3 // LANES),
        in_specs=[pl.BlockSpec((1, s, LANES), lambda bi, cb: (bi, 0, cb)),
                  pl.BlockSpec((GDN_CONV, LANES), lambda bi, cb: (0, cb))],
        out_specs=pl.BlockSpec((1, s, LANES), lambda bi, cb: (bi, 0, cb)),
        out_shape=jax.ShapeDtypeStruct((b, s, w3), BF16),
        compiler_params=_params("arbitrary", "arbitrary"),
        name="gdn_prep",
    )(gqkv, conv_w)

    grp = GDN_GROUP * c
    head_blk = lambda off: pl.BlockSpec((1, grp, LANES), lambda bi, h, n: (bi, n, off + h))
    head_shape = jax.ShapeDtypeStruct((b, s, hw), BF16)
    u, w, qd, kd, qk = pl.pallas_call(
        _gdn_intra_kernel,
        grid=(b, nh, s // grp),
        in_specs=[head_blk(0), head_blk(nh), head_blk(2 * nh),
                  pl.BlockSpec((1, grp, LANES), lambda bi, h, n: (bi, n, 0))],
        out_specs=[head_blk(0)] * 4 + [pl.BlockSpec((1, 1, grp, c), lambda bi, h, n: (bi, h, n, 0))],
        out_shape=[head_shape] * 4 + [jax.ShapeDtypeStruct((b, nh, s, c), BF16)],
        compiler_params=_params("arbitrary", "arbitrary", "arbitrary"),
        name="gdn_intra",
    )(qkv, qkv, qkv, bg)

    full = pl.BlockSpec((1, s, hw), lambda bi: (bi, 0, 0))
    return pl.pallas_call(
        functools.partial(_gdn_scan_kernel, seq=s),
        grid=(b,),
        in_specs=[full, full, full, full, pl.BlockSpec((1, nh, s, c), lambda bi: (bi, 0, 0, 0)), full, seq_blk,
                  const],
        out_specs=full,
        out_shape=jax.ShapeDtypeStruct((b, s, hw), F32),
        compiler_params=_params("arbitrary"),
        name="gdn_scan",
    )(u, w, qd, kd, qk, gz, bg, norm_g.reshape(1, -1))


def _mem_kv_kernel(m_ref, g_ref, w_ref, kv_ref):
    kv_ref[...] = _dot(_rms(m_ref[...], g_ref[...]).astype(BF16), w_ref[...]).astype(BF16)


def _mem_kv(mem2d, g, w_kv, rows):
    t, d = mem2d.shape
    n = w_kv.shape[1]
    return pl.pallas_call(
        _mem_kv_kernel,
        grid=(t // rows,),
        in_specs=[pl.BlockSpec((rows, d), lambda i: (i, 0)), pl.BlockSpec((1, d), lambda i: (0, 0)),
                  pl.BlockSpec((d, n), lambda i: (0, 0))],
        out_specs=pl.BlockSpec((rows, n), lambda i: (i, 0)),
        out_shape=jax.ShapeDtypeStruct((t, n), BF16),
        compiler_params=_params("arbitrary"),
        name="mem_kv",
    )(mem2d, g.reshape(1, d), w_kv.astype(BF16))


def _mid_kernel(x_ref, om_ref, og_ref, mg_ref, wout_ref, xg_ref, wq_ref, kv_ref, wo_ref, fg_ref, rw_ref, rb_ref,
                x2_ref, h3_ref, route_ref, *, mw, xw):
    mo = _rms(om_ref[...], mg_ref[...]).astype(BF16)
    x1 = x_ref[...] + _dot(mo, wout_ref[:mw, :]) + _dot(og_ref[...].astype(BF16), wout_ref[mw:, :])

    h2 = _rms(x1, xg_ref[...]).astype(BF16)
    q = (_dot(h2, wq_ref[...]) * (XATTN_HEAD_DIM ** -0.5)).astype(BF16)
    head_cols = [slice(h * XATTN_HEAD_DIM, (h + 1) * XATTN_HEAD_DIM) for h in range(xw // XATTN_HEAD_DIM)]
    scores = [_dot_nt(q[:, sl], kv_ref[:, sl]) for sl in head_cols]
    probs = [jnp.exp(s - jnp.max(s, axis=1, keepdims=True)) for s in scores]
    heads = [_dot(p.astype(BF16), kv_ref[:, xw + sl.start:xw + sl.stop]) / jnp.sum(p, axis=1, keepdims=True)
             for p, sl in zip(probs, head_cols)]
    x2 = x1 + _dot(jnp.concatenate(heads, axis=1).astype(BF16), wo_ref[...])
    x2_ref[...] = x2

    h3 = _rms(x2, fg_ref[...])
    h3_ref[...] = h3
    logits = jnp.dot(h3, rw_ref[...], preferred_element_type=F32, precision=lax.Precision.HIGHEST) + rb_ref[...]
    ne = logits.shape[1]
    col = lax.broadcasted_iota(jnp.int32, logits.shape, 1)
    lane = lax.broadcasted_iota(jnp.int32, (1, LANES), 1)
    route = jnp.zeros((logits.shape[0], LANES), F32)
    top, denom = None, None
    for kk in range(TOP_K):
        m = jnp.max(logits, axis=1, keepdims=True)
        idx = jnp.min(jnp.where(logits == m, col, ne), axis=1, keepdims=True)
        logits = jnp.where(col == idx, NEG_INF, logits)
        if kk == 0:
            top = m
        e = jnp.exp(m - top)
        denom = e if kk == 0 else denom + e
        route = jnp.where(lane == kk, e, route)
        route = jnp.where(lane == TOP_K + kk, idx.astype(F32), route)
    route_ref[...] = jnp.where(lane < TOP_K, route / denom, route)


def _mid(x2d, om, og, moba_g, w_out, xattn_g, w_q, kv, w_o, ffn_g, router_w, router_b, seq, mem_len):
    t, d = x2d.shape
    mw, gw, xw, ne = om.shape[1], og.shape[1], w_q.shape[1], router_w.shape[1]
    tiles_per_seq = seq // MID_ROWS
    row = lambda n: pl.BlockSpec((MID_ROWS, n), lambda i: (i, 0))
    const = lambda r, c: pl.BlockSpec((r, c), lambda i: (0, 0))
    return pl.pallas_call(
        functools.partial(_mid_kernel, mw=mw, xw=xw),
        grid=(t // MID_ROWS,),
        in_specs=[row(d), row(mw), row(gw), const(1, mw), const(mw + gw, d), const(1, d), const(d, xw),
                  pl.BlockSpec((mem_len, 2 * xw), lambda i: (i // tiles_per_seq, 0)),
                  const(xw, d), const(1, d), const(d, ne), const(1, ne)],
        out_specs=[row(d), row(d), row(LANES)],
        out_shape=[jax.ShapeDtypeStruct((t, d), F32), jax.ShapeDtypeStruct((t, d), F32),
                   jax.ShapeDtypeStruct((t, LANES), F32)],
        compiler_params=_params("arbitrary"),
        name="mid",
    )(x2d, om, og, moba_g.reshape(1, mw), w_out.astype(BF16), xattn_g.reshape(1, d), w_q.astype(BF16), kv,
      w_o.astype(BF16), ffn_g.reshape(1, d), router_w, router_b.reshape(1, ne))


def _moe_kernel(be_ref, bv_ref, tok_ref, tokn_ref, dst_ref, h_hbm, wgu_ref, bgu_ref, wd_ref, bd_ref, y_hbm,
                xbuf, ybuf, wgu_b16, wd_b16, sem_in, sem_out, *, dff):
    blk = pl.program_id(0)
    nv = bv_ref[blk]
    nv_next = bv_ref[blk + 1]
    slot = blk % 2
    rows = MOE_ROWS
    chunk = MOE_CHUNK
    n_chunks = rows // chunk

    def row_in(s, r, t):
        return pltpu.make_async_copy(h_hbm.at[pl.ds(t, 1), :], xbuf.at[s, pl.ds(r, 1), :], sem_in.at[s])

    def row_out(s, r, d):
        return pltpu.make_async_copy(ybuf.at[s, pl.ds(r, 1), :], y_hbm.at[pl.ds(d, 1), :], sem_out.at[s])

    def wait_gather(s):
        pltpu.make_async_copy(h_hbm.at[pl.ds(0, rows), :], xbuf.at[s], sem_in.at[s]).wait()

    def wait_scatter(s):
        pltpu.make_async_copy(ybuf.at[s], y_hbm.at[pl.ds(0, rows), :], sem_out.at[s]).wait()

    @pl.when(blk == 0)
    def _():
        ybuf[1] = jnp.zeros(ybuf.shape[1:], F32)
        spare = pltpu.make_async_copy(ybuf.at[1], y_hbm.at[pl.ds(y_hbm.shape[0] - rows, rows), :], sem_out.at[1])
        spare.start()
        spare.wait()

    @pl.when((blk == 0) & (nv > 0))
    def _():
        for r in range(rows):
            row_in(0, r, tok_ref[0, 0, r]).start()

    @pl.when(nv > 0)
    def _():
        @pl.when((blk == 0) | (be_ref[blk] != be_ref[jnp.maximum(blk - 1, 0)]))
        def _():
            wgu_b16[...] = wgu_ref[0].astype(BF16)
            wd_b16[...] = wd_ref[0].astype(BF16)

        wait_gather(slot)

        @pl.when(blk >= 2)
        def _():
            wait_scatter(slot)

        per_chunk_in = rows // (n_chunks // 2)
        for c in range(n_chunks):
            if c < n_chunks // 2:
                for r in range(c * per_chunk_in, (c + 1) * per_chunk_in):
                    row_in(1 - slot, r, tokn_ref[0, 0, r]).start()
            rw = slice(c * chunk, (c + 1) * chunk)
            x = xbuf[slot, rw, :].astype(BF16)
            gu = _dot(x, wgu_b16[...]) + bgu_ref[0]
            gate = jnp.minimum(gu[:, :dff], SWIGLU_LIMIT)
            up = jnp.clip(gu[:, dff:], -SWIGLU_LIMIT, SWIGLU_LIMIT)
            act = (up + 1.0) * gate * _sigmoid(SWIGLU_ALPHA * gate)
            ybuf[slot, rw, :] = _dot(act.astype(BF16), wd_b16[...]) + bd_ref[0]
            for r in range(c * chunk, (c + 1) * chunk):
                row_out(slot, r, dst_ref[0, 0, r]).start()

        @pl.when(nv_next == 0)
        def _():
            wait_gather(1 - slot)

            @pl.when(blk >= 1)
            def _():
                wait_scatter(1 - slot)
            wait_scatter(slot)


def _moe(h3, block_expert, block_valid, row_tok, row_dst, w_gu, b_gu, w_d, b_d):
    t, d = h3.shape
    ne, _, n2 = w_gu.shape
    dff = n2 // 2
    nblk = block_expert.shape[0]
    idx_blk = pl.BlockSpec((1, 1, MOE_ROWS), lambda i, be, bv: (i, 0, 0), memory_space=pltpu.SMEM)
    idx_next = pl.BlockSpec((1, 1, MOE_ROWS), lambda i, be, bv: (jnp.minimum(i + 1, nblk - 1), 0, 0),
                            memory_space=pltpu.SMEM)
    grid_spec = pltpu.PrefetchScalarGridSpec(
        num_scalar_prefetch=2,
        grid=(nblk,),
        in_specs=[idx_blk, idx_next, idx_blk, pl.BlockSpec(memory_space=pl.ANY),
                  pl.BlockSpec((1, d, n2), lambda i, be, bv: (be[i], 0, 0)),
                  pl.BlockSpec((1, 1, n2), lambda i, be, bv: (be[i], 0, 0)),
                  pl.BlockSpec((1, dff, d), lambda i, be, bv: (be[i], 0, 0)),
                  pl.BlockSpec((1, 1, d), lambda i, be, bv: (be[i], 0, 0))],
        out_specs=pl.BlockSpec(memory_space=pl.ANY),
        scratch_shapes=[pltpu.VMEM((2, MOE_ROWS, d), F32), pltpu.VMEM((2, MOE_ROWS, d), F32),
                        pltpu.VMEM((d, n2), BF16), pltpu.VMEM((dff, d), BF16),
                        pltpu.SemaphoreType.DMA((2,)), pltpu.SemaphoreType.DMA((2,))],
    )
    tok3 = row_tok.reshape(nblk, 1, MOE_ROWS)
    valid_ext = jnp.concatenate([block_valid, jnp.zeros((1,), jnp.int32)])
    return pl.pallas_call(
        functools.partial(_moe_kernel, dff=dff),
        grid_spec=grid_spec,
        out_shape=jax.ShapeDtypeStruct((TOP_K * t + MOE_ROWS, d), F32),
        compiler_params=_params("arbitrary"),
        name="moe",
    )(block_expert, valid_ext, tok3, tok3, row_dst.reshape(nblk, 1, MOE_ROWS), h3,
      w_gu, b_gu.reshape(ne, 1, n2), w_d, b_d.reshape(ne, 1, d))


def _route_plan(expert, t):
    n_pairs = t * TOP_K
    e_flat = expert.reshape(-1)
    order = jnp.argsort(e_flat, stable=True).astype(jnp.int32)
    experts = jnp.arange(N_EXPERTS, dtype=jnp.int32)
    counts = jnp.sum((e_flat[:, None] == experts[None, :]).astype(jnp.int32), axis=0)
    padded = (counts + MOE_ROWS - 1) // MOE_ROWS * MOE_ROWS
    pend = jnp.cumsum(padded)
    pstart = pend - padded
    gstart = jnp.cumsum(counts) - counts
    nblk = n_pairs // MOE_ROWS + N_EXPERTS
    blk_row0 = jnp.arange(nblk, dtype=jnp.int32) * MOE_ROWS
    block_expert = jnp.minimum(jnp.sum((pend[None, :] <= blk_row0[:, None]).astype(jnp.int32), axis=1), N_EXPERTS - 1)
    pick = lambda table: jnp.sum(jnp.where(block_expert[:, None] == experts[None, :], table[None, :], 0), axis=1)
    in_group = blk_row0 - pick(pstart)
    block_valid = jnp.where(blk_row0 < pend[-1], jnp.clip(pick(counts) - in_group, 0, MOE_ROWS), 0)
    first_sorted = pick(gstart) + in_group
    local = jnp.arange(MOE_ROWS, dtype=jnp.int32)
    srt = first_sorted[:, None] + local[None, :]
    pair = order[jnp.clip(srt, 0, n_pairs - 1)]
    row_tok = pair // TOP_K
    row_dst = jnp.where(local[None, :] < block_valid[:, None], (pair % TOP_K) * t + row_tok, n_pairs + local[None, :])
    return (block_expert.astype(jnp.int32), block_valid.astype(jnp.int32), row_tok.astype(jnp.int32),
            row_dst.astype(jnp.int32))


def _combine_kernel(x2_ref, route_ref, g_ref, *rest, final):
    y_refs, o_ref = rest[:TOP_K], rest[TOP_K]
    acc = x2_ref[...]
    route = route_ref[...]
    for kk in range(TOP_K):
        acc = acc + route[:, kk:kk + 1] * y_refs[kk][...]
    o_ref[...] = _rms(acc, g_ref[...]) if final else acc


def _combine(x2, route, y, g, final):
    t, d = x2.shape
    tiles = t // MID_ROWS
    slot_spec = lambda kk: pl.BlockSpec((MID_ROWS, d), lambda i: (kk * tiles + i, 0))
    return pl.pallas_call(
        functools.partial(_combine_kernel, final=final),
        grid=(tiles,),
        in_specs=[pl.BlockSpec((MID_ROWS, d), lambda i: (i, 0)), pl.BlockSpec((MID_ROWS, LANES), lambda i: (i, 0)),
                  pl.BlockSpec((1, d), lambda i: (0, 0))] + [slot_spec(kk) for kk in range(TOP_K)],
        out_specs=pl.BlockSpec((MID_ROWS, d), lambda i: (i, 0)),
        out_shape=jax.ShapeDtypeStruct((t, d), F32),
        compiler_params=_params("arbitrary"),
        name="combine",
    )(x2, route, g.reshape(1, d), *([y] * TOP_K))


def kernel(x, mem, norm_mix_g, w_in, gdn_conv_w, gdn_A_log, gdn_dt_bias, gdn_norm_g, moba_norm_g, w_out,
           norm_xattn_g, norm_mem_g, xattn_w_q, xattn_w_kv, xattn_w_o, norm_ffn_g, router_w, router_b,
           w_gate_up, b_gate_up, w_down, b_down, final_norm_g):
    b, s, d = x.shape
    t = b * s
    mem_len = mem.shape[1]
    mw = moba_norm_g.shape[1]
    gw = GDN_HEADS * GDN_HEAD_DIM
    xcur = x.reshape(t, d)
    for l in range(w_in.shape[0]):
        qt, mk, vt, gqkv, gz, gba = _in_proj(xcur, norm_mix_g[l], w_in[l], mw, gw)
        o_moba = _moba(qt, mk, vt, b, s)
        o_gdn = _gdn(gqkv.reshape(b, s, 3 * gw), gz.reshape(b, s, gw), gba.reshape(b, s, LANES),
                     gdn_conv_w[l], gdn_A_log[l], gdn_dt_bias[l], gdn_norm_g[l])
        kv = _mem_kv(mem.reshape(b * mem_len, d), norm_mem_g[l], xattn_w_kv[l], mem_len)
        x2, h3, route = _mid(xcur, o_moba.reshape(t, mw), o_gdn.reshape(t, gw), moba_norm_g[l], w_out[l],
                             norm_xattn_g[l], xattn_w_q[l], kv, xattn_w_o[l], norm_ffn_g[l], router_w[l],
                             router_b[l], s, mem_len)
        expert = route[:, TOP_K:2 * TOP_K].astype(jnp.int32)
        plan = _route_plan(expert, t)
        y = _moe(h3, *plan, w_gate_up[l], b_gate_up[l], w_down[l], b_down[l])
        xcur = _combine(x2, route, y, final_norm_g, l == w_in.shape[0] - 1)
    return xcur.reshape(b, s, d)
```

```python
import functools

import jax
import jax.numpy as jnp
from jax import lax
from jax.experimental import pallas as pl
from jax.experimental.pallas import tpu as pltpu

F32 = jnp.float32
BF16 = jnp.bfloat16

RMS_EPS = 1e-6
MOBA_HEAD_DIM = 64
MOBA_BLOCK = 256
MOBA_TOPK = 3
GDN_HEAD_DIM = 128
GDN_HEADS = 4
GDN_CONV = 4
GDN_CHUNK = 64
XATTN_HEAD_DIM = 128
N_EXPERTS = 32
TOP_K = 4
SWIGLU_LIMIT = 7.0
SWIGLU_ALPHA = 1.702

LANES = 128
VMEM_LIMIT = 56 * 1024 * 1024

IN_ROWS = 512
MID_ROWS = 256
MOE_ROWS = 512
MOE_CHUNK = 128
GDN_GROUP = 8
NEG_INF = float("-inf")


def _params(*sem):
    return pltpu.CompilerParams(dimension_semantics=sem, vmem_limit_bytes=VMEM_LIMIT)


def _rms(x, g):
    return x * lax.rsqrt(jnp.mean(x * x, axis=-1, keepdims=True) + RMS_EPS) * g


def _dot(a, b):
    return jnp.dot(a, b, preferred_element_type=F32)


def _dot_nt(a, b):
    return lax.dot_general(a, b, (((1,), (1,)), ((), ())), preferred_element_type=F32)


def _sigmoid(x):
    return 1.0 / (1.0 + jnp.exp(-x))


def _in_proj_kernel(x_ref, g_ref, w_ref, wt_ref, qt_ref, mk_ref, vt_ref, gqkv_ref, gz_ref, gba_ref, *, mw, gw):
    hn = _rms(x_ref[...], g_ref[...]).astype(BF16)
    mm = lambda lo, hi: _dot(hn, w_ref[:, lo:hi])
    qt_ref[...] = (_dot_nt(wt_ref[:mw, :], hn) * (MOBA_HEAD_DIM ** -0.5)).astype(BF16)
    vt_ref[...] = _dot_nt(wt_ref[mw:, :], hn).astype(BF16)
    mk_ref[...] = mm(0, mw).astype(BF16)
    gqkv_ref[...] = mm(mw, mw + 3 * gw)
    gz_ref[...] = mm(mw + 3 * gw, mw + 4 * gw)
    gba_ref[...] = mm(mw + 4 * gw, mw + 4 * gw + LANES)


def _in_proj(x2d, g, w_in, mw, gw):
    t, d = x2d.shape
    n_real = w_in.shape[1] - 2 * mw
    n_pad = mw + 4 * gw + LANES
    w = jnp.concatenate([w_in[:, mw:2 * mw], w_in[:, 3 * mw:]], axis=1)
    w = jnp.pad(w, ((0, 0), (0, n_pad - n_real))).astype(BF16)
    w_t = jnp.concatenate([w_in[:, :mw], w_in[:, 2 * mw:3 * mw]], axis=1).T.astype(BF16)
    row = lambda n: pl.BlockSpec((IN_ROWS, n), lambda i: (i, 0))
    feat = pl.BlockSpec((mw, IN_ROWS), lambda i: (0, i))
    feat_shape = jax.ShapeDtypeStruct((mw, t), BF16)
    return pl.pallas_call(
        functools.partial(_in_proj_kernel, mw=mw, gw=gw),
        grid=(t // IN_ROWS,),
        in_specs=[row(d), pl.BlockSpec((1, d), lambda i: (0, 0)), pl.BlockSpec((d, n_pad), lambda i: (0, 0)),
                  pl.BlockSpec((2 * mw, d), lambda i: (0, 0))],
        out_specs=[feat, row(mw), feat, row(3 * gw), row(gw), row(LANES)],
        out_shape=[feat_shape, jax.ShapeDtypeStruct((t, mw), BF16), feat_shape,
                   jax.ShapeDtypeStruct((t, 3 * gw), F32), jax.ShapeDtypeStruct((t, gw), F32),
                   jax.ShapeDtypeStruct((t, LANES), F32)],
        compiler_params=_params("arbitrary"),
        name="in_proj",
    )(x2d, g.reshape(1, d), w, w_t)


def _moba_select(g_t, i):
    nb = g_t.shape[0]
    row = lax.broadcasted_iota(jnp.int32, g_t.shape, 0)
    valid = row < i
    sel = jnp.zeros_like(g_t)
    for j in range(nb):
        gj = g_t[j:j + 1, :]
        beats = valid & ((g_t > gj) | ((g_t == gj) & (row < j)))
        rank = jnp.sum(jnp.where(beats, 1.0, 0.0), axis=0, keepdims=True)
        sel = jnp.where(row == j, jnp.where(rank < MOBA_TOPK, 1.0, 0.0), sel)
    return jnp.where(valid, sel, 0.0)


def _moba_kernel(qt_ref, k_ref, vt_ref, o_ref, kmean_ref, *, nb):
    i = pl.program_id(2)
    bs = MOBA_BLOCK

    @pl.when(i == 0)
    def _():
        for j in range(nb):
            kb = k_ref[0, j * bs:(j + 1) * bs, :].astype(F32)
            kmean_ref[j:j + 1, :] = jnp.mean(kb, axis=0, keepdims=True)

    qt = qt_ref[...]
    low = lax.broadcasted_iota(jnp.int32, (LANES, 1), 0) < MOBA_HEAD_DIM
    zero = jnp.zeros_like(qt)
    qt_heads = (jnp.where(low, qt, zero), jnp.where(low, zero, qt))
    kmean = kmean_ref[...]
    sels = [_moba_select(jnp.dot(kmean, qh.astype(F32), preferred_element_type=F32,
                                 precision=lax.Precision.HIGHEST), i) for qh in qt_heads]

    key_ix = lax.broadcasted_iota(jnp.int32, (bs, bs), 0)
    qry_ix = lax.broadcasted_iota(jnp.int32, (bs, bs), 1)
    causal_bias = jnp.where(key_ix <= qry_ix, 0.0, NEG_INF)

    def attend(width):
        keys = k_ref[0, :width * bs, :]
        scores = [_dot(keys, qh) for qh in qt_heads]
        probs, sums = [], []
        for h, s in enumerate(scores):
            blocks = []
            for j in range(width):
                past = jnp.where(sels[h][j:j + 1, :] > 0.5, 0.0, NEG_INF)
                bias = jnp.where(j < i, past, jnp.where(j == i, 0.0, NEG_INF))
                own = jnp.where(j == i, causal_bias, 0.0)
                blocks.append(s[j * bs:(j + 1) * bs, :] + bias + own)
            m = blocks[0].max(axis=0, keepdims=True)
            for blk in blocks[1:]:
                m = jnp.maximum(m, blk.max(axis=0, keepdims=True))
            ps = [jnp.exp(blk - m) for blk in blocks]
            l = ps[0].sum(axis=0, keepdims=True)
            for p in ps[1:]:
                l = l + p.sum(axis=0, keepdims=True)
            probs.append(jnp.concatenate([p.astype(BF16) for p in ps], axis=0))
            sums.append(l)
        accs = [_dot(vt_ref[:, :width * bs], p) for p in probs]
        o_ref[0] = jnp.where(low, accs[0] / sums[0], accs[1] / sums[1]).T

    half = nb // 2

    @pl.when(i < half)
    def _():
        attend(half)

    @pl.when(i >= half)
    def _():
        attend(nb)


def _moba(qt, mk, vt, b, s):
    mw = mk.shape[-1]
    nb = s // MOBA_BLOCK
    return pl.pallas_call(
        functools.partial(_moba_kernel, nb=nb),
        grid=(b, mw // LANES, nb),
        in_specs=[pl.BlockSpec((LANES, MOBA_BLOCK), lambda bi, hp, i: (hp, bi * nb + i)),
                  pl.BlockSpec((1, s, LANES), lambda bi, hp, i: (bi, 0, hp)),
                  pl.BlockSpec((LANES, s), lambda bi, hp, i: (hp, bi))],
        out_specs=pl.BlockSpec((1, MOBA_BLOCK, LANES), lambda bi, hp, i: (bi, i, hp)),
        out_shape=jax.ShapeDtypeStruct((b, s, mw), F32),
        scratch_shapes=[pltpu.VMEM((nb, LANES), F32)],
        compiler_params=_params("arbitrary", "arbitrary", "arbitrary"),
        name="moba",
    )(qt, mk.reshape(b, s, mw), vt)


def _gdn_gate_kernel(ba_ref, alog_ref, dtb_ref, bg_ref, *, seq):
    x = ba_ref[0]
    lane = lax.broadcasted_iota(jnp.int32, (1, LANES), 1)
    xa = x + dtb_ref[...]
    softplus = jnp.maximum(xa, 0.0) + jnp.log(1.0 + jnp.exp(-jnp.abs(xa)))
    g = jnp.where((lane >= GDN_HEADS) & (lane < 2 * GDN_HEADS), -jnp.exp(alog_ref[...]) * softplus, 0.0)
    pos = lax.broadcasted_iota(jnp.int32, (seq, 1), 0) % GDN_CHUNK
    sft = 1
    while sft < GDN_CHUNK:
        g = g + jnp.where(pos >= sft, pltpu.roll(g, sft, 0), 0.0)
        sft *= 2
    bg_ref[0] = jnp.where(lane < GDN_HEADS, _sigmoid(x), g)


def _gdn_prep_kernel(x_ref, w_ref, o_ref, *, seq):
    cb = pl.program_id(1)
    row = lax.broadcasted_iota(jnp.int32, (seq, 1), 0)
    x = x_ref[0]
    y = w_ref[GDN_CONV - 1:GDN_CONV, :] * x
    for sft in range(1, GDN_CONV):
        y = y + w_ref[GDN_CONV - 1 - sft:GDN_CONV - sft, :] * jnp.where(row >= sft, pltpu.roll(x, sft, 0), 0.0)
    y = y * _sigmoid(y)

    @pl.when(cb < 2 * GDN_HEADS)
    def _():
        scale = jnp.where(cb < GDN_HEADS, GDN_HEAD_DIM ** -0.5, 1.0)
        o_ref[0] = (y * (lax.rsqrt(jnp.sum(y * y, axis=-1, keepdims=True) + RMS_EPS) * scale)).astype(BF16)

    @pl.when(cb >= 2 * GDN_HEADS)
    def _():
        o_ref[0] = y.astype(BF16)


def _unit_lower_inverses(mats):
    c = mats[0].shape[0]
    r = lax.broadcasted_iota(jnp.int32, (c, c), 0)
    cc = lax.broadcasted_iota(jnp.int32, (c, c), 1)
    eye = jnp.where(r == cc, 1.0, 0.0)
    pair = (r // 2) == (cc // 2)
    invs = [eye - jnp.where(pair, a, 0.0) for a in mats]
    size = 4
    while size <= c:
        level = ((r // size) == (cc // size)) & ((r // (size // 2)) != (cc // (size // 2)))
        inv_b = [inv.astype(BF16) for inv in invs]
        left = [_dot(ib, jnp.where(level, a, 0.0).astype(BF16)).astype(BF16) for ib, a in zip(inv_b, mats)]
        invs = [inv - _dot(lf, ib) for inv, lf, ib in zip(invs, left, inv_b)]
        size *= 2
    return invs


def _gdn_intra_kernel(q_ref, k_ref, v_ref, bg_ref, u_ref, w_ref, qd_ref, kd_ref, qk_ref):
    h = pl.program_id(1)
    c = GDN_CHUNK
    chunks = range(GDN_GROUP)
    lane = lax.broadcasted_iota(jnp.int32, (1, LANES), 1)
    r_ix = lax.broadcasted_iota(jnp.int32, (c, c), 0)
    c_ix = lax.broadcasted_iota(jnp.int32, (c, c), 1)
    rows = [slice(gi * c, (gi + 1) * c) for gi in chunks]
    k_b16 = [k_ref[0, rw, :] for rw in rows]
    q_b16 = [q_ref[0, rw, :] for rw in rows]
    bgs = [bg_ref[0, rw, :] for rw in rows]
    beta = [jnp.sum(jnp.where(lane == h, bg, 0.0), axis=1, keepdims=True) for bg in bgs]
    gam = [jnp.sum(jnp.where(lane == GDN_HEADS + h, bg, 0.0), axis=1, keepdims=True) for bg in bgs]
    kb = [kk.astype(F32) * bt for kk, bt in zip(k_b16, beta)]
    kk_raw = [_dot_nt(x.astype(BF16), kk) for x, kk in zip(kb, k_b16)]
    qk_raw = [_dot_nt(qq, kk) for qq, kk in zip(q_b16, k_b16)]
    decay = []
    for gm in gam:
        gam_r = jnp.sum(jnp.where(r_ix == c_ix, gm, 0.0), axis=0, keepdims=True)
        decay.append(jnp.exp(jnp.where(c_ix <= r_ix, gm - gam_r, NEG_INF)))
    t_inv = _unit_lower_inverses([jnp.where(c_ix < r_ix, x * dc, 0.0) for x, dc in zip(kk_raw, decay)])
    eg = [jnp.exp(gm) for gm in gam]
    rhs = [jnp.concatenate([v_ref[0, rw, :].astype(F32) * bt, x * e], axis=1).astype(BF16)
           for rw, bt, x, e in zip(rows, beta, kb, eg)]
    uw = [_dot(ti.astype(BF16), rh) for ti, rh in zip(t_inv, rhs)]
    for gi in chunks:
        rw = rows[gi]
        u_ref[0, rw, :] = uw[gi][:, :LANES].astype(BF16)
        w_ref[0, rw, :] = uw[gi][:, LANES:].astype(BF16)
        qk_ref[0, 0, rw, :] = (qk_raw[gi] * decay[gi]).astype(BF16)
        qd_ref[0, rw, :] = (q_b16[gi].astype(F32) * eg[gi]).astype(BF16)
        kd_ref[0, rw, :] = (k_b16[gi].astype(F32) * jnp.exp(gam[gi][c - 1:c, :] - gam[gi])).astype(BF16)


def _gdn_scan_kernel(u_ref, w_ref, qd_ref, kd_ref, qk_ref, z_ref, bg_ref, ng_ref, o_ref, *, seq):
    c = GDN_CHUNK
    ng = ng_ref[...]
    heads = range(GDN_HEADS)
    cols = [slice(h * GDN_HEAD_DIM, (h + 1) * GDN_HEAD_DIM) for h in heads]

    def step(n, states):
        r0 = pl.multiple_of(n * c, c)
        rows = pl.ds(r0, c)
        bg_tail = bg_ref[0, pl.ds(r0 + c - 8, 8), :]
        s_b = [st.astype(BF16) for st in states]
        ws = [_dot(w_ref[0, rows, cols[h]], s_b[h]) for h in heads]
        qs = [_dot(qd_ref[0, rows, cols[h]], s_b[h]) for h in heads]
        v_b = [(u_ref[0, rows, cols[h]].astype(F32) - ws[h]).astype(BF16) for h in heads]
        kd_v = [lax.dot_general(kd_ref[0, rows, cols[h]], v_b[h], (((0,), (0,)), ((), ())),
                                preferred_element_type=F32) for h in heads]
        qkv = [_dot(qk_ref[0, h, rows, :], v_b[h]) for h in heads]
        new_states = []
        for h in heads:
            g_last = bg_tail[7:8, GDN_HEADS + h:GDN_HEADS + h + 1]
            new_states.append(states[h] * jnp.exp(g_last) + kd_v[h])
            z = z_ref[0, rows, cols[h]]
            o_ref[0, rows, cols[h]] = _rms(qs[h] + qkv[h], ng) * (z * _sigmoid(z))
        return tuple(new_states)

    zero = jnp.zeros((GDN_HEAD_DIM, GDN_HEAD_DIM), F32)
    lax.fori_loop(0, seq // c, step, (zero,) * GDN_HEADS)


def _gdn(gqkv, gz, gba, conv_w, a_log, dt_bias, norm_g):
    b, s, w3 = gqkv.shape
    nh = GDN_HEADS
    hw = nh * GDN_HEAD_DIM
    c = GDN_CHUNK
    lane_pad = lambda v: jnp.pad(v.reshape(1, -1), ((0, 0), (nh, LANES - 2 * nh)))
    seq_blk = pl.BlockSpec((1, s, LANES), lambda bi: (bi, 0, 0))
    const = pl.BlockSpec((1, LANES), lambda bi: (0, 0))
    bg = pl.pallas_call(
        functools.partial(_gdn_gate_kernel, seq=s),
        grid=(b,),
        in_specs=[seq_blk, const, const],
        out_specs=seq_blk,
        out_shape=jax.ShapeDtypeStruct((b, s, LANES), F32),
        compiler_params=_params("arbitrary"),
        name="gdn_gate",
    )(gba, lane_pad(a_log), lane_pad(dt_bias))

    qkv = pl.pallas_call(
        functools.partial(_gdn_prep_kernel, seq=s),
        grid=(b, w3 // LANES),
        in_specs=[pl.BlockSpec((1, s, LANES), lambda bi, cb: (bi, 0, cb)),
                  pl.BlockSpec((GDN_CONV, LANES), lambda bi, cb: (0, cb))],
        out_specs=pl.BlockSpec((1, s, LANES), lambda bi, cb: (bi, 0, cb)),
        out_shape=jax.ShapeDtypeStruct((b, s, w3), BF16),
        compiler_params=_params("arbitrary", "arbitrary"),
        name="gdn_prep",
    )(gqkv, conv_w)

    grp = GDN_GROUP * c
    head_blk = lambda off: pl.BlockSpec((1, grp, LANES), lambda bi, h, n: (bi, n, off + h))
    head_shape = jax.ShapeDtypeStruct((b, s, hw), BF16)
    u, w, qd, kd, qk = pl.pallas_call(
        _gdn_intra_kernel,
        grid=(b, nh, s // grp),
        in_specs=[head_blk(0), head_blk(nh), head_blk(2 * nh),
                  pl.BlockSpec((1, grp, LANES), lambda bi, h, n: (bi, n, 0))],
        out_specs=[head_blk(0)] * 4 + [pl.BlockSpec((1, 1, grp, c), lambda bi, h, n: (bi, h, n, 0))],
        out_shape=[head_shape] * 4 + [jax.ShapeDtypeStruct((b, nh, s, c), BF16)],
        compiler_params=_params("arbitrary", "arbitrary", "arbitrary"),
        name="gdn_intra",
    )(qkv, qkv, qkv, bg)

    full = pl.BlockSpec((1, s, hw), lambda bi: (bi, 0, 0))
    return pl.pallas_call(
        functools.partial(_gdn_scan_kernel, seq=s),
        grid=(b,),
        in_specs=[full, full, full, full, pl.BlockSpec((1, nh, s, c), lambda bi: (bi, 0, 0, 0)), full, seq_blk,
                  const],
        out_specs=full,
        out_shape=jax.ShapeDtypeStruct((b, s, hw), F32),
        compiler_params=_params("arbitrary"),
        name="gdn_scan",
    )(u, w, qd, kd, qk, gz, bg, norm_g.reshape(1, -1))


def _mem_kv_kernel(m_ref, g_ref, w_ref, kv_ref):
    kv_ref[...] = _dot(_rms(m_ref[...], g_ref[...]).astype(BF16), w_ref[...]).astype(BF16)


def _mem_kv(mem2d, g, w_kv, rows):
    t, d = mem2d.shape
    n = w_kv.shape[1]
    return pl.pallas_call(
        _mem_kv_kernel,
        grid=(t // rows,),
        in_specs=[pl.BlockSpec((rows, d), lambda i: (i, 0)), pl.BlockSpec((1, d), lambda i: (0, 0)),
                  pl.BlockSpec((d, n), lambda i: (0, 0))],
        out_specs=pl.BlockSpec((rows, n), lambda i: (i, 0)),
        out_shape=jax.ShapeDtypeStruct((t, n), BF16),
        compiler_params=_params("arbitrary"),
        name="mem_kv",
    )(mem2d, g.reshape(1, d), w_kv.astype(BF16))


def _mid_kernel(x_ref, om_ref, og_ref, mg_ref, wout_ref, xg_ref, wq_ref, kv_ref, wo_ref, fg_ref, rw_ref, rb_ref,
                x2_ref, h3_ref, route_ref, *, mw, xw):
    mo = _rms(om_ref[...], mg_ref[...]).astype(BF16)
    x1 = x_ref[...] + _dot(mo, wout_ref[:mw, :]) + _dot(og_ref[...].astype(BF16), wout_ref[mw:, :])

    h2 = _rms(x1, xg_ref[...]).astype(BF16)
    q = (_dot(h2, wq_ref[...]) * (XATTN_HEAD_DIM ** -0.5)).astype(BF16)
    head_cols = [slice(h * XATTN_HEAD_DIM, (h + 1) * XATTN_HEAD_DIM) for h in range(xw // XATTN_HEAD_DIM)]
    scores = [_dot_nt(q[:, sl], kv_ref[:, sl]) for sl in head_cols]
    probs = [jnp.exp(s - jnp.max(s, axis=1, keepdims=True)) for s in scores]
    heads = [_dot(p.astype(BF16), kv_ref[:, xw + sl.start:xw + sl.stop]) / jnp.sum(p, axis=1, keepdims=True)
             for p, sl in zip(probs, head_cols)]
    x2 = x1 + _dot(jnp.concatenate(heads, axis=1).astype(BF16), wo_ref[...])
    x2_ref[...] = x2

    h3 = _rms(x2, fg_ref[...])
    h3_ref[...] = h3
    logits = jnp.dot(h3, rw_ref[...], preferred_element_type=F32, precision=lax.Precision.HIGHEST) + rb_ref[...]
    ne = logits.shape[1]
    col = lax.broadcasted_iota(jnp.int32, logits.shape, 1)
    lane = lax.broadcasted_iota(jnp.int32, (1, LANES), 1)
    route = jnp.zeros((logits.shape[0], LANES), F32)
    top, denom = None, None
    for kk in range(TOP_K):
        m = jnp.max(logits, axis=1, keepdims=True)
        idx = jnp.min(jnp.where(logits == m, col, ne), axis=1, keepdims=True)
        logits = jnp.where(col == idx, NEG_INF, logits)
        if kk == 0:
            top = m
        e = jnp.exp(m - top)
        denom = e if kk == 0 else denom + e
        route = jnp.where(lane == kk, e, route)
        route = jnp.where(lane == TOP_K + kk, idx.astype(F32), route)
    route_ref[...] = jnp.where(lane < TOP_K, route / denom, route)


def _mid(x2d, om, og, moba_g, w_out, xattn_g, w_q, kv, w_o, ffn_g, router_w, router_b, seq, mem_len):
    t, d = x2d.shape
    mw, gw, xw, ne = om.shape[1], og.shape[1], w_q.shape[1], router_w.shape[1]
    tiles_per_seq = seq // MID_ROWS
    row = lambda n: pl.BlockSpec((MID_ROWS, n), lambda i: (i, 0))
    const = lambda r, c: pl.BlockSpec((r, c), lambda i: (0, 0))
    return pl.pallas_call(
        functools.partial(_mid_kernel, mw=mw, xw=xw),
        grid=(t // MID_ROWS,),
        in_specs=[row(d), row(mw), row(gw), const(1, mw), const(mw + gw, d), const(1, d), const(d, xw),
                  pl.BlockSpec((mem_len, 2 * xw), lambda i: (i // tiles_per_seq, 0)),
                  const(xw, d), const(1, d), const(d, ne), const(1, ne)],
        out_specs=[row(d), row(d), row(LANES)],
        out_shape=[jax.ShapeDtypeStruct((t, d), F32), jax.ShapeDtypeStruct((t, d), F32),
                   jax.ShapeDtypeStruct((t, LANES), F32)],
        compiler_params=_params("arbitrary"),
        name="mid",
    )(x2d, om, og, moba_g.reshape(1, mw), w_out.astype(BF16), xattn_g.reshape(1, d), w_q.astype(BF16), kv,
      w_o.astype(BF16), ffn_g.reshape(1, d), router_w, router_b.reshape(1, ne))


def _moe_kernel(be_ref, bv_ref, tok_ref, tokn_ref, dst_ref, h_hbm, wgu_ref, bgu_ref, wd_ref, bd_ref, y_hbm,
                xbuf, ybuf, wgu_b16, wd_b16, sem_in, sem_out, *, dff):
    blk = pl.program_id(0)
    nv = bv_ref[blk]
    nv_next = bv_ref[blk + 1]
    slot = blk % 2
    rows = MOE_ROWS
    chunk = MOE_CHUNK
    n_chunks = rows // chunk

    def row_in(s, r, t):
        return pltpu.make_async_copy(h_hbm.at[pl.ds(t, 1), :], xbuf.at[s, pl.ds(r, 1), :], sem_in.at[s])

    def row_out(s, r, d):
        return pltpu.make_async_copy(ybuf.at[s, pl.ds(r, 1), :], y_hbm.at[pl.ds(d, 1), :], sem_out.at[s])

    def wait_gather(s):
        pltpu.make_async_copy(h_hbm.at[pl.ds(0, rows), :], xbuf.at[s], sem_in.at[s]).wait()

    def wait_scatter(s):
        pltpu.make_async_copy(ybuf.at[s], y_hbm.at[pl.ds(0, rows), :], sem_out.at[s]).wait()

    @pl.when(blk == 0)
    def _():
        ybuf[1] = jnp.zeros(ybuf.shape[1:], F32)
        spare = pltpu.make_async_copy(ybuf.at[1], y_hbm.at[pl.ds(y_hbm.shape[0] - rows, rows), :], sem_out.at[1])
        spare.start()
        spare.wait()

    @pl.when((blk == 0) & (nv > 0))
    def _():
        for r in range(rows):
            row_in(0, r, tok_ref[0, 0, r]).start()

    @pl.when(nv > 0)
    def _():
        @pl.when((blk == 0) | (be_ref[blk] != be_ref[jnp.maximum(blk - 1, 0)]))
        def _():
            wgu_b16[...] = wgu_ref[0].astype(BF16)
            wd_b16[...] = wd_ref[0].astype(BF16)

        wait_gather(slot)

        @pl.when(blk >= 2)
        def _():
            wait_scatter(slot)

        per_chunk_in = rows // (n_chunks // 2)
        for c in range(n_chunks):
            if c < n_chunks // 2:
                for r in range(c * per_chunk_in, (c + 1) * per_chunk_in):
                    row_in(1 - slot, r, tokn_ref[0, 0, r]).start()
            rw = slice(c * chunk, (c + 1) * chunk)
            x = xbuf[slot, rw, :].astype(BF16)
            gu = _dot(x, wgu_b16[...]) + bgu_ref[0]
            gate = jnp.minimum(gu[:, :dff], SWIGLU_LIMIT)
            up = jnp.clip(gu[:, dff:], -SWIGLU_LIMIT, SWIGLU_LIMIT)
            act = (up + 1.0) * gate * _sigmoid(SWIGLU_ALPHA * gate)
            ybuf[slot, rw, :] = _dot(act.astype(BF16), wd_b16[...]) + bd_ref[0]
            for r in range(c * chunk, (c + 1) * chunk):
                row_out(slot, r, dst_ref[0, 0, r]).start()

        @pl.when(nv_next == 0)
        def _():
            wait_gather(1 - slot)

            @pl.when(blk >= 1)
            def _():
                wait_scatter(1 - slot)
            wait_scatter(slot)


def _moe(h3, block_expert, block_valid, row_tok, row_dst, w_gu, b_gu, w_d, b_d):
    t, d = h3.shape
    ne, _, n2 = w_gu.shape
    dff = n2 // 2
    nblk = block_expert.shape[0]
    idx_blk = pl.BlockSpec((1, 1, MOE_ROWS), lambda i, be, bv: (i, 0, 0), memory_space=pltpu.SMEM)
    idx_next = pl.BlockSpec((1, 1, MOE_ROWS), lambda i, be, bv: (jnp.minimum(i + 1, nblk - 1), 0, 0),
                            memory_space=pltpu.SMEM)
    grid_spec = pltpu.PrefetchScalarGridSpec(
        num_scalar_prefetch=2,
        grid=(nblk,),
        in_specs=[idx_blk, idx_next, idx_blk, pl.BlockSpec(memory_space=pl.ANY),
                  pl.BlockSpec((1, d, n2), lambda i, be, bv: (be[i], 0, 0)),
                  pl.BlockSpec((1, 1, n2), lambda i, be, bv: (be[i], 0, 0)),
                  pl.BlockSpec((1, dff, d), lambda i, be, bv: (be[i], 0, 0)),
                  pl.BlockSpec((1, 1, d), lambda i, be, bv: (be[i], 0, 0))],
        out_specs=pl.BlockSpec(memory_space=pl.ANY),
        scratch_shapes=[pltpu.VMEM((2, MOE_ROWS, d), F32), pltpu.VMEM((2, MOE_ROWS, d), F32),
                        pltpu.VMEM((d, n2), BF16), pltpu.VMEM((dff, d), BF16),
                        pltpu.SemaphoreType.DMA((2,)), pltpu.SemaphoreType.DMA((2,))],
    )
    tok3 = row_tok.reshape(nblk, 1, MOE_ROWS)
    valid_ext = jnp.concatenate([block_valid, jnp.zeros((1,), jnp.int32)])
    return pl.pallas_call(
        functools.partial(_moe_kernel, dff=dff),
        grid_spec=grid_spec,
        out_shape=jax.ShapeDtypeStruct((TOP_K * t + MOE_ROWS, d), F32),
        compiler_params=_params("arbitrary"),
        name="moe",
    )(block_expert, valid_ext, tok3, tok3, row_dst.reshape(nblk, 1, MOE_ROWS), h3,
      w_gu, b_gu.reshape(ne, 1, n2), w_d, b_d.reshape(ne, 1, d))


def _route_plan(expert, t):
    n_pairs = t * TOP_K
    e_flat = expert.reshape(-1)
    order = jnp.argsort(e_flat, stable=True).astype(jnp.int32)
    experts = jnp.arange(N_EXPERTS, dtype=jnp.int32)
    counts = jnp.sum((e_flat[:, None] == experts[None, :]).astype(jnp.int32), axis=0)
    padded = (counts + MOE_ROWS - 1) // MOE_ROWS * MOE_ROWS
    pend = jnp.cumsum(padded)
    pstart = pend - padded
    gstart = jnp.cumsum(counts) - counts
    nblk = n_pairs // MOE_ROWS + N_EXPERTS
    blk_row0 = jnp.arange(nblk, dtype=jnp.int32) * MOE_ROWS
    block_expert = jnp.minimum(jnp.sum((pend[None, :] <= blk_row0[:, None]).astype(jnp.int32), axis=1), N_EXPERTS - 1)
    pick = lambda table: jnp.sum(jnp.where(block_expert[:, None] == experts[None, :], table[None, :], 0), axis=1)
    in_group = blk_row0 - pick(pstart)
    block_valid = jnp.where(blk_row0 < pend[-1], jnp.clip(pick(counts) - in_group, 0, MOE_ROWS), 0)
    first_sorted = pick(gstart) + in_group
    local = jnp.arange(MOE_ROWS, dtype=jnp.int32)
    srt = first_sorted[:, None] + local[None, :]
    pair = order[jnp.clip(srt, 0, n_pairs - 1)]
    row_tok = pair // TOP_K
    row_dst = jnp.where(local[None, :] < block_valid[:, None], (pair % TOP_K) * t + row_tok, n_pairs + local[None, :])
    return (block_expert.astype(jnp.int32), block_valid.astype(jnp.int32), row_tok.astype(jnp.int32),
            row_dst.astype(jnp.int32))


def _combine_kernel(x2_ref, route_ref, g_ref, *rest, final):
    y_refs, o_ref = rest[:TOP_K], rest[TOP_K]
    acc = x2_ref[...]
    route = route_ref[...]
    for kk in range(TOP_K):
        acc = acc + route[:, kk:kk + 1] * y_refs[kk][...]
    o_ref[...] = _rms(acc, g_ref[...]) if final else acc


def _combine(x2, route, y, g, final):
    t, d = x2.shape
    tiles = t // MID_ROWS
    slot_spec = lambda kk: pl.BlockSpec((MID_ROWS, d), lambda i: (kk * tiles + i, 0))
    return pl.pallas_call(
        functools.partial(_combine_kernel, final=final),
        grid=(tiles,),
        in_specs=[pl.BlockSpec((MID_ROWS, d), lambda i: (i, 0)), pl.BlockSpec((MID_ROWS, LANES), lambda i: (i, 0)),
                  pl.BlockSpec((1, d), lambda i: (0, 0))] + [slot_spec(kk) for kk in range(TOP_K)],
        out_specs=pl.BlockSpec((MID_ROWS, d), lambda i: (i, 0)),
        out_shape=jax.ShapeDtypeStruct((t, d), F32),
        compiler_params=_params("arbitrary"),
        name="combine",
    )(x2, route, g.reshape(1, d), *([y] * TOP_K))


def kernel(x, mem, norm_mix_g, w_in, gdn_conv_w, gdn_A_log, gdn_dt_bias, gdn_norm_g, moba_norm_g, w_out,
           norm_xattn_g, norm_mem_g, xattn_w_q, xattn_w_kv, xattn_w_o, norm_ffn_g, router_w, router_b,
           w_gate_up, b_gate_up, w_down, b_down, final_norm_g):
    b, s, d = x.shape
    t = b * s
    mem_len = mem.shape[1]
    mw = moba_norm_g.shape[1]
    gw = GDN_HEADS * GDN_HEAD_DIM
    xcur = x.reshape(t, d)
    for l in range(w_in.shape[0]):
        qt, mk, vt, gqkv, gz, gba = _in_proj(xcur, norm_mix_g[l], w_in[l], mw, gw)
        o_moba = _moba(qt, mk, vt, b, s)
        o_gdn = _gdn(gqkv.reshape(b, s, 3 * gw), gz.reshape(b, s, gw), gba.reshape(b, s, LANES),
                     gdn_conv_w[l], gdn_A_log[l], gdn_dt_bias[l], gdn_norm_g[l])
        kv = _mem_kv(mem.reshape(b * mem_len, d), norm_mem_g[l], xattn_w_kv[l], mem_len)
        x2, h3, route = _mid(xcur, o_moba.reshape(t, mw), o_gdn.reshape(t, gw), moba_norm_g[l], w_out[l],
                             norm_xattn_g[l], xattn_w_q[l], kv, xattn_w_o[l], norm_ffn_g[l], router_w[l],
                             router_b[l], s, mem_len)
        expert = route[:, TOP_K:2 * TOP_K].astype(jnp.int32)
        plan = _route_plan(expert, t)
        y = _moe(h3, *plan, w_gate_up[l], b_gate_up[l], w_down[l], b_down[l])
        xcur = _combine(x2, route, y, final_norm_g, l == w_in.shape[0] - 1)
    return xcur.reshape(b, s, d)
```

```python
import functools

import jax
import jax.numpy as jnp
from jax import lax
from jax.experimental import pallas as pl
from jax.experimental.pallas import tpu as pltpu

F32 = jnp.float32
BF16 = jnp.bfloat16

RMS_EPS = 1e-6
MOBA_HEAD_DIM = 64
MOBA_BLOCK = 256
MOBA_TOPK = 3
GDN_HEAD_DIM = 128
GDN_HEADS = 4
GDN_CONV = 4
GDN_CHUNK = 64
XATTN_HEAD_DIM = 128
N_EXPERTS = 32
TOP_K = 4
SWIGLU_LIMIT = 7.0
SWIGLU_ALPHA = 1.702

LANES = 128
ROW_TILE = 8
VMEM_LIMIT = 56 * 1024 * 1024

IN_ROWS = 512
MID_ROWS = 256
MOE_ROWS = 512
MOE_CHUNK = 128
GDN_GROUP = 8
PREP_ROWS = 256
NEG_INF = float("-inf")
MOBA_Q_SCALE = 1.4426950408889634 / MOBA_HEAD_DIM ** 0.5
MOBA_MASKED = -1e30


def _params(*sem):
    return pltpu.CompilerParams(dimension_semantics=sem, vmem_limit_bytes=VMEM_LIMIT)


def _rms(x, g):
    return x * lax.rsqrt(jnp.mean(x * x, axis=-1, keepdims=True) + RMS_EPS) * g


def _dot(a, b):
    return jnp.dot(a, b, preferred_element_type=F32)


def _dot_nt(a, b):
    return lax.dot_general(a, b, (((1,), (1,)), ((), ())), preferred_element_type=F32)


def _sigmoid(x):
    return 1.0 / (1.0 + jnp.exp(-x))


def _in_proj_kernel(x_ref, g_ref, w_ref, wt_ref, qt_ref, mk_ref, mv_ref, gqkv_ref, gz_ref, gba_ref, *, mw, gw):
    hn = _rms(x_ref[...], g_ref[...]).astype(BF16)
    mm = lambda lo, hi: _dot(hn, w_ref[:, lo:hi])
    qt_ref[...] = (_dot_nt(wt_ref[...], hn) * MOBA_Q_SCALE).astype(BF16)
    mk_ref[...] = mm(0, mw).astype(BF16)
    mv_ref[...] = mm(mw, 2 * mw).astype(BF16)
    gqkv_ref[...] = mm(2 * mw, 2 * mw + 3 * gw)
    gz_ref[...] = mm(2 * mw + 3 * gw, 2 * mw + 4 * gw)
    gba_ref[...] = mm(2 * mw + 4 * gw, 2 * mw + 4 * gw + LANES)


def _in_proj(x2d, g, w_in, mw, gw):
    t, d = x2d.shape
    n_real = w_in.shape[1] - mw
    n_pad = 2 * mw + 4 * gw + LANES
    w = jnp.pad(w_in[:, mw:], ((0, 0), (0, n_pad - n_real))).astype(BF16)
    w_t = w_in[:, :mw].T.astype(BF16)
    row = lambda n: pl.BlockSpec((IN_ROWS, n), lambda i: (i, 0))
    return pl.pallas_call(
        functools.partial(_in_proj_kernel, mw=mw, gw=gw),
        grid=(t // IN_ROWS,),
        in_specs=[row(d), pl.BlockSpec((1, d), lambda i: (0, 0)), pl.BlockSpec((d, n_pad), lambda i: (0, 0)),
                  pl.BlockSpec((mw, d), lambda i: (0, 0))],
        out_specs=[pl.BlockSpec((mw, IN_ROWS), lambda i: (0, i)), row(mw), row(mw), row(3 * gw), row(gw),
                   row(LANES)],
        out_shape=[jax.ShapeDtypeStruct((mw, t), BF16), jax.ShapeDtypeStruct((t, mw), BF16),
                   jax.ShapeDtypeStruct((t, mw), BF16),
                   jax.ShapeDtypeStruct((t, 3 * gw), F32), jax.ShapeDtypeStruct((t, gw), F32),
                   jax.ShapeDtypeStruct((t, LANES), F32)],
        compiler_params=_params("arbitrary"),
        name="in_proj",
    )(x2d, g.reshape(1, d), w, w_t)


def _moba_select(g_t, i):
    nb = g_t.shape[0]
    row = lax.broadcasted_iota(jnp.int32, g_t.shape, 0)
    valid = row < i
    sel = jnp.zeros_like(g_t)
    for j in range(nb):
        gj = g_t[j:j + 1, :]
        beats = valid & ((g_t > gj) | ((g_t == gj) & (row < j)))
        rank = jnp.sum(jnp.where(beats, 1.0, 0.0), axis=0, keepdims=True)
        sel = jnp.where(row == j, jnp.where(rank < MOBA_TOPK, 1.0, 0.0), sel)
    return jnp.where(valid, sel, 0.0)


def _dot_tn(a, b):
    return lax.dot_general(a, b, (((0,), (0,)), ((), ())), preferred_element_type=F32)


def _moba_kernel(qt_ref, k_ref, v_ref, o_ref, kmean_ref, kaug_ref, *, nb):
    i = pl.program_id(2)
    bs = MOBA_BLOCK
    hd = MOBA_HEAD_DIM
    lane = lax.broadcasted_iota(jnp.int32, (1, LANES), 1)

    @pl.when(i == 0)
    def _():
        for j in range(nb):
            rows = slice(j * bs, (j + 1) * bs)
            kb = k_ref[0, rows, :]
            kmean_ref[j:j + 1, :] = jnp.mean(kb.astype(F32), axis=0, keepdims=True)
            kaug_ref[0, rows, :] = jnp.where(lane < hd, kb, jnp.where(lane == hd + j, 1.0, 0.0).astype(BF16))
            kaug_ref[1, rows, :] = jnp.where(lane >= hd, kb, jnp.where(lane == j, 1.0, 0.0).astype(BF16))

    qt = qt_ref[...]
    qtf = qt.astype(F32)
    low = lax.broadcasted_iota(jnp.int32, (LANES, 1), 0) < hd
    kmean = kmean_ref[...]
    gate = [jnp.dot(kmean, qh, preferred_element_type=F32, precision=lax.Precision.HIGHEST)
            for qh in (jnp.where(low, qtf, 0.0), jnp.where(low, 0.0, qtf))]
    bias = [jnp.where(_moba_select(g, i) > 0.5, 0.0, MOBA_MASKED) for g in gate]
    pad = jnp.zeros((hd - nb, bs), F32)
    q_past = (jnp.concatenate([qtf[:hd], bias[0], pad], axis=0).astype(BF16),
              jnp.concatenate([bias[1], pad, qtf[hd:]], axis=0).astype(BF16))
    zero = jnp.zeros_like(qt)
    q_own = (jnp.where(low, qt, zero), jnp.where(low, zero, qt))

    key_ix = lax.broadcasted_iota(jnp.int32, (bs, bs), 0)
    qry_ix = lax.broadcasted_iota(jnp.int32, (bs, bs), 1)
    causal_bias = jnp.where(key_ix <= qry_ix, 0.0, NEG_INF)
    own = pl.ds(pl.multiple_of(i * bs, bs), bs)
    k_own = k_ref[0, own, :]
    v_own = v_ref[0, own, :]

    def attend(width):
        heads = (0, 1)
        s_past = [_dot(kaug_ref[h, :width * bs, :], q_past[h]) for h in heads]
        s_own = [_dot(k_own, q_own[h]) + causal_bias for h in heads]
        m = [jnp.maximum(s_past[h].max(axis=0, keepdims=True), s_own[h].max(axis=0, keepdims=True)) for h in heads]
        p_past = [jnp.exp2(s_past[h] - m[h]) for h in heads]
        p_own = [jnp.exp2(s_own[h] - m[h]) for h in heads]
        l = [p_past[h].sum(axis=0, keepdims=True) + p_own[h].sum(axis=0, keepdims=True) for h in heads]
        acc = [_dot_tn(v_ref[0, :width * bs, :], p_past[h].astype(BF16)) + _dot_tn(v_own, p_own[h].astype(BF16))
               for h in heads]
        o_ref[0] = jnp.where(low, acc[0] / l[0], acc[1] / l[1]).T

    half = nb // 2

    @pl.when(i < half)
    def _():
        attend(half)

    @pl.when(i >= half)
    def _():
        attend(nb)


def _moba(qt, mk, mv, b, s):
    mw = mk.shape[-1]
    nb = s // MOBA_BLOCK
    seq_blk = pl.BlockSpec((1, s, LANES), lambda bi, hp, i: (bi, 0, hp))
    return pl.pallas_call(
        functools.partial(_moba_kernel, nb=nb),
        grid=(b, mw // LANES, nb),
        in_specs=[pl.BlockSpec((LANES, MOBA_BLOCK), lambda bi, hp, i: (hp, bi * nb + i)), seq_blk, seq_blk],
        out_specs=pl.BlockSpec((1, MOBA_BLOCK, LANES), lambda bi, hp, i: (bi, i, hp)),
        out_shape=jax.ShapeDtypeStruct((b, s, mw), F32),
        scratch_shapes=[pltpu.VMEM((nb, LANES), F32), pltpu.VMEM((2, s, LANES), BF16)],
        compiler_params=_params("arbitrary", "arbitrary", "arbitrary"),
        name="moba",
    )(qt, mk.reshape(b, s, mw), mv.reshape(b, s, mw))


def _gdn_gate_kernel(ba_ref, alog_ref, dtb_ref, bg_ref, *, seq):
    x = ba_ref[0]
    lane = lax.broadcasted_iota(jnp.int32, (1, LANES), 1)
    xa = x + dtb_ref[...]
    softplus = jnp.maximum(xa, 0.0) + jnp.log(1.0 + jnp.exp(-jnp.abs(xa)))
    g = jnp.where((lane >= GDN_HEADS) & (lane < 2 * GDN_HEADS), -jnp.exp(alog_ref[...]) * softplus, 0.0)
    pos = lax.broadcasted_iota(jnp.int32, (seq, 1), 0) % GDN_CHUNK
    sft = 1
    while sft < GDN_CHUNK:
        g = g + jnp.where(pos >= sft, pltpu.roll(g, sft, 0), 0.0)
        sft *= 2
    bg_ref[0] = jnp.where(lane < GDN_HEADS, _sigmoid(x), g)


def _gdn_prep_kernel(x_ref, w_ref, o_ref, *, seq):
    cb = pl.program_id(1)
    is_qk = cb < 2 * GDN_HEADS
    scale = jnp.where(cb < GDN_HEADS, GDN_HEAD_DIM ** -0.5, 1.0)
    taps = [w_ref[j:j + 1, :] for j in range(GDN_CONV)]
    halo = 8
    for r0 in range(0, seq, PREP_ROWS):
        if r0 == 0:
            xe = jnp.concatenate([jnp.zeros((halo, LANES), F32), x_ref[0, :PREP_ROWS, :]], axis=0)
        else:
            xe = x_ref[0, r0 - halo:r0 + PREP_ROWS, :]
        y = taps[GDN_CONV - 1] * xe[halo:, :]
        for sft in range(1, GDN_CONV):
            y = y + taps[GDN_CONV - 1 - sft] * xe[halo - sft:halo - sft + PREP_ROWS, :]
        y = y * _sigmoid(y)
        normed = y * (lax.rsqrt(jnp.sum(y * y, axis=-1, keepdims=True) + RMS_EPS) * scale)
        o_ref[0, r0:r0 + PREP_ROWS, :] = jnp.where(is_qk, normed, y).astype(BF16)


def _unit_lower_inverses(mats):
    c = mats[0].shape[0]
    r = lax.broadcasted_iota(jnp.int32, (c, c), 0)
    cc = lax.broadcasted_iota(jnp.int32, (c, c), 1)
    eye = jnp.where(r == cc, 1.0, 0.0)
    pair = (r // 2) == (cc // 2)
    invs = [eye - jnp.where(pair, a, 0.0) for a in mats]
    size = 4
    while size <= c:
        level = ((r // size) == (cc // size)) & ((r // (size // 2)) != (cc // (size // 2)))
        inv_b = [inv.astype(BF16) for inv in invs]
        left = [_dot(ib, jnp.where(level, a, 0.0).astype(BF16)).astype(BF16) for ib, a in zip(inv_b, mats)]
        invs = [inv - _dot(lf, ib) for inv, lf, ib in zip(invs, left, inv_b)]
        size *= 2
    return invs


def _gdn_intra_kernel(q_ref, k_ref, v_ref, bg_ref, u_ref, w_ref, qd_ref, kd_ref, qk_ref):
    h = pl.program_id(1)
    c = GDN_CHUNK
    chunks = range(GDN_GROUP)
    lane = lax.broadcasted_iota(jnp.int32, (1, LANES), 1)
    r_ix = lax.broadcasted_iota(jnp.int32, (c, c), 0)
    c_ix = lax.broadcasted_iota(jnp.int32, (c, c), 1)
    rows = [slice(gi * c, (gi + 1) * c) for gi in chunks]
    k_b16 = [k_ref[0, rw, :] for rw in rows]
    q_b16 = [q_ref[0, rw, :] for rw in rows]
    bgs = [bg_ref[0, rw, :] for rw in rows]
    beta = [jnp.sum(jnp.where(lane == h, bg, 0.0), axis=1, keepdims=True) for bg in bgs]
    gam = [jnp.sum(jnp.where(lane == GDN_HEADS + h, bg, 0.0), axis=1, keepdims=True) for bg in bgs]
    kb = [kk.astype(F32) * bt for kk, bt in zip(k_b16, beta)]
    kk_raw = [_dot_nt(x.astype(BF16), kk) for x, kk in zip(kb, k_b16)]
    qk_raw = [_dot_nt(qq, kk) for qq, kk in zip(q_b16, k_b16)]
    decay = []
    for gm in gam:
        gam_r = jnp.sum(jnp.where(r_ix == c_ix, gm, 0.0), axis=0, keepdims=True)
        decay.append(jnp.exp(jnp.where(c_ix <= r_ix, gm - gam_r, NEG_INF)))
    t_inv = _unit_lower_inverses([jnp.where(c_ix < r_ix, x * dc, 0.0) for x, dc in zip(kk_raw, decay)])
    eg = [jnp.exp(gm) for gm in gam]
    rhs = [jnp.concatenate([v_ref[0, rw, :].astype(F32) * bt, x * e], axis=1).astype(BF16)
           for rw, bt, x, e in zip(rows, beta, kb, eg)]
    uw = [_dot(ti.astype(BF16), rh) for ti, rh in zip(t_inv, rhs)]
    for gi in chunks:
        rw = rows[gi]
        u_ref[0, rw, :] = uw[gi][:, :LANES].astype(BF16)
        w_ref[0, rw, :] = uw[gi][:, LANES:].astype(BF16)
        qk_ref[0, 0, rw, :] = (qk_raw[gi] * decay[gi]).astype(BF16)
        qd_ref[0, rw, :] = (q_b16[gi].astype(F32) * eg[gi]).astype(BF16)
        kd_ref[0, rw, :] = (k_b16[gi].astype(F32) * jnp.exp(gam[gi][c - 1:c, :] - gam[gi])).astype(BF16)


def _gdn_scan_kernel(u_ref, w_ref, qd_ref, kd_ref, qk_ref, z_ref, bg_ref, ng_ref, o_ref, *, seq):
    c = GDN_CHUNK
    ng = ng_ref[...]
    heads = range(GDN_HEADS)
    cols = [slice(h * GDN_HEAD_DIM, (h + 1) * GDN_HEAD_DIM) for h in heads]

    def step(n, states):
        r0 = pl.multiple_of(n * c, c)
        rows = pl.ds(r0, c)
        bg_tail = bg_ref[0, pl.ds(r0 + c - 8, 8), :]
        s_b = [st.astype(BF16) for st in states]
        ws = [_dot(w_ref[0, rows, cols[h]], s_b[h]) for h in heads]
        qs = [_dot(qd_ref[0, rows, cols[h]], s_b[h]) for h in heads]
        v_b = [(u_ref[0, rows, cols[h]].astype(F32) - ws[h]).astype(BF16) for h in heads]
        kd_v = [lax.dot_general(kd_ref[0, rows, cols[h]], v_b[h], (((0,), (0,)), ((), ())),
                                preferred_element_type=F32) for h in heads]
        qkv = [_dot(qk_ref[0, h, rows, :], v_b[h]) for h in heads]
        new_states = []
        for h in heads:
            g_last = bg_tail[7:8, GDN_HEADS + h:GDN_HEADS + h + 1]
            new_states.append(states[h] * jnp.exp(g_last) + kd_v[h])
            z = z_ref[0, rows, cols[h]]
            o_ref[0, rows, cols[h]] = _rms(qs[h] + qkv[h], ng) * (z * _sigmoid(z))
        return tuple(new_states)

    zero = jnp.zeros((GDN_HEAD_DIM, GDN_HEAD_DIM), F32)
    lax.fori_loop(0, seq // c, step, (zero,) * GDN_HEADS)


def _gdn(gqkv, gz, gba, conv_w, a_log, dt_bias, norm_g):
    b, s, w3 = gqkv.shape
    nh = GDN_HEADS
    hw = nh * GDN_HEAD_DIM
    c = GDN_CHUNK
    lane_pad = lambda v: jnp.pad(v.reshape(1, -1), ((0, 0), (nh, LANES - 2 * nh)))
    seq_blk = pl.BlockSpec((1, s, LANES), lambda bi: (bi, 0, 0))
    const = pl.BlockSpec((1, LANES), lambda bi: (0, 0))
    bg = pl.pallas_call(
        functools.partial(_gdn_gate_kernel, seq=s),
        grid=(b,),
        in_specs=[seq_blk, const, const],
        out_specs=seq_blk,
        out_shape=jax.ShapeDtypeStruct((b, s, LANES), F32),
        compiler_params=_params("arbitrary"),
        name="gdn_gate",
    )(gba, lane_pad(a_log), lane_pad(dt_bias))

    qkv = pl.pallas_call(
        functools.partial(_gdn_prep_kernel, seq=s),
        grid=(b, w3 // LANES),
        in_specs=[pl.BlockSpec((1, s, LANES), lambda bi, cb: (bi, 0, cb)),
                  pl.BlockSpec((GDN_CONV, LANES), lambda bi, cb: (0, cb))],
        out_specs=pl.BlockSpec((1, s, LANES), lambda bi, cb: (bi, 0, cb)),
        out_shape=jax.ShapeDtypeStruct((b, s, w3), BF16),
        compiler_params=_params("arbitrary", "arbitrary"),
        name="gdn_prep",
    )(gqkv, conv_w)

    grp = GDN_GROUP * c
    head_blk = lambda off: pl.BlockSpec((1, grp, LANES), lambda bi, h, n: (bi, n, off + h))
    head_shape = jax.ShapeDtypeStruct((b, s, hw), BF16)
    u, w, qd, kd, qk = pl.pallas_call(
        _gdn_intra_kernel,
        grid=(b, nh, s // grp),
        in_specs=[head_blk(0), head_blk(nh), head_blk(2 * nh),
                  pl.BlockSpec((1, grp, LANES), lambda bi, h, n: (bi, n, 0))],
        out_specs=[head_blk(0)] * 4 + [pl.BlockSpec((1, 1, grp, c), lambda bi, h, n: (bi, h, n, 0))],
        out_shape=[head_shape] * 4 + [jax.ShapeDtypeStruct((b, nh, s, c), BF16)],
        compiler_params=_params("arbitrary", "arbitrary", "arbitrary"),
        name="gdn_intra",
    )(qkv, qkv, qkv, bg)

    full = pl.BlockSpec((1, s, hw), lambda bi: (bi, 0, 0))
    return pl.pallas_call(
        functools.partial(_gdn_scan_kernel, seq=s),
        grid=(b,),
        in_specs=[full, full, full, full, pl.BlockSpec((1, nh, s, c), lambda bi: (bi, 0, 0, 0)), full, seq_blk,
                  const],
        out_specs=full,
        out_shape=jax.ShapeDtypeStruct((b, s, hw), F32),
        compiler_params=_params("arbitrary"),
        name="gdn_scan",
    )(u, w, qd, kd, qk, gz, bg, norm_g.reshape(1, -1))


def _mem_kv_kernel(m_ref, g_ref, w_ref, kv_ref):
    kv_ref[...] = _dot(_rms(m_ref[...], g_ref[...]).astype(BF16), w_ref[...]).astype(BF16)


def _mem_kv(mem2d, g, w_kv, rows):
    t, d = mem2d.shape
    n = w_kv.shape[1]
    return pl.pallas_call(
        _mem_kv_kernel,
        grid=(t // rows,),
        in_specs=[pl.BlockSpec((rows, d), lambda i: (i, 0)), pl.BlockSpec((1, d), lambda i: (0, 0)),
                  pl.BlockSpec((d, n), lambda i: (0, 0))],
        out_specs=pl.BlockSpec((rows, n), lambda i: (i, 0)),
        out_shape=jax.ShapeDtypeStruct((t, n), BF16),
        compiler_params=_params("arbitrary"),
        name="mem_kv",
    )(mem2d, g.reshape(1, d), w_kv.astype(BF16))


def _mid_kernel(x_ref, om_ref, og_ref, mg_ref, wout_ref, xg_ref, wq_ref, kv_ref, wo_ref, fg_ref, rw_ref, rb_ref,
                x2_ref, h3_ref, route_ref, *, mw, xw):
    mo = _rms(om_ref[...], mg_ref[...]).astype(BF16)
    x1 = x_ref[...] + _dot(mo, wout_ref[:mw, :]) + _dot(og_ref[...].astype(BF16), wout_ref[mw:, :])

    h2 = _rms(x1, xg_ref[...]).astype(BF16)
    q = (_dot(h2, wq_ref[...]) * (XATTN_HEAD_DIM ** -0.5)).astype(BF16)
    head_cols = [slice(h * XATTN_HEAD_DIM, (h + 1) * XATTN_HEAD_DIM) for h in range(xw // XATTN_HEAD_DIM)]
    scores = [_dot_nt(q[:, sl], kv_ref[:, sl]) for sl in head_cols]
    probs = [jnp.exp(s - jnp.max(s, axis=1, keepdims=True)) for s in scores]
    heads = [_dot(p.astype(BF16), kv_ref[:, xw + sl.start:xw + sl.stop]) / jnp.sum(p, axis=1, keepdims=True)
             for p, sl in zip(probs, head_cols)]
    x2 = x1 + _dot(jnp.concatenate(heads, axis=1).astype(BF16), wo_ref[...])
    x2_ref[...] = x2

    h3 = _rms(x2, fg_ref[...])
    slabs = h3.shape[1] // LANES
    for j in range(slabs):
        h3_ref[pl.ds(j, h3.shape[0], stride=slabs), :] = h3[:, j * LANES:(j + 1) * LANES]
    ne = rb_ref.shape[0]
    h_hi = h3.astype(BF16)
    h_lo = (h3 - h_hi.astype(F32)).astype(BF16)
    by_hi = _dot_nt(rw_ref[...], h_hi)
    logits = by_hi[:ne, :] + by_hi[ne:, :] + _dot_nt(rw_ref[:ne, :], h_lo) + rb_ref[...]
    row = lax.broadcasted_iota(jnp.int32, logits.shape, 0)
    weights, picks, top = [], [], None
    for kk in range(TOP_K):
        m = jnp.max(logits, axis=0, keepdims=True)
        idx = jnp.min(jnp.where(logits == m, row, ne), axis=0, keepdims=True)
        logits = jnp.where(row == idx, NEG_INF, logits)
        top = m if top is None else top
        weights.append(jnp.exp(m - top))
        picks.append(idx.astype(F32))
    denom = sum(weights[1:], weights[0])
    rows = [wk / denom for wk in weights] + picks
    rows.append(jnp.zeros((LANES - len(rows), logits.shape[1]), F32))
    route_ref[...] = jnp.concatenate(rows, axis=0).T


def _mid(x2d, om, og, moba_g, w_out, xattn_g, w_q, kv, w_o, ffn_g, router_w, router_b, seq, mem_len):
    t, d = x2d.shape
    mw, gw, xw, ne = om.shape[1], og.shape[1], w_q.shape[1], router_w.shape[1]
    tiles_per_seq = seq // MID_ROWS
    row = lambda n: pl.BlockSpec((MID_ROWS, n), lambda i: (i, 0))
    const = lambda r, c: pl.BlockSpec((r, c), lambda i: (0, 0))
    rw_hi = router_w.T.astype(BF16)
    rw_lo = (router_w.T - rw_hi.astype(F32)).astype(BF16)
    return pl.pallas_call(
        functools.partial(_mid_kernel, mw=mw, xw=xw),
        grid=(t // MID_ROWS,),
        in_specs=[row(d), row(mw), row(gw), const(1, mw), const(mw + gw, d), const(1, d), const(d, xw),
                  pl.BlockSpec((mem_len, 2 * xw), lambda i: (i // tiles_per_seq, 0)),
                  const(xw, d), const(1, d), const(2 * ne, d), const(ne, 1)],
        out_specs=[row(d), pl.BlockSpec((MID_ROWS * (d // LANES), LANES), lambda i: (i, 0)), row(LANES)],
        out_shape=[jax.ShapeDtypeStruct((t, d), F32), jax.ShapeDtypeStruct((t * (d // LANES), LANES), F32),
                   jax.ShapeDtypeStruct((t, LANES), F32)],
        compiler_params=_params("arbitrary"),
        name="mid",
    )(x2d, om, og, moba_g.reshape(1, mw), w_out.astype(BF16), xattn_g.reshape(1, d), w_q.astype(BF16), kv,
      w_o.astype(BF16), ffn_g.reshape(1, d), jnp.concatenate([rw_hi, rw_lo], axis=0), router_b.reshape(ne, 1))


def _moe_kernel(be_ref, bv_ref, tok_ref, tokn_ref, dst_ref, h_hbm, wgu_ref, bgu_ref, wd_ref, bd_ref, y_hbm,
                xbuf, ybuf, wgu_b16, wd_b16, sem_in, sem_out, *, dff):
    blk = pl.program_id(0)
    nv = bv_ref[blk]
    nv_next = bv_ref[blk + 1]
    slot = blk % 2
    rows = MOE_ROWS
    chunk = MOE_CHUNK
    n_chunks = rows // chunk
    rt = ROW_TILE

    def row_in(s, r, t):
        return pltpu.make_async_copy(h_hbm.at[pl.ds(pl.multiple_of(t, rt), rt), :],
                                     xbuf.at[s, pl.ds(r * rt, rt), :], sem_in.at[s])

    def row_out(s, r, d):
        return pltpu.make_async_copy(ybuf.at[s, pl.ds(r * rt, rt), :],
                                     y_hbm.at[pl.ds(pl.multiple_of(d, rt), rt), :], sem_out.at[s])

    def wait_gather(s):
        pltpu.make_async_copy(h_hbm.at[pl.ds(0, rows * rt), :], xbuf.at[s], sem_in.at[s]).wait()

    def wait_scatter(s):
        pltpu.make_async_copy(ybuf.at[s], y_hbm.at[pl.ds(0, rows * rt), :], sem_out.at[s]).wait()

    @pl.when(blk == 0)
    def _():
        ybuf[1] = jnp.zeros(ybuf.shape[1:], F32)
        spare = pltpu.make_async_copy(ybuf.at[1], y_hbm.at[pl.ds(y_hbm.shape[0] - rows * rt, rows * rt), :],
                                      sem_out.at[1])
        spare.start()
        spare.wait()

    @pl.when((blk == 0) & (nv > 0))
    def _():
        for r in range(rows):
            row_in(0, r, tok_ref[0, 0, r]).start()

    @pl.when(nv > 0)
    def _():
        @pl.when((blk == 0) | (be_ref[blk] != be_ref[jnp.maximum(blk - 1, 0)]))
        def _():
            wgu_b16[...] = wgu_ref[0].astype(BF16)
            wd_b16[...] = wd_ref[0].astype(BF16)

        wait_gather(slot)

        @pl.when(blk >= 2)
        def _():
            wait_scatter(slot)

        per_chunk_in = rows // (n_chunks // 2)
        for c in range(n_chunks):
            if c < n_chunks // 2:
                for r in range(c * per_chunk_in, (c + 1) * per_chunk_in):
                    row_in(1 - slot, r, tokn_ref[0, 0, r]).start()
            slab = lambda j: pl.ds(c * chunk * rt + j, chunk, stride=rt)
            x = jnp.concatenate([xbuf[slot, slab(j), :] for j in range(rt)], axis=1).astype(BF16)
            gu = _dot(x, wgu_b16[...]) + bgu_ref[0]
            gate = jnp.minimum(gu[:, :dff], SWIGLU_LIMIT)
            up = jnp.clip(gu[:, dff:], -SWIGLU_LIMIT, SWIGLU_LIMIT)
            act = (up + 1.0) * gate * _sigmoid(SWIGLU_ALPHA * gate)
            y = _dot(act.astype(BF16), wd_b16[...]) + bd_ref[0]
            for j in range(rt):
                ybuf[slot, slab(j), :] = y[:, j * LANES:(j + 1) * LANES]
            for r in range(c * chunk, (c + 1) * chunk):
                row_out(slot, r, dst_ref[0, 0, r]).start()

        @pl.when(nv_next == 0)
        def _():
            wait_gather(1 - slot)

            @pl.when(blk >= 1)
            def _():
                wait_scatter(1 - slot)
            wait_scatter(slot)


def _moe(h3_tiles, block_expert, block_valid, row_tok, row_dst, w_gu, b_gu, w_d, b_d):
    ne, d, n2 = w_gu.shape
    assert d == ROW_TILE * LANES
    t = h3_tiles.shape[0] // ROW_TILE
    dff = n2 // 2
    nblk = block_expert.shape[0]
    idx_blk = pl.BlockSpec((1, 1, MOE_ROWS), lambda i, be, bv: (i, 0, 0), memory_space=pltpu.SMEM)
    idx_next = pl.BlockSpec((1, 1, MOE_ROWS), lambda i, be, bv: (jnp.minimum(i + 1, nblk - 1), 0, 0),
                            memory_space=pltpu.SMEM)
    grid_spec = pltpu.PrefetchScalarGridSpec(
        num_scalar_prefetch=2,
        grid=(nblk,),
        in_specs=[idx_blk, idx_next, idx_blk, pl.BlockSpec(memory_space=pl.ANY),
                  pl.BlockSpec((1, d, n2), lambda i, be, bv: (be[i], 0, 0)),
                  pl.BlockSpec((1, 1, n2), lambda i, be, bv: (be[i], 0, 0)),
                  pl.BlockSpec((1, dff, d), lambda i, be, bv: (be[i], 0, 0)),
                  pl.BlockSpec((1, 1, d), lambda i, be, bv: (be[i], 0, 0))],
        out_specs=pl.BlockSpec(memory_space=pl.ANY),
        scratch_shapes=[pltpu.VMEM((2, MOE_ROWS * ROW_TILE, LANES), F32),
                        pltpu.VMEM((2, MOE_ROWS * ROW_TILE, LANES), F32),
                        pltpu.VMEM((d, n2), BF16), pltpu.VMEM((dff, d), BF16),
                        pltpu.SemaphoreType.DMA((2,)), pltpu.SemaphoreType.DMA((2,))],
    )
    tok3 = (row_tok * ROW_TILE).reshape(nblk, 1, MOE_ROWS)
    valid_ext = jnp.concatenate([block_valid, jnp.zeros((1,), jnp.int32)])
    return pl.pallas_call(
        functools.partial(_moe_kernel, dff=dff),
        grid_spec=grid_spec,
        out_shape=jax.ShapeDtypeStruct(((TOP_K * t + MOE_ROWS) * ROW_TILE, LANES), F32),
        compiler_params=_params("arbitrary"),
        name="moe",
    )(block_expert, valid_ext, tok3, tok3, (row_dst * ROW_TILE).reshape(nblk, 1, MOE_ROWS), h3_tiles,
      w_gu, b_gu.reshape(ne, 1, n2), w_d, b_d.reshape(ne, 1, d))


def _route_plan(expert, t):
    n_pairs = t * TOP_K
    e_flat = expert.reshape(-1)
    order = jnp.argsort(e_flat, stable=True).astype(jnp.int32)
    experts = jnp.arange(N_EXPERTS, dtype=jnp.int32)
    counts = jnp.sum((e_flat[:, None] == experts[None, :]).astype(jnp.int32), axis=0)
    padded = (counts + MOE_ROWS - 1) // MOE_ROWS * MOE_ROWS
    pend = jnp.cumsum(padded)
    pstart = pend - padded
    gstart = jnp.cumsum(counts) - counts
    nblk = n_pairs // MOE_ROWS + N_EXPERTS
    blk_row0 = jnp.arange(nblk, dtype=jnp.int32) * MOE_ROWS
    block_expert = jnp.minimum(jnp.sum((pend[None, :] <= blk_row0[:, None]).astype(jnp.int32), axis=1), N_EXPERTS - 1)
    pick = lambda table: jnp.sum(jnp.where(block_expert[:, None] == experts[None, :], table[None, :], 0), axis=1)
    in_group = blk_row0 - pick(pstart)
    block_valid = jnp.where(blk_row0 < pend[-1], jnp.clip(pick(counts) - in_group, 0, MOE_ROWS), 0)
    first_sorted = pick(gstart) + in_group
    local = jnp.arange(MOE_ROWS, dtype=jnp.int32)
    srt = first_sorted[:, None] + local[None, :]
    pair = order[jnp.clip(srt, 0, n_pairs - 1)]
    row_tok = pair // TOP_K
    row_dst = jnp.where(local[None, :] < block_valid[:, None], (pair % TOP_K) * t + row_tok, n_pairs + local[None, :])
    return (block_expert.astype(jnp.int32), block_valid.astype(jnp.int32), row_tok.astype(jnp.int32),
            row_dst.astype(jnp.int32))


def _combine_kernel(x2_ref, route_ref, g_ref, *rest, final):
    y_refs, o_ref = rest[:TOP_K], rest[TOP_K]
    route = route_ref[...]
    gates = [route[:, kk:kk + 1] for kk in range(TOP_K)]
    slabs = []
    for j in range(ROW_TILE):
        acc = x2_ref[:, j * LANES:(j + 1) * LANES]
        for kk in range(TOP_K):
            acc = acc + gates[kk] * y_refs[kk][pl.ds(j, x2_ref.shape[0], stride=ROW_TILE), :]
        slabs.append(acc)
    out = jnp.concatenate(slabs, axis=1)
    o_ref[...] = _rms(out, g_ref[...]) if final else out


def _combine(x2, route, y, g, final):
    t, d = x2.shape
    tiles = t // MID_ROWS
    slot_spec = lambda kk: pl.BlockSpec((MID_ROWS * ROW_TILE, LANES), lambda i: (kk * tiles + i, 0))
    return pl.pallas_call(
        functools.partial(_combine_kernel, final=final),
        grid=(tiles,),
        in_specs=[pl.BlockSpec((MID_ROWS, d), lambda i: (i, 0)), pl.BlockSpec((MID_ROWS, LANES), lambda i: (i, 0)),
                  pl.BlockSpec((1, d), lambda i: (0, 0))] + [slot_spec(kk) for kk in range(TOP_K)],
        out_specs=pl.BlockSpec((MID_ROWS, d), lambda i: (i, 0)),
        out_shape=jax.ShapeDtypeStruct((t, d), F32),
        compiler_params=_params("arbitrary"),
        name="combine",
    )(x2, route, g.reshape(1, d), *([y] * TOP_K))


def kernel(x, mem, norm_mix_g, w_in, gdn_conv_w, gdn_A_log, gdn_dt_bias, gdn_norm_g, moba_norm_g, w_out,
           norm_xattn_g, norm_mem_g, xattn_w_q, xattn_w_kv, xattn_w_o, norm_ffn_g, router_w, router_b,
           w_gate_up, b_gate_up, w_down, b_down, final_norm_g):
    b, s, d = x.shape
    t = b * s
    mem_len = mem.shape[1]
    mw = moba_norm_g.shape[1]
    gw = GDN_HEADS * GDN_HEAD_DIM
    xcur = x.reshape(t, d)
    for l in range(w_in.shape[0]):
        qt, mk, mv, gqkv, gz, gba = _in_proj(xcur, norm_mix_g[l], w_in[l], mw, gw)
        o_moba = _moba(qt, mk, mv, b, s)
        o_gdn = _gdn(gqkv.reshape(b, s, 3 * gw), gz.reshape(b, s, gw), gba.reshape(b, s, LANES),
                     gdn_conv_w[l], gdn_A_log[l], gdn_dt_bias[l], gdn_norm_g[l])
        kv = _mem_kv(mem.reshape(b * mem_len, d), norm_mem_g[l], xattn_w_kv[l], mem_len)
        x2, h3, route = _mid(xcur, o_moba.reshape(t, mw), o_gdn.reshape(t, gw), moba_norm_g[l], w_out[l],
                             norm_xattn_g[l], xattn_w_q[l], kv, xattn_w_o[l], norm_ffn_g[l], router_w[l],
                             router_b[l], s, mem_len)
        expert = route[:, TOP_K:2 * TOP_K].astype(jnp.int32)
        plan = _route_plan(expert, t)
        y = _moe(h3, *plan, w_gate_up[l], b_gate_up[l], w_down[l], b_down[l])
        xcur = _combine(x2, route, y, final_norm_g, l == w_in.shape[0] - 1)
    return xcur.reshape(b, s, d)
```

```python
import functools

import jax
import jax.numpy as jnp
from jax import lax
from jax.experimental import pallas as pl
from jax.experimental.pallas import tpu as pltpu

F32 = jnp.float32
BF16 = jnp.bfloat16

RMS_EPS = 1e-6
MOBA_HEAD_DIM = 64
MOBA_BLOCK = 256
MOBA_TOPK = 3
GDN_HEAD_DIM = 128
GDN_HEADS = 4
GDN_CONV = 4
GDN_CHUNK = 64
XATTN_HEAD_DIM = 128
N_EXPERTS = 32
TOP_K = 4
SWIGLU_LIMIT = 7.0
SWIGLU_ALPHA = 1.702

LANES = 128
ROW_TILE = 8
VMEM_LIMIT = 56 * 1024 * 1024

IN_ROWS = 512
MID_ROWS = 256
MOE_ROWS = 512
MOE_CHUNK = 128
GDN_GROUP = 8
PREP_ROWS = 256
NEG_INF = float("-inf")
MOBA_Q_SCALE = 1.4426950408889634 / MOBA_HEAD_DIM ** 0.5
MOBA_MASKED = -1e30


def _params(*sem):
    return pltpu.CompilerParams(dimension_semantics=sem, vmem_limit_bytes=VMEM_LIMIT)


def _rms(x, g):
    return x * lax.rsqrt(jnp.mean(x * x, axis=-1, keepdims=True) + RMS_EPS) * g


def _dot(a, b):
    return jnp.dot(a, b, preferred_element_type=F32)


def _dot_nt(a, b):
    return lax.dot_general(a, b, (((1,), (1,)), ((), ())), preferred_element_type=F32)


def _sigmoid(x):
    return 1.0 / (1.0 + jnp.exp(-x))


def _in_proj_kernel(x_ref, g_ref, w_ref, wt_ref, qt_ref, mk_ref, mv_ref, gqkv_ref, gz_ref, gba_ref, *, mw, gw):
    hn = _rms(x_ref[...], g_ref[...]).astype(BF16)
    mm = lambda lo, hi: _dot(hn, w_ref[:, lo:hi])
    qt_ref[...] = (_dot_nt(wt_ref[...], hn) * MOBA_Q_SCALE).astype(BF16)
    mk_ref[...] = mm(0, mw).astype(BF16)
    mv_ref[...] = mm(mw, 2 * mw).astype(BF16)
    gqkv_ref[...] = mm(2 * mw, 2 * mw + 3 * gw)
    gz_ref[...] = mm(2 * mw + 3 * gw, 2 * mw + 4 * gw)
    gba_ref[...] = mm(2 * mw + 4 * gw, 2 * mw + 4 * gw + LANES)


def _in_proj(x2d, g, w_in, mw, gw):
    t, d = x2d.shape
    n_real = w_in.shape[1] - mw
    n_pad = 2 * mw + 4 * gw + LANES
    w = jnp.pad(w_in[:, mw:], ((0, 0), (0, n_pad - n_real))).astype(BF16)
    w_t = w_in[:, :mw].T.astype(BF16)
    row = lambda n: pl.BlockSpec((IN_ROWS, n), lambda i: (i, 0))
    return pl.pallas_call(
        functools.partial(_in_proj_kernel, mw=mw, gw=gw),
        grid=(t // IN_ROWS,),
        in_specs=[row(d), pl.BlockSpec((1, d), lambda i: (0, 0)), pl.BlockSpec((d, n_pad), lambda i: (0, 0)),
                  pl.BlockSpec((mw, d), lambda i: (0, 0))],
        out_specs=[pl.BlockSpec((mw, IN_ROWS), lambda i: (0, i)), row(mw), row(mw), row(3 * gw), row(gw),
                   row(LANES)],
        out_shape=[jax.ShapeDtypeStruct((mw, t), BF16), jax.ShapeDtypeStruct((t, mw), BF16),
                   jax.ShapeDtypeStruct((t, mw), BF16),
                   jax.ShapeDtypeStruct((t, 3 * gw), F32), jax.ShapeDtypeStruct((t, gw), F32),
                   jax.ShapeDtypeStruct((t, LANES), F32)],
        compiler_params=_params("arbitrary"),
        name="in_proj",
    )(x2d, g.reshape(1, d), w, w_t)


def _moba_select(g_t, i):
    nb = g_t.shape[0]
    row = lax.broadcasted_iota(jnp.int32, g_t.shape, 0)
    valid = row < i
    sel = jnp.zeros_like(g_t)
    for j in range(nb):
        gj = g_t[j:j + 1, :]
        beats = valid & ((g_t > gj) | ((g_t == gj) & (row < j)))
        rank = jnp.sum(jnp.where(beats, 1.0, 0.0), axis=0, keepdims=True)
        sel = jnp.where(row == j, jnp.where(rank < MOBA_TOPK, 1.0, 0.0), sel)
    return jnp.where(valid, sel, 0.0)


def _dot_tn(a, b):
    return lax.dot_general(a, b, (((0,), (0,)), ((), ())), preferred_element_type=F32)


def _moba_kernel(qt_ref, k_ref, v_ref, o_ref, kmean_ref, kaug_ref, *, nb):
    i = pl.program_id(2)
    bs = MOBA_BLOCK
    hd = MOBA_HEAD_DIM
    lane = lax.broadcasted_iota(jnp.int32, (1, LANES), 1)

    @pl.when(i == 0)
    def _():
        for j in range(nb):
            rows = slice(j * bs, (j + 1) * bs)
            kb = k_ref[0, rows, :]
            kmean_ref[j:j + 1, :] = jnp.mean(kb.astype(F32), axis=0, keepdims=True)
            kaug_ref[0, rows, :] = jnp.where(lane < hd, kb, jnp.where(lane == hd + j, 1.0, 0.0).astype(BF16))
            kaug_ref[1, rows, :] = jnp.where(lane >= hd, kb, jnp.where(lane == j, 1.0, 0.0).astype(BF16))

    qt = qt_ref[...]
    qtf = qt.astype(F32)
    low = lax.broadcasted_iota(jnp.int32, (LANES, 1), 0) < hd
    kmean = kmean_ref[...]
    gate = [jnp.dot(kmean, qh, preferred_element_type=F32, precision=lax.Precision.HIGHEST)
            for qh in (jnp.where(low, qtf, 0.0), jnp.where(low, 0.0, qtf))]
    bias = [jnp.where(_moba_select(g, i) > 0.5, 0.0, MOBA_MASKED) for g in gate]
    pad = jnp.zeros((hd - nb, bs), F32)
    q_past = (jnp.concatenate([qtf[:hd], bias[0], pad], axis=0).astype(BF16),
              jnp.concatenate([bias[1], pad, qtf[hd:]], axis=0).astype(BF16))
    zero = jnp.zeros_like(qt)
    q_own = (jnp.where(low, qt, zero), jnp.where(low, zero, qt))

    key_ix = lax.broadcasted_iota(jnp.int32, (bs, bs), 0)
    qry_ix = lax.broadcasted_iota(jnp.int32, (bs, bs), 1)
    causal_bias = jnp.where(key_ix <= qry_ix, 0.0, NEG_INF)
    own = pl.ds(pl.multiple_of(i * bs, bs), bs)
    k_own = k_ref[0, own, :]
    v_own = v_ref[0, own, :]

    def attend(width):
        heads = (0, 1)
        s_own = [_dot(k_own, q_own[h]) + causal_bias for h in heads]
        m = [s.max(axis=0, keepdims=True) for s in s_own]
        if width:
            s_past = [_dot(kaug_ref[h, :width * bs, :], q_past[h]) for h in heads]
            m = [jnp.maximum(m[h], s_past[h].max(axis=0, keepdims=True)) for h in heads]
        p_own = [jnp.exp2(s_own[h] - m[h]) for h in heads]
        l = [p.sum(axis=0, keepdims=True) for p in p_own]
        acc = [_dot_tn(v_own, p_own[h].astype(BF16)) for h in heads]
        if width:
            p_past = [jnp.exp2(s_past[h] - m[h]) for h in heads]
            l = [l[h] + p_past[h].sum(axis=0, keepdims=True) for h in heads]
            acc = [acc[h] + _dot_tn(v_ref[0, :width * bs, :], p_past[h].astype(BF16)) for h in heads]
        o_ref[0] = jnp.where(low, acc[0] / l[0], acc[1] / l[1]).T

    for width in range(nb):
        pl.when(i == width)(functools.partial(attend, width))


def _moba(qt, mk, mv, b, s):
    mw = mk.shape[-1]
    nb = s // MOBA_BLOCK
    seq_blk = pl.BlockSpec((1, s, LANES), lambda bi, hp, i: (bi, 0, hp))
    return pl.pallas_call(
        functools.partial(_moba_kernel, nb=nb),
        grid=(b, mw // LANES, nb),
        in_specs=[pl.BlockSpec((LANES, MOBA_BLOCK), lambda bi, hp, i: (hp, bi * nb + i)), seq_blk, seq_blk],
        out_specs=pl.BlockSpec((1, MOBA_BLOCK, LANES), lambda bi, hp, i: (bi, i, hp)),
        out_shape=jax.ShapeDtypeStruct((b, s, mw), F32),
        scratch_shapes=[pltpu.VMEM((nb, LANES), F32), pltpu.VMEM((2, s, LANES), BF16)],
        compiler_params=_params("arbitrary", "arbitrary", "arbitrary"),
        name="moba",
    )(qt, mk.reshape(b, s, mw), mv.reshape(b, s, mw))


def _gdn_gate_kernel(ba_ref, alog_ref, dtb_ref, bg_ref, *, seq):
    x = ba_ref[0]
    lane = lax.broadcasted_iota(jnp.int32, (1, LANES), 1)
    xa = x + dtb_ref[...]
    softplus = jnp.maximum(xa, 0.0) + jnp.log(1.0 + jnp.exp(-jnp.abs(xa)))
    g = jnp.where((lane >= GDN_HEADS) & (lane < 2 * GDN_HEADS), -jnp.exp(alog_ref[...]) * softplus, 0.0)
    pos = lax.broadcasted_iota(jnp.int32, (seq, 1), 0) % GDN_CHUNK
    sft = 1
    while sft < GDN_CHUNK:
        g = g + jnp.where(pos >= sft, pltpu.roll(g, sft, 0), 0.0)
        sft *= 2
    bg_ref[0] = jnp.where(lane < GDN_HEADS, _sigmoid(x), g)


def _gdn_prep_kernel(x_ref, w_ref, o_ref, *, seq):
    cb = pl.program_id(1)
    is_qk = cb < 2 * GDN_HEADS
    scale = jnp.where(cb < GDN_HEADS, GDN_HEAD_DIM ** -0.5, 1.0)
    taps = [w_ref[j:j + 1, :] for j in range(GDN_CONV)]
    halo = 8
    for r0 in range(0, seq, PREP_ROWS):
        if r0 == 0:
            xe = jnp.concatenate([jnp.zeros((halo, LANES), F32), x_ref[0, :PREP_ROWS, :]], axis=0)
        else:
            xe = x_ref[0, r0 - halo:r0 + PREP_ROWS, :]
        y = taps[GDN_CONV - 1] * xe[halo:, :]
        for sft in range(1, GDN_CONV):
            y = y + taps[GDN_CONV - 1 - sft] * xe[halo - sft:halo - sft + PREP_ROWS, :]
        y = y * _sigmoid(y)
        normed = y * (lax.rsqrt(jnp.sum(y * y, axis=-1, keepdims=True) + RMS_EPS) * scale)
        o_ref[0, r0:r0 + PREP_ROWS, :] = jnp.where(is_qk, normed, y).astype(BF16)


def _unit_lower_inverses(mats):
    c = mats[0].shape[0]
    r = lax.broadcasted_iota(jnp.int32, (c, c), 0)
    cc = lax.broadcasted_iota(jnp.int32, (c, c), 1)
    eye = jnp.where(r == cc, 1.0, 0.0)
    pair = (r // 2) == (cc // 2)
    invs = [eye - jnp.where(pair, a, 0.0) for a in mats]
    size = 4
    while size <= c:
        level = ((r // size) == (cc // size)) & ((r // (size // 2)) != (cc // (size // 2)))
        inv_b = [inv.astype(BF16) for inv in invs]
        left = [_dot(ib, jnp.where(level, a, 0.0).astype(BF16)).astype(BF16) for ib, a in zip(inv_b, mats)]
        invs = [inv - _dot(lf, ib) for inv, lf, ib in zip(invs, left, inv_b)]
        size *= 2
    return invs


def _gdn_intra_kernel(q_ref, k_ref, v_ref, bg_ref, u_ref, w_ref, qd_ref, kd_ref, qk_ref):
    h = pl.program_id(1)
    c = GDN_CHUNK
    chunks = range(GDN_GROUP)
    lane = lax.broadcasted_iota(jnp.int32, (1, LANES), 1)
    r_ix = lax.broadcasted_iota(jnp.int32, (c, c), 0)
    c_ix = lax.broadcasted_iota(jnp.int32, (c, c), 1)
    rows = [slice(gi * c, (gi + 1) * c) for gi in chunks]
    k_b16 = [k_ref[0, rw, :] for rw in rows]
    q_b16 = [q_ref[0, rw, :] for rw in rows]
    bgs = [bg_ref[0, rw, :] for rw in rows]
    beta = [jnp.sum(jnp.where(lane == h, bg, 0.0), axis=1, keepdims=True) for bg in bgs]
    gam = [jnp.sum(jnp.where(lane == GDN_HEADS + h, bg, 0.0), axis=1, keepdims=True) for bg in bgs]
    kb = [kk.astype(F32) * bt for kk, bt in zip(k_b16, beta)]
    kk_raw = [_dot_nt(x.astype(BF16), kk) for x, kk in zip(kb, k_b16)]
    qk_raw = [_dot_nt(qq, kk) for qq, kk in zip(q_b16, k_b16)]
    decay = []
    for gm in gam:
        gam_r = jnp.sum(jnp.where(r_ix == c_ix, gm, 0.0), axis=0, keepdims=True)
        decay.append(jnp.exp(jnp.where(c_ix <= r_ix, gm - gam_r, NEG_INF)))
    t_inv = _unit_lower_inverses([jnp.where(c_ix < r_ix, x * dc, 0.0) for x, dc in zip(kk_raw, decay)])
    eg = [jnp.exp(gm) for gm in gam]
    rhs = [jnp.concatenate([v_ref[0, rw, :].astype(F32) * bt, x * e], axis=1).astype(BF16)
           for rw, bt, x, e in zip(rows, beta, kb, eg)]
    uw = [_dot(ti.astype(BF16), rh) for ti, rh in zip(t_inv, rhs)]
    for gi in chunks:
        rw = rows[gi]
        u_ref[0, rw, :] = uw[gi][:, :LANES].astype(BF16)
        w_ref[0, rw, :] = uw[gi][:, LANES:].astype(BF16)
        qk_ref[0, 0, rw, :] = (qk_raw[gi] * decay[gi]).astype(BF16)
        qd_ref[0, rw, :] = (q_b16[gi].astype(F32) * eg[gi]).astype(BF16)
        kd_ref[0, rw, :] = (k_b16[gi].astype(F32) * jnp.exp(gam[gi][c - 1:c, :] - gam[gi])).astype(BF16)


def _gdn_scan_kernel(u_ref, w_ref, qd_ref, kd_ref, qk_ref, z_ref, bg_ref, ng_ref, o_ref, *, seq):
    c = GDN_CHUNK
    ng = ng_ref[...]
    heads = range(GDN_HEADS)
    cols = [slice(h * GDN_HEAD_DIM, (h + 1) * GDN_HEAD_DIM) for h in heads]

    def step(n, states):
        r0 = pl.multiple_of(n * c, c)
        rows = pl.ds(r0, c)
        bg_tail = bg_ref[0, pl.ds(r0 + c - 8, 8), :]
        s_b = [st.astype(BF16) for st in states]
        ws = [_dot(w_ref[0, rows, cols[h]], s_b[h]) for h in heads]
        qs = [_dot(qd_ref[0, rows, cols[h]], s_b[h]) for h in heads]
        v_b = [(u_ref[0, rows, cols[h]].astype(F32) - ws[h]).astype(BF16) for h in heads]
        kd_v = [lax.dot_general(kd_ref[0, rows, cols[h]], v_b[h], (((0,), (0,)), ((), ())),
                                preferred_element_type=F32) for h in heads]
        qkv = [_dot(qk_ref[0, h, rows, :], v_b[h]) for h in heads]
        new_states = []
        for h in heads:
            g_last = bg_tail[7:8, GDN_HEADS + h:GDN_HEADS + h + 1]
            new_states.append(states[h] * jnp.exp(g_last) + kd_v[h])
            z = z_ref[0, rows, cols[h]]
            o_ref[0, rows, cols[h]] = _rms(qs[h] + qkv[h], ng) * (z * _sigmoid(z))
        return tuple(new_states)

    zero = jnp.zeros((GDN_HEAD_DIM, GDN_HEAD_DIM), F32)
    lax.fori_loop(0, seq // c, step, (zero,) * GDN_HEADS)


def _gdn(gqkv, gz, gba, conv_w, a_log, dt_bias, norm_g):
    b, s, w3 = gqkv.shape
    nh = GDN_HEADS
    hw = nh * GDN_HEAD_DIM
    c = GDN_CHUNK
    lane_pad = lambda v: jnp.pad(v.reshape(1, -1), ((0, 0), (nh, LANES - 2 * nh)))
    seq_blk = pl.BlockSpec((1, s, LANES), lambda bi: (bi, 0, 0))
    const = pl.BlockSpec((1, LANES), lambda bi: (0, 0))
    bg = pl.pallas_call(
        functools.partial(_gdn_gate_kernel, seq=s),
        grid=(b,),
        in_specs=[seq_blk, const, const],
        out_specs=seq_blk,
        out_shape=jax.ShapeDtypeStruct((b, s, LANES), F32),
        compiler_params=_params("arbitrary"),
        name="gdn_gate",
    )(gba, lane_pad(a_log), lane_pad(dt_bias))

    qkv = pl.pallas_call(
        functools.partial(_gdn_prep_kernel, seq=s),
        grid=(b, w3 // LANES),
        in_specs=[pl.BlockSpec((1, s, LANES), lambda bi, cb: (bi, 0, cb)),
                  pl.BlockSpec((GDN_CONV, LANES), lambda bi, cb: (0, cb))],
        out_specs=pl.BlockSpec((1, s, LANES), lambda bi, cb: (bi, 0, cb)),
        out_shape=jax.ShapeDtypeStruct((b, s, w3), BF16),
        compiler_params=_params("arbitrary", "arbitrary"),
        name="gdn_prep",
    )(gqkv, conv_w)

    grp = GDN_GROUP * c
    head_blk = lambda off: pl.BlockSpec((1, grp, LANES), lambda bi, h, n: (bi, n, off + h))
    head_shape = jax.ShapeDtypeStruct((b, s, hw), BF16)
    u, w, qd, kd, qk = pl.pallas_call(
        _gdn_intra_kernel,
        grid=(b, nh, s // grp),
        in_specs=[head_blk(0), head_blk(nh), head_blk(2 * nh),
                  pl.BlockSpec((1, grp, LANES), lambda bi, h, n: (bi, n, 0))],
        out_specs=[head_blk(0)] * 4 + [pl.BlockSpec((1, 1, grp, c), lambda bi, h, n: (bi, h, n, 0))],
        out_shape=[head_shape] * 4 + [jax.ShapeDtypeStruct((b, nh, s, c), BF16)],
        compiler_params=_params("arbitrary", "arbitrary", "arbitrary"),
        name="gdn_intra",
    )(qkv, qkv, qkv, bg)

    full = pl.BlockSpec((1, s, hw), lambda bi: (bi, 0, 0))
    return pl.pallas_call(
        functools.partial(_gdn_scan_kernel, seq=s),
        grid=(b,),
        in_specs=[full, full, full, full, pl.BlockSpec((1, nh, s, c), lambda bi: (bi, 0, 0, 0)), full, seq_blk,
                  const],
        out_specs=full,
        out_shape=jax.ShapeDtypeStruct((b, s, hw), F32),
        compiler_params=_params("arbitrary"),
        name="gdn_scan",
    )(u, w, qd, kd, qk, gz, bg, norm_g.reshape(1, -1))


def _mem_kv_kernel(m_ref, g_ref, w_ref, kv_ref):
    kv_ref[...] = _dot(_rms(m_ref[...], g_ref[...]).astype(BF16), w_ref[...]).astype(BF16)


def _mem_kv(mem2d, g, w_kv, rows):
    t, d = mem2d.shape
    n = w_kv.shape[1]
    return pl.pallas_call(
        _mem_kv_kernel,
        grid=(t // rows,),
        in_specs=[pl.BlockSpec((rows, d), lambda i: (i, 0)), pl.BlockSpec((1, d), lambda i: (0, 0)),
                  pl.BlockSpec((d, n), lambda i: (0, 0))],
        out_specs=pl.BlockSpec((rows, n), lambda i: (i, 0)),
        out_shape=jax.ShapeDtypeStruct((t, n), BF16),
        compiler_params=_params("arbitrary"),
        name="mem_kv",
    )(mem2d, g.reshape(1, d), w_kv.astype(BF16))


def _mid_kernel(x_ref, om_ref, og_ref, mg_ref, wout_ref, xg_ref, wq_ref, kv_ref, wo_ref, fg_ref, rw_ref, rb_ref,
                x2_ref, h3_ref, route_ref, *, mw, xw):
    mo = _rms(om_ref[...], mg_ref[...]).astype(BF16)
    x1 = x_ref[...] + _dot(mo, wout_ref[:mw, :]) + _dot(og_ref[...].astype(BF16), wout_ref[mw:, :])

    h2 = _rms(x1, xg_ref[...]).astype(BF16)
    q = (_dot(h2, wq_ref[...]) * (XATTN_HEAD_DIM ** -0.5)).astype(BF16)
    head_cols = [slice(h * XATTN_HEAD_DIM, (h + 1) * XATTN_HEAD_DIM) for h in range(xw // XATTN_HEAD_DIM)]
    scores = [_dot_nt(q[:, sl], kv_ref[:, sl]) for sl in head_cols]
    probs = [jnp.exp(s - jnp.max(s, axis=1, keepdims=True)) for s in scores]
    heads = [_dot(p.astype(BF16), kv_ref[:, xw + sl.start:xw + sl.stop]) / jnp.sum(p, axis=1, keepdims=True)
             for p, sl in zip(probs, head_cols)]
    x2 = x1 + _dot(jnp.concatenate(heads, axis=1).astype(BF16), wo_ref[...])
    x2_ref[...] = x2

    h3 = _rms(x2, fg_ref[...])
    slabs = h3.shape[1] // LANES
    for j in range(slabs):
        h3_ref[pl.ds(j, h3.shape[0], stride=slabs), :] = h3[:, j * LANES:(j + 1) * LANES]
    ne = rb_ref.shape[0]
    h_hi = h3.astype(BF16)
    h_lo = (h3 - h_hi.astype(F32)).astype(BF16)
    by_hi = _dot_nt(rw_ref[...], h_hi)
    logits = by_hi[:ne, :] + by_hi[ne:, :] + _dot_nt(rw_ref[:ne, :], h_lo) + rb_ref[...]
    row = lax.broadcasted_iota(jnp.int32, logits.shape, 0)
    weights, picks, top = [], [], None
    for kk in range(TOP_K):
        m = jnp.max(logits, axis=0, keepdims=True)
        idx = jnp.min(jnp.where(logits == m, row, ne), axis=0, keepdims=True)
        logits = jnp.where(row == idx, NEG_INF, logits)
        top = m if top is None else top
        weights.append(jnp.exp(m - top))
        picks.append(idx.astype(F32))
    denom = sum(weights[1:], weights[0])
    rows = [wk / denom for wk in weights] + picks
    rows.append(jnp.zeros((LANES - len(rows), logits.shape[1]), F32))
    route_ref[...] = jnp.concatenate(rows, axis=0).T


def _mid(x2d, om, og, moba_g, w_out, xattn_g, w_q, kv, w_o, ffn_g, router_w, router_b, seq, mem_len):
    t, d = x2d.shape
    mw, gw, xw, ne = om.shape[1], og.shape[1], w_q.shape[1], router_w.shape[1]
    tiles_per_seq = seq // MID_ROWS
    row = lambda n: pl.BlockSpec((MID_ROWS, n), lambda i: (i, 0))
    const = lambda r, c: pl.BlockSpec((r, c), lambda i: (0, 0))
    rw_hi = router_w.T.astype(BF16)
    rw_lo = (router_w.T - rw_hi.astype(F32)).astype(BF16)
    return pl.pallas_call(
        functools.partial(_mid_kernel, mw=mw, xw=xw),
        grid=(t // MID_ROWS,),
        in_specs=[row(d), row(mw), row(gw), const(1, mw), const(mw + gw, d), const(1, d), const(d, xw),
                  pl.BlockSpec((mem_len, 2 * xw), lambda i: (i // tiles_per_seq, 0)),
                  const(xw, d), const(1, d), const(2 * ne, d), const(ne, 1)],
        out_specs=[row(d), pl.BlockSpec((MID_ROWS * (d // LANES), LANES), lambda i: (i, 0)), row(LANES)],
        out_shape=[jax.ShapeDtypeStruct((t, d), F32), jax.ShapeDtypeStruct((t * (d // LANES), LANES), F32),
                   jax.ShapeDtypeStruct((t, LANES), F32)],
        compiler_params=_params("arbitrary"),
        name="mid",
    )(x2d, om, og, moba_g.reshape(1, mw), w_out.astype(BF16), xattn_g.reshape(1, d), w_q.astype(BF16), kv,
      w_o.astype(BF16), ffn_g.reshape(1, d), jnp.concatenate([rw_hi, rw_lo], axis=0), router_b.reshape(ne, 1))


def _moe_kernel(be_ref, bv_ref, tok_ref, tokn_ref, dst_ref, h_hbm, wgu_ref, bgu_ref, wd_ref, bd_ref, y_hbm,
                xbuf, ybuf, wgu_b16, wd_b16, sem_in, sem_out, *, dff):
    blk = pl.program_id(0)
    nv = bv_ref[blk]
    nv_next = bv_ref[blk + 1]
    slot = blk % 2
    rows = MOE_ROWS
    chunk = MOE_CHUNK
    n_chunks = rows // chunk
    rt = ROW_TILE

    def row_in(s, r, t):
        return pltpu.make_async_copy(h_hbm.at[pl.ds(pl.multiple_of(t, rt), rt), :],
                                     xbuf.at[s, pl.ds(r * rt, rt), :], sem_in.at[s])

    def row_out(s, r, d):
        return pltpu.make_async_copy(ybuf.at[s, pl.ds(r * rt, rt), :],
                                     y_hbm.at[pl.ds(pl.multiple_of(d, rt), rt), :], sem_out.at[s])

    def wait_gather(s):
        pltpu.make_async_copy(h_hbm.at[pl.ds(0, rows * rt), :], xbuf.at[s], sem_in.at[s]).wait()

    def wait_scatter(s):
        pltpu.make_async_copy(ybuf.at[s], y_hbm.at[pl.ds(0, rows * rt), :], sem_out.at[s]).wait()

    @pl.when(blk == 0)
    def _():
        ybuf[1] = jnp.zeros(ybuf.shape[1:], F32)
        spare = pltpu.make_async_copy(ybuf.at[1], y_hbm.at[pl.ds(y_hbm.shape[0] - rows * rt, rows * rt), :],
                                      sem_out.at[1])
        spare.start()
        spare.wait()

    @pl.when((blk == 0) & (nv > 0))
    def _():
        for r in range(rows):
            row_in(0, r, tok_ref[0, 0, r]).start(priority=r % 2)

    @pl.when(nv > 0)
    def _():
        @pl.when((blk == 0) | (be_ref[blk] != be_ref[jnp.maximum(blk - 1, 0)]))
        def _():
            wgu_b16[...] = wgu_ref[0].astype(BF16)
            wd_b16[...] = wd_ref[0].astype(BF16)

        wait_gather(slot)

        @pl.when(blk >= 2)
        def _():
            wait_scatter(slot)

        per_chunk_in = rows // (n_chunks // 2)
        for c in range(n_chunks):
            if c < n_chunks // 2:
                for r in range(c * per_chunk_in, (c + 1) * per_chunk_in):
                    row_in(1 - slot, r, tokn_ref[0, 0, r]).start(priority=r % 2)
            slab = lambda j: pl.ds(c * chunk * rt + j, chunk, stride=rt)
            x = jnp.concatenate([xbuf[slot, slab(j), :] for j in range(rt)], axis=1).astype(BF16)
            gu = _dot(x, wgu_b16[...]) + bgu_ref[0]
            gate = jnp.minimum(gu[:, :dff], SWIGLU_LIMIT)
            up = jnp.clip(gu[:, dff:], -SWIGLU_LIMIT, SWIGLU_LIMIT)
            act = (up + 1.0) * gate * _sigmoid(SWIGLU_ALPHA * gate)
            y = _dot(act.astype(BF16), wd_b16[...]) + bd_ref[0]
            for j in range(rt):
                ybuf[slot, slab(j), :] = y[:, j * LANES:(j + 1) * LANES]
            for r in range(c * chunk, (c + 1) * chunk):
                row_out(slot, r, dst_ref[0, 0, r]).start(priority=r % 2)

        @pl.when(nv_next == 0)
        def _():
            wait_gather(1 - slot)

            @pl.when(blk >= 1)
            def _():
                wait_scatter(1 - slot)
            wait_scatter(slot)


def _moe(h3_tiles, block_expert, block_valid, row_tok, row_dst, w_gu, b_gu, w_d, b_d):
    ne, d, n2 = w_gu.shape
    assert d == ROW_TILE * LANES
    t = h3_tiles.shape[0] // ROW_TILE
    dff = n2 // 2
    nblk = block_expert.shape[0]
    idx_blk = pl.BlockSpec((1, 1, MOE_ROWS), lambda i, be, bv: (i, 0, 0), memory_space=pltpu.SMEM)
    idx_next = pl.BlockSpec((1, 1, MOE_ROWS), lambda i, be, bv: (jnp.minimum(i + 1, nblk - 1), 0, 0),
                            memory_space=pltpu.SMEM)
    grid_spec = pltpu.PrefetchScalarGridSpec(
        num_scalar_prefetch=2,
        grid=(nblk,),
        in_specs=[idx_blk, idx_next, idx_blk, pl.BlockSpec(memory_space=pl.ANY),
                  pl.BlockSpec((1, d, n2), lambda i, be, bv: (be[i], 0, 0)),
                  pl.BlockSpec((1, 1, n2), lambda i, be, bv: (be[i], 0, 0)),
                  pl.BlockSpec((1, dff, d), lambda i, be, bv: (be[i], 0, 0)),
                  pl.BlockSpec((1, 1, d), lambda i, be, bv: (be[i], 0, 0))],
        out_specs=pl.BlockSpec(memory_space=pl.ANY),
        scratch_shapes=[pltpu.VMEM((2, MOE_ROWS * ROW_TILE, LANES), F32),
                        pltpu.VMEM((2, MOE_ROWS * ROW_TILE, LANES), F32),
                        pltpu.VMEM((d, n2), BF16), pltpu.VMEM((dff, d), BF16),
                        pltpu.SemaphoreType.DMA((2,)), pltpu.SemaphoreType.DMA((2,))],
    )
    tok3 = (row_tok * ROW_TILE).reshape(nblk, 1, MOE_ROWS)
    valid_ext = jnp.concatenate([block_valid, jnp.zeros((1,), jnp.int32)])
    return pl.pallas_call(
        functools.partial(_moe_kernel, dff=dff),
        grid_spec=grid_spec,
        out_shape=jax.ShapeDtypeStruct(((TOP_K * t + MOE_ROWS) * ROW_TILE, LANES), F32),
        compiler_params=_params("arbitrary"),
        name="moe",
    )(block_expert, valid_ext, tok3, tok3, (row_dst * ROW_TILE).reshape(nblk, 1, MOE_ROWS), h3_tiles,
      w_gu, b_gu.reshape(ne, 1, n2), w_d, b_d.reshape(ne, 1, d))


def _route_plan(expert, t):
    n_pairs = t * TOP_K
    e_flat = expert.reshape(-1)
    order = jnp.argsort(e_flat, stable=True).astype(jnp.int32)
    experts = jnp.arange(N_EXPERTS, dtype=jnp.int32)
    counts = jnp.sum((e_flat[:, None] == experts[None, :]).astype(jnp.int32), axis=0)
    padded = (counts + MOE_ROWS - 1) // MOE_ROWS * MOE_ROWS
    pend = jnp.cumsum(padded)
    pstart = pend - padded
    gstart = jnp.cumsum(counts) - counts
    nblk = n_pairs // MOE_ROWS + N_EXPERTS
    blk_row0 = jnp.arange(nblk, dtype=jnp.int32) * MOE_ROWS
    block_expert = jnp.minimum(jnp.sum((pend[None, :] <= blk_row0[:, None]).astype(jnp.int32), axis=1), N_EXPERTS - 1)
    pick = lambda table: jnp.sum(jnp.where(block_expert[:, None] == experts[None, :], table[None, :], 0), axis=1)
    in_group = blk_row0 - pick(pstart)
    block_valid = jnp.where(blk_row0 < pend[-1], jnp.clip(pick(counts) - in_group, 0, MOE_ROWS), 0)
    first_sorted = pick(gstart) + in_group
    local = jnp.arange(MOE_ROWS, dtype=jnp.int32)
    srt = first_sorted[:, None] + local[None, :]
    pair = order[jnp.clip(srt, 0, n_pairs - 1)]
    row_tok = pair // TOP_K
    row_dst = jnp.where(local[None, :] < block_valid[:, None], (pair % TOP_K) * t + row_tok, n_pairs + local[None, :])
    return (block_expert.astype(jnp.int32), block_valid.astype(jnp.int32), row_tok.astype(jnp.int32),
            row_dst.astype(jnp.int32))


def _combine_kernel(x2_ref, route_ref, g_ref, *rest, final):
    y_refs, o_ref = rest[:TOP_K], rest[TOP_K]
    route = route_ref[...]
    gates = [route[:, kk:kk + 1] for kk in range(TOP_K)]
    slabs = []
    for j in range(ROW_TILE):
        acc = x2_ref[:, j * LANES:(j + 1) * LANES]
        for kk in range(TOP_K):
            acc = acc + gates[kk] * y_refs[kk][pl.ds(j, x2_ref.shape[0], stride=ROW_TILE), :]
        slabs.append(acc)
    out = jnp.concatenate(slabs, axis=1)
    o_ref[...] = _rms(out, g_ref[...]) if final else out


def _combine(x2, route, y, g, final):
    t, d = x2.shape
    tiles = t // MID_ROWS
    slot_spec = lambda kk: pl.BlockSpec((MID_ROWS * ROW_TILE, LANES), lambda i: (kk * tiles + i, 0))
    return pl.pallas_call(
        functools.partial(_combine_kernel, final=final),
        grid=(tiles,),
        in_specs=[pl.BlockSpec((MID_ROWS, d), lambda i: (i, 0)), pl.BlockSpec((MID_ROWS, LANES), lambda i: (i, 0)),
                  pl.BlockSpec((1, d), lambda i: (0, 0))] + [slot_spec(kk) for kk in range(TOP_K)],
        out_specs=pl.BlockSpec((MID_ROWS, d), lambda i: (i, 0)),
        out_shape=jax.ShapeDtypeStruct((t, d), F32),
        compiler_params=_params("arbitrary"),
        name="combine",
    )(x2, route, g.reshape(1, d), *([y] * TOP_K))


def kernel(x, mem, norm_mix_g, w_in, gdn_conv_w, gdn_A_log, gdn_dt_bias, gdn_norm_g, moba_norm_g, w_out,
           norm_xattn_g, norm_mem_g, xattn_w_q, xattn_w_kv, xattn_w_o, norm_ffn_g, router_w, router_b,
           w_gate_up, b_gate_up, w_down, b_down, final_norm_g):
    b, s, d = x.shape
    t = b * s
    mem_len = mem.shape[1]
    mw = moba_norm_g.shape[1]
    gw = GDN_HEADS * GDN_HEAD_DIM
    xcur = x.reshape(t, d)
    for l in range(w_in.shape[0]):
        qt, mk, mv, gqkv, gz, gba = _in_proj(xcur, norm_mix_g[l], w_in[l], mw, gw)
        o_moba = _moba(qt, mk, mv, b, s)
        o_gdn = _gdn(gqkv.reshape(b, s, 3 * gw), gz.reshape(b, s, gw), gba.reshape(b, s, LANES),
                     gdn_conv_w[l], gdn_A_log[l], gdn_dt_bias[l], gdn_norm_g[l])
        kv = _mem_kv(mem.reshape(b * mem_len, d), norm_mem_g[l], xattn_w_kv[l], mem_len)
        x2, h3, route = _mid(xcur, o_moba.reshape(t, mw), o_gdn.reshape(t, gw), moba_norm_g[l], w_out[l],
                             norm_xattn_g[l], xattn_w_q[l], kv, xattn_w_o[l], norm_ffn_g[l], router_w[l],
                             router_b[l], s, mem_len)
        expert = route[:, TOP_K:2 * TOP_K].astype(jnp.int32)
        plan = _route_plan(expert, t)
        y = _moe(h3, *plan, w_gate_up[l], b_gate_up[l], w_down[l], b_down[l])
        xcur = _combine(x2, route, y, final_norm_g, l == w_in.shape[0] - 1)
    return xcur.reshape(b, s, d)
```

```python
import functools

import jax
import jax.numpy as jnp
from jax import lax
from jax.experimental import pallas as pl
from jax.experimental.pallas import tpu as pltpu

F32 = jnp.float32
BF16 = jnp.bfloat16

RMS_EPS = 1e-6
MOBA_HEAD_DIM = 64
MOBA_BLOCK = 256
MOBA_TOPK = 3
GDN_HEAD_DIM = 128
GDN_HEADS = 4
GDN_CONV = 4
GDN_CHUNK = 64
XATTN_HEAD_DIM = 128
N_EXPERTS = 32
TOP_K = 4
SWIGLU_LIMIT = 7.0
SWIGLU_ALPHA = 1.702

LANES = 128
ROW_TILE = 8
VMEM_LIMIT = 56 * 1024 * 1024

IN_ROWS = 512
MID_ROWS = 512
MOE_ROWS = 512
MOE_CHUNK = 128
GDN_GROUP = 32
PREP_ROWS = 256
NEG_INF = float("-inf")
MOBA_Q_SCALE = 1.4426950408889634 / MOBA_HEAD_DIM ** 0.5
MOBA_MASKED = -1e30


def _params(*sem):
    return pltpu.CompilerParams(dimension_semantics=sem, vmem_limit_bytes=VMEM_LIMIT)


def _rms(x, g):
    return x * lax.rsqrt(jnp.mean(x * x, axis=-1, keepdims=True) + RMS_EPS) * g


def _dot(a, b):
    return jnp.dot(a, b, preferred_element_type=F32)


def _dot_nt(a, b):
    return lax.dot_general(a, b, (((1,), (1,)), ((), ())), preferred_element_type=F32)


def _sigmoid(x):
    return 1.0 / (1.0 + jnp.exp(-x))


def _in_proj_kernel(x_ref, g_ref, w_ref, wt_ref, qt_ref, mk_ref, mv_ref, gqkv_ref, gz_ref, gba_ref, *, mw, gw):
    hn = _rms(x_ref[...], g_ref[...]).astype(BF16)
    mm = lambda lo, hi: _dot(hn, w_ref[:, lo:hi])
    qt_ref[...] = (_dot_nt(wt_ref[...], hn) * MOBA_Q_SCALE).astype(BF16)
    mk_ref[...] = mm(0, mw).astype(BF16)
    mv_ref[...] = mm(mw, 2 * mw).astype(BF16)
    gqkv_ref[...] = mm(2 * mw, 2 * mw + 3 * gw)
    gz_ref[...] = mm(2 * mw + 3 * gw, 2 * mw + 4 * gw)
    gba_ref[...] = mm(2 * mw + 4 * gw, 2 * mw + 4 * gw + LANES)


def _in_proj(x2d, g, w_in, mw, gw):
    t, d = x2d.shape
    n_real = w_in.shape[1] - mw
    n_pad = 2 * mw + 4 * gw + LANES
    w = jnp.pad(w_in[:, mw:], ((0, 0), (0, n_pad - n_real))).astype(BF16)
    w_t = w_in[:, :mw].T.astype(BF16)
    row = lambda n: pl.BlockSpec((IN_ROWS, n), lambda i: (i, 0))
    return pl.pallas_call(
        functools.partial(_in_proj_kernel, mw=mw, gw=gw),
        grid=(t // IN_ROWS,),
        in_specs=[row(d), pl.BlockSpec((1, d), lambda i: (0, 0)), pl.BlockSpec((d, n_pad), lambda i: (0, 0)),
                  pl.BlockSpec((mw, d), lambda i: (0, 0))],
        out_specs=[pl.BlockSpec((mw, IN_ROWS), lambda i: (0, i)), row(mw), row(mw), row(3 * gw), row(gw),
                   row(LANES)],
        out_shape=[jax.ShapeDtypeStruct((mw, t), BF16), jax.ShapeDtypeStruct((t, mw), BF16),
                   jax.ShapeDtypeStruct((t, mw), BF16),
                   jax.ShapeDtypeStruct((t, 3 * gw), F32), jax.ShapeDtypeStruct((t, gw), F32),
                   jax.ShapeDtypeStruct((t, LANES), F32)],
        compiler_params=_params("arbitrary"),
        name="in_proj",
    )(x2d, g.reshape(1, d), w, w_t)


def _moba_select(g_t, i):
    nb = g_t.shape[0]
    row = lax.broadcasted_iota(jnp.int32, g_t.shape, 0)
    valid = row < i
    sel = jnp.zeros_like(g_t)
    for j in range(nb):
        gj = g_t[j:j + 1, :]
        beats = valid & ((g_t > gj) | ((g_t == gj) & (row < j)))
        rank = jnp.sum(jnp.where(beats, 1.0, 0.0), axis=0, keepdims=True)
        sel = jnp.where(row == j, jnp.where(rank < MOBA_TOPK, 1.0, 0.0), sel)
    return jnp.where(valid, sel, 0.0)


def _dot_tn(a, b):
    return lax.dot_general(a, b, (((0,), (0,)), ((), ())), preferred_element_type=F32)


def _moba_kernel(qt_ref, k_ref, v_ref, o_ref, kmean_ref, kaug_ref, *, nb):
    i = pl.program_id(2)
    bs = MOBA_BLOCK
    hd = MOBA_HEAD_DIM
    lane = lax.broadcasted_iota(jnp.int32, (1, LANES), 1)

    @pl.when(i == 0)
    def _():
        for j in range(nb):
            rows = slice(j * bs, (j + 1) * bs)
            kb = k_ref[0, rows, :]
            kmean_ref[j:j + 1, :] = jnp.mean(kb.astype(F32), axis=0, keepdims=True)
            kaug_ref[0, rows, :] = jnp.where(lane < hd, kb, jnp.where(lane == hd + j, 1.0, 0.0).astype(BF16))
            kaug_ref[1, rows, :] = jnp.where(lane >= hd, kb, jnp.where(lane == j, 1.0, 0.0).astype(BF16))

    qt = qt_ref[...]
    qtf = qt.astype(F32)
    low = lax.broadcasted_iota(jnp.int32, (LANES, 1), 0) < hd
    kmean = kmean_ref[...]
    gate = [jnp.dot(kmean, qh, preferred_element_type=F32, precision=lax.Precision.HIGHEST)
            for qh in (jnp.where(low, qtf, 0.0), jnp.where(low, 0.0, qtf))]
    bias = [jnp.where(_moba_select(g, i) > 0.5, 0.0, MOBA_MASKED) for g in gate]
    pad = jnp.zeros((hd - nb, bs), F32)
    q_past = (jnp.concatenate([qtf[:hd], bias[0], pad], axis=0).astype(BF16),
              jnp.concatenate([bias[1], pad, qtf[hd:]], axis=0).astype(BF16))
    zero = jnp.zeros_like(qt)
    q_own = (jnp.where(low, qt, zero), jnp.where(low, zero, qt))

    key_ix = lax.broadcasted_iota(jnp.int32, (bs, bs), 0)
    qry_ix = lax.broadcasted_iota(jnp.int32, (bs, bs), 1)
    causal_bias = jnp.where(key_ix <= qry_ix, 0.0, NEG_INF)
    own = pl.ds(pl.multiple_of(i * bs, bs), bs)
    k_own = k_ref[0, own, :]
    v_own = v_ref[0, own, :]

    def attend(width):
        heads = (0, 1)
        s_own = [_dot(k_own, q_own[h]) + causal_bias for h in heads]
        m = [s.max(axis=0, keepdims=True) for s in s_own]
        if width:
            s_past = [_dot(kaug_ref[h, :width * bs, :], q_past[h]) for h in heads]
            m = [jnp.maximum(m[h], s_past[h].max(axis=0, keepdims=True)) for h in heads]
        p_own = [jnp.exp2(s_own[h] - m[h]) for h in heads]
        l = [p.sum(axis=0, keepdims=True) for p in p_own]
        acc = [_dot_tn(v_own, p_own[h].astype(BF16)) for h in heads]
        if width:
            p_past = [jnp.exp2(s_past[h] - m[h]) for h in heads]
            l = [l[h] + p_past[h].sum(axis=0, keepdims=True) for h in heads]
            acc = [acc[h] + _dot_tn(v_ref[0, :width * bs, :], p_past[h].astype(BF16)) for h in heads]
        o_ref[0] = jnp.where(low, acc[0] / l[0], acc[1] / l[1]).T

    for width in range(nb):
        pl.when(i == width)(functools.partial(attend, width))


def _moba(qt, mk, mv, b, s):
    mw = mk.shape[-1]
    nb = s // MOBA_BLOCK
    seq_blk = pl.BlockSpec((1, s, LANES), lambda bi, hp, i: (bi, 0, hp))
    return pl.pallas_call(
        functools.partial(_moba_kernel, nb=nb),
        grid=(b, mw // LANES, nb),
        in_specs=[pl.BlockSpec((LANES, MOBA_BLOCK), lambda bi, hp, i: (hp, bi * nb + i)), seq_blk, seq_blk],
        out_specs=pl.BlockSpec((1, MOBA_BLOCK, LANES), lambda bi, hp, i: (bi, i, hp)),
        out_shape=jax.ShapeDtypeStruct((b, s, mw), F32),
        scratch_shapes=[pltpu.VMEM((nb, LANES), F32), pltpu.VMEM((2, s, LANES), BF16)],
        compiler_params=_params("arbitrary", "arbitrary", "arbitrary"),
        name="moba",
    )(qt, mk.reshape(b, s, mw), mv.reshape(b, s, mw))


def _gdn_gate_kernel(ba_ref, alog_ref, dtb_ref, bg_ref, *, seq):
    x = ba_ref[0]
    lane = lax.broadcasted_iota(jnp.int32, (1, LANES), 1)
    xa = x + dtb_ref[...]
    softplus = jnp.maximum(xa, 0.0) + jnp.log(1.0 + jnp.exp(-jnp.abs(xa)))
    g = jnp.where((lane >= GDN_HEADS) & (lane < 2 * GDN_HEADS), -jnp.exp(alog_ref[...]) * softplus, 0.0)
    pos = lax.broadcasted_iota(jnp.int32, (seq, 1), 0) % GDN_CHUNK
    sft = 1
    while sft < GDN_CHUNK:
        g = g + jnp.where(pos >= sft, pltpu.roll(g, sft, 0), 0.0)
        sft *= 2
    bg_ref[0] = jnp.where(lane < GDN_HEADS, _sigmoid(x), g)


def _gdn_prep_kernel(x_ref, w_ref, o_ref, *, seq):
    cb = pl.program_id(1)
    is_qk = cb < 2 * GDN_HEADS
    scale = jnp.where(cb < GDN_HEADS, GDN_HEAD_DIM ** -0.5, 1.0)
    taps = [w_ref[j:j + 1, :] for j in range(GDN_CONV)]
    halo = 8
    for r0 in range(0, seq, PREP_ROWS):
        if r0 == 0:
            xe = jnp.concatenate([jnp.zeros((halo, LANES), F32), x_ref[0, :PREP_ROWS, :]], axis=0)
        else:
            xe = x_ref[0, r0 - halo:r0 + PREP_ROWS, :]
        y = taps[GDN_CONV - 1] * xe[halo:, :]
        for sft in range(1, GDN_CONV):
            y = y + taps[GDN_CONV - 1 - sft] * xe[halo - sft:halo - sft + PREP_ROWS, :]
        y = y * _sigmoid(y)
        normed = y * (lax.rsqrt(jnp.sum(y * y, axis=-1, keepdims=True) + RMS_EPS) * scale)
        o_ref[0, r0:r0 + PREP_ROWS, :] = jnp.where(is_qk, normed, y).astype(BF16)


def _unit_lower_inverses(mats):
    c = mats[0].shape[0]
    r = lax.broadcasted_iota(jnp.int32, (c, c), 0)
    cc = lax.broadcasted_iota(jnp.int32, (c, c), 1)
    eye = jnp.where(r == cc, 1.0, 0.0)
    pair = (r // 2) == (cc // 2)
    invs = [eye - jnp.where(pair, a, 0.0) for a in mats]
    size = 4
    while size <= c:
        level = ((r // size) == (cc // size)) & ((r // (size // 2)) != (cc // (size // 2)))
        inv_b = [inv.astype(BF16) for inv in invs]
        left = [_dot(ib, jnp.where(level, a, 0.0).astype(BF16)).astype(BF16) for ib, a in zip(inv_b, mats)]
        invs = [inv - _dot(lf, ib) for inv, lf, ib in zip(invs, left, inv_b)]
        size *= 2
    return invs


def _gdn_intra_kernel(q_ref, k_ref, v_ref, bg_ref, u_ref, w_ref, qd_ref, kd_ref, qk_ref):
    h = pl.program_id(1)
    c = GDN_CHUNK
    chunks = range(GDN_GROUP)
    lane = lax.broadcasted_iota(jnp.int32, (1, LANES), 1)
    r_ix = lax.broadcasted_iota(jnp.int32, (c, c), 0)
    c_ix = lax.broadcasted_iota(jnp.int32, (c, c), 1)
    rows = [slice(gi * c, (gi + 1) * c) for gi in chunks]
    k_b16 = [k_ref[0, rw, :] for rw in rows]
    q_b16 = [q_ref[0, rw, :] for rw in rows]
    bgs = [bg_ref[0, rw, :] for rw in rows]
    beta = [jnp.sum(jnp.where(lane == h, bg, 0.0), axis=1, keepdims=True) for bg in bgs]
    gam = [jnp.sum(jnp.where(lane == GDN_HEADS + h, bg, 0.0), axis=1, keepdims=True) for bg in bgs]
    kb = [kk.astype(F32) * bt for kk, bt in zip(k_b16, beta)]
    kk_raw = [_dot_nt(x.astype(BF16), kk) for x, kk in zip(kb, k_b16)]
    qk_raw = [_dot_nt(qq, kk) for qq, kk in zip(q_b16, k_b16)]
    decay = []
    for gm in gam:
        gam_r = jnp.sum(jnp.where(r_ix == c_ix, gm, 0.0), axis=0, keepdims=True)
        decay.append(jnp.exp(jnp.where(c_ix <= r_ix, gm - gam_r, NEG_INF)))
    t_inv = _unit_lower_inverses([jnp.where(c_ix < r_ix, x * dc, 0.0) for x, dc in zip(kk_raw, decay)])
    eg = [jnp.exp(gm) for gm in gam]
    rhs = [jnp.concatenate([v_ref[0, rw, :].astype(F32) * bt, x * e], axis=1).astype(BF16)
           for rw, bt, x, e in zip(rows, beta, kb, eg)]
    uw = [_dot(ti.astype(BF16), rh) for ti, rh in zip(t_inv, rhs)]
    for gi in chunks:
        rw = rows[gi]
        u_ref[0, rw, :] = uw[gi][:, :LANES].astype(BF16)
        w_ref[0, rw, :] = uw[gi][:, LANES:].astype(BF16)
        qk_ref[0, 0, rw, :] = (qk_raw[gi] * decay[gi]).astype(BF16)
        qd_ref[0, rw, :] = (q_b16[gi].astype(F32) * eg[gi]).astype(BF16)
        kd_ref[0, rw, :] = (k_b16[gi].astype(F32) * jnp.exp(gam[gi][c - 1:c, :] - gam[gi])).astype(BF16)


def _gdn_scan_kernel(u_ref, w_ref, qd_ref, kd_ref, qk_ref, z_ref, bg_ref, ng_ref, o_ref, *, seq):
    c = GDN_CHUNK
    ng = ng_ref[...]
    heads = range(GDN_HEADS)
    cols = [slice(h * GDN_HEAD_DIM, (h + 1) * GDN_HEAD_DIM) for h in heads]

    def step(n, states):
        r0 = pl.multiple_of(n * c, c)
        rows = pl.ds(r0, c)
        bg_tail = bg_ref[0, pl.ds(r0 + c - 8, 8), :]
        s_b = [st.astype(BF16) for st in states]
        ws = [_dot(w_ref[0, rows, cols[h]], s_b[h]) for h in heads]
        qs = [_dot(qd_ref[0, rows, cols[h]], s_b[h]) for h in heads]
        v_b = [(u_ref[0, rows, cols[h]].astype(F32) - ws[h]).astype(BF16) for h in heads]
        kd_v = [_dot_tn(kd_ref[0, rows, cols[h]], v_b[h]) for h in heads]
        qkv = [_dot(qk_ref[0, h, rows, :], v_b[h]) for h in heads]
        new_states = []
        for h in heads:
            g_last = bg_tail[7:8, GDN_HEADS + h:GDN_HEADS + h + 1]
            new_states.append(states[h] * jnp.exp(g_last) + kd_v[h])
            z = z_ref[0, rows, cols[h]]
            o_ref[0, rows, cols[h]] = _rms(qs[h] + qkv[h], ng) * (z * _sigmoid(z))
        return tuple(new_states)

    zero = jnp.zeros((GDN_HEAD_DIM, GDN_HEAD_DIM), F32)
    lax.fori_loop(0, seq // c, step, (zero,) * GDN_HEADS)


def _gdn(gqkv, gz, gba, conv_w, a_log, dt_bias, norm_g):
    b, s, w3 = gqkv.shape
    nh = GDN_HEADS
    hw = nh * GDN_HEAD_DIM
    c = GDN_CHUNK
    lane_pad = lambda v: jnp.pad(v.reshape(1, -1), ((0, 0), (nh, LANES - 2 * nh)))
    seq_blk = pl.BlockSpec((1, s, LANES), lambda bi: (bi, 0, 0))
    const = pl.BlockSpec((1, LANES), lambda bi: (0, 0))
    bg = pl.pallas_call(
        functools.partial(_gdn_gate_kernel, seq=s),
        grid=(b,),
        in_specs=[seq_blk, const, const],
        out_specs=seq_blk,
        out_shape=jax.ShapeDtypeStruct((b, s, LANES), F32),
        compiler_params=_params("arbitrary"),
        name="gdn_gate",
    )(gba, lane_pad(a_log), lane_pad(dt_bias))

    qkv = pl.pallas_call(
        functools.partial(_gdn_prep_kernel, seq=s),
        grid=(b, w3 // LANES),
        in_specs=[pl.BlockSpec((1, s, LANES), lambda bi, cb: (bi, 0, cb)),
                  pl.BlockSpec((GDN_CONV, LANES), lambda bi, cb: (0, cb))],
        out_specs=pl.BlockSpec((1, s, LANES), lambda bi, cb: (bi, 0, cb)),
        out_shape=jax.ShapeDtypeStruct((b, s, w3), BF16),
        compiler_params=_params("arbitrary", "arbitrary"),
        name="gdn_prep",
    )(gqkv, conv_w)

    grp = GDN_GROUP * c
    head_blk = lambda off: pl.BlockSpec((1, grp, LANES), lambda bi, h, n: (bi, n, off + h))
    head_shape = jax.ShapeDtypeStruct((b, s, hw), BF16)
    u, w, qd, kd, qk = pl.pallas_call(
        _gdn_intra_kernel,
        grid=(b, nh, s // grp),
        in_specs=[head_blk(0), head_blk(nh), head_blk(2 * nh),
                  pl.BlockSpec((1, grp, LANES), lambda bi, h, n: (bi, n, 0))],
        out_specs=[head_blk(0)] * 4 + [pl.BlockSpec((1, 1, grp, c), lambda bi, h, n: (bi, h, n, 0))],
        out_shape=[head_shape] * 4 + [jax.ShapeDtypeStruct((b, nh, s, c), BF16)],
        compiler_params=_params("arbitrary", "arbitrary", "arbitrary"),
        name="gdn_intra",
    )(qkv, qkv, qkv, bg)

    full = pl.BlockSpec((1, s, hw), lambda bi: (bi, 0, 0))
    return pl.pallas_call(
        functools.partial(_gdn_scan_kernel, seq=s),
        grid=(b,),
        in_specs=[full, full, full, full, pl.BlockSpec((1, nh, s, c), lambda bi: (bi, 0, 0, 0)), full, seq_blk,
                  const],
        out_specs=full,
        out_shape=jax.ShapeDtypeStruct((b, s, hw), F32),
        compiler_params=_params("arbitrary"),
        name="gdn_scan",
    )(u, w, qd, kd, qk, gz, bg, norm_g.reshape(1, -1))


def _mem_kv_kernel(m_ref, g_ref, w_ref, kv_ref):
    kv_ref[...] = _dot(_rms(m_ref[...], g_ref[...]).astype(BF16), w_ref[...]).astype(BF16)


def _mem_kv(mem2d, g, w_kv, rows):
    t, d = mem2d.shape
    n = w_kv.shape[1]
    return pl.pallas_call(
        _mem_kv_kernel,
        grid=(t // rows,),
        in_specs=[pl.BlockSpec((rows, d), lambda i: (i, 0)), pl.BlockSpec((1, d), lambda i: (0, 0)),
                  pl.BlockSpec((d, n), lambda i: (0, 0))],
        out_specs=pl.BlockSpec((rows, n), lambda i: (i, 0)),
        out_shape=jax.ShapeDtypeStruct((t, n), BF16),
        compiler_params=_params("arbitrary"),
        name="mem_kv",
    )(mem2d, g.reshape(1, d), w_kv.astype(BF16))


def _mid_kernel(x_ref, om_ref, og_ref, mg_ref, wout_ref, xg_ref, wq_ref, kv_ref, wo_ref, fg_ref, rw_ref, rb_ref,
                x2_ref, h3_ref, route_ref, *, mw, xw):
    mo = _rms(om_ref[...], mg_ref[...]).astype(BF16)
    x1 = x_ref[...] + _dot(mo, wout_ref[:mw, :]) + _dot(og_ref[...].astype(BF16), wout_ref[mw:, :])

    h2 = _rms(x1, xg_ref[...]).astype(BF16)
    q = (_dot(h2, wq_ref[...]) * (XATTN_HEAD_DIM ** -0.5)).astype(BF16)
    head_cols = [slice(h * XATTN_HEAD_DIM, (h + 1) * XATTN_HEAD_DIM) for h in range(xw // XATTN_HEAD_DIM)]
    scores = [_dot_nt(q[:, sl], kv_ref[:, sl]) for sl in head_cols]
    probs = [jnp.exp(s - jnp.max(s, axis=1, keepdims=True)) for s in scores]
    heads = [_dot(p.astype(BF16), kv_ref[:, xw + sl.start:xw + sl.stop]) / jnp.sum(p, axis=1, keepdims=True)
             for p, sl in zip(probs, head_cols)]
    x2 = x1 + _dot(jnp.concatenate(heads, axis=1).astype(BF16), wo_ref[...])
    x2_ref[...] = x2

    h3 = _rms(x2, fg_ref[...])
    slabs = h3.shape[1] // LANES
    for j in range(slabs):
        h3_ref[pl.ds(j, h3.shape[0], stride=slabs), :] = h3[:, j * LANES:(j + 1) * LANES]
    ne = rb_ref.shape[0]
    h_hi = h3.astype(BF16)
    h_lo = (h3 - h_hi.astype(F32)).astype(BF16)
    by_hi = _dot_nt(rw_ref[...], h_hi)
    logits = by_hi[:ne, :] + by_hi[ne:, :] + _dot_nt(rw_ref[:ne, :], h_lo) + rb_ref[...]
    row = lax.broadcasted_iota(jnp.int32, logits.shape, 0)
    weights, picks, top = [], [], None
    for kk in range(TOP_K):
        m = jnp.max(logits, axis=0, keepdims=True)
        idx = jnp.min(jnp.where(logits == m, row, ne), axis=0, keepdims=True)
        logits = jnp.where(row == idx, NEG_INF, logits)
        top = m if top is None else top
        weights.append(jnp.exp(m - top))
        picks.append(idx.astype(F32))
    denom = sum(weights[1:], weights[0])
    rows = [wk / denom for wk in weights] + picks
    rows.append(jnp.zeros((LANES - len(rows), logits.shape[1]), F32))
    route_ref[...] = jnp.concatenate(rows, axis=0).T


def _mid(x2d, om, og, moba_g, w_out, xattn_g, w_q, kv, w_o, ffn_g, router_w, router_b, seq, mem_len):
    t, d = x2d.shape
    mw, gw, xw, ne = om.shape[1], og.shape[1], w_q.shape[1], router_w.shape[1]
    tiles_per_seq = seq // MID_ROWS
    row = lambda n: pl.BlockSpec((MID_ROWS, n), lambda i: (i, 0))
    const = lambda r, c: pl.BlockSpec((r, c), lambda i: (0, 0))
    rw_hi = router_w.T.astype(BF16)
    rw_lo = (router_w.T - rw_hi.astype(F32)).astype(BF16)
    return pl.pallas_call(
        functools.partial(_mid_kernel, mw=mw, xw=xw),
        grid=(t // MID_ROWS,),
        in_specs=[row(d), row(mw), row(gw), const(1, mw), const(mw + gw, d), const(1, d), const(d, xw),
                  pl.BlockSpec((mem_len, 2 * xw), lambda i: (i // tiles_per_seq, 0)),
                  const(xw, d), const(1, d), const(2 * ne, d), const(ne, 1)],
        out_specs=[row(d), pl.BlockSpec((MID_ROWS * (d // LANES), LANES), lambda i: (i, 0)), row(LANES)],
        out_shape=[jax.ShapeDtypeStruct((t, d), F32), jax.ShapeDtypeStruct((t * (d // LANES), LANES), F32),
                   jax.ShapeDtypeStruct((t, LANES), F32)],
        compiler_params=_params("arbitrary"),
        name="mid",
    )(x2d, om, og, moba_g.reshape(1, mw), w_out.astype(BF16), xattn_g.reshape(1, d), w_q.astype(BF16), kv,
      w_o.astype(BF16), ffn_g.reshape(1, d), jnp.concatenate([rw_hi, rw_lo], axis=0), router_b.reshape(ne, 1))


def _moe_kernel(be_ref, bv_ref, tok_ref, tokn_ref, dst_ref, h_hbm, wgu_ref, bgu_ref, wd_ref, bd_ref, y_hbm,
                xbuf, ybuf, wgu_b16, wd_b16, sem_in, sem_out, *, dff):
    blk = pl.program_id(0)
    nv = bv_ref[blk]
    nv_next = bv_ref[blk + 1]
    slot = blk % 2
    rows = MOE_ROWS
    chunk = MOE_CHUNK
    n_chunks = rows // chunk
    rt = ROW_TILE

    def row_in(s, r, t):
        return pltpu.make_async_copy(h_hbm.at[pl.ds(pl.multiple_of(t, rt), rt), :],
                                     xbuf.at[s, pl.ds(r * rt, rt), :], sem_in.at[s])

    def row_out(s, r, d):
        return pltpu.make_async_copy(ybuf.at[s, pl.ds(r * rt, rt), :],
                                     y_hbm.at[pl.ds(pl.multiple_of(d, rt), rt), :], sem_out.at[s])

    def wait_gather(s):
        pltpu.make_async_copy(h_hbm.at[pl.ds(0, rows * rt), :], xbuf.at[s], sem_in.at[s]).wait()

    def wait_scatter(s):
        pltpu.make_async_copy(ybuf.at[s], y_hbm.at[pl.ds(0, rows * rt), :], sem_out.at[s]).wait()

    @pl.when(blk == 0)
    def _():
        ybuf[1] = jnp.zeros(ybuf.shape[1:], F32)
        spare = pltpu.make_async_copy(ybuf.at[1], y_hbm.at[pl.ds(y_hbm.shape[0] - rows * rt, rows * rt), :],
                                      sem_out.at[1])
        spare.start()
        spare.wait()

    @pl.when((blk == 0) & (nv > 0))
    def _():
        for r in range(rows):
            row_in(0, r, tok_ref[0, 0, r]).start(priority=r % 2)

    @pl.when(nv > 0)
    def _():
        @pl.when((blk == 0) | (be_ref[blk] != be_ref[jnp.maximum(blk - 1, 0)]))
        def _():
            wgu_b16[...] = wgu_ref[0].astype(BF16)
            wd_b16[...] = wd_ref[0].astype(BF16)

        wait_gather(slot)

        @pl.when(blk >= 2)
        def _():
            wait_scatter(slot)

        per_chunk_in = rows // (n_chunks // 2)
        for c in range(n_chunks):
            if c < n_chunks // 2:
                for r in range(c * per_chunk_in, (c + 1) * per_chunk_in):
                    row_in(1 - slot, r, tokn_ref[0, 0, r]).start(priority=r % 2)
            slab = lambda j: pl.ds(c * chunk * rt + j, chunk, stride=rt)
            x = jnp.concatenate([xbuf[slot, slab(j), :] for j in range(rt)], axis=1).astype(BF16)
            gu = _dot(x, wgu_b16[...]) + bgu_ref[0]
            gate = jnp.minimum(gu[:, :dff], SWIGLU_LIMIT)
            up = jnp.clip(gu[:, dff:], -SWIGLU_LIMIT, SWIGLU_LIMIT)
            act = (up + 1.0) * gate * _sigmoid(SWIGLU_ALPHA * gate)
            y = _dot(act.astype(BF16), wd_b16[...]) + bd_ref[0]
            for j in range(rt):
                ybuf[slot, slab(j), :] = y[:, j * LANES:(j + 1) * LANES]
            for r in range(c * chunk, (c + 1) * chunk):
                row_out(slot, r, dst_ref[0, 0, r]).start(priority=r % 2)

        @pl.when(nv_next == 0)
        def _():
            wait_gather(1 - slot)

            @pl.when(blk >= 1)
            def _():
                wait_scatter(1 - slot)
            wait_scatter(slot)


def _moe(h3_tiles, block_expert, block_valid, row_tok, row_dst, w_gu, b_gu, w_d, b_d):
    ne, d, n2 = w_gu.shape
    assert d == ROW_TILE * LANES
    t = h3_tiles.shape[0] // ROW_TILE
    dff = n2 // 2
    nblk = block_expert.shape[0]
    idx_blk = pl.BlockSpec((1, 1, MOE_ROWS), lambda i, be, bv: (i, 0, 0), memory_space=pltpu.SMEM)
    idx_next = pl.BlockSpec((1, 1, MOE_ROWS), lambda i, be, bv: (jnp.minimum(i + 1, nblk - 1), 0, 0),
                            memory_space=pltpu.SMEM)
    grid_spec = pltpu.PrefetchScalarGridSpec(
        num_scalar_prefetch=2,
        grid=(nblk,),
        in_specs=[idx_blk, idx_next, idx_blk, pl.BlockSpec(memory_space=pl.ANY),
                  pl.BlockSpec((1, d, n2), lambda i, be, bv: (be[i], 0, 0)),
                  pl.BlockSpec((1, 1, n2), lambda i, be, bv: (be[i], 0, 0)),
                  pl.BlockSpec((1, dff, d), lambda i, be, bv: (be[i], 0, 0)),
                  pl.BlockSpec((1, 1, d), lambda i, be, bv: (be[i], 0, 0))],
        out_specs=pl.BlockSpec(memory_space=pl.ANY),
        scratch_shapes=[pltpu.VMEM((2, MOE_ROWS * ROW_TILE, LANES), F32),
                        pltpu.VMEM((2, MOE_ROWS * ROW_TILE, LANES), F32),
                        pltpu.VMEM((d, n2), BF16), pltpu.VMEM((dff, d), BF16),
                        pltpu.SemaphoreType.DMA((2,)), pltpu.SemaphoreType.DMA((2,))],
    )
    tok3 = (row_tok * ROW_TILE).reshape(nblk, 1, MOE_ROWS)
    valid_ext = jnp.concatenate([block_valid, jnp.zeros((1,), jnp.int32)])
    return pl.pallas_call(
        functools.partial(_moe_kernel, dff=dff),
        grid_spec=grid_spec,
        out_shape=jax.ShapeDtypeStruct(((TOP_K * t + MOE_ROWS) * ROW_TILE, LANES), F32),
        compiler_params=_params("arbitrary"),
        name="moe",
    )(block_expert, valid_ext, tok3, tok3, (row_dst * ROW_TILE).reshape(nblk, 1, MOE_ROWS), h3_tiles,
      w_gu, b_gu.reshape(ne, 1, n2), w_d, b_d.reshape(ne, 1, d))


def _route_plan(expert, t):
    n_pairs = t * TOP_K
    e_flat = expert.reshape(-1)
    order = jnp.argsort(e_flat, stable=True).astype(jnp.int32)
    experts = jnp.arange(N_EXPERTS, dtype=jnp.int32)
    counts = jnp.sum((e_flat[:, None] == experts[None, :]).astype(jnp.int32), axis=0)
    padded = (counts + MOE_ROWS - 1) // MOE_ROWS * MOE_ROWS
    pend = jnp.cumsum(padded)
    pstart = pend - padded
    gstart = jnp.cumsum(counts) - counts
    nblk = n_pairs // MOE_ROWS + N_EXPERTS
    blk_row0 = jnp.arange(nblk, dtype=jnp.int32) * MOE_ROWS
    block_expert = jnp.minimum(jnp.sum((pend[None, :] <= blk_row0[:, None]).astype(jnp.int32), axis=1), N_EXPERTS - 1)
    pick = lambda table: jnp.sum(jnp.where(block_expert[:, None] == experts[None, :], table[None, :], 0), axis=1)
    in_group = blk_row0 - pick(pstart)
    block_valid = jnp.where(blk_row0 < pend[-1], jnp.clip(pick(counts) - in_group, 0, MOE_ROWS), 0)
    first_sorted = pick(gstart) + in_group
    local = jnp.arange(MOE_ROWS, dtype=jnp.int32)
    srt = first_sorted[:, None] + local[None, :]
    pair = order[jnp.clip(srt, 0, n_pairs - 1)]
    row_tok = pair // TOP_K
    row_dst = jnp.where(local[None, :] < block_valid[:, None], (pair % TOP_K) * t + row_tok, n_pairs + local[None, :])
    return (block_expert.astype(jnp.int32), block_valid.astype(jnp.int32), row_tok.astype(jnp.int32),
            row_dst.astype(jnp.int32))


def _combine_kernel(x2_ref, route_ref, g_ref, *rest, final):
    y_refs, o_ref = rest[:TOP_K], rest[TOP_K]
    route = route_ref[...]
    gates = [route[:, kk:kk + 1] for kk in range(TOP_K)]
    slabs = []
    for j in range(ROW_TILE):
        acc = x2_ref[:, j * LANES:(j + 1) * LANES]
        for kk in range(TOP_K):
            acc = acc + gates[kk] * y_refs[kk][pl.ds(j, x2_ref.shape[0], stride=ROW_TILE), :]
        slabs.append(acc)
    out = jnp.concatenate(slabs, axis=1)
    o_ref[...] = _rms(out, g_ref[...]) if final else out


def _combine(x2, route, y, g, final):
    t, d = x2.shape
    tiles = t // MID_ROWS
    slot_spec = lambda kk: pl.BlockSpec((MID_ROWS * ROW_TILE, LANES), lambda i: (kk * tiles + i, 0))
    return pl.pallas_call(
        functools.partial(_combine_kernel, final=final),
        grid=(tiles,),
        in_specs=[pl.BlockSpec((MID_ROWS, d), lambda i: (i, 0)), pl.BlockSpec((MID_ROWS, LANES), lambda i: (i, 0)),
                  pl.BlockSpec((1, d), lambda i: (0, 0))] + [slot_spec(kk) for kk in range(TOP_K)],
        out_specs=pl.BlockSpec((MID_ROWS, d), lambda i: (i, 0)),
        out_shape=jax.ShapeDtypeStruct((t, d), F32),
        compiler_params=_params("arbitrary"),
        name="combine",
    )(x2, route, g.reshape(1, d), *([y] * TOP_K))


def kernel(x, mem, norm_mix_g, w_in, gdn_conv_w, gdn_A_log, gdn_dt_bias, gdn_norm_g, moba_norm_g, w_out,
           norm_xattn_g, norm_mem_g, xattn_w_q, xattn_w_kv, xattn_w_o, norm_ffn_g, router_w, router_b,
           w_gate_up, b_gate_up, w_down, b_down, final_norm_g):
    b, s, d = x.shape
    t = b * s
    mem_len = mem.shape[1]
    mw = moba_norm_g.shape[1]
    gw = GDN_HEADS * GDN_HEAD_DIM
    xcur = x.reshape(t, d)
    for l in range(w_in.shape[0]):
        qt, mk, mv, gqkv, gz, gba = _in_proj(xcur, norm_mix_g[l], w_in[l], mw, gw)
        o_moba = _moba(qt, mk, mv, b, s)
        o_gdn = _gdn(gqkv.reshape(b, s, 3 * gw), gz.reshape(b, s, gw), gba.reshape(b, s, LANES),
                     gdn_conv_w[l], gdn_A_log[l], gdn_dt_bias[l], gdn_norm_g[l])
        kv = _mem_kv(mem.reshape(b * mem_len, d), norm_mem_g[l], xattn_w_kv[l], mem_len)
        x2, h3, route = _mid(xcur, o_moba.reshape(t, mw), o_gdn.reshape(t, gw), moba_norm_g[l], w_out[l],
                             norm_xattn_g[l], xattn_w_q[l], kv, xattn_w_o[l], norm_ffn_g[l], router_w[l],
                             router_b[l], s, mem_len)
        expert = route[:, TOP_K:2 * TOP_K].astype(jnp.int32)
        plan = _route_plan(expert, t)
        y = _moe(h3, *plan, w_gate_up[l], b_gate_up[l], w_down[l], b_down[l])
        xcur = _combine(x2, route, y, final_norm_g, l == w_in.shape[0] - 1)
    return xcur.reshape(b, s, d)
```

```python
import functools

import jax
import jax.numpy as jnp
from jax import lax
from jax.experimental import pallas as pl
from jax.experimental.pallas import tpu as pltpu

F32 = jnp.float32
BF16 = jnp.bfloat16

RMS_EPS = 1e-6
MOBA_HEAD_DIM = 64
MOBA_BLOCK = 256
MOBA_TOPK = 3
GDN_HEAD_DIM = 128
GDN_HEADS = 4
GDN_CONV = 4
GDN_CHUNK = 64
XATTN_HEAD_DIM = 128
N_EXPERTS = 32
TOP_K = 4
SWIGLU_LIMIT = 7.0
SWIGLU_ALPHA = 1.702

LANES = 128
ROW_TILE = 8
VMEM_LIMIT = 56 * 1024 * 1024

IN_ROWS = 512
MID_ROWS = 512
MOE_ROWS = 512
MOE_CHUNK = 128
GDN_GROUP = 32
PREP_ROWS = 256
NEG_INF = float("-inf")
MOBA_Q_SCALE = 1.4426950408889634 / MOBA_HEAD_DIM ** 0.5
MOBA_MASKED = -1e30


def _params(*sem):
    return pltpu.CompilerParams(dimension_semantics=sem, vmem_limit_bytes=VMEM_LIMIT)


def _rms(x, g):
    return x * lax.rsqrt(jnp.mean(x * x, axis=-1, keepdims=True) + RMS_EPS) * g


def _dot(a, b):
    return jnp.dot(a, b, preferred_element_type=F32)


def _dot_nt(a, b):
    return lax.dot_general(a, b, (((1,), (1,)), ((), ())), preferred_element_type=F32)


def _sigmoid(x):
    return 1.0 / (1.0 + jnp.exp(-x))


def _in_proj_kernel(x_ref, g_ref, w_ref, wt_ref, qt_ref, mk_ref, mv_ref, gqkv_ref, gz_ref, gba_ref, *, mw, gw):
    hn = _rms(x_ref[...], g_ref[...]).astype(BF16)
    mm = lambda lo, hi: _dot(hn, w_ref[:, lo:hi])
    qt_ref[...] = (_dot_nt(wt_ref[...], hn) * MOBA_Q_SCALE).astype(BF16)
    mk_ref[...] = mm(0, mw).astype(BF16)
    mv_ref[...] = mm(mw, 2 * mw).astype(BF16)
    gqkv_ref[...] = mm(2 * mw, 2 * mw + 3 * gw)
    gz_ref[...] = mm(2 * mw + 3 * gw, 2 * mw + 4 * gw)
    gba_ref[...] = mm(2 * mw + 4 * gw, 2 * mw + 4 * gw + LANES)


def _in_proj(x2d, g, w_in, mw, gw):
    t, d = x2d.shape
    n_real = w_in.shape[1] - mw
    n_pad = 2 * mw + 4 * gw + LANES
    w = jnp.pad(w_in[:, mw:], ((0, 0), (0, n_pad - n_real))).astype(BF16)
    w_t = w_in[:, :mw].T.astype(BF16)
    row = lambda n: pl.BlockSpec((IN_ROWS, n), lambda i: (i, 0))
    return pl.pallas_call(
        functools.partial(_in_proj_kernel, mw=mw, gw=gw),
        grid=(t // IN_ROWS,),
        in_specs=[row(d), pl.BlockSpec((1, d), lambda i: (0, 0)), pl.BlockSpec((d, n_pad), lambda i: (0, 0)),
                  pl.BlockSpec((mw, d), lambda i: (0, 0))],
        out_specs=[pl.BlockSpec((mw, IN_ROWS), lambda i: (0, i)), row(mw), row(mw), row(3 * gw), row(gw),
                   row(LANES)],
        out_shape=[jax.ShapeDtypeStruct((mw, t), BF16), jax.ShapeDtypeStruct((t, mw), BF16),
                   jax.ShapeDtypeStruct((t, mw), BF16),
                   jax.ShapeDtypeStruct((t, 3 * gw), F32), jax.ShapeDtypeStruct((t, gw), F32),
                   jax.ShapeDtypeStruct((t, LANES), F32)],
        compiler_params=_params("arbitrary"),
        name="in_proj",
    )(x2d, g.reshape(1, d), w, w_t)


def _moba_select(g_t, i):
    nb = g_t.shape[0]
    row = lax.broadcasted_iota(jnp.int32, g_t.shape, 0)
    valid = row < i
    sel = jnp.zeros_like(g_t)
    for j in range(nb):
        gj = g_t[j:j + 1, :]
        beats = valid & ((g_t > gj) | ((g_t == gj) & (row < j)))
        rank = jnp.sum(jnp.where(beats, 1.0, 0.0), axis=0, keepdims=True)
        sel = jnp.where(row == j, jnp.where(rank < MOBA_TOPK, 1.0, 0.0), sel)
    return jnp.where(valid, sel, 0.0)


def _dot_tn(a, b):
    return lax.dot_general(a, b, (((0,), (0,)), ((), ())), preferred_element_type=F32)


def _moba_kernel(qt_ref, k_ref, v_ref, o_ref, kmean_ref, kaug_ref, *, nb):
    i = pl.program_id(2)
    bs = MOBA_BLOCK
    hd = MOBA_HEAD_DIM
    lane = lax.broadcasted_iota(jnp.int32, (1, LANES), 1)

    @pl.when(i == 0)
    def _():
        for j in range(nb):
            rows = slice(j * bs, (j + 1) * bs)
            kb = k_ref[0, rows, :]
            kmean_ref[j:j + 1, :] = jnp.mean(kb.astype(F32), axis=0, keepdims=True)
            kaug_ref[0, rows, :] = jnp.where(lane < hd, kb, jnp.where(lane == hd + j, 1.0, 0.0).astype(BF16))
            kaug_ref[1, rows, :] = jnp.where(lane >= hd, kb, jnp.where(lane == j, 1.0, 0.0).astype(BF16))

    qt = qt_ref[...]
    qtf = qt.astype(F32)
    low = lax.broadcasted_iota(jnp.int32, (LANES, 1), 0) < hd
    kmean = kmean_ref[...]
    gate = [jnp.dot(kmean, qh, preferred_element_type=F32, precision=lax.Precision.HIGHEST)
            for qh in (jnp.where(low, qtf, 0.0), jnp.where(low, 0.0, qtf))]
    bias = [jnp.where(_moba_select(g, i) > 0.5, 0.0, MOBA_MASKED) for g in gate]
    pad = jnp.zeros((hd - nb, bs), F32)
    q_past = (jnp.concatenate([qtf[:hd], bias[0], pad], axis=0).astype(BF16),
              jnp.concatenate([bias[1], pad, qtf[hd:]], axis=0).astype(BF16))
    zero = jnp.zeros_like(qt)
    q_own = (jnp.where(low, qt, zero), jnp.where(low, zero, qt))

    key_ix = lax.broadcasted_iota(jnp.int32, (bs, bs), 0)
    qry_ix = lax.broadcasted_iota(jnp.int32, (bs, bs), 1)
    causal_bias = jnp.where(key_ix <= qry_ix, 0.0, NEG_INF)
    own = pl.ds(pl.multiple_of(i * bs, bs), bs)
    k_own = k_ref[0, own, :]
    v_own = v_ref[0, own, :]

    def attend(width):
        heads = (0, 1)
        s_own = [_dot(k_own, q_own[h]) + causal_bias for h in heads]
        m = [s.max(axis=0, keepdims=True) for s in s_own]
        if width:
            s_past = [_dot(kaug_ref[h, :width * bs, :], q_past[h]) for h in heads]
            m = [jnp.maximum(m[h], s_past[h].max(axis=0, keepdims=True)) for h in heads]
        p_own = [jnp.exp2(s_own[h] - m[h]) for h in heads]
        l = [p.sum(axis=0, keepdims=True) for p in p_own]
        acc = [_dot_tn(v_own, p_own[h].astype(BF16)) for h in heads]
        if width:
            p_past = [jnp.exp2(s_past[h] - m[h]) for h in heads]
            l = [l[h] + p_past[h].sum(axis=0, keepdims=True) for h in heads]
            acc = [acc[h] + _dot_tn(v_ref[0, :width * bs, :], p_past[h].astype(BF16)) for h in heads]
        o_ref[0] = jnp.where(low, acc[0] / l[0], acc[1] / l[1]).T

    for width in range(nb):
        pl.when(i == width)(functools.partial(attend, width))


def _moba(qt, mk, mv, b, s):
    mw = mk.shape[-1]
    nb = s // MOBA_BLOCK
    seq_blk = pl.BlockSpec((1, s, LANES), lambda bi, hp, i: (bi, 0, hp))
    return pl.pallas_call(
        functools.partial(_moba_kernel, nb=nb),
        grid=(b, mw // LANES, nb),
        in_specs=[pl.BlockSpec((LANES, MOBA_BLOCK), lambda bi, hp, i: (hp, bi * nb + i)), seq_blk, seq_blk],
        out_specs=pl.BlockSpec((1, MOBA_BLOCK, LANES), lambda bi, hp, i: (bi, i, hp)),
        out_shape=jax.ShapeDtypeStruct((b, s, mw), F32),
        scratch_shapes=[pltpu.VMEM((nb, LANES), F32), pltpu.VMEM((2, s, LANES), BF16)],
        compiler_params=_params("arbitrary", "arbitrary", "arbitrary"),
        name="moba",
    )(qt, mk.reshape(b, s, mw), mv.reshape(b, s, mw))


def _gdn_gate_kernel(ba_ref, alog_ref, dtb_ref, bg_ref, *, seq):
    x = ba_ref[0]
    lane = lax.broadcasted_iota(jnp.int32, (1, LANES), 1)
    xa = x + dtb_ref[...]
    softplus = jnp.maximum(xa, 0.0) + jnp.log(1.0 + jnp.exp(-jnp.abs(xa)))
    g = jnp.where((lane >= GDN_HEADS) & (lane < 2 * GDN_HEADS), -jnp.exp(alog_ref[...]) * softplus, 0.0)
    pos = lax.broadcasted_iota(jnp.int32, (seq, 1), 0) % GDN_CHUNK
    sft = 1
    while sft < GDN_CHUNK:
        g = g + jnp.where(pos >= sft, pltpu.roll(g, sft, 0), 0.0)
        sft *= 2
    bg_ref[0] = jnp.where(lane < GDN_HEADS, _sigmoid(x), g)


def _gdn_prep_kernel(x_ref, w_ref, o_ref, *, seq):
    cb = pl.program_id(1)
    is_qk = cb < 2 * GDN_HEADS
    scale = jnp.where(cb < GDN_HEADS, GDN_HEAD_DIM ** -0.5, 1.0)
    taps = [w_ref[j:j + 1, :] for j in range(GDN_CONV)]
    halo = 8
    for r0 in range(0, seq, PREP_ROWS):
        if r0 == 0:
            xe = jnp.concatenate([jnp.zeros((halo, LANES), F32), x_ref[0, :PREP_ROWS, :]], axis=0)
        else:
            xe = x_ref[0, r0 - halo:r0 + PREP_ROWS, :]
        y = taps[GDN_CONV - 1] * xe[halo:, :]
        for sft in range(1, GDN_CONV):
            y = y + taps[GDN_CONV - 1 - sft] * xe[halo - sft:halo - sft + PREP_ROWS, :]
        y = y * _sigmoid(y)
        normed = y * (lax.rsqrt(jnp.sum(y * y, axis=-1, keepdims=True) + RMS_EPS) * scale)
        o_ref[0, r0:r0 + PREP_ROWS, :] = jnp.where(is_qk, normed, y).astype(BF16)


def _unit_lower_inverses(mats):
    c = mats[0].shape[0]
    r = lax.broadcasted_iota(jnp.int32, (c, c), 0)
    cc = lax.broadcasted_iota(jnp.int32, (c, c), 1)
    eye = jnp.where(r == cc, 1.0, 0.0)
    pair = (r // 2) == (cc // 2)
    invs = [eye - jnp.where(pair, a, 0.0) for a in mats]
    size = 4
    while size <= c:
        level = ((r // size) == (cc // size)) & ((r // (size // 2)) != (cc // (size // 2)))
        inv_b = [inv.astype(BF16) for inv in invs]
        left = [_dot(ib, jnp.where(level, a, 0.0).astype(BF16)).astype(BF16) for ib, a in zip(inv_b, mats)]
        invs = [inv - _dot(lf, ib) for inv, lf, ib in zip(invs, left, inv_b)]
        size *= 2
    return invs


def _gdn_intra_kernel(q_ref, k_ref, v_ref, bg_ref, u_ref, w_ref, qd_ref, kd_ref, qk_ref):
    h = pl.program_id(1)
    c = GDN_CHUNK
    chunks = range(GDN_GROUP)
    lane = lax.broadcasted_iota(jnp.int32, (1, LANES), 1)
    r_ix = lax.broadcasted_iota(jnp.int32, (c, c), 0)
    c_ix = lax.broadcasted_iota(jnp.int32, (c, c), 1)
    rows = [slice(gi * c, (gi + 1) * c) for gi in chunks]
    k_b16 = [k_ref[0, rw, :] for rw in rows]
    q_b16 = [q_ref[0, rw, :] for rw in rows]
    bgs = [bg_ref[0, rw, :] for rw in rows]
    beta = [jnp.sum(jnp.where(lane == h, bg, 0.0), axis=1, keepdims=True) for bg in bgs]
    gam = [jnp.sum(jnp.where(lane == GDN_HEADS + h, bg, 0.0), axis=1, keepdims=True) for bg in bgs]
    kb = [kk.astype(F32) * bt for kk, bt in zip(k_b16, beta)]
    kk_raw = [_dot_nt(x.astype(BF16), kk) for x, kk in zip(kb, k_b16)]
    qk_raw = [_dot_nt(qq, kk) for qq, kk in zip(q_b16, k_b16)]
    decay = []
    for gm in gam:
        gam_r = jnp.sum(jnp.where(r_ix == c_ix, gm, 0.0), axis=0, keepdims=True)
        decay.append(jnp.exp(jnp.where(c_ix <= r_ix, gm - gam_r, NEG_INF)))
    t_inv = _unit_lower_inverses([jnp.where(c_ix < r_ix, x * dc, 0.0) for x, dc in zip(kk_raw, decay)])
    eg = [jnp.exp(gm) for gm in gam]
    rhs = [jnp.concatenate([v_ref[0, rw, :].astype(F32) * bt, x * e], axis=1).astype(BF16)
           for rw, bt, x, e in zip(rows, beta, kb, eg)]
    uw = [_dot(ti.astype(BF16), rh) for ti, rh in zip(t_inv, rhs)]
    for gi in chunks:
        rw = rows[gi]
        u_ref[0, rw, :] = uw[gi][:, :LANES].astype(BF16)
        w_ref[0, rw, :] = uw[gi][:, LANES:].astype(BF16)
        qk_ref[0, 0, rw, :] = (qk_raw[gi] * decay[gi]).astype(BF16)
        qd_ref[0, rw, :] = (q_b16[gi].astype(F32) * eg[gi]).astype(BF16)
        kd_ref[0, rw, :] = (k_b16[gi].astype(F32) * jnp.exp(gam[gi][c - 1:c, :] - gam[gi])).astype(BF16)


def _gdn_scan_kernel(u_ref, w_ref, qd_ref, kd_ref, qk_ref, z_ref, bg_ref, ng_ref, o_ref, *, seq):
    c = GDN_CHUNK
    ng = ng_ref[...]
    heads = range(GDN_HEADS)
    cols = [slice(h * GDN_HEAD_DIM, (h + 1) * GDN_HEAD_DIM) for h in heads]

    def step(n, states):
        r0 = pl.multiple_of(n * c, c)
        rows = pl.ds(r0, c)
        bg_tail = bg_ref[0, pl.ds(r0 + c - 8, 8), :]
        s_b = [st.astype(BF16) for st in states]
        ws = [_dot(w_ref[0, rows, cols[h]], s_b[h]) for h in heads]
        qs = [_dot(qd_ref[0, rows, cols[h]], s_b[h]) for h in heads]
        v_b = [(u_ref[0, rows, cols[h]].astype(F32) - ws[h]).astype(BF16) for h in heads]
        kd_v = [_dot_tn(kd_ref[0, rows, cols[h]], v_b[h]) for h in heads]
        qkv = [_dot(qk_ref[0, h, rows, :], v_b[h]) for h in heads]
        new_states = []
        for h in heads:
            g_last = bg_tail[7:8, GDN_HEADS + h:GDN_HEADS + h + 1]
            new_states.append(states[h] * jnp.exp(g_last) + kd_v[h])
            z = z_ref[0, rows, cols[h]]
            o_ref[0, rows, cols[h]] = _rms(qs[h] + qkv[h], ng) * (z * _sigmoid(z))
        return tuple(new_states)

    zero = jnp.zeros((GDN_HEAD_DIM, GDN_HEAD_DIM), F32)
    lax.fori_loop(0, seq // c, step, (zero,) * GDN_HEADS)


def _gdn(gqkv, gz, gba, conv_w, a_log, dt_bias, norm_g):
    b, s, w3 = gqkv.shape
    nh = GDN_HEADS
    hw = nh * GDN_HEAD_DIM
    c = GDN_CHUNK
    lane_pad = lambda v: jnp.pad(v.reshape(1, -1), ((0, 0), (nh, LANES - 2 * nh)))
    seq_blk = pl.BlockSpec((1, s, LANES), lambda bi: (bi, 0, 0))
    const = pl.BlockSpec((1, LANES), lambda bi: (0, 0))
    bg = pl.pallas_call(
        functools.partial(_gdn_gate_kernel, seq=s),
        grid=(b,),
        in_specs=[seq_blk, const, const],
        out_specs=seq_blk,
        out_shape=jax.ShapeDtypeStruct((b, s, LANES), F32),
        compiler_params=_params("arbitrary"),
        name="gdn_gate",
    )(gba, lane_pad(a_log), lane_pad(dt_bias))

    qkv = pl.pallas_call(
        functools.partial(_gdn_prep_kernel, seq=s),
        grid=(b, w3 // LANES),
        in_specs=[pl.BlockSpec((1, s, LANES), lambda bi, cb: (bi, 0, cb)),
                  pl.BlockSpec((GDN_CONV, LANES), lambda bi, cb: (0, cb))],
        out_specs=pl.BlockSpec((1, s, LANES), lambda bi, cb: (bi, 0, cb)),
        out_shape=jax.ShapeDtypeStruct((b, s, w3), BF16),
        compiler_params=_params("arbitrary", "arbitrary"),
        name="gdn_prep",
    )(gqkv, conv_w)

    grp = GDN_GROUP * c
    head_blk = lambda off: pl.BlockSpec((1, grp, LANES), lambda bi, h, n: (bi, n, off + h))
    head_shape = jax.ShapeDtypeStruct((b, s, hw), BF16)
    u, w, qd, kd, qk = pl.pallas_call(
        _gdn_intra_kernel,
        grid=(b, nh, s // grp),
        in_specs=[head_blk(0), head_blk(nh), head_blk(2 * nh),
                  pl.BlockSpec((1, grp, LANES), lambda bi, h, n: (bi, n, 0))],
        out_specs=[head_blk(0)] * 4 + [pl.BlockSpec((1, 1, grp, c), lambda bi, h, n: (bi, h, n, 0))],
        out_shape=[head_shape] * 4 + [jax.ShapeDtypeStruct((b, nh, s, c), BF16)],
        compiler_params=_params("arbitrary", "arbitrary", "arbitrary"),
        name="gdn_intra",
    )(qkv, qkv, qkv, bg)

    full = pl.BlockSpec((1, s, hw), lambda bi: (bi, 0, 0))
    return pl.pallas_call(
        functools.partial(_gdn_scan_kernel, seq=s),
        grid=(b,),
        in_specs=[full, full, full, full, pl.BlockSpec((1, nh, s, c), lambda bi: (bi, 0, 0, 0)), full, seq_blk,
                  const],
        out_specs=full,
        out_shape=jax.ShapeDtypeStruct((b, s, hw), F32),
        compiler_params=_params("arbitrary"),
        name="gdn_scan",
    )(u, w, qd, kd, qk, gz, bg, norm_g.reshape(1, -1))


def _mem_kv_kernel(m_ref, g_ref, w_ref, kv_ref):
    kv_ref[...] = _dot(_rms(m_ref[...], g_ref[...]).astype(BF16), w_ref[...]).astype(BF16)


def _mem_kv(mem2d, g, w_kv, rows):
    t, d = mem2d.shape
    n = w_kv.shape[1]
    return pl.pallas_call(
        _mem_kv_kernel,
        grid=(t // rows,),
        in_specs=[pl.BlockSpec((rows, d), lambda i: (i, 0)), pl.BlockSpec((1, d), lambda i: (0, 0)),
                  pl.BlockSpec((d, n), lambda i: (0, 0))],
        out_specs=pl.BlockSpec((rows, n), lambda i: (i, 0)),
        out_shape=jax.ShapeDtypeStruct((t, n), BF16),
        compiler_params=_params("arbitrary"),
        name="mem_kv",
    )(mem2d, g.reshape(1, d), w_kv.astype(BF16))


def _mid_kernel(x_ref, om_ref, og_ref, mg_ref, wout_ref, xg_ref, wq_ref, kv_ref, wo_ref, fg_ref, rw_ref, rb_ref,
                x2_ref, h3_ref, route_ref, *, mw, xw):
    mo = _rms(om_ref[...], mg_ref[...]).astype(BF16)
    x1 = x_ref[...] + _dot(mo, wout_ref[:mw, :]) + _dot(og_ref[...].astype(BF16), wout_ref[mw:, :])

    h2 = _rms(x1, xg_ref[...]).astype(BF16)
    q = (_dot(h2, wq_ref[...]) * (XATTN_HEAD_DIM ** -0.5)).astype(BF16)
    head_cols = [slice(h * XATTN_HEAD_DIM, (h + 1) * XATTN_HEAD_DIM) for h in range(xw // XATTN_HEAD_DIM)]
    scores = [_dot_nt(q[:, sl], kv_ref[:, sl]) for sl in head_cols]
    probs = [jnp.exp(s - jnp.max(s, axis=1, keepdims=True)) for s in scores]
    heads = [_dot(p.astype(BF16), kv_ref[:, xw + sl.start:xw + sl.stop]) / jnp.sum(p, axis=1, keepdims=True)
             for p, sl in zip(probs, head_cols)]
    x2 = x1 + _dot(jnp.concatenate(heads, axis=1).astype(BF16), wo_ref[...])
    x2_ref[...] = x2

    h3 = _rms(x2, fg_ref[...])
    slabs = h3.shape[1] // LANES
    for j in range(slabs):
        h3_ref[pl.ds(j, h3.shape[0], stride=slabs), :] = h3[:, j * LANES:(j + 1) * LANES]
    ne = rb_ref.shape[0]
    h_hi = h3.astype(BF16)
    h_lo = (h3 - h_hi.astype(F32)).astype(BF16)
    by_hi = _dot_nt(rw_ref[...], h_hi)
    logits = by_hi[:ne, :] + by_hi[ne:, :] + _dot_nt(rw_ref[:ne, :], h_lo) + rb_ref[...]
    row = lax.broadcasted_iota(jnp.int32, logits.shape, 0)
    weights, picks, top = [], [], None
    for kk in range(TOP_K):
        m = jnp.max(logits, axis=0, keepdims=True)
        idx = jnp.min(jnp.where(logits == m, row, ne), axis=0, keepdims=True)
        logits = jnp.where(row == idx, NEG_INF, logits)
        top = m if top is None else top
        weights.append(jnp.exp(m - top))
        picks.append(idx.astype(F32))
    denom = sum(weights[1:], weights[0])
    rows = [wk / denom for wk in weights] + picks
    rows.append(jnp.zeros((LANES - len(rows), logits.shape[1]), F32))
    route_ref[...] = jnp.concatenate(rows, axis=0).T


def _mid(x2d, om, og, moba_g, w_out, xattn_g, w_q, kv, w_o, ffn_g, router_w, router_b, seq, mem_len):
    t, d = x2d.shape
    mw, gw, xw, ne = om.shape[1], og.shape[1], w_q.shape[1], router_w.shape[1]
    tiles_per_seq = seq // MID_ROWS
    row = lambda n: pl.BlockSpec((MID_ROWS, n), lambda i: (i, 0))
    const = lambda r, c: pl.BlockSpec((r, c), lambda i: (0, 0))
    rw_hi = router_w.T.astype(BF16)
    rw_lo = (router_w.T - rw_hi.astype(F32)).astype(BF16)
    return pl.pallas_call(
        functools.partial(_mid_kernel, mw=mw, xw=xw),
        grid=(t // MID_ROWS,),
        in_specs=[row(d), row(mw), row(gw), const(1, mw), const(mw + gw, d), const(1, d), const(d, xw),
                  pl.BlockSpec((mem_len, 2 * xw), lambda i: (i // tiles_per_seq, 0)),
                  const(xw, d), const(1, d), const(2 * ne, d), const(ne, 1)],
        out_specs=[row(d), pl.BlockSpec((MID_ROWS * (d // LANES), LANES), lambda i: (i, 0)), row(LANES)],
        out_shape=[jax.ShapeDtypeStruct((t, d), F32), jax.ShapeDtypeStruct((t * (d // LANES), LANES), F32),
                   jax.ShapeDtypeStruct((t, LANES), F32)],
        compiler_params=_params("arbitrary"),
        name="mid",
    )(x2d, om, og, moba_g.reshape(1, mw), w_out.astype(BF16), xattn_g.reshape(1, d), w_q.astype(BF16), kv,
      w_o.astype(BF16), ffn_g.reshape(1, d), jnp.concatenate([rw_hi, rw_lo], axis=0), router_b.reshape(ne, 1))


def _moe_kernel(be_ref, bv_ref, tok_ref, tokn_ref, dst_ref, h_hbm, wgu_ref, bgu_ref, wd_ref, bd_ref, y_hbm,
                xbuf, ybuf, wgu_b16, wd_b16, sem_in, sem_out, *, dff):
    blk = pl.program_id(0)
    nv = bv_ref[blk]
    nv_next = bv_ref[blk + 1]
    slot = blk % 2
    rows = MOE_ROWS
    chunk = MOE_CHUNK
    n_chunks = rows // chunk
    rt = ROW_TILE

    def row_in(s, r, t):
        return pltpu.make_async_copy(h_hbm.at[pl.ds(pl.multiple_of(t, rt), rt), :],
                                     xbuf.at[s, pl.ds(r * rt, rt), :], sem_in.at[s])

    def row_out(s, r, d):
        return pltpu.make_async_copy(ybuf.at[s, pl.ds(r * rt, rt), :],
                                     y_hbm.at[pl.ds(pl.multiple_of(d, rt), rt), :], sem_out.at[s])

    def wait_gather(s):
        pltpu.make_async_copy(h_hbm.at[pl.ds(0, rows * rt), :], xbuf.at[s], sem_in.at[s]).wait()

    def wait_scatter(s):
        pltpu.make_async_copy(ybuf.at[s], y_hbm.at[pl.ds(0, rows * rt), :], sem_out.at[s]).wait()

    @pl.when(blk == 0)
    def _():
        ybuf[1] = jnp.zeros(ybuf.shape[1:], F32)
        spare = pltpu.make_async_copy(ybuf.at[1], y_hbm.at[pl.ds(y_hbm.shape[0] - rows * rt, rows * rt), :],
                                      sem_out.at[1])
        spare.start()
        spare.wait()

    @pl.when((blk == 0) & (nv > 0))
    def _():
        for r in range(rows):
            row_in(0, r, tok_ref[0, 0, r]).start(priority=r % 2)

    @pl.when(nv > 0)
    def _():
        @pl.when((blk == 0) | (be_ref[blk] != be_ref[jnp.maximum(blk - 1, 0)]))
        def _():
            wgu_b16[...] = wgu_ref[0].astype(BF16)
            wd_b16[...] = wd_ref[0].astype(BF16)

        wait_gather(slot)

        @pl.when(blk >= 2)
        def _():
            wait_scatter(slot)

        slab = lambda c, j: pl.ds(c * chunk * rt + j, chunk, stride=rt)
        load_x = lambda c: jnp.concatenate([xbuf[slot, slab(c, j), :] for j in range(rt)], axis=1).astype(BF16)
        per_chunk_in = rows // (n_chunks // 2)
        x_next = load_x(0)
        for c in range(n_chunks):
            x = x_next
            if c < n_chunks // 2:
                for r in range(c * per_chunk_in, (c + 1) * per_chunk_in):
                    row_in(1 - slot, r, tokn_ref[0, 0, r]).start(priority=r % 2)
            if c:
                for r in range((c - 1) * chunk, c * chunk):
                    row_out(slot, r, dst_ref[0, 0, r]).start(priority=r % 2)
            gu = _dot(x, wgu_b16[...]) + bgu_ref[0]
            gate = jnp.minimum(gu[:, :dff], SWIGLU_LIMIT)
            up = jnp.clip(gu[:, dff:], -SWIGLU_LIMIT, SWIGLU_LIMIT)
            act = (up + 1.0) * gate * _sigmoid(SWIGLU_ALPHA * gate)
            y = _dot(act.astype(BF16), wd_b16[...]) + bd_ref[0]
            if c + 1 < n_chunks:
                x_next = load_x(c + 1)
            for j in range(rt):
                ybuf[slot, slab(c, j), :] = y[:, j * LANES:(j + 1) * LANES]
        for r in range(rows - chunk, rows):
            row_out(slot, r, dst_ref[0, 0, r]).start(priority=r % 2)

        @pl.when(nv_next == 0)
        def _():
            wait_gather(1 - slot)

            @pl.when(blk >= 1)
            def _():
                wait_scatter(1 - slot)
            wait_scatter(slot)


def _moe(h3_tiles, block_expert, block_valid, row_tok, row_dst, w_gu, b_gu, w_d, b_d):
    ne, d, n2 = w_gu.shape
    assert d == ROW_TILE * LANES
    t = h3_tiles.shape[0] // ROW_TILE
    dff = n2 // 2
    nblk = block_expert.shape[0]
    idx_blk = pl.BlockSpec((1, 1, MOE_ROWS), lambda i, be, bv: (i, 0, 0), memory_space=pltpu.SMEM)
    idx_next = pl.BlockSpec((1, 1, MOE_ROWS), lambda i, be, bv: (jnp.minimum(i + 1, nblk - 1), 0, 0),
                            memory_space=pltpu.SMEM)
    grid_spec = pltpu.PrefetchScalarGridSpec(
        num_scalar_prefetch=2,
        grid=(nblk,),
        in_specs=[idx_blk, idx_next, idx_blk, pl.BlockSpec(memory_space=pl.ANY),
                  pl.BlockSpec((1, d, n2), lambda i, be, bv: (be[i], 0, 0)),
                  pl.BlockSpec((1, 1, n2), lambda i, be, bv: (be[i], 0, 0)),
                  pl.BlockSpec((1, dff, d), lambda i, be, bv: (be[i], 0, 0)),
                  pl.BlockSpec((1, 1, d), lambda i, be, bv: (be[i], 0, 0))],
        out_specs=pl.BlockSpec(memory_space=pl.ANY),
        scratch_shapes=[pltpu.VMEM((2, MOE_ROWS * ROW_TILE, LANES), F32),
                        pltpu.VMEM((2, MOE_ROWS * ROW_TILE, LANES), F32),
                        pltpu.VMEM((d, n2), BF16), pltpu.VMEM((dff, d), BF16),
                        pltpu.SemaphoreType.DMA((2,)), pltpu.SemaphoreType.DMA((2,))],
    )
    tok3 = (row_tok * ROW_TILE).reshape(nblk, 1, MOE_ROWS)
    valid_ext = jnp.concatenate([block_valid, jnp.zeros((1,), jnp.int32)])
    return pl.pallas_call(
        functools.partial(_moe_kernel, dff=dff),
        grid_spec=grid_spec,
        out_shape=jax.ShapeDtypeStruct(((TOP_K * t + MOE_ROWS) * ROW_TILE, LANES), F32),
        compiler_params=_params("arbitrary"),
        name="moe",
    )(block_expert, valid_ext, tok3, tok3, (row_dst * ROW_TILE).reshape(nblk, 1, MOE_ROWS), h3_tiles,
      w_gu, b_gu.reshape(ne, 1, n2), w_d, b_d.reshape(ne, 1, d))


def _route_plan(expert, t):
    n_pairs = t * TOP_K
    e_flat = expert.reshape(-1)
    order = jnp.argsort(e_flat, stable=True).astype(jnp.int32)
    experts = jnp.arange(N_EXPERTS, dtype=jnp.int32)
    counts = jnp.sum((e_flat[:, None] == experts[None, :]).astype(jnp.int32), axis=0)
    padded = (counts + MOE_ROWS - 1) // MOE_ROWS * MOE_ROWS
    pend = jnp.cumsum(padded)
    pstart = pend - padded
    gstart = jnp.cumsum(counts) - counts
    nblk = n_pairs // MOE_ROWS + N_EXPERTS
    blk_row0 = jnp.arange(nblk, dtype=jnp.int32) * MOE_ROWS
    block_expert = jnp.minimum(jnp.sum((pend[None, :] <= blk_row0[:, None]).astype(jnp.int32), axis=1), N_EXPERTS - 1)
    pick = lambda table: jnp.sum(jnp.where(block_expert[:, None] == experts[None, :], table[None, :], 0), axis=1)
    in_group = blk_row0 - pick(pstart)
    block_valid = jnp.where(blk_row0 < pend[-1], jnp.clip(pick(counts) - in_group, 0, MOE_ROWS), 0)
    first_sorted = pick(gstart) + in_group
    local = jnp.arange(MOE_ROWS, dtype=jnp.int32)
    srt = first_sorted[:, None] + local[None, :]
    pair = order[jnp.clip(srt, 0, n_pairs - 1)]
    row_tok = pair // TOP_K
    row_dst = jnp.where(local[None, :] < block_valid[:, None], (pair % TOP_K) * t + row_tok, n_pairs + local[None, :])
    return (block_expert.astype(jnp.int32), block_valid.astype(jnp.int32), row_tok.astype(jnp.int32),
            row_dst.astype(jnp.int32))


def _combine_kernel(x2_ref, route_ref, g_ref, *rest, final):
    y_refs, o_ref = rest[:TOP_K], rest[TOP_K]
    route = route_ref[...]
    gates = [route[:, kk:kk + 1] for kk in range(TOP_K)]
    slabs = []
    for j in range(ROW_TILE):
        acc = x2_ref[:, j * LANES:(j + 1) * LANES]
        for kk in range(TOP_K):
            acc = acc + gates[kk] * y_refs[kk][pl.ds(j, x2_ref.shape[0], stride=ROW_TILE), :]
        slabs.append(acc)
    out = jnp.concatenate(slabs, axis=1)
    o_ref[...] = _rms(out, g_ref[...]) if final else out


def _combine(x2, route, y, g, final):
    t, d = x2.shape
    tiles = t // MID_ROWS
    slot_spec = lambda kk: pl.BlockSpec((MID_ROWS * ROW_TILE, LANES), lambda i: (kk * tiles + i, 0))
    return pl.pallas_call(
        functools.partial(_combine_kernel, final=final),
        grid=(tiles,),
        in_specs=[pl.BlockSpec((MID_ROWS, d), lambda i: (i, 0)), pl.BlockSpec((MID_ROWS, LANES), lambda i: (i, 0)),
                  pl.BlockSpec((1, d), lambda i: (0, 0))] + [slot_spec(kk) for kk in range(TOP_K)],
        out_specs=pl.BlockSpec((MID_ROWS, d), lambda i: (i, 0)),
        out_shape=jax.ShapeDtypeStruct((t, d), F32),
        compiler_params=_params("arbitrary"),
        name="combine",
    )(x2, route, g.reshape(1, d), *([y] * TOP_K))


def kernel(x, mem, norm_mix_g, w_in, gdn_conv_w, gdn_A_log, gdn_dt_bias, gdn_norm_g, moba_norm_g, w_out,
           norm_xattn_g, norm_mem_g, xattn_w_q, xattn_w_kv, xattn_w_o, norm_ffn_g, router_w, router_b,
           w_gate_up, b_gate_up, w_down, b_down, final_norm_g):
    b, s, d = x.shape
    t = b * s
    mem_len = mem.shape[1]
    mw = moba_norm_g.shape[1]
    gw = GDN_HEADS * GDN_HEAD_DIM
    xcur = x.reshape(t, d)
    for l in range(w_in.shape[0]):
        qt, mk, mv, gqkv, gz, gba = _in_proj(xcur, norm_mix_g[l], w_in[l], mw, gw)
        o_moba = _moba(qt, mk, mv, b, s)
        o_gdn = _gdn(gqkv.reshape(b, s, 3 * gw), gz.reshape(b, s, gw), gba.reshape(b, s, LANES),
                     gdn_conv_w[l], gdn_A_log[l], gdn_dt_bias[l], gdn_norm_g[l])
        kv = _mem_kv(mem.reshape(b * mem_len, d), norm_mem_g[l], xattn_w_kv[l], mem_len)
        x2, h3, route = _mid(xcur, o_moba.reshape(t, mw), o_gdn.reshape(t, gw), moba_norm_g[l], w_out[l],
                             norm_xattn_g[l], xattn_w_q[l], kv, xattn_w_o[l], norm_ffn_g[l], router_w[l],
                             router_b[l], s, mem_len)
        expert = route[:, TOP_K:2 * TOP_K].astype(jnp.int32)
        plan = _route_plan(expert, t)
        y = _moe(h3, *plan, w_gate_up[l], b_gate_up[l], w_down[l], b_down[l])
        xcur = _combine(x2, route, y, final_norm_g, l == w_in.shape[0] - 1)
    return xcur.reshape(b, s, d)
```

```python
import functools

import jax
import jax.numpy as jnp
from jax import lax
from jax.experimental import pallas as pl
from jax.experimental.pallas import tpu as pltpu

F32 = jnp.float32
BF16 = jnp.bfloat16

RMS_EPS = 1e-6
MOBA_HEAD_DIM = 64
MOBA_BLOCK = 256
MOBA_TOPK = 3
GDN_HEAD_DIM = 128
GDN_HEADS = 4
GDN_CONV = 4
GDN_CHUNK = 64
XATTN_HEAD_DIM = 128
N_EXPERTS = 32
TOP_K = 4
SWIGLU_LIMIT = 7.0
SWIGLU_ALPHA = 1.702

LANES = 128
ROW_TILE = 8
VMEM_LIMIT = 56 * 1024 * 1024

IN_ROWS = 512
MID_ROWS = 512
MOE_ROWS = 512
MOE_CHUNK = 128
GDN_GROUP = 32
PREP_ROWS = 256
SCAN_BATCH = 2
SCAN_ROWS = 1024
NEG_INF = float("-inf")
MOBA_Q_SCALE = 1.4426950408889634 / MOBA_HEAD_DIM ** 0.5
MOBA_MASKED = -1e30


def _params(*sem):
    return pltpu.CompilerParams(dimension_semantics=sem, vmem_limit_bytes=VMEM_LIMIT)


def _rms(x, g):
    return x * lax.rsqrt(jnp.mean(x * x, axis=-1, keepdims=True) + RMS_EPS) * g


def _dot(a, b):
    return jnp.dot(a, b, preferred_element_type=F32)


def _dot_nt(a, b):
    return lax.dot_general(a, b, (((1,), (1,)), ((), ())), preferred_element_type=F32)


def _sigmoid(x):
    return 1.0 / (1.0 + jnp.exp(-x))


def _in_proj_kernel(x_ref, g_ref, w_ref, wt_ref, qt_ref, mk_ref, mv_ref, gqkv_ref, gz_ref, gba_ref, *, mw, gw):
    hn = _rms(x_ref[...], g_ref[...]).astype(BF16)
    mm = lambda lo, hi: _dot(hn, w_ref[:, lo:hi])
    qt_ref[...] = (_dot_nt(wt_ref[...], hn) * MOBA_Q_SCALE).astype(BF16)
    mk_ref[...] = mm(0, mw).astype(BF16)
    mv_ref[...] = mm(mw, 2 * mw).astype(BF16)
    gqkv_ref[...] = mm(2 * mw, 2 * mw + 3 * gw)
    gz_ref[...] = mm(2 * mw + 3 * gw, 2 * mw + 4 * gw)
    gba_ref[...] = mm(2 * mw + 4 * gw, 2 * mw + 4 * gw + LANES)


def _in_proj(x2d, g, w_in, mw, gw):
    t, d = x2d.shape
    n_real = w_in.shape[1] - mw
    n_pad = 2 * mw + 4 * gw + LANES
    w = jnp.pad(w_in[:, mw:], ((0, 0), (0, n_pad - n_real))).astype(BF16)
    w_t = w_in[:, :mw].T.astype(BF16)
    row = lambda n: pl.BlockSpec((IN_ROWS, n), lambda i: (i, 0))
    return pl.pallas_call(
        functools.partial(_in_proj_kernel, mw=mw, gw=gw),
        grid=(t // IN_ROWS,),
        in_specs=[row(d), pl.BlockSpec((1, d), lambda i: (0, 0)), pl.BlockSpec((d, n_pad), lambda i: (0, 0)),
                  pl.BlockSpec((mw, d), lambda i: (0, 0))],
        out_specs=[pl.BlockSpec((mw, IN_ROWS), lambda i: (0, i)), row(mw), row(mw), row(3 * gw), row(gw),
                   row(LANES)],
        out_shape=[jax.ShapeDtypeStruct((mw, t), BF16), jax.ShapeDtypeStruct((t, mw), BF16),
                   jax.ShapeDtypeStruct((t, mw), BF16),
                   jax.ShapeDtypeStruct((t, 3 * gw), F32), jax.ShapeDtypeStruct((t, gw), F32),
                   jax.ShapeDtypeStruct((t, LANES), F32)],
        compiler_params=_params("arbitrary"),
        name="in_proj",
    )(x2d, g.reshape(1, d), w, w_t)


def _moba_select(g_t, i):
    nb = g_t.shape[0]
    row = lax.broadcasted_iota(jnp.int32, g_t.shape, 0)
    valid = row < i
    sel = jnp.zeros_like(g_t)
    for j in range(nb):
        gj = g_t[j:j + 1, :]
        beats = valid & ((g_t > gj) | ((g_t == gj) & (row < j)))
        rank = jnp.sum(jnp.where(beats, 1.0, 0.0), axis=0, keepdims=True)
        sel = jnp.where(row == j, jnp.where(rank < MOBA_TOPK, 1.0, 0.0), sel)
    return jnp.where(valid, sel, 0.0)


def _dot_tn(a, b):
    return lax.dot_general(a, b, (((0,), (0,)), ((), ())), preferred_element_type=F32)


def _moba_kernel(qt_ref, qt_all_ref, k_ref, v_ref, o_ref, kaug_ref, bias_ref, *, nb):
    i = pl.program_id(2)
    bs = MOBA_BLOCK
    hd = MOBA_HEAD_DIM
    lane = lax.broadcasted_iota(jnp.int32, (1, LANES), 1)
    low = lax.broadcasted_iota(jnp.int32, (LANES, 1), 0) < hd

    @pl.when(i == 0)
    def _():
        kmean = []
        for j in range(nb):
            rows = slice(j * bs, (j + 1) * bs)
            kb = k_ref[0, rows, :]
            kmean.append(jnp.mean(kb.astype(F32), axis=0, keepdims=True))
            kaug_ref[0, rows, :] = jnp.where(lane < hd, kb, jnp.where(lane == hd + j, 1.0, 0.0).astype(BF16))
            kaug_ref[1, rows, :] = jnp.where(lane >= hd, kb, jnp.where(lane == j, 1.0, 0.0).astype(BF16))
        kmean = jnp.concatenate(kmean, axis=0)
        q_all = qt_all_ref[...].astype(F32)
        for h, qh in enumerate((jnp.where(low, q_all, 0.0), jnp.where(low, 0.0, q_all))):
            gate = jnp.dot(kmean, qh, preferred_element_type=F32, precision=lax.Precision.HIGHEST)
            for jq in range(nb):
                sel = _moba_select(gate[:, jq * bs:(jq + 1) * bs], jq)
                bias_ref[h, jq] = jnp.where(sel > 0.5, 0.0, MOBA_MASKED)

    qt = qt_ref[...]
    qtf = qt.astype(F32)
    bias = [bias_ref[h, i] for h in (0, 1)]
    pad = jnp.zeros((hd - nb, bs), F32)
    q_past = (jnp.concatenate([qtf[:hd], bias[0], pad], axis=0).astype(BF16),
              jnp.concatenate([bias[1], pad, qtf[hd:]], axis=0).astype(BF16))
    zero = jnp.zeros_like(qt)
    q_own = (jnp.where(low, qt, zero), jnp.where(low, zero, qt))

    key_ix = lax.broadcasted_iota(jnp.int32, (bs, bs), 0)
    qry_ix = lax.broadcasted_iota(jnp.int32, (bs, bs), 1)
    causal_bias = jnp.where(key_ix <= qry_ix, 0.0, NEG_INF)
    own = pl.ds(pl.multiple_of(i * bs, bs), bs)
    k_own = k_ref[0, own, :]
    v_own = v_ref[0, own, :]

    def attend(width):
        heads = (0, 1)
        s_own = [_dot(k_own, q_own[h]) + causal_bias for h in heads]
        m = [s.max(axis=0, keepdims=True) for s in s_own]
        parts = [slice(lo * bs, min(lo + 2, width) * bs) for lo in range(0, width, 2)]
        s_past = []
        for keys in parts:
            s_part = [_dot(kaug_ref[h, keys, :], q_past[h]) for h in heads]
            m = [jnp.maximum(m[h], s_part[h].max(axis=0, keepdims=True)) for h in heads]
            s_past.append(s_part)
        p_own = [jnp.exp2(s_own[h] - m[h]) for h in heads]
        l = [p.sum(axis=0, keepdims=True) for p in p_own]
        acc = [_dot_tn(v_own, p_own[h].astype(BF16)) for h in heads]
        for keys, s_part in zip(parts, s_past):
            p_part = [jnp.exp2(s_part[h] - m[h]) for h in heads]
            l = [l[h] + p_part[h].sum(axis=0, keepdims=True) for h in heads]
            acc = [acc[h] + _dot_tn(v_ref[0, keys, :], p_part[h].astype(BF16)) for h in heads]
        o_ref[0] = jnp.where(low, acc[0] / l[0], acc[1] / l[1]).T

    for width in range(nb):
        pl.when(i == width)(functools.partial(attend, width))


def _moba(qt, mk, mv, b, s):
    mw = mk.shape[-1]
    nb = s // MOBA_BLOCK
    seq_blk = pl.BlockSpec((1, s, LANES), lambda bi, hp, i: (bi, 0, hp))
    return pl.pallas_call(
        functools.partial(_moba_kernel, nb=nb),
        grid=(b, mw // LANES, nb),
        in_specs=[pl.BlockSpec((LANES, MOBA_BLOCK), lambda bi, hp, i: (hp, bi * nb + i)),
                  pl.BlockSpec((LANES, s), lambda bi, hp, i: (hp, bi)), seq_blk, seq_blk],
        out_specs=pl.BlockSpec((1, MOBA_BLOCK, LANES), lambda bi, hp, i: (bi, i, hp)),
        out_shape=jax.ShapeDtypeStruct((b, s, mw), F32),
        scratch_shapes=[pltpu.VMEM((2, s, LANES), BF16), pltpu.VMEM((2, nb, nb, MOBA_BLOCK), F32)],
        compiler_params=_params("arbitrary", "arbitrary", "arbitrary"),
        name="moba",
    )(qt, qt, mk.reshape(b, s, mw), mv.reshape(b, s, mw))


def _gdn_gate_kernel(ba_ref, alog_ref, dtb_ref, bg_ref, *, seq):
    x = ba_ref[0]
    lane = lax.broadcasted_iota(jnp.int32, (1, LANES), 1)
    xa = x + dtb_ref[...]
    softplus = jnp.maximum(xa, 0.0) + jnp.log(1.0 + jnp.exp(-jnp.abs(xa)))
    g = jnp.where((lane >= GDN_HEADS) & (lane < 2 * GDN_HEADS), -jnp.exp(alog_ref[...]) * softplus, 0.0)
    pos = lax.broadcasted_iota(jnp.int32, (seq, 1), 0) % GDN_CHUNK
    sft = 1
    while sft < GDN_CHUNK:
        g = g + jnp.where(pos >= sft, pltpu.roll(g, sft, 0), 0.0)
        sft *= 2
    bg_ref[0] = jnp.where(lane < GDN_HEADS, _sigmoid(x), g)


def _gdn_prep_kernel(x_ref, w_ref, o_ref, *, seq):
    cb = pl.program_id(1)
    is_qk = cb < 2 * GDN_HEADS
    scale = jnp.where(cb < GDN_HEADS, GDN_HEAD_DIM ** -0.5, 1.0)
    taps = [w_ref[j:j + 1, :] for j in range(GDN_CONV)]
    halo = 8
    for r0 in range(0, seq, PREP_ROWS):
        if r0 == 0:
            xe = jnp.concatenate([jnp.zeros((halo, LANES), F32), x_ref[0, :PREP_ROWS, :]], axis=0)
        else:
            xe = x_ref[0, r0 - halo:r0 + PREP_ROWS, :]
        y = taps[GDN_CONV - 1] * xe[halo:, :]
        for sft in range(1, GDN_CONV):
            y = y + taps[GDN_CONV - 1 - sft] * xe[halo - sft:halo - sft + PREP_ROWS, :]
        y = y * _sigmoid(y)
        normed = y * (lax.rsqrt(jnp.sum(y * y, axis=-1, keepdims=True) + RMS_EPS) * scale)
        o_ref[0, r0:r0 + PREP_ROWS, :] = jnp.where(is_qk, normed, y).astype(BF16)


def _unit_lower_inverses(mats):
    c = mats[0].shape[0]
    r = lax.broadcasted_iota(jnp.int32, (c, c), 0)
    cc = lax.broadcasted_iota(jnp.int32, (c, c), 1)
    eye = jnp.where(r == cc, 1.0, 0.0)
    pair = (r // 2) == (cc // 2)
    invs = [eye - jnp.where(pair, a, 0.0) for a in mats]
    size = 4
    while size <= c:
        level = ((r // size) == (cc // size)) & ((r // (size // 2)) != (cc // (size // 2)))
        inv_b = [inv.astype(BF16) for inv in invs]
        left = [_dot(ib, jnp.where(level, a, 0.0).astype(BF16)).astype(BF16) for ib, a in zip(inv_b, mats)]
        invs = [inv - _dot(lf, ib) for inv, lf, ib in zip(invs, left, inv_b)]
        size *= 2
    return invs


def _gdn_intra_kernel(q_ref, k_ref, v_ref, bg_ref, u_ref, w_ref, qd_ref, kd_ref, qk_ref):
    h = pl.program_id(1)
    c = GDN_CHUNK
    chunks = range(GDN_GROUP)
    lane = lax.broadcasted_iota(jnp.int32, (1, LANES), 1)
    r_ix = lax.broadcasted_iota(jnp.int32, (c, c), 0)
    c_ix = lax.broadcasted_iota(jnp.int32, (c, c), 1)
    rows = [slice(gi * c, (gi + 1) * c) for gi in chunks]
    k_b16 = [k_ref[0, rw, :] for rw in rows]
    q_b16 = [q_ref[0, rw, :] for rw in rows]
    bgs = [bg_ref[0, rw, :] for rw in rows]
    beta = [jnp.sum(jnp.where(lane == h, bg, 0.0), axis=1, keepdims=True) for bg in bgs]
    gam = [jnp.sum(jnp.where(lane == GDN_HEADS + h, bg, 0.0), axis=1, keepdims=True) for bg in bgs]
    kb = [kk.astype(F32) * bt for kk, bt in zip(k_b16, beta)]
    kk_raw = [_dot_nt(x.astype(BF16), kk) for x, kk in zip(kb, k_b16)]
    qk_raw = [_dot_nt(qq, kk) for qq, kk in zip(q_b16, k_b16)]
    decay = []
    for gm in gam:
        gam_r = jnp.sum(jnp.where(r_ix == c_ix, gm, 0.0), axis=0, keepdims=True)
        decay.append(jnp.exp(jnp.where(c_ix <= r_ix, gm - gam_r, NEG_INF)))
    t_inv = _unit_lower_inverses([jnp.where(c_ix < r_ix, x * dc, 0.0) for x, dc in zip(kk_raw, decay)])
    eg = [jnp.exp(gm) for gm in gam]
    rhs = [jnp.concatenate([v_ref[0, rw, :].astype(F32) * bt, x * e], axis=1).astype(BF16)
           for rw, bt, x, e in zip(rows, beta, kb, eg)]
    uw = [_dot(ti.astype(BF16), rh) for ti, rh in zip(t_inv, rhs)]
    for gi in chunks:
        rw = rows[gi]
        u_ref[0, rw, :] = uw[gi][:, :LANES].astype(BF16)
        w_ref[0, rw, :] = uw[gi][:, LANES:].astype(BF16)
        qk_ref[0, 0, rw, :] = (qk_raw[gi] * decay[gi]).astype(BF16)
        qd_ref[0, rw, :] = (q_b16[gi].astype(F32) * eg[gi]).astype(BF16)
        kd_ref[0, rw, :] = (k_b16[gi].astype(F32) * jnp.exp(gam[gi][c - 1:c, :] - gam[gi])).astype(BF16)


def _gdn_scan_kernel(u_ref, w_ref, qd_ref, kd_ref, qk_ref, z_ref, bg_ref, ng_ref, o_ref, state_ref, *, rows_per_step):
    c = GDN_CHUNK
    ng = ng_ref[...]
    chains = [(bi, h) for bi in range(SCAN_BATCH) for h in range(GDN_HEADS)]
    cols = [slice(h * GDN_HEAD_DIM, (h + 1) * GDN_HEAD_DIM) for h in range(GDN_HEADS)]

    @pl.when(pl.program_id(1) == 0)
    def _():
        state_ref[...] = jnp.zeros(state_ref.shape, F32)

    def step(n, states):
        r0 = pl.multiple_of(n * c, c)
        rows = pl.ds(r0, c)
        tails = [bg_ref[bi, pl.ds(r0 + c - 8, 8), :] for bi in range(SCAN_BATCH)]
        s_b = [st.astype(BF16) for st in states]
        ws = [_dot(w_ref[bi, rows, cols[h]], s_b[k]) for k, (bi, h) in enumerate(chains)]
        qs = [_dot(qd_ref[bi, rows, cols[h]], s_b[k]) for k, (bi, h) in enumerate(chains)]
        v_b = [(u_ref[bi, rows, cols[h]].astype(F32) - ws[k]).astype(BF16) for k, (bi, h) in enumerate(chains)]
        kd_v = [_dot_tn(kd_ref[bi, rows, cols[h]], v_b[k]) for k, (bi, h) in enumerate(chains)]
        qkv = [_dot(qk_ref[bi, h, rows, :], v_b[k]) for k, (bi, h) in enumerate(chains)]
        new_states = []
        for k, (bi, h) in enumerate(chains):
            g_last = tails[bi][7:8, GDN_HEADS + h:GDN_HEADS + h + 1]
            new_states.append(states[k] * jnp.exp(g_last) + kd_v[k])
            z = z_ref[bi, rows, cols[h]]
            o_ref[bi, rows, cols[h]] = _rms(qs[k] + qkv[k], ng) * (z * _sigmoid(z))
        return tuple(new_states)

    init = tuple(state_ref[k] for k in range(len(chains)))
    final = lax.fori_loop(0, rows_per_step // c, step, init)
    for k, st in enumerate(final):
        state_ref[k] = st


def _gdn(gqkv, gz, gba, conv_w, a_log, dt_bias, norm_g):
    b, s, w3 = gqkv.shape
    nh = GDN_HEADS
    hw = nh * GDN_HEAD_DIM
    c = GDN_CHUNK
    lane_pad = lambda v: jnp.pad(v.reshape(1, -1), ((0, 0), (nh, LANES - 2 * nh)))
    seq_blk = pl.BlockSpec((1, s, LANES), lambda bi: (bi, 0, 0))
    const = pl.BlockSpec((1, LANES), lambda bi: (0, 0))
    bg = pl.pallas_call(
        functools.partial(_gdn_gate_kernel, seq=s),
        grid=(b,),
        in_specs=[seq_blk, const, const],
        out_specs=seq_blk,
        out_shape=jax.ShapeDtypeStruct((b, s, LANES), F32),
        compiler_params=_params("arbitrary"),
        name="gdn_gate",
    )(gba, lane_pad(a_log), lane_pad(dt_bias))

    qkv = pl.pallas_call(
        functools.partial(_gdn_prep_kernel, seq=s),
        grid=(b, w3 // LANES),
        in_specs=[pl.BlockSpec((1, s, LANES), lambda bi, cb: (bi, 0, cb)),
                  pl.BlockSpec((GDN_CONV, LANES), lambda bi, cb: (0, cb))],
        out_specs=pl.BlockSpec((1, s, LANES), lambda bi, cb: (bi, 0, cb)),
        out_shape=jax.ShapeDtypeStruct((b, s, w3), BF16),
        compiler_params=_params("arbitrary", "arbitrary"),
        name="gdn_prep",
    )(gqkv, conv_w)

    grp = GDN_GROUP * c
    head_blk = lambda off: pl.BlockSpec((1, grp, LANES), lambda bi, h, n: (bi, n, off + h))
    head_shape = jax.ShapeDtypeStruct((b, s, hw), BF16)
    u, w, qd, kd, qk = pl.pallas_call(
        _gdn_intra_kernel,
        grid=(b, nh, s // grp),
        in_specs=[head_blk(0), head_blk(nh), head_blk(2 * nh),
                  pl.BlockSpec((1, grp, LANES), lambda bi, h, n: (bi, n, 0))],
        out_specs=[head_blk(0)] * 4 + [pl.BlockSpec((1, 1, grp, c), lambda bi, h, n: (bi, h, n, 0))],
        out_shape=[head_shape] * 4 + [jax.ShapeDtypeStruct((b, nh, s, c), BF16)],
        compiler_params=_params("arbitrary", "arbitrary", "arbitrary"),
        name="gdn_intra",
    )(qkv, qkv, qkv, bg)

    tile = lambda width: pl.BlockSpec((SCAN_BATCH, SCAN_ROWS, width), lambda bi, ti: (bi, ti, 0))
    return pl.pallas_call(
        functools.partial(_gdn_scan_kernel, rows_per_step=SCAN_ROWS),
        grid=(b // SCAN_BATCH, s // SCAN_ROWS),
        in_specs=[tile(hw), tile(hw), tile(hw), tile(hw),
                  pl.BlockSpec((SCAN_BATCH, nh, SCAN_ROWS, c), lambda bi, ti: (bi, 0, ti, 0)), tile(hw), tile(LANES),
                  pl.BlockSpec((1, LANES), lambda bi, ti: (0, 0))],
        out_specs=tile(hw),
        out_shape=jax.ShapeDtypeStruct((b, s, hw), F32),
        scratch_shapes=[pltpu.VMEM((SCAN_BATCH * nh, GDN_HEAD_DIM, GDN_HEAD_DIM), F32)],
        compiler_params=_params("arbitrary", "arbitrary"),
        name="gdn_scan",
    )(u, w, qd, kd, qk, gz, bg, norm_g.reshape(1, -1))


def _mem_kv_kernel(m_ref, g_ref, w_ref, kv_ref):
    kv_ref[...] = _dot(_rms(m_ref[...], g_ref[...]).astype(BF16), w_ref[...]).astype(BF16)


def _mem_kv(mem2d, g, w_kv, rows):
    t, d = mem2d.shape
    n = w_kv.shape[1]
    return pl.pallas_call(
        _mem_kv_kernel,
        grid=(t // rows,),
        in_specs=[pl.BlockSpec((rows, d), lambda i: (i, 0)), pl.BlockSpec((1, d), lambda i: (0, 0)),
                  pl.BlockSpec((d, n), lambda i: (0, 0))],
        out_specs=pl.BlockSpec((rows, n), lambda i: (i, 0)),
        out_shape=jax.ShapeDtypeStruct((t, n), BF16),
        compiler_params=_params("arbitrary"),
        name="mem_kv",
    )(mem2d, g.reshape(1, d), w_kv.astype(BF16))


def _mid_kernel(x_ref, om_ref, og_ref, mg_ref, wout_ref, xg_ref, wq_ref, kv_ref, wo_ref, fg_ref, rw_ref, rb_ref,
                x2_ref, h3_ref, route_ref, *, mw, xw):
    mo = _rms(om_ref[...], mg_ref[...]).astype(BF16)
    x1 = x_ref[...] + _dot(mo, wout_ref[:mw, :]) + _dot(og_ref[...].astype(BF16), wout_ref[mw:, :])

    h2 = _rms(x1, xg_ref[...]).astype(BF16)
    q = (_dot(h2, wq_ref[...]) * (XATTN_HEAD_DIM ** -0.5)).astype(BF16)
    head_cols = [slice(h * XATTN_HEAD_DIM, (h + 1) * XATTN_HEAD_DIM) for h in range(xw // XATTN_HEAD_DIM)]
    scores = [_dot_nt(q[:, sl], kv_ref[:, sl]) for sl in head_cols]
    probs = [jnp.exp(s - jnp.max(s, axis=1, keepdims=True)) for s in scores]
    heads = [_dot(p.astype(BF16), kv_ref[:, xw + sl.start:xw + sl.stop]) / jnp.sum(p, axis=1, keepdims=True)
             for p, sl in zip(probs, head_cols)]
    x2 = x1 + _dot(jnp.concatenate(heads, axis=1).astype(BF16), wo_ref[...])
    x2_ref[...] = x2

    h3 = _rms(x2, fg_ref[...])
    slabs = h3.shape[1] // LANES
    for j in range(slabs):
        h3_ref[pl.ds(j, h3.shape[0], stride=slabs), :] = h3[:, j * LANES:(j + 1) * LANES]
    ne = rb_ref.shape[0]
    h_hi = h3.astype(BF16)
    h_lo = (h3 - h_hi.astype(F32)).astype(BF16)
    by_hi = _dot_nt(rw_ref[...], h_hi)
    logits = by_hi[:ne, :] + by_hi[ne:, :] + _dot_nt(rw_ref[:ne, :], h_lo) + rb_ref[...]
    row = lax.broadcasted_iota(jnp.int32, logits.shape, 0)
    weights, picks, top = [], [], None
    for kk in range(TOP_K):
        m = jnp.max(logits, axis=0, keepdims=True)
        idx = jnp.min(jnp.where(logits == m, row, ne), axis=0, keepdims=True)
        logits = jnp.where(row == idx, NEG_INF, logits)
        top = m if top is None else top
        weights.append(jnp.exp(m - top))
        picks.append(idx.astype(F32))
    denom = sum(weights[1:], weights[0])
    rows = [wk / denom for wk in weights] + picks
    rows.append(jnp.zeros((LANES - len(rows), logits.shape[1]), F32))
    route_ref[...] = jnp.concatenate(rows, axis=0).T


def _mid(x2d, om, og, moba_g, w_out, xattn_g, w_q, kv, w_o, ffn_g, router_w, router_b, seq, mem_len):
    t, d = x2d.shape
    mw, gw, xw, ne = om.shape[1], og.shape[1], w_q.shape[1], router_w.shape[1]
    tiles_per_seq = seq // MID_ROWS
    row = lambda n: pl.BlockSpec((MID_ROWS, n), lambda i: (i, 0))
    const = lambda r, c: pl.BlockSpec((r, c), lambda i: (0, 0))
    rw_hi = router_w.T.astype(BF16)
    rw_lo = (router_w.T - rw_hi.astype(F32)).astype(BF16)
    return pl.pallas_call(
        functools.partial(_mid_kernel, mw=mw, xw=xw),
        grid=(t // MID_ROWS,),
        in_specs=[row(d), row(mw), row(gw), const(1, mw), const(mw + gw, d), const(1, d), const(d, xw),
                  pl.BlockSpec((mem_len, 2 * xw), lambda i: (i // tiles_per_seq, 0)),
                  const(xw, d), const(1, d), const(2 * ne, d), const(ne, 1)],
        out_specs=[row(d), pl.BlockSpec((MID_ROWS * (d // LANES), LANES), lambda i: (i, 0)), row(LANES)],
        out_shape=[jax.ShapeDtypeStruct((t, d), F32), jax.ShapeDtypeStruct((t * (d // LANES), LANES), F32),
                   jax.ShapeDtypeStruct((t, LANES), F32)],
        compiler_params=_params("arbitrary"),
        name="mid",
    )(x2d, om, og, moba_g.reshape(1, mw), w_out.astype(BF16), xattn_g.reshape(1, d), w_q.astype(BF16), kv,
      w_o.astype(BF16), ffn_g.reshape(1, d), jnp.concatenate([rw_hi, rw_lo], axis=0), router_b.reshape(ne, 1))


def _moe_kernel(be_ref, bv_ref, tok_ref, tokn_ref, dst_ref, h_hbm, wgu_ref, bgu_ref, wd_ref, bd_ref, y_hbm,
                xbuf, ybuf, wgu_b16, wd_b16, sem_in, sem_out, *, dff):
    blk = pl.program_id(0)
    nv = bv_ref[blk]
    nv_next = bv_ref[blk + 1]
    slot = blk % 2
    rows = MOE_ROWS
    chunk = MOE_CHUNK
    n_chunks = rows // chunk
    rt = ROW_TILE

    def row_in(s, r, t):
        return pltpu.make_async_copy(h_hbm.at[pl.ds(pl.multiple_of(t, rt), rt), :],
                                     xbuf.at[s, pl.ds(r * rt, rt), :], sem_in.at[s])

    def row_out(s, r, d):
        return pltpu.make_async_copy(ybuf.at[s, pl.ds(r * rt, rt), :],
                                     y_hbm.at[pl.ds(pl.multiple_of(d, rt), rt), :], sem_out.at[s])

    def wait_gather(s):
        pltpu.make_async_copy(h_hbm.at[pl.ds(0, rows * rt), :], xbuf.at[s], sem_in.at[s]).wait()

    def wait_scatter(s):
        pltpu.make_async_copy(ybuf.at[s], y_hbm.at[pl.ds(0, rows * rt), :], sem_out.at[s]).wait()

    @pl.when(blk == 0)
    def _():
        ybuf[1] = jnp.zeros(ybuf.shape[1:], F32)
        spare = pltpu.make_async_copy(ybuf.at[1], y_hbm.at[pl.ds(y_hbm.shape[0] - rows * rt, rows * rt), :],
                                      sem_out.at[1])
        spare.start()
        spare.wait()

    @pl.when((blk == 0) & (nv > 0))
    def _():
        for r in range(rows):
            row_in(0, r, tok_ref[0, 0, r]).start(priority=r % 2)

    @pl.when(nv > 0)
    def _():
        @pl.when((blk == 0) | (be_ref[blk] != be_ref[jnp.maximum(blk - 1, 0)]))
        def _():
            wgu_b16[...] = wgu_ref[0].astype(BF16)
            wd_b16[...] = wd_ref[0].astype(BF16)

        wait_gather(slot)

        @pl.when(blk >= 2)
        def _():
            wait_scatter(slot)

        slab = lambda c, j: pl.ds(c * chunk * rt + j, chunk, stride=rt)
        load_x = lambda c: jnp.concatenate([xbuf[slot, slab(c, j), :] for j in range(rt)], axis=1).astype(BF16)
        per_chunk_in = rows // (n_chunks // 2)
        x_next = load_x(0)
        for c in range(n_chunks):
            x = x_next
            if c < n_chunks // 2:
                for r in range(c * per_chunk_in, (c + 1) * per_chunk_in):
                    row_in(1 - slot, r, tokn_ref[0, 0, r]).start(priority=r % 2)
            if c:
                for r in range((c - 1) * chunk, c * chunk):
                    row_out(slot, r, dst_ref[0, 0, r]).start(priority=r % 2)
            gu = _dot(x, wgu_b16[...]) + bgu_ref[0]
            gate = jnp.minimum(gu[:, :dff], SWIGLU_LIMIT)
            up = jnp.clip(gu[:, dff:], -SWIGLU_LIMIT, SWIGLU_LIMIT)
            act = (up + 1.0) * gate * _sigmoid(SWIGLU_ALPHA * gate)
            y = _dot(act.astype(BF16), wd_b16[...]) + bd_ref[0]
            if c + 1 < n_chunks:
                x_next = load_x(c + 1)
            for j in range(rt):
                ybuf[slot, slab(c, j), :] = y[:, j * LANES:(j + 1) * LANES]
        for r in range(rows - chunk, rows):
            row_out(slot, r, dst_ref[0, 0, r]).start(priority=r % 2)

        @pl.when(nv_next == 0)
        def _():
            wait_gather(1 - slot)

            @pl.when(blk >= 1)
            def _():
                wait_scatter(1 - slot)
            wait_scatter(slot)


def _moe(h3_tiles, block_expert, block_valid, row_tok, row_dst, w_gu, b_gu, w_d, b_d):
    ne, d, n2 = w_gu.shape
    assert d == ROW_TILE * LANES
    t = h3_tiles.shape[0] // ROW_TILE
    dff = n2 // 2
    nblk = block_expert.shape[0]
    idx_blk = pl.BlockSpec((1, 1, MOE_ROWS), lambda i, be, bv: (i, 0, 0), memory_space=pltpu.SMEM)
    idx_next = pl.BlockSpec((1, 1, MOE_ROWS), lambda i, be, bv: (jnp.minimum(i + 1, nblk - 1), 0, 0),
                            memory_space=pltpu.SMEM)
    grid_spec = pltpu.PrefetchScalarGridSpec(
        num_scalar_prefetch=2,
        grid=(nblk,),
        in_specs=[idx_blk, idx_next, idx_blk, pl.BlockSpec(memory_space=pl.ANY),
                  pl.BlockSpec((1, d, n2), lambda i, be, bv: (be[i], 0, 0)),
                  pl.BlockSpec((1, 1, n2), lambda i, be, bv: (be[i], 0, 0)),
                  pl.BlockSpec((1, dff, d), lambda i, be, bv: (be[i], 0, 0)),
                  pl.BlockSpec((1, 1, d), lambda i, be, bv: (be[i], 0, 0))],
        out_specs=pl.BlockSpec(memory_space=pl.ANY),
        scratch_shapes=[pltpu.VMEM((2, MOE_ROWS * ROW_TILE, LANES), F32),
                        pltpu.VMEM((2, MOE_ROWS * ROW_TILE, LANES), F32),
                        pltpu.VMEM((d, n2), BF16), pltpu.VMEM((dff, d), BF16),
                        pltpu.SemaphoreType.DMA((2,)), pltpu.SemaphoreType.DMA((2,))],
    )
    tok3 = (row_tok * ROW_TILE).reshape(nblk, 1, MOE_ROWS)
    valid_ext = jnp.concatenate([block_valid, jnp.zeros((1,), jnp.int32)])
    return pl.pallas_call(
        functools.partial(_moe_kernel, dff=dff),
        grid_spec=grid_spec,
        out_shape=jax.ShapeDtypeStruct(((TOP_K * t + MOE_ROWS) * ROW_TILE, LANES), F32),
        compiler_params=_params("arbitrary"),
        name="moe",
    )(block_expert, valid_ext, tok3, tok3, (row_dst * ROW_TILE).reshape(nblk, 1, MOE_ROWS), h3_tiles,
      w_gu, b_gu.reshape(ne, 1, n2), w_d, b_d.reshape(ne, 1, d))


def _route_plan(expert, t):
    n_pairs = t * TOP_K
    e_flat = expert.reshape(-1)
    order = jnp.argsort(e_flat, stable=True).astype(jnp.int32)
    experts = jnp.arange(N_EXPERTS, dtype=jnp.int32)
    counts = jnp.sum((e_flat[:, None] == experts[None, :]).astype(jnp.int32), axis=0)
    padded = (counts + MOE_ROWS - 1) // MOE_ROWS * MOE_ROWS
    pend = jnp.cumsum(padded)
    pstart = pend - padded
    gstart = jnp.cumsum(counts) - counts
    nblk = n_pairs // MOE_ROWS + N_EXPERTS
    blk_row0 = jnp.arange(nblk, dtype=jnp.int32) * MOE_ROWS
    block_expert = jnp.minimum(jnp.sum((pend[None, :] <= blk_row0[:, None]).astype(jnp.int32), axis=1), N_EXPERTS - 1)
    pick = lambda table: jnp.sum(jnp.where(block_expert[:, None] == experts[None, :], table[None, :], 0), axis=1)
    in_group = blk_row0 - pick(pstart)
    block_valid = jnp.where(blk_row0 < pend[-1], jnp.clip(pick(counts) - in_group, 0, MOE_ROWS), 0)
    first_sorted = pick(gstart) + in_group
    local = jnp.arange(MOE_ROWS, dtype=jnp.int32)
    srt = first_sorted[:, None] + local[None, :]
    pair = order[jnp.clip(srt, 0, n_pairs - 1)]
    row_tok = pair // TOP_K
    row_dst = jnp.where(local[None, :] < block_valid[:, None], (pair % TOP_K) * t + row_tok, n_pairs + local[None, :])
    return (block_expert.astype(jnp.int32), block_valid.astype(jnp.int32), row_tok.astype(jnp.int32),
            row_dst.astype(jnp.int32))


def _combine_kernel(x2_ref, route_ref, g_ref, *rest, final):
    y_refs, o_ref = rest[:TOP_K], rest[TOP_K]
    route = route_ref[...]
    gates = [route[:, kk:kk + 1] for kk in range(TOP_K)]
    slabs = []
    for j in range(ROW_TILE):
        acc = x2_ref[:, j * LANES:(j + 1) * LANES]
        for kk in range(TOP_K):
            acc = acc + gates[kk] * y_refs[kk][pl.ds(j, x2_ref.shape[0], stride=ROW_TILE), :]
        slabs.append(acc)
    out = jnp.concatenate(slabs, axis=1)
    o_ref[...] = _rms(out, g_ref[...]) if final else out


def _combine(x2, route, y, g, final):
    t, d = x2.shape
    tiles = t // MID_ROWS
    slot_spec = lambda kk: pl.BlockSpec((MID_ROWS * ROW_TILE, LANES), lambda i: (kk * tiles + i, 0))
    return pl.pallas_call(
        functools.partial(_combine_kernel, final=final),
        grid=(tiles,),
        in_specs=[pl.BlockSpec((MID_ROWS, d), lambda i: (i, 0)), pl.BlockSpec((MID_ROWS, LANES), lambda i: (i, 0)),
                  pl.BlockSpec((1, d), lambda i: (0, 0))] + [slot_spec(kk) for kk in range(TOP_K)],
        out_specs=pl.BlockSpec((MID_ROWS, d), lambda i: (i, 0)),
        out_shape=jax.ShapeDtypeStruct((t, d), F32),
        compiler_params=_params("arbitrary"),
        name="combine",
    )(x2, route, g.reshape(1, d), *([y] * TOP_K))


def kernel(x, mem, norm_mix_g, w_in, gdn_conv_w, gdn_A_log, gdn_dt_bias, gdn_norm_g, moba_norm_g, w_out,
           norm_xattn_g, norm_mem_g, xattn_w_q, xattn_w_kv, xattn_w_o, norm_ffn_g, router_w, router_b,
           w_gate_up, b_gate_up, w_down, b_down, final_norm_g):
    b, s, d = x.shape
    t = b * s
    mem_len = mem.shape[1]
    mw = moba_norm_g.shape[1]
    gw = GDN_HEADS * GDN_HEAD_DIM
    xcur = x.reshape(t, d)
    for l in range(w_in.shape[0]):
        qt, mk, mv, gqkv, gz, gba = _in_proj(xcur, norm_mix_g[l], w_in[l], mw, gw)
        o_moba = _moba(qt, mk, mv, b, s)
        o_gdn = _gdn(gqkv.reshape(b, s, 3 * gw), gz.reshape(b, s, gw), gba.reshape(b, s, LANES),
                     gdn_conv_w[l], gdn_A_log[l], gdn_dt_bias[l], gdn_norm_g[l])
        kv = _mem_kv(mem.reshape(b * mem_len, d), norm_mem_g[l], xattn_w_kv[l], mem_len)
        x2, h3, route = _mid(xcur, o_moba.reshape(t, mw), o_gdn.reshape(t, gw), moba_norm_g[l], w_out[l],
                             norm_xattn_g[l], xattn_w_q[l], kv, xattn_w_o[l], norm_ffn_g[l], router_w[l],
                             router_b[l], s, mem_len)
        expert = route[:, TOP_K:2 * TOP_K].astype(jnp.int32)
        plan = _route_plan(expert, t)
        y = _moe(h3, *plan, w_gate_up[l], b_gate_up[l], w_down[l], b_down[l])
        xcur = _combine(x2, route, y, final_norm_g, l == w_in.shape[0] - 1)
    return xcur.reshape(b, s, d)
```

```python
import functools

import jax
import jax.numpy as jnp
from jax import lax
from jax.experimental import pallas as pl
from jax.experimental.pallas import tpu as pltpu

F32 = jnp.float32
BF16 = jnp.bfloat16

RMS_EPS = 1e-6
MOBA_HEAD_DIM = 64
MOBA_BLOCK = 256
MOBA_TOPK = 3
GDN_HEAD_DIM = 128
GDN_HEADS = 4
GDN_CONV = 4
GDN_CHUNK = 64
XATTN_HEAD_DIM = 128
N_EXPERTS = 32
TOP_K = 4
SWIGLU_LIMIT = 7.0
SWIGLU_ALPHA = 1.702

LANES = 128
ROW_TILE = 8
VMEM_LIMIT = 56 * 1024 * 1024

IN_ROWS = 512
MID_ROWS = 512
MOE_ROWS = 512
MOE_CHUNK = 128
GDN_GROUP = 32
PREP_ROWS = 256
SCAN_BATCH = 2
SCAN_ROWS = 1024
NEG_INF = float("-inf")
MOBA_Q_SCALE = 1.4426950408889634 / MOBA_HEAD_DIM ** 0.5
MOBA_MASKED = -1e30


def _params(*sem):
    return pltpu.CompilerParams(dimension_semantics=sem, vmem_limit_bytes=VMEM_LIMIT)


def _rms(x, g):
    return x * lax.rsqrt(jnp.mean(x * x, axis=-1, keepdims=True) + RMS_EPS) * g


def _dot(a, b):
    return jnp.dot(a, b, preferred_element_type=F32)


def _dot_nt(a, b):
    return lax.dot_general(a, b, (((1,), (1,)), ((), ())), preferred_element_type=F32)


def _sigmoid(x):
    return 1.0 / (1.0 + jnp.exp(-x))


def _in_proj_kernel(x_ref, g_ref, w_ref, wt_ref, qt_ref, mk_ref, mv_ref, gqkv_ref, gz_ref, gba_ref, *, mw, gw):
    hn = _rms(x_ref[...], g_ref[...]).astype(BF16)
    mm = lambda lo, hi: _dot(hn, w_ref[:, lo:hi])
    qt_ref[...] = (_dot_nt(wt_ref[...], hn) * MOBA_Q_SCALE).astype(BF16)
    mk_ref[...] = mm(0, mw).astype(BF16)
    mv_ref[...] = mm(mw, 2 * mw).astype(BF16)
    gqkv_ref[...] = mm(2 * mw, 2 * mw + 3 * gw)
    gz_ref[...] = mm(2 * mw + 3 * gw, 2 * mw + 4 * gw)
    gba_ref[...] = mm(2 * mw + 4 * gw, 2 * mw + 4 * gw + LANES)


def _in_proj(x2d, g, w_in, mw, gw):
    t, d = x2d.shape
    n_real = w_in.shape[1] - mw
    n_pad = 2 * mw + 4 * gw + LANES
    w = jnp.pad(w_in[:, mw:], ((0, 0), (0, n_pad - n_real))).astype(BF16)
    w_t = w_in[:, :mw].T.astype(BF16)
    row = lambda n: pl.BlockSpec((IN_ROWS, n), lambda i: (i, 0))
    return pl.pallas_call(
        functools.partial(_in_proj_kernel, mw=mw, gw=gw),
        grid=(t // IN_ROWS,),
        in_specs=[row(d), pl.BlockSpec((1, d), lambda i: (0, 0)), pl.BlockSpec((d, n_pad), lambda i: (0, 0)),
                  pl.BlockSpec((mw, d), lambda i: (0, 0))],
        out_specs=[pl.BlockSpec((mw, IN_ROWS), lambda i: (0, i)), row(mw), row(mw), row(3 * gw), row(gw),
                   row(LANES)],
        out_shape=[jax.ShapeDtypeStruct((mw, t), BF16), jax.ShapeDtypeStruct((t, mw), BF16),
                   jax.ShapeDtypeStruct((t, mw), BF16),
                   jax.ShapeDtypeStruct((t, 3 * gw), F32), jax.ShapeDtypeStruct((t, gw), F32),
                   jax.ShapeDtypeStruct((t, LANES), F32)],
        compiler_params=_params("arbitrary"),
        name="in_proj",
    )(x2d, g.reshape(1, d), w, w_t)


def _moba_select(g_t, i):
    nb = g_t.shape[0]
    row = lax.broadcasted_iota(jnp.int32, g_t.shape, 0)
    valid = row < i
    sel = jnp.zeros_like(g_t)
    for j in range(nb):
        gj = g_t[j:j + 1, :]
        beats = valid & ((g_t > gj) | ((g_t == gj) & (row < j)))
        rank = jnp.sum(jnp.where(beats, 1.0, 0.0), axis=0, keepdims=True)
        sel = jnp.where(row == j, jnp.where(rank < MOBA_TOPK, 1.0, 0.0), sel)
    return jnp.where(valid, sel, 0.0)


def _dot_tn(a, b):
    return lax.dot_general(a, b, (((0,), (0,)), ((), ())), preferred_element_type=F32)


def _moba_kernel(qt_ref, qt_all_ref, k_ref, v_ref, o_ref, kaug_ref, bias_ref, *, nb):
    i = pl.program_id(2)
    bs = MOBA_BLOCK
    hd = MOBA_HEAD_DIM
    lane = lax.broadcasted_iota(jnp.int32, (1, LANES), 1)
    low = lax.broadcasted_iota(jnp.int32, (LANES, 1), 0) < hd

    @pl.when(i == 0)
    def _():
        kmean = []
        for j in range(nb):
            rows = slice(j * bs, (j + 1) * bs)
            kb = k_ref[0, rows, :]
            kmean.append(jnp.mean(kb.astype(F32), axis=0, keepdims=True))
            kaug_ref[0, rows, :] = jnp.where(lane < hd, kb, jnp.where(lane == hd + j, 1.0, 0.0).astype(BF16))
            kaug_ref[1, rows, :] = jnp.where(lane >= hd, kb, jnp.where(lane == j, 1.0, 0.0).astype(BF16))
        kmean = jnp.concatenate(kmean, axis=0)
        q_all = qt_all_ref[...].astype(F32)
        for h, qh in enumerate((jnp.where(low, q_all, 0.0), jnp.where(low, 0.0, q_all))):
            gate = jnp.dot(kmean, qh, preferred_element_type=F32, precision=lax.Precision.HIGHEST)
            for jq in range(nb):
                sel = _moba_select(gate[:, jq * bs:(jq + 1) * bs], jq)
                bias_ref[h, jq] = jnp.where(sel > 0.5, 0.0, MOBA_MASKED)

    qt = qt_ref[...]
    qtf = qt.astype(F32)
    bias = [bias_ref[h, i] for h in (0, 1)]
    pad = jnp.zeros((hd - nb, bs), F32)
    q_past = (jnp.concatenate([qtf[:hd], bias[0], pad], axis=0).astype(BF16),
              jnp.concatenate([bias[1], pad, qtf[hd:]], axis=0).astype(BF16))
    zero = jnp.zeros_like(qt)
    q_own = (jnp.where(low, qt, zero), jnp.where(low, zero, qt))

    key_ix = lax.broadcasted_iota(jnp.int32, (bs, bs), 0)
    qry_ix = lax.broadcasted_iota(jnp.int32, (bs, bs), 1)
    causal_bias = jnp.where(key_ix <= qry_ix, 0.0, NEG_INF)
    own = pl.ds(pl.multiple_of(i * bs, bs), bs)
    k_own = k_ref[0, own, :]
    v_own = v_ref[0, own, :]

    def attend(width):
        heads = (0, 1)
        parts = [slice(lo * bs, (lo + 1) * bs) for lo in range(width)]
        n = len(parts) + 1
        past_scores = lambda k: [_dot(kaug_ref[h, parts[k - 1], :], q_past[h]) for h in heads]
        scores = {0: [_dot(k_own, q_own[h]) + causal_bias for h in heads]}
        if n > 1:
            scores[1] = past_scores(1)
        m = l = acc = None
        for k in range(n):
            s_k = scores.pop(k)
            m_new = [s.max(axis=0, keepdims=True) for s in s_k]
            if m is not None:
                m_new = [jnp.maximum(m[h], m_new[h]) for h in heads]
            p = [jnp.exp2(s_k[h] - m_new[h]) for h in heads]
            p_sum = [x.sum(axis=0, keepdims=True) for x in p]
            if k + 2 < n:
                scores[k + 2] = past_scores(k + 2)
            values = v_own if k == 0 else v_ref[0, parts[k - 1], :]
            pv = [_dot_tn(values, p[h].astype(BF16)) for h in heads]
            if m is None:
                l, acc = p_sum, pv
            else:
                alpha = [jnp.exp2(m[h] - m_new[h]) for h in heads]
                l = [alpha[h] * l[h] + p_sum[h] for h in heads]
                acc = [alpha[h] * acc[h] + pv[h] for h in heads]
            m = m_new
        o_ref[0] = jnp.where(low, acc[0] / l[0], acc[1] / l[1]).T

    for width in range(nb):
        pl.when(i == width)(functools.partial(attend, width))


def _moba(qt, mk, mv, b, s):
    mw = mk.shape[-1]
    nb = s // MOBA_BLOCK
    seq_blk = pl.BlockSpec((1, s, LANES), lambda bi, hp, i: (bi, 0, hp))
    return pl.pallas_call(
        functools.partial(_moba_kernel, nb=nb),
        grid=(b, mw // LANES, nb),
        in_specs=[pl.BlockSpec((LANES, MOBA_BLOCK), lambda bi, hp, i: (hp, bi * nb + i)),
                  pl.BlockSpec((LANES, s), lambda bi, hp, i: (hp, bi)), seq_blk, seq_blk],
        out_specs=pl.BlockSpec((1, MOBA_BLOCK, LANES), lambda bi, hp, i: (bi, i, hp)),
        out_shape=jax.ShapeDtypeStruct((b, s, mw), F32),
        scratch_shapes=[pltpu.VMEM((2, s, LANES), BF16), pltpu.VMEM((2, nb, nb, MOBA_BLOCK), F32)],
        compiler_params=_params("arbitrary", "arbitrary", "arbitrary"),
        name="moba",
    )(qt, qt, mk.reshape(b, s, mw), mv.reshape(b, s, mw))


def _gdn_gate_kernel(ba_ref, alog_ref, dtb_ref, bg_ref, *, seq):
    x = ba_ref[0]
    lane = lax.broadcasted_iota(jnp.int32, (1, LANES), 1)
    xa = x + dtb_ref[...]
    softplus = jnp.maximum(xa, 0.0) + jnp.log(1.0 + jnp.exp(-jnp.abs(xa)))
    g = jnp.where((lane >= GDN_HEADS) & (lane < 2 * GDN_HEADS), -jnp.exp(alog_ref[...]) * softplus, 0.0)
    pos = lax.broadcasted_iota(jnp.int32, (seq, 1), 0) % GDN_CHUNK
    sft = 1
    while sft < GDN_CHUNK:
        g = g + jnp.where(pos >= sft, pltpu.roll(g, sft, 0), 0.0)
        sft *= 2
    bg_ref[0] = jnp.where(lane < GDN_HEADS, _sigmoid(x), g)


def _gdn_prep_kernel(x_ref, w_ref, o_ref, *, seq):
    cb = pl.program_id(1)
    is_qk = cb < 2 * GDN_HEADS
    scale = jnp.where(cb < GDN_HEADS, GDN_HEAD_DIM ** -0.5, 1.0)
    taps = [w_ref[j:j + 1, :] for j in range(GDN_CONV)]
    halo = 8
    for r0 in range(0, seq, PREP_ROWS):
        if r0 == 0:
            xe = jnp.concatenate([jnp.zeros((halo, LANES), F32), x_ref[0, :PREP_ROWS, :]], axis=0)
        else:
            xe = x_ref[0, r0 - halo:r0 + PREP_ROWS, :]
        y = taps[GDN_CONV - 1] * xe[halo:, :]
        for sft in range(1, GDN_CONV):
            y = y + taps[GDN_CONV - 1 - sft] * xe[halo - sft:halo - sft + PREP_ROWS, :]
        y = y * _sigmoid(y)
        normed = y * (lax.rsqrt(jnp.sum(y * y, axis=-1, keepdims=True) + RMS_EPS) * scale)
        o_ref[0, r0:r0 + PREP_ROWS, :] = jnp.where(is_qk, normed, y).astype(BF16)


def _unit_lower_inverses(mats):
    c = mats[0].shape[0]
    r = lax.broadcasted_iota(jnp.int32, (c, c), 0)
    cc = lax.broadcasted_iota(jnp.int32, (c, c), 1)
    eye = jnp.where(r == cc, 1.0, 0.0)
    pair = (r // 2) == (cc // 2)
    invs = [eye - jnp.where(pair, a, 0.0) for a in mats]
    size = 4
    while size <= c:
        level = ((r // size) == (cc // size)) & ((r // (size // 2)) != (cc // (size // 2)))
        inv_b = [inv.astype(BF16) for inv in invs]
        left = [_dot(ib, jnp.where(level, a, 0.0).astype(BF16)).astype(BF16) for ib, a in zip(inv_b, mats)]
        invs = [inv - _dot(lf, ib) for inv, lf, ib in zip(invs, left, inv_b)]
        size *= 2
    return invs


def _gdn_intra_kernel(q_ref, k_ref, v_ref, bg_ref, u_ref, w_ref, qd_ref, kd_ref, qk_ref):
    h = pl.program_id(1)
    c = GDN_CHUNK
    chunks = range(GDN_GROUP)
    lane = lax.broadcasted_iota(jnp.int32, (1, LANES), 1)
    r_ix = lax.broadcasted_iota(jnp.int32, (c, c), 0)
    c_ix = lax.broadcasted_iota(jnp.int32, (c, c), 1)
    rows = [slice(gi * c, (gi + 1) * c) for gi in chunks]
    k_b16 = [k_ref[0, rw, :] for rw in rows]
    q_b16 = [q_ref[0, rw, :] for rw in rows]
    bgs = [bg_ref[0, rw, :] for rw in rows]
    beta = [jnp.sum(jnp.where(lane == h, bg, 0.0), axis=1, keepdims=True) for bg in bgs]
    gam = [jnp.sum(jnp.where(lane == GDN_HEADS + h, bg, 0.0), axis=1, keepdims=True) for bg in bgs]
    kb = [kk.astype(F32) * bt for kk, bt in zip(k_b16, beta)]
    kk_raw = [_dot_nt(x.astype(BF16), kk) for x, kk in zip(kb, k_b16)]
    qk_raw = [_dot_nt(qq, kk) for qq, kk in zip(q_b16, k_b16)]
    decay = []
    for gm in gam:
        gam_r = jnp.sum(jnp.where(r_ix == c_ix, gm, 0.0), axis=0, keepdims=True)
        decay.append(jnp.exp(jnp.where(c_ix <= r_ix, gm - gam_r, NEG_INF)))
    t_inv = _unit_lower_inverses([jnp.where(c_ix < r_ix, x * dc, 0.0) for x, dc in zip(kk_raw, decay)])
    eg = [jnp.exp(gm) for gm in gam]
    rhs = [jnp.concatenate([v_ref[0, rw, :].astype(F32) * bt, x * e], axis=1).astype(BF16)
           for rw, bt, x, e in zip(rows, beta, kb, eg)]
    uw = [_dot(ti.astype(BF16), rh) for ti, rh in zip(t_inv, rhs)]
    for gi in chunks:
        rw = rows[gi]
        u_ref[0, rw, :] = uw[gi][:, :LANES].astype(BF16)
        w_ref[0, rw, :] = uw[gi][:, LANES:].astype(BF16)
        qk_ref[0, 0, rw, :] = (qk_raw[gi] * decay[gi]).astype(BF16)
        qd_ref[0, rw, :] = (q_b16[gi].astype(F32) * eg[gi]).astype(BF16)
        kd_ref[0, rw, :] = (k_b16[gi].astype(F32) * jnp.exp(gam[gi][c - 1:c, :] - gam[gi])).astype(BF16)


def _gdn_scan_kernel(u_ref, w_ref, qd_ref, kd_ref, qk_ref, z_ref, bg_ref, ng_ref, o_ref, state_ref, *, rows_per_step):
    c = GDN_CHUNK
    ng = ng_ref[...]
    chains = [(bi, h) for bi in range(SCAN_BATCH) for h in range(GDN_HEADS)]
    cols = [slice(h * GDN_HEAD_DIM, (h + 1) * GDN_HEAD_DIM) for h in range(GDN_HEADS)]

    @pl.when(pl.program_id(1) == 0)
    def _():
        state_ref[...] = jnp.zeros(state_ref.shape, F32)

    def step(n, states):
        r0 = pl.multiple_of(n * c, c)
        rows = pl.ds(r0, c)
        tails = [bg_ref[bi, pl.ds(r0 + c - 8, 8), :] for bi in range(SCAN_BATCH)]
        s_b = [st.astype(BF16) for st in states]
        ws = [_dot(w_ref[bi, rows, cols[h]], s_b[k]) for k, (bi, h) in enumerate(chains)]
        qs = [_dot(qd_ref[bi, rows, cols[h]], s_b[k]) for k, (bi, h) in enumerate(chains)]
        v_b = [(u_ref[bi, rows, cols[h]].astype(F32) - ws[k]).astype(BF16) for k, (bi, h) in enumerate(chains)]
        kd_v = [_dot_tn(kd_ref[bi, rows, cols[h]], v_b[k]) for k, (bi, h) in enumerate(chains)]
        qkv = [_dot(qk_ref[bi, h, rows, :], v_b[k]) for k, (bi, h) in enumerate(chains)]
        new_states = []
        for k, (bi, h) in enumerate(chains):
            g_last = tails[bi][7:8, GDN_HEADS + h:GDN_HEADS + h + 1]
            new_states.append(states[k] * jnp.exp(g_last) + kd_v[k])
            z = z_ref[bi, rows, cols[h]]
            o_ref[bi, rows, cols[h]] = _rms(qs[k] + qkv[k], ng) * (z * _sigmoid(z))
        return tuple(new_states)

    init = tuple(state_ref[k] for k in range(len(chains)))
    final = lax.fori_loop(0, rows_per_step // c, step, init)
    for k, st in enumerate(final):
        state_ref[k] = st


def _gdn(gqkv, gz, gba, conv_w, a_log, dt_bias, norm_g):
    b, s, w3 = gqkv.shape
    nh = GDN_HEADS
    hw = nh * GDN_HEAD_DIM
    c = GDN_CHUNK
    lane_pad = lambda v: jnp.pad(v.reshape(1, -1), ((0, 0), (nh, LANES - 2 * nh)))
    seq_blk = pl.BlockSpec((1, s, LANES), lambda bi: (bi, 0, 0))
    const = pl.BlockSpec((1, LANES), lambda bi: (0, 0))
    bg = pl.pallas_call(
        functools.partial(_gdn_gate_kernel, seq=s),
        grid=(b,),
        in_specs=[seq_blk, const, const],
        out_specs=seq_blk,
        out_shape=jax.ShapeDtypeStruct((b, s, LANES), F32),
        compiler_params=_params("arbitrary"),
        name="gdn_gate",
    )(gba, lane_pad(a_log), lane_pad(dt_bias))

    qkv = pl.pallas_call(
        functools.partial(_gdn_prep_kernel, seq=s),
        grid=(b, w3 // LANES),
        in_specs=[pl.BlockSpec((1, s, LANES), lambda bi, cb: (bi, 0, cb)),
                  pl.BlockSpec((GDN_CONV, LANES), lambda bi, cb: (0, cb))],
        out_specs=pl.BlockSpec((1, s, LANES), lambda bi, cb: (bi, 0, cb)),
        out_shape=jax.ShapeDtypeStruct((b, s, w3), BF16),
        compiler_params=_params("arbitrary", "arbitrary"),
        name="gdn_prep",
    )(gqkv, conv_w)

    grp = GDN_GROUP * c
    head_blk = lambda off: pl.BlockSpec((1, grp, LANES), lambda bi, h, n: (bi, n, off + h))
    head_shape = jax.ShapeDtypeStruct((b, s, hw), BF16)
    u, w, qd, kd, qk = pl.pallas_call(
        _gdn_intra_kernel,
        grid=(b, nh, s // grp),
        in_specs=[head_blk(0), head_blk(nh), head_blk(2 * nh),
                  pl.BlockSpec((1, grp, LANES), lambda bi, h, n: (bi, n, 0))],
        out_specs=[head_blk(0)] * 4 + [pl.BlockSpec((1, 1, grp, c), lambda bi, h, n: (bi, h, n, 0))],
        out_shape=[head_shape] * 4 + [jax.ShapeDtypeStruct((b, nh, s, c), BF16)],
        compiler_params=_params("arbitrary", "arbitrary", "arbitrary"),
        name="gdn_intra",
    )(qkv, qkv, qkv, bg)

    tile = lambda width: pl.BlockSpec((SCAN_BATCH, SCAN_ROWS, width), lambda bi, ti: (bi, ti, 0))
    return pl.pallas_call(
        functools.partial(_gdn_scan_kernel, rows_per_step=SCAN_ROWS),
        grid=(b // SCAN_BATCH, s // SCAN_ROWS),
        in_specs=[tile(hw), tile(hw), tile(hw), tile(hw),
                  pl.BlockSpec((SCAN_BATCH, nh, SCAN_ROWS, c), lambda bi, ti: (bi, 0, ti, 0)), tile(hw), tile(LANES),
                  pl.BlockSpec((1, LANES), lambda bi, ti: (0, 0))],
        out_specs=tile(hw),
        out_shape=jax.ShapeDtypeStruct((b, s, hw), F32),
        scratch_shapes=[pltpu.VMEM((SCAN_BATCH * nh, GDN_HEAD_DIM, GDN_HEAD_DIM), F32)],
        compiler_params=_params("arbitrary", "arbitrary"),
        name="gdn_scan",
    )(u, w, qd, kd, qk, gz, bg, norm_g.reshape(1, -1))


def _mem_kv_kernel(m_ref, g_ref, w_ref, kv_ref):
    kv_ref[...] = _dot(_rms(m_ref[...], g_ref[...]).astype(BF16), w_ref[...]).astype(BF16)


def _mem_kv(mem2d, g, w_kv, rows):
    t, d = mem2d.shape
    n = w_kv.shape[1]
    return pl.pallas_call(
        _mem_kv_kernel,
        grid=(t // rows,),
        in_specs=[pl.BlockSpec((rows, d), lambda i: (i, 0)), pl.BlockSpec((1, d), lambda i: (0, 0)),
                  pl.BlockSpec((d, n), lambda i: (0, 0))],
        out_specs=pl.BlockSpec((rows, n), lambda i: (i, 0)),
        out_shape=jax.ShapeDtypeStruct((t, n), BF16),
        compiler_params=_params("arbitrary"),
        name="mem_kv",
    )(mem2d, g.reshape(1, d), w_kv.astype(BF16))


def _mid_kernel(x_ref, om_ref, og_ref, mg_ref, wout_ref, xg_ref, wq_ref, kv_ref, wo_ref, fg_ref, rw_ref, rb_ref,
                x2_ref, h3_ref, route_ref, *, mw, xw):
    mo = _rms(om_ref[...], mg_ref[...]).astype(BF16)
    x1 = x_ref[...] + _dot(mo, wout_ref[:mw, :]) + _dot(og_ref[...].astype(BF16), wout_ref[mw:, :])

    h2 = _rms(x1, xg_ref[...]).astype(BF16)
    q = (_dot(h2, wq_ref[...]) * (XATTN_HEAD_DIM ** -0.5)).astype(BF16)
    head_cols = [slice(h * XATTN_HEAD_DIM, (h + 1) * XATTN_HEAD_DIM) for h in range(xw // XATTN_HEAD_DIM)]
    scores = [_dot_nt(q[:, sl], kv_ref[:, sl]) for sl in head_cols]
    probs = [jnp.exp(s - jnp.max(s, axis=1, keepdims=True)) for s in scores]
    heads = [_dot(p.astype(BF16), kv_ref[:, xw + sl.start:xw + sl.stop]) / jnp.sum(p, axis=1, keepdims=True)
             for p, sl in zip(probs, head_cols)]
    x2 = x1 + _dot(jnp.concatenate(heads, axis=1).astype(BF16), wo_ref[...])
    x2_ref[...] = x2

    h3 = _rms(x2, fg_ref[...])
    slabs = h3.shape[1] // LANES
    for j in range(slabs):
        h3_ref[pl.ds(j, h3.shape[0], stride=slabs), :] = h3[:, j * LANES:(j + 1) * LANES]
    ne = rb_ref.shape[0]
    h_hi = h3.astype(BF16)
    h_lo = (h3 - h_hi.astype(F32)).astype(BF16)
    by_hi = _dot_nt(rw_ref[...], h_hi)
    logits = by_hi[:ne, :] + by_hi[ne:, :] + _dot_nt(rw_ref[:ne, :], h_lo) + rb_ref[...]
    row = lax.broadcasted_iota(jnp.int32, logits.shape, 0)
    weights, picks, top = [], [], None
    for kk in range(TOP_K):
        m = jnp.max(logits, axis=0, keepdims=True)
        idx = jnp.min(jnp.where(logits == m, row, ne), axis=0, keepdims=True)
        logits = jnp.where(row == idx, NEG_INF, logits)
        top = m if top is None else top
        weights.append(jnp.exp(m - top))
        picks.append(idx.astype(F32))
    denom = sum(weights[1:], weights[0])
    rows = [wk / denom for wk in weights] + picks
    rows.append(jnp.zeros((LANES - len(rows), logits.shape[1]), F32))
    route_ref[...] = jnp.concatenate(rows, axis=0).T


def _mid(x2d, om, og, moba_g, w_out, xattn_g, w_q, kv, w_o, ffn_g, router_w, router_b, seq, mem_len):
    t, d = x2d.shape
    mw, gw, xw, ne = om.shape[1], og.shape[1], w_q.shape[1], router_w.shape[1]
    tiles_per_seq = seq // MID_ROWS
    row = lambda n: pl.BlockSpec((MID_ROWS, n), lambda i: (i, 0))
    const = lambda r, c: pl.BlockSpec((r, c), lambda i: (0, 0))
    rw_hi = router_w.T.astype(BF16)
    rw_lo = (router_w.T - rw_hi.astype(F32)).astype(BF16)
    return pl.pallas_call(
        functools.partial(_mid_kernel, mw=mw, xw=xw),
        grid=(t // MID_ROWS,),
        in_specs=[row(d), row(mw), row(gw), const(1, mw), const(mw + gw, d), const(1, d), const(d, xw),
                  pl.BlockSpec((mem_len, 2 * xw), lambda i: (i // tiles_per_seq, 0)),
                  const(xw, d), const(1, d), const(2 * ne, d), const(ne, 1)],
        out_specs=[row(d), pl.BlockSpec((MID_ROWS * (d // LANES), LANES), lambda i: (i, 0)), row(LANES)],
        out_shape=[jax.ShapeDtypeStruct((t, d), F32), jax.ShapeDtypeStruct((t * (d // LANES), LANES), F32),
                   jax.ShapeDtypeStruct((t, LANES), F32)],
        compiler_params=_params("arbitrary"),
        name="mid",
    )(x2d, om, og, moba_g.reshape(1, mw), w_out.astype(BF16), xattn_g.reshape(1, d), w_q.astype(BF16), kv,
      w_o.astype(BF16), ffn_g.reshape(1, d), jnp.concatenate([rw_hi, rw_lo], axis=0), router_b.reshape(ne, 1))


def _moe_kernel(be_ref, bv_ref, tok_ref, tokn_ref, dst_ref, h_hbm, wgu_ref, bgu_ref, wd_ref, bd_ref, y_hbm,
                xbuf, ybuf, wgu_b16, wd_b16, sem_in, sem_out, *, dff):
    blk = pl.program_id(0)
    nv = bv_ref[blk]
    nv_next = bv_ref[blk + 1]
    slot = blk % 2
    rows = MOE_ROWS
    chunk = MOE_CHUNK
    n_chunks = rows // chunk
    rt = ROW_TILE

    def row_in(s, r, t):
        return pltpu.make_async_copy(h_hbm.at[pl.ds(pl.multiple_of(t, rt), rt), :],
                                     xbuf.at[s, pl.ds(r * rt, rt), :], sem_in.at[s])

    def row_out(s, r, d):
        return pltpu.make_async_copy(ybuf.at[s, pl.ds(r * rt, rt), :],
                                     y_hbm.at[pl.ds(pl.multiple_of(d, rt), rt), :], sem_out.at[s])

    def wait_gather(s):
        pltpu.make_async_copy(h_hbm.at[pl.ds(0, rows * rt), :], xbuf.at[s], sem_in.at[s]).wait()

    def wait_scatter(s):
        pltpu.make_async_copy(ybuf.at[s], y_hbm.at[pl.ds(0, rows * rt), :], sem_out.at[s]).wait()

    @pl.when(blk == 0)
    def _():
        ybuf[1] = jnp.zeros(ybuf.shape[1:], F32)
        spare = pltpu.make_async_copy(ybuf.at[1], y_hbm.at[pl.ds(y_hbm.shape[0] - rows * rt, rows * rt), :],
                                      sem_out.at[1])
        spare.start()
        spare.wait()

    @pl.when((blk == 0) & (nv > 0))
    def _():
        for r in range(rows):
            row_in(0, r, tok_ref[0, 0, r]).start(priority=r % 2)

    @pl.when(nv > 0)
    def _():
        @pl.when((blk == 0) | (be_ref[blk] != be_ref[jnp.maximum(blk - 1, 0)]))
        def _():
            wgu_b16[...] = wgu_ref[0].astype(BF16)
            wd_b16[...] = wd_ref[0].astype(BF16)

        wait_gather(slot)

        @pl.when(blk >= 2)
        def _():
            wait_scatter(slot)

        slab = lambda c, j: pl.ds(c * chunk * rt + j, chunk, stride=rt)
        load_x = lambda c: jnp.concatenate([xbuf[slot, slab(c, j), :] for j in range(rt)], axis=1).astype(BF16)
        per_chunk_in = rows // (n_chunks // 2)
        x_next = load_x(0)
        for c in range(n_chunks):
            x = x_next
            if c < n_chunks // 2:
                for r in range(c * per_chunk_in, (c + 1) * per_chunk_in):
                    row_in(1 - slot, r, tokn_ref[0, 0, r]).start(priority=r % 2)
            if c:
                for r in range((c - 1) * chunk, c * chunk):
                    row_out(slot, r, dst_ref[0, 0, r]).start(priority=r % 2)
            gu = _dot(x, wgu_b16[...]) + bgu_ref[0]
            gate = jnp.minimum(gu[:, :dff], SWIGLU_LIMIT)
            up = jnp.clip(gu[:, dff:], -SWIGLU_LIMIT, SWIGLU_LIMIT)
            act = (up + 1.0) * gate * _sigmoid(SWIGLU_ALPHA * gate)
            y = _dot(act.astype(BF16), wd_b16[...]) + bd_ref[0]
            if c + 1 < n_chunks:
                x_next = load_x(c + 1)
            for j in range(rt):
                ybuf[slot, slab(c, j), :] = y[:, j * LANES:(j + 1) * LANES]
        for r in range(rows - chunk, rows):
            row_out(slot, r, dst_ref[0, 0, r]).start(priority=r % 2)

        @pl.when(nv_next == 0)
        def _():
            wait_gather(1 - slot)

            @pl.when(blk >= 1)
            def _():
                wait_scatter(1 - slot)
            wait_scatter(slot)


def _moe(h3_tiles, block_expert, block_valid, row_tok, row_dst, w_gu, b_gu, w_d, b_d):
    ne, d, n2 = w_gu.shape
    assert d == ROW_TILE * LANES
    t = h3_tiles.shape[0] // ROW_TILE
    dff = n2 // 2
    nblk = block_expert.shape[0]
    idx_blk = pl.BlockSpec((1, 1, MOE_ROWS), lambda i, be, bv: (i, 0, 0), memory_space=pltpu.SMEM)
    idx_next = pl.BlockSpec((1, 1, MOE_ROWS), lambda i, be, bv: (jnp.minimum(i + 1, nblk - 1), 0, 0),
                            memory_space=pltpu.SMEM)
    grid_spec = pltpu.PrefetchScalarGridSpec(
        num_scalar_prefetch=2,
        grid=(nblk,),
        in_specs=[idx_blk, idx_next, idx_blk, pl.BlockSpec(memory_space=pl.ANY),
                  pl.BlockSpec((1, d, n2), lambda i, be, bv: (be[i], 0, 0)),
                  pl.BlockSpec((1, 1, n2), lambda i, be, bv: (be[i], 0, 0)),
                  pl.BlockSpec((1, dff, d), lambda i, be, bv: (be[i], 0, 0)),
                  pl.BlockSpec((1, 1, d), lambda i, be, bv: (be[i], 0, 0))],
        out_specs=pl.BlockSpec(memory_space=pl.ANY),
        scratch_shapes=[pltpu.VMEM((2, MOE_ROWS * ROW_TILE, LANES), F32),
                        pltpu.VMEM((2, MOE_ROWS * ROW_TILE, LANES), F32),
                        pltpu.VMEM((d, n2), BF16), pltpu.VMEM((dff, d), BF16),
                        pltpu.SemaphoreType.DMA((2,)), pltpu.SemaphoreType.DMA((2,))],
    )
    tok3 = (row_tok * ROW_TILE).reshape(nblk, 1, MOE_ROWS)
    valid_ext = jnp.concatenate([block_valid, jnp.zeros((1,), jnp.int32)])
    return pl.pallas_call(
        functools.partial(_moe_kernel, dff=dff),
        grid_spec=grid_spec,
        out_shape=jax.ShapeDtypeStruct(((TOP_K * t + MOE_ROWS) * ROW_TILE, LANES), F32),
        compiler_params=_params("arbitrary"),
        name="moe",
    )(block_expert, valid_ext, tok3, tok3, (row_dst * ROW_TILE).reshape(nblk, 1, MOE_ROWS), h3_tiles,
      w_gu, b_gu.reshape(ne, 1, n2), w_d, b_d.reshape(ne, 1, d))


def _route_plan(expert, t):
    n_pairs = t * TOP_K
    e_flat = expert.reshape(-1)
    order = jnp.argsort(e_flat, stable=True).astype(jnp.int32)
    experts = jnp.arange(N_EXPERTS, dtype=jnp.int32)
    counts = jnp.sum((e_flat[:, None] == experts[None, :]).astype(jnp.int32), axis=0)
    padded = (counts + MOE_ROWS - 1) // MOE_ROWS * MOE_ROWS
    pend = jnp.cumsum(padded)
    pstart = pend - padded
    gstart = jnp.cumsum(counts) - counts
    nblk = n_pairs // MOE_ROWS + N_EXPERTS
    blk_row0 = jnp.arange(nblk, dtype=jnp.int32) * MOE_ROWS
    block_expert = jnp.minimum(jnp.sum((pend[None, :] <= blk_row0[:, None]).astype(jnp.int32), axis=1), N_EXPERTS - 1)
    pick = lambda table: jnp.sum(jnp.where(block_expert[:, None] == experts[None, :], table[None, :], 0), axis=1)
    in_group = blk_row0 - pick(pstart)
    block_valid = jnp.where(blk_row0 < pend[-1], jnp.clip(pick(counts) - in_group, 0, MOE_ROWS), 0)
    first_sorted = pick(gstart) + in_group
    local = jnp.arange(MOE_ROWS, dtype=jnp.int32)
    srt = first_sorted[:, None] + local[None, :]
    pair = order[jnp.clip(srt, 0, n_pairs - 1)]
    row_tok = pair // TOP_K
    row_dst = jnp.where(local[None, :] < block_valid[:, None], (pair % TOP_K) * t + row_tok, n_pairs + local[None, :])
    return (block_expert.astype(jnp.int32), block_valid.astype(jnp.int32), row_tok.astype(jnp.int32),
            row_dst.astype(jnp.int32))


def _combine_kernel(x2_ref, route_ref, g_ref, *rest, final):
    y_refs, o_ref = rest[:TOP_K], rest[TOP_K]
    route = route_ref[...]
    gates = [route[:, kk:kk + 1] for kk in range(TOP_K)]
    slabs = []
    for j in range(ROW_TILE):
        acc = x2_ref[:, j * LANES:(j + 1) * LANES]
        for kk in range(TOP_K):
            acc = acc + gates[kk] * y_refs[kk][pl.ds(j, x2_ref.shape[0], stride=ROW_TILE), :]
        slabs.append(acc)
    out = jnp.concatenate(slabs, axis=1)
    o_ref[...] = _rms(out, g_ref[...]) if final else out


def _combine(x2, route, y, g, final):
    t, d = x2.shape
    tiles = t // MID_ROWS
    slot_spec = lambda kk: pl.BlockSpec((MID_ROWS * ROW_TILE, LANES), lambda i: (kk * tiles + i, 0))
    return pl.pallas_call(
        functools.partial(_combine_kernel, final=final),
        grid=(tiles,),
        in_specs=[pl.BlockSpec((MID_ROWS, d), lambda i: (i, 0)), pl.BlockSpec((MID_ROWS, LANES), lambda i: (i, 0)),
                  pl.BlockSpec((1, d), lambda i: (0, 0))] + [slot_spec(kk) for kk in range(TOP_K)],
        out_specs=pl.BlockSpec((MID_ROWS, d), lambda i: (i, 0)),
        out_shape=jax.ShapeDtypeStruct((t, d), F32),
        compiler_params=_params("arbitrary"),
        name="combine",
    )(x2, route, g.reshape(1, d), *([y] * TOP_K))


def kernel(x, mem, norm_mix_g, w_in, gdn_conv_w, gdn_A_log, gdn_dt_bias, gdn_norm_g, moba_norm_g, w_out,
           norm_xattn_g, norm_mem_g, xattn_w_q, xattn_w_kv, xattn_w_o, norm_ffn_g, router_w, router_b,
           w_gate_up, b_gate_up, w_down, b_down, final_norm_g):
    b, s, d = x.shape
    t = b * s
    mem_len = mem.shape[1]
    mw = moba_norm_g.shape[1]
    gw = GDN_HEADS * GDN_HEAD_DIM
    xcur = x.reshape(t, d)
    for l in range(w_in.shape[0]):
        qt, mk, mv, gqkv, gz, gba = _in_proj(xcur, norm_mix_g[l], w_in[l], mw, gw)
        o_moba = _moba(qt, mk, mv, b, s)
        o_gdn = _gdn(gqkv.reshape(b, s, 3 * gw), gz.reshape(b, s, gw), gba.reshape(b, s, LANES),
                     gdn_conv_w[l], gdn_A_log[l], gdn_dt_bias[l], gdn_norm_g[l])
        kv = _mem_kv(mem.reshape(b * mem_len, d), norm_mem_g[l], xattn_w_kv[l], mem_len)
        x2, h3, route = _mid(xcur, o_moba.reshape(t, mw), o_gdn.reshape(t, gw), moba_norm_g[l], w_out[l],
                             norm_xattn_g[l], xattn_w_q[l], kv, xattn_w_o[l], norm_ffn_g[l], router_w[l],
                             router_b[l], s, mem_len)
        expert = route[:, TOP_K:2 * TOP_K].astype(jnp.int32)
        plan = _route_plan(expert, t)
        y = _moe(h3, *plan, w_gate_up[l], b_gate_up[l], w_down[l], b_down[l])
        xcur = _combine(x2, route, y, final_norm_g, l == w_in.shape[0] - 1)
    return xcur.reshape(b, s, d)
```

```python
import functools

import jax
import jax.numpy as jnp
from jax import lax
from jax.experimental import pallas as pl
from jax.experimental.pallas import tpu as pltpu

F32 = jnp.float32
BF16 = jnp.bfloat16

RMS_EPS = 1e-6
MOBA_HEAD_DIM = 64
MOBA_BLOCK = 256
MOBA_TOPK = 3
GDN_HEAD_DIM = 128
GDN_HEADS = 4
GDN_CONV = 4
GDN_CHUNK = 64
XATTN_HEAD_DIM = 128
N_EXPERTS = 32
TOP_K = 4
SWIGLU_LIMIT = 7.0
SWIGLU_ALPHA = 1.702

LANES = 128
ROW_TILE = 8
VMEM_LIMIT = 56 * 1024 * 1024

IN_ROWS = 512
MID_ROWS = 512
MOE_ROWS = 512
MOE_CHUNK = 128
GDN_GROUP = 32
PREP_ROWS = 256
GDN_HALO = 8
SCAN_BATCH = 2
SCAN_ROWS = 1024
NEG_INF = float("-inf")
MOBA_Q_SCALE = 1.4426950408889634 / MOBA_HEAD_DIM ** 0.5
MOBA_MASKED = -1e30


def _params(*sem):
    return pltpu.CompilerParams(dimension_semantics=sem, vmem_limit_bytes=VMEM_LIMIT)


def _rms(x, g):
    return x * lax.rsqrt(jnp.mean(x * x, axis=-1, keepdims=True) + RMS_EPS) * g


def _dot(a, b):
    return jnp.dot(a, b, preferred_element_type=F32)


def _dot_nt(a, b):
    return lax.dot_general(a, b, (((1,), (1,)), ((), ())), preferred_element_type=F32)


def _sigmoid(x):
    return 1.0 / (1.0 + jnp.exp(-x))


def _in_proj_kernel(x_ref, g_ref, w_ref, wt_ref, cw_ref, qt_ref, mk_ref, mv_ref, qkv_ref, gz_ref, gba_ref, buf_ref,
                    *, mw, gw, tiles_per_seq):
    hn = _rms(x_ref[...], g_ref[...]).astype(BF16)
    mm = lambda lo, hi: _dot(hn, w_ref[:, lo:hi])
    rows = x_ref.shape[0]
    halo = GDN_HALO

    @pl.when(pl.program_id(0) % tiles_per_seq == 0)
    def _():
        buf_ref[:halo, :] = jnp.zeros((halo, buf_ref.shape[1]), F32)

    @pl.when(pl.program_id(0) % tiles_per_seq != 0)
    def _():
        buf_ref[:halo, :] = buf_ref[rows:rows + halo, :]

    buf_ref[halo:, :] = mm(2 * mw, 2 * mw + 3 * gw)
    qt_ref[...] = (_dot_nt(wt_ref[...], hn) * MOBA_Q_SCALE).astype(BF16)
    mk_ref[...] = mm(0, mw).astype(BF16)
    mv_ref[...] = mm(mw, 2 * mw).astype(BF16)
    gz_ref[...] = mm(2 * mw + 3 * gw, 2 * mw + 4 * gw)
    gba_ref[...] = mm(2 * mw + 4 * gw, 2 * mw + 4 * gw + LANES)

    for cb in range(3 * gw // LANES):
        cols = slice(cb * LANES, (cb + 1) * LANES)
        taps = [cw_ref[j:j + 1, cols] for j in range(GDN_CONV)]
        for r0 in range(0, rows, PREP_ROWS):
            y = taps[GDN_CONV - 1] * buf_ref[halo + r0:halo + r0 + PREP_ROWS, cols]
            for sft in range(1, GDN_CONV):
                y = y + taps[GDN_CONV - 1 - sft] * buf_ref[halo + r0 - sft:halo + r0 - sft + PREP_ROWS, cols]
            y = y * _sigmoid(y)
            if cb < 2 * GDN_HEADS:
                scale = GDN_HEAD_DIM ** -0.5 if cb < GDN_HEADS else 1.0
                y = y * (lax.rsqrt(jnp.sum(y * y, axis=-1, keepdims=True) + RMS_EPS) * scale)
            qkv_ref[r0:r0 + PREP_ROWS, cols] = y.astype(BF16)


def _in_proj(x2d, g, w_in, conv_w, mw, gw, seq):
    t, d = x2d.shape
    n_real = w_in.shape[1] - mw
    n_pad = 2 * mw + 4 * gw + LANES
    w = jnp.pad(w_in[:, mw:], ((0, 0), (0, n_pad - n_real))).astype(BF16)
    w_t = w_in[:, :mw].T.astype(BF16)
    row = lambda n: pl.BlockSpec((IN_ROWS, n), lambda i: (i, 0))
    const = lambda shape: pl.BlockSpec(shape, lambda i: (0, 0))
    return pl.pallas_call(
        functools.partial(_in_proj_kernel, mw=mw, gw=gw, tiles_per_seq=seq // IN_ROWS),
        grid=(t // IN_ROWS,),
        in_specs=[row(d), const((1, d)), const((d, n_pad)), const((mw, d)), const((GDN_CONV, 3 * gw))],
        out_specs=[pl.BlockSpec((mw, IN_ROWS), lambda i: (0, i)), row(mw), row(mw), row(3 * gw), row(gw),
                   row(LANES)],
        out_shape=[jax.ShapeDtypeStruct((mw, t), BF16), jax.ShapeDtypeStruct((t, mw), BF16),
                   jax.ShapeDtypeStruct((t, mw), BF16),
                   jax.ShapeDtypeStruct((t, 3 * gw), BF16), jax.ShapeDtypeStruct((t, gw), F32),
                   jax.ShapeDtypeStruct((t, LANES), F32)],
        scratch_shapes=[pltpu.VMEM((GDN_HALO + IN_ROWS, 3 * gw), F32)],
        compiler_params=_params("arbitrary"),
        name="in_proj",
    )(x2d, g.reshape(1, d), w, w_t, conv_w)


def _moba_select(g_t, i):
    nb = g_t.shape[0]
    row = lax.broadcasted_iota(jnp.int32, g_t.shape, 0)
    valid = row < i
    sel = jnp.zeros_like(g_t)
    for j in range(nb):
        gj = g_t[j:j + 1, :]
        beats = valid & ((g_t > gj) | ((g_t == gj) & (row < j)))
        rank = jnp.sum(jnp.where(beats, 1.0, 0.0), axis=0, keepdims=True)
        sel = jnp.where(row == j, jnp.where(rank < MOBA_TOPK, 1.0, 0.0), sel)
    return jnp.where(valid, sel, 0.0)


def _dot_tn(a, b):
    return lax.dot_general(a, b, (((0,), (0,)), ((), ())), preferred_element_type=F32)


def _moba_kernel(qt_ref, qt_all_ref, k_ref, v_ref, o_ref, kaug_ref, bias_ref, *, nb):
    i = pl.program_id(2)
    bs = MOBA_BLOCK
    hd = MOBA_HEAD_DIM
    lane = lax.broadcasted_iota(jnp.int32, (1, LANES), 1)
    low = lax.broadcasted_iota(jnp.int32, (LANES, 1), 0) < hd

    @pl.when(i == 0)
    def _():
        kmean = []
        for j in range(nb):
            rows = slice(j * bs, (j + 1) * bs)
            kb = k_ref[0, rows, :]
            kmean.append(jnp.mean(kb.astype(F32), axis=0, keepdims=True))
            kaug_ref[0, rows, :] = jnp.where(lane < hd, kb, jnp.where(lane == hd + j, 1.0, 0.0).astype(BF16))
            kaug_ref[1, rows, :] = jnp.where(lane >= hd, kb, jnp.where(lane == j, 1.0, 0.0).astype(BF16))
        kmean = jnp.concatenate(kmean, axis=0)
        q_all = qt_all_ref[...].astype(F32)
        for h, qh in enumerate((jnp.where(low, q_all, 0.0), jnp.where(low, 0.0, q_all))):
            gate = jnp.dot(kmean, qh, preferred_element_type=F32, precision=lax.Precision.HIGHEST)
            for jq in range(nb):
                sel = _moba_select(gate[:, jq * bs:(jq + 1) * bs], jq)
                bias_ref[h, jq] = jnp.where(sel > 0.5, 0.0, MOBA_MASKED)

    qt = qt_ref[...]
    qtf = qt.astype(F32)
    bias = [bias_ref[h, i] for h in (0, 1)]
    pad = jnp.zeros((hd - nb, bs), F32)
    q_past = (jnp.concatenate([qtf[:hd], bias[0], pad], axis=0).astype(BF16),
              jnp.concatenate([bias[1], pad, qtf[hd:]], axis=0).astype(BF16))
    zero = jnp.zeros_like(qt)
    q_own = (jnp.where(low, qt, zero), jnp.where(low, zero, qt))

    key_ix = lax.broadcasted_iota(jnp.int32, (bs, bs), 0)
    qry_ix = lax.broadcasted_iota(jnp.int32, (bs, bs), 1)
    causal_bias = jnp.where(key_ix <= qry_ix, 0.0, NEG_INF)
    own = pl.ds(pl.multiple_of(i * bs, bs), bs)
    k_own = k_ref[0, own, :]
    v_own = v_ref[0, own, :]

    def attend(width):
        heads = (0, 1)
        parts = [slice(lo * bs, (lo + 1) * bs) for lo in range(width)]
        n = len(parts) + 1
        past_scores = lambda k: [_dot(kaug_ref[h, parts[k - 1], :], q_past[h]) for h in heads]
        scores = {0: [_dot(k_own, q_own[h]) + causal_bias for h in heads]}
        if n > 1:
            scores[1] = past_scores(1)
        m = l = acc = None
        for k in range(n):
            s_k = scores.pop(k)
            m_new = [s.max(axis=0, keepdims=True) for s in s_k]
            if m is not None:
                m_new = [jnp.maximum(m[h], m_new[h]) for h in heads]
            p = [jnp.exp2(s_k[h] - m_new[h]) for h in heads]
            p_sum = [x.sum(axis=0, keepdims=True) for x in p]
            if k + 2 < n:
                scores[k + 2] = past_scores(k + 2)
            values = v_own if k == 0 else v_ref[0, parts[k - 1], :]
            pv = [_dot_tn(values, p[h].astype(BF16)) for h in heads]
            if m is None:
                l, acc = p_sum, pv
            else:
                alpha = [jnp.exp2(m[h] - m_new[h]) for h in heads]
                l = [alpha[h] * l[h] + p_sum[h] for h in heads]
                acc = [alpha[h] * acc[h] + pv[h] for h in heads]
            m = m_new
        o_ref[0] = jnp.where(low, acc[0] / l[0], acc[1] / l[1]).T

    for width in range(nb):
        pl.when(i == width)(functools.partial(attend, width))


def _moba(qt, mk, mv, b, s):
    mw = mk.shape[-1]
    nb = s // MOBA_BLOCK
    seq_blk = pl.BlockSpec((1, s, LANES), lambda bi, hp, i: (bi, 0, hp))
    return pl.pallas_call(
        functools.partial(_moba_kernel, nb=nb),
        grid=(b, mw // LANES, nb),
        in_specs=[pl.BlockSpec((LANES, MOBA_BLOCK), lambda bi, hp, i: (hp, bi * nb + i)),
                  pl.BlockSpec((LANES, s), lambda bi, hp, i: (hp, bi)), seq_blk, seq_blk],
        out_specs=pl.BlockSpec((1, MOBA_BLOCK, LANES), lambda bi, hp, i: (bi, i, hp)),
        out_shape=jax.ShapeDtypeStruct((b, s, mw), F32),
        scratch_shapes=[pltpu.VMEM((2, s, LANES), BF16), pltpu.VMEM((2, nb, nb, MOBA_BLOCK), F32)],
        compiler_params=_params("arbitrary", "arbitrary", "arbitrary"),
        name="moba",
    )(qt, qt, mk.reshape(b, s, mw), mv.reshape(b, s, mw))


def _gdn_gate_kernel(ba_ref, alog_ref, dtb_ref, bg_ref, *, seq):
    x = ba_ref[0]
    lane = lax.broadcasted_iota(jnp.int32, (1, LANES), 1)
    xa = x + dtb_ref[...]
    softplus = jnp.maximum(xa, 0.0) + jnp.log(1.0 + jnp.exp(-jnp.abs(xa)))
    g = jnp.where((lane >= GDN_HEADS) & (lane < 2 * GDN_HEADS), -jnp.exp(alog_ref[...]) * softplus, 0.0)
    pos = lax.broadcasted_iota(jnp.int32, (seq, 1), 0) % GDN_CHUNK
    sft = 1
    while sft < GDN_CHUNK:
        g = g + jnp.where(pos >= sft, pltpu.roll(g, sft, 0), 0.0)
        sft *= 2
    bg_ref[0] = jnp.where(lane < GDN_HEADS, _sigmoid(x), g)


def _unit_lower_inverses(mats):
    c = mats[0].shape[0]
    r = lax.broadcasted_iota(jnp.int32, (c, c), 0)
    cc = lax.broadcasted_iota(jnp.int32, (c, c), 1)
    eye = jnp.where(r == cc, 1.0, 0.0)
    pair = (r // 2) == (cc // 2)
    invs = [eye - jnp.where(pair, a, 0.0) for a in mats]
    size = 4
    while size <= c:
        level = ((r // size) == (cc // size)) & ((r // (size // 2)) != (cc // (size // 2)))
        inv_b = [inv.astype(BF16) for inv in invs]
        left = [_dot(ib, jnp.where(level, a, 0.0).astype(BF16)).astype(BF16) for ib, a in zip(inv_b, mats)]
        invs = [inv - _dot(lf, ib) for inv, lf, ib in zip(invs, left, inv_b)]
        size *= 2
    return invs


def _gdn_intra_kernel(q_ref, k_ref, v_ref, bg_ref, u_ref, w_ref, qd_ref, kd_ref, qk_ref):
    h = pl.program_id(1)
    c = GDN_CHUNK
    chunks = range(GDN_GROUP)
    lane = lax.broadcasted_iota(jnp.int32, (1, LANES), 1)
    r_ix = lax.broadcasted_iota(jnp.int32, (c, c), 0)
    c_ix = lax.broadcasted_iota(jnp.int32, (c, c), 1)
    rows = [slice(gi * c, (gi + 1) * c) for gi in chunks]
    k_b16 = [k_ref[0, rw, :] for rw in rows]
    q_b16 = [q_ref[0, rw, :] for rw in rows]
    bgs = [bg_ref[0, rw, :] for rw in rows]
    beta = [jnp.sum(jnp.where(lane == h, bg, 0.0), axis=1, keepdims=True) for bg in bgs]
    gam = [jnp.sum(jnp.where(lane == GDN_HEADS + h, bg, 0.0), axis=1, keepdims=True) for bg in bgs]
    kb = [kk.astype(F32) * bt for kk, bt in zip(k_b16, beta)]
    kk_raw = [_dot_nt(x.astype(BF16), kk) for x, kk in zip(kb, k_b16)]
    qk_raw = [_dot_nt(qq, kk) for qq, kk in zip(q_b16, k_b16)]
    decay = []
    for gm in gam:
        gam_r = jnp.sum(jnp.where(r_ix == c_ix, gm, 0.0), axis=0, keepdims=True)
        decay.append(jnp.exp(jnp.where(c_ix <= r_ix, gm - gam_r, NEG_INF)))
    t_inv = _unit_lower_inverses([jnp.where(c_ix < r_ix, x * dc, 0.0) for x, dc in zip(kk_raw, decay)])
    eg = [jnp.exp(gm) for gm in gam]
    rhs = [jnp.concatenate([v_ref[0, rw, :].astype(F32) * bt, x * e], axis=1).astype(BF16)
           for rw, bt, x, e in zip(rows, beta, kb, eg)]
    uw = [_dot(ti.astype(BF16), rh) for ti, rh in zip(t_inv, rhs)]
    for gi in chunks:
        rw = rows[gi]
        u_ref[0, rw, :] = uw[gi][:, :LANES].astype(BF16)
        w_ref[0, rw, :] = uw[gi][:, LANES:].astype(BF16)
        qk_ref[0, 0, rw, :] = (qk_raw[gi] * decay[gi]).astype(BF16)
        qd_ref[0, rw, :] = (q_b16[gi].astype(F32) * eg[gi]).astype(BF16)
        kd_ref[0, rw, :] = (k_b16[gi].astype(F32) * jnp.exp(gam[gi][c - 1:c, :] - gam[gi])).astype(BF16)


def _gdn_scan_kernel(u_ref, w_ref, qd_ref, kd_ref, qk_ref, z_ref, bg_ref, ng_ref, o_ref, state_ref, *, rows_per_step):
    c = GDN_CHUNK
    ng = ng_ref[...]
    chains = [(bi, h) for bi in range(SCAN_BATCH) for h in range(GDN_HEADS)]
    cols = [slice(h * GDN_HEAD_DIM, (h + 1) * GDN_HEAD_DIM) for h in range(GDN_HEADS)]

    @pl.when(pl.program_id(1) == 0)
    def _():
        state_ref[...] = jnp.zeros(state_ref.shape, F32)

    def step(n, states):
        r0 = pl.multiple_of(n * c, c)
        rows = pl.ds(r0, c)
        tails = [bg_ref[bi, pl.ds(r0 + c - 8, 8), :] for bi in range(SCAN_BATCH)]
        s_b = [st.astype(BF16) for st in states]
        ws = [_dot(w_ref[bi, rows, cols[h]], s_b[k]) for k, (bi, h) in enumerate(chains)]
        qs = [_dot(qd_ref[bi, rows, cols[h]], s_b[k]) for k, (bi, h) in enumerate(chains)]
        v_b = [(u_ref[bi, rows, cols[h]].astype(F32) - ws[k]).astype(BF16) for k, (bi, h) in enumerate(chains)]
        kd_v = [_dot_tn(kd_ref[bi, rows, cols[h]], v_b[k]) for k, (bi, h) in enumerate(chains)]
        qkv = [_dot(qk_ref[bi, h, rows, :], v_b[k]) for k, (bi, h) in enumerate(chains)]
        new_states = []
        for k, (bi, h) in enumerate(chains):
            g_last = tails[bi][7:8, GDN_HEADS + h:GDN_HEADS + h + 1]
            new_states.append(states[k] * jnp.exp(g_last) + kd_v[k])
            z = z_ref[bi, rows, cols[h]]
            o_ref[bi, rows, cols[h]] = _rms(qs[k] + qkv[k], ng) * (z * _sigmoid(z))
        return tuple(new_states)

    init = tuple(state_ref[k] for k in range(len(chains)))
    final = lax.fori_loop(0, rows_per_step // c, step, init)
    for k, st in enumerate(final):
        state_ref[k] = st


def _gdn(qkv, gz, gba, a_log, dt_bias, norm_g):
    b, s, w3 = qkv.shape
    nh = GDN_HEADS
    hw = nh * GDN_HEAD_DIM
    c = GDN_CHUNK
    lane_pad = lambda v: jnp.pad(v.reshape(1, -1), ((0, 0), (nh, LANES - 2 * nh)))
    seq_blk = pl.BlockSpec((1, s, LANES), lambda bi: (bi, 0, 0))
    const = pl.BlockSpec((1, LANES), lambda bi: (0, 0))
    bg = pl.pallas_call(
        functools.partial(_gdn_gate_kernel, seq=s),
        grid=(b,),
        in_specs=[seq_blk, const, const],
        out_specs=seq_blk,
        out_shape=jax.ShapeDtypeStruct((b, s, LANES), F32),
        compiler_params=_params("arbitrary"),
        name="gdn_gate",
    )(gba, lane_pad(a_log), lane_pad(dt_bias))

    grp = GDN_GROUP * c
    head_blk = lambda off: pl.BlockSpec((1, grp, LANES), lambda bi, h, n: (bi, n, off + h))
    head_shape = jax.ShapeDtypeStruct((b, s, hw), BF16)
    u, w, qd, kd, qk = pl.pallas_call(
        _gdn_intra_kernel,
        grid=(b, nh, s // grp),
        in_specs=[head_blk(0), head_blk(nh), head_blk(2 * nh),
                  pl.BlockSpec((1, grp, LANES), lambda bi, h, n: (bi, n, 0))],
        out_specs=[head_blk(0)] * 4 + [pl.BlockSpec((1, 1, grp, c), lambda bi, h, n: (bi, h, n, 0))],
        out_shape=[head_shape] * 4 + [jax.ShapeDtypeStruct((b, nh, s, c), BF16)],
        compiler_params=_params("arbitrary", "arbitrary", "arbitrary"),
        name="gdn_intra",
    )(qkv, qkv, qkv, bg)

    tile = lambda width: pl.BlockSpec((SCAN_BATCH, SCAN_ROWS, width), lambda bi, ti: (bi, ti, 0))
    return pl.pallas_call(
        functools.partial(_gdn_scan_kernel, rows_per_step=SCAN_ROWS),
        grid=(b // SCAN_BATCH, s // SCAN_ROWS),
        in_specs=[tile(hw), tile(hw), tile(hw), tile(hw),
                  pl.BlockSpec((SCAN_BATCH, nh, SCAN_ROWS, c), lambda bi, ti: (bi, 0, ti, 0)), tile(hw), tile(LANES),
                  pl.BlockSpec((1, LANES), lambda bi, ti: (0, 0))],
        out_specs=tile(hw),
        out_shape=jax.ShapeDtypeStruct((b, s, hw), F32),
        scratch_shapes=[pltpu.VMEM((SCAN_BATCH * nh, GDN_HEAD_DIM, GDN_HEAD_DIM), F32)],
        compiler_params=_params("arbitrary", "arbitrary"),
        name="gdn_scan",
    )(u, w, qd, kd, qk, gz, bg, norm_g.reshape(1, -1))


def _mem_kv_kernel(m_ref, g_ref, w_ref, kv_ref):
    kv_ref[...] = _dot(_rms(m_ref[...], g_ref[...]).astype(BF16), w_ref[...]).astype(BF16)


def _mem_kv(mem2d, g, w_kv, rows):
    t, d = mem2d.shape
    n = w_kv.shape[1]
    return pl.pallas_call(
        _mem_kv_kernel,
        grid=(t // rows,),
        in_specs=[pl.BlockSpec((rows, d), lambda i: (i, 0)), pl.BlockSpec((1, d), lambda i: (0, 0)),
                  pl.BlockSpec((d, n), lambda i: (0, 0))],
        out_specs=pl.BlockSpec((rows, n), lambda i: (i, 0)),
        out_shape=jax.ShapeDtypeStruct((t, n), BF16),
        compiler_params=_params("arbitrary"),
        name="mem_kv",
    )(mem2d, g.reshape(1, d), w_kv.astype(BF16))


def _mid_kernel(x_ref, om_ref, og_ref, mg_ref, wout_ref, xg_ref, wq_ref, kv_ref, wo_ref, fg_ref, rw_ref, rb_ref,
                x2_ref, h3_ref, route_ref, *, mw, xw):
    mo = _rms(om_ref[...], mg_ref[...]).astype(BF16)
    x1 = x_ref[...] + _dot(mo, wout_ref[:mw, :]) + _dot(og_ref[...].astype(BF16), wout_ref[mw:, :])

    h2 = _rms(x1, xg_ref[...]).astype(BF16)
    q = (_dot(h2, wq_ref[...]) * (XATTN_HEAD_DIM ** -0.5)).astype(BF16)
    head_cols = [slice(h * XATTN_HEAD_DIM, (h + 1) * XATTN_HEAD_DIM) for h in range(xw // XATTN_HEAD_DIM)]
    scores = [_dot_nt(q[:, sl], kv_ref[:, sl]) for sl in head_cols]
    probs = [jnp.exp(s - jnp.max(s, axis=1, keepdims=True)) for s in scores]
    heads = [_dot(p.astype(BF16), kv_ref[:, xw + sl.start:xw + sl.stop]) / jnp.sum(p, axis=1, keepdims=True)
             for p, sl in zip(probs, head_cols)]
    x2 = x1 + _dot(jnp.concatenate(heads, axis=1).astype(BF16), wo_ref[...])
    x2_ref[...] = x2

    h3 = _rms(x2, fg_ref[...])
    slabs = h3.shape[1] // LANES
    for j in range(slabs):
        h3_ref[pl.ds(j, h3.shape[0], stride=slabs), :] = h3[:, j * LANES:(j + 1) * LANES]
    ne = rb_ref.shape[0]
    h_hi = h3.astype(BF16)
    h_lo = (h3 - h_hi.astype(F32)).astype(BF16)
    by_hi = _dot_nt(rw_ref[...], h_hi)
    logits = by_hi[:ne, :] + by_hi[ne:, :] + _dot_nt(rw_ref[:ne, :], h_lo) + rb_ref[...]
    row = lax.broadcasted_iota(jnp.int32, logits.shape, 0)
    weights, picks, top = [], [], None
    for kk in range(TOP_K):
        m = jnp.max(logits, axis=0, keepdims=True)
        idx = jnp.min(jnp.where(logits == m, row, ne), axis=0, keepdims=True)
        logits = jnp.where(row == idx, NEG_INF, logits)
        top = m if top is None else top
        weights.append(jnp.exp(m - top))
        picks.append(idx.astype(F32))
    denom = sum(weights[1:], weights[0])
    rows = [wk / denom for wk in weights] + picks
    rows.append(jnp.zeros((LANES - len(rows), logits.shape[1]), F32))
    route_ref[...] = jnp.concatenate(rows, axis=0).T


def _mid(x2d, om, og, moba_g, w_out, xattn_g, w_q, kv, w_o, ffn_g, router_w, router_b, seq, mem_len):
    t, d = x2d.shape
    mw, gw, xw, ne = om.shape[1], og.shape[1], w_q.shape[1], router_w.shape[1]
    tiles_per_seq = seq // MID_ROWS
    row = lambda n: pl.BlockSpec((MID_ROWS, n), lambda i: (i, 0))
    const = lambda r, c: pl.BlockSpec((r, c), lambda i: (0, 0))
    rw_hi = router_w.T.astype(BF16)
    rw_lo = (router_w.T - rw_hi.astype(F32)).astype(BF16)
    return pl.pallas_call(
        functools.partial(_mid_kernel, mw=mw, xw=xw),
        grid=(t // MID_ROWS,),
        in_specs=[row(d), row(mw), row(gw), const(1, mw), const(mw + gw, d), const(1, d), const(d, xw),
                  pl.BlockSpec((mem_len, 2 * xw), lambda i: (i // tiles_per_seq, 0)),
                  const(xw, d), const(1, d), const(2 * ne, d), const(ne, 1)],
        out_specs=[row(d), pl.BlockSpec((MID_ROWS * (d // LANES), LANES), lambda i: (i, 0)), row(LANES)],
        out_shape=[jax.ShapeDtypeStruct((t, d), F32), jax.ShapeDtypeStruct((t * (d // LANES), LANES), F32),
                   jax.ShapeDtypeStruct((t, LANES), F32)],
        compiler_params=_params("arbitrary"),
        name="mid",
    )(x2d, om, og, moba_g.reshape(1, mw), w_out.astype(BF16), xattn_g.reshape(1, d), w_q.astype(BF16), kv,
      w_o.astype(BF16), ffn_g.reshape(1, d), jnp.concatenate([rw_hi, rw_lo], axis=0), router_b.reshape(ne, 1))


def _moe_kernel(be_ref, bv_ref, tok_ref, tokn_ref, dst_ref, h_hbm, wgu_ref, bgu_ref, wd_ref, bd_ref, y_hbm,
                xbuf, ybuf, wgu_b16, wd_b16, sem_in, sem_out, *, dff):
    blk = pl.program_id(0)
    nv = bv_ref[blk]
    nv_next = bv_ref[blk + 1]
    slot = blk % 2
    rows = MOE_ROWS
    chunk = MOE_CHUNK
    n_chunks = rows // chunk
    rt = ROW_TILE

    def row_in(s, r, t):
        return pltpu.make_async_copy(h_hbm.at[pl.ds(pl.multiple_of(t, rt), rt), :],
                                     xbuf.at[s, pl.ds(r * rt, rt), :], sem_in.at[s])

    def row_out(s, r, d):
        return pltpu.make_async_copy(ybuf.at[s, pl.ds(r * rt, rt), :],
                                     y_hbm.at[pl.ds(pl.multiple_of(d, rt), rt), :], sem_out.at[s])

    def wait_gather(s):
        pltpu.make_async_copy(h_hbm.at[pl.ds(0, rows * rt), :], xbuf.at[s], sem_in.at[s]).wait()

    def wait_scatter(s):
        pltpu.make_async_copy(ybuf.at[s], y_hbm.at[pl.ds(0, rows * rt), :], sem_out.at[s]).wait()

    @pl.when(blk == 0)
    def _():
        ybuf[1] = jnp.zeros(ybuf.shape[1:], F32)
        spare = pltpu.make_async_copy(ybuf.at[1], y_hbm.at[pl.ds(y_hbm.shape[0] - rows * rt, rows * rt), :],
                                      sem_out.at[1])
        spare.start()
        spare.wait()

    @pl.when((blk == 0) & (nv > 0))
    def _():
        for r in range(rows):
            row_in(0, r, tok_ref[0, 0, r]).start(priority=r % 2)

    @pl.when(nv > 0)
    def _():
        @pl.when((blk == 0) | (be_ref[blk] != be_ref[jnp.maximum(blk - 1, 0)]))
        def _():
            wgu_b16[...] = wgu_ref[0].astype(BF16)
            wd_b16[...] = wd_ref[0].astype(BF16)

        wait_gather(slot)

        @pl.when(blk >= 2)
        def _():
            wait_scatter(slot)

        slab = lambda c, j: pl.ds(c * chunk * rt + j, chunk, stride=rt)
        load_x = lambda c: jnp.concatenate([xbuf[slot, slab(c, j), :] for j in range(rt)], axis=1).astype(BF16)
        per_chunk_in = rows // (n_chunks // 2)
        x_next = load_x(0)
        for c in range(n_chunks):
            x = x_next
            if c < n_chunks // 2:
                for r in range(c * per_chunk_in, (c + 1) * per_chunk_in):
                    row_in(1 - slot, r, tokn_ref[0, 0, r]).start(priority=r % 2)
            if c:
                for r in range((c - 1) * chunk, c * chunk):
                    row_out(slot, r, dst_ref[0, 0, r]).start(priority=r % 2)
            gu = _dot(x, wgu_b16[...]) + bgu_ref[0]
            gate = jnp.minimum(gu[:, :dff], SWIGLU_LIMIT)
            up = jnp.clip(gu[:, dff:], -SWIGLU_LIMIT, SWIGLU_LIMIT)
            act = (up + 1.0) * gate * _sigmoid(SWIGLU_ALPHA * gate)
            y = _dot(act.astype(BF16), wd_b16[...]) + bd_ref[0]
            if c + 1 < n_chunks:
                x_next = load_x(c + 1)
            for j in range(rt):
                ybuf[slot, slab(c, j), :] = y[:, j * LANES:(j + 1) * LANES]
        for r in range(rows - chunk, rows):
            row_out(slot, r, dst_ref[0, 0, r]).start(priority=r % 2)

        @pl.when(nv_next == 0)
        def _():
            wait_gather(1 - slot)

            @pl.when(blk >= 1)
            def _():
                wait_scatter(1 - slot)
            wait_scatter(slot)


def _moe(h3_tiles, block_expert, block_valid, row_tok, row_dst, w_gu, b_gu, w_d, b_d):
    ne, d, n2 = w_gu.shape
    assert d == ROW_TILE * LANES
    t = h3_tiles.shape[0] // ROW_TILE
    dff = n2 // 2
    nblk = block_expert.shape[0]
    idx_blk = pl.BlockSpec((1, 1, MOE_ROWS), lambda i, be, bv: (i, 0, 0), memory_space=pltpu.SMEM)
    idx_next = pl.BlockSpec((1, 1, MOE_ROWS), lambda i, be, bv: (jnp.minimum(i + 1, nblk - 1), 0, 0),
                            memory_space=pltpu.SMEM)
    grid_spec = pltpu.PrefetchScalarGridSpec(
        num_scalar_prefetch=2,
        grid=(nblk,),
        in_specs=[idx_blk, idx_next, idx_blk, pl.BlockSpec(memory_space=pl.ANY),
                  pl.BlockSpec((1, d, n2), lambda i, be, bv: (be[i], 0, 0)),
                  pl.BlockSpec((1, 1, n2), lambda i, be, bv: (be[i], 0, 0)),
                  pl.BlockSpec((1, dff, d), lambda i, be, bv: (be[i], 0, 0)),
                  pl.BlockSpec((1, 1, d), lambda i, be, bv: (be[i], 0, 0))],
        out_specs=pl.BlockSpec(memory_space=pl.ANY),
        scratch_shapes=[pltpu.VMEM((2, MOE_ROWS * ROW_TILE, LANES), F32),
                        pltpu.VMEM((2, MOE_ROWS * ROW_TILE, LANES), F32),
                        pltpu.VMEM((d, n2), BF16), pltpu.VMEM((dff, d), BF16),
                        pltpu.SemaphoreType.DMA((2,)), pltpu.SemaphoreType.DMA((2,))],
    )
    tok3 = (row_tok * ROW_TILE).reshape(nblk, 1, MOE_ROWS)
    valid_ext = jnp.concatenate([block_valid, jnp.zeros((1,), jnp.int32)])
    return pl.pallas_call(
        functools.partial(_moe_kernel, dff=dff),
        grid_spec=grid_spec,
        out_shape=jax.ShapeDtypeStruct(((TOP_K * t + MOE_ROWS) * ROW_TILE, LANES), F32),
        compiler_params=_params("arbitrary"),
        name="moe",
    )(block_expert, valid_ext, tok3, tok3, (row_dst * ROW_TILE).reshape(nblk, 1, MOE_ROWS), h3_tiles,
      w_gu, b_gu.reshape(ne, 1, n2), w_d, b_d.reshape(ne, 1, d))


def _route_plan(expert, t):
    n_pairs = t * TOP_K
    e_flat = expert.reshape(-1)
    order = jnp.argsort(e_flat, stable=True).astype(jnp.int32)
    experts = jnp.arange(N_EXPERTS, dtype=jnp.int32)
    counts = jnp.sum((e_flat[:, None] == experts[None, :]).astype(jnp.int32), axis=0)
    padded = (counts + MOE_ROWS - 1) // MOE_ROWS * MOE_ROWS
    pend = jnp.cumsum(padded)
    pstart = pend - padded
    gstart = jnp.cumsum(counts) - counts
    nblk = n_pairs // MOE_ROWS + N_EXPERTS
    blk_row0 = jnp.arange(nblk, dtype=jnp.int32) * MOE_ROWS
    block_expert = jnp.minimum(jnp.sum((pend[None, :] <= blk_row0[:, None]).astype(jnp.int32), axis=1), N_EXPERTS - 1)
    pick = lambda table: jnp.sum(jnp.where(block_expert[:, None] == experts[None, :], table[None, :], 0), axis=1)
    in_group = blk_row0 - pick(pstart)
    block_valid = jnp.where(blk_row0 < pend[-1], jnp.clip(pick(counts) - in_group, 0, MOE_ROWS), 0)
    first_sorted = pick(gstart) + in_group
    local = jnp.arange(MOE_ROWS, dtype=jnp.int32)
    srt = first_sorted[:, None] + local[None, :]
    pair = order[jnp.clip(srt, 0, n_pairs - 1)]
    row_tok = pair // TOP_K
    row_dst = jnp.where(local[None, :] < block_valid[:, None], (pair % TOP_K) * t + row_tok, n_pairs + local[None, :])
    return (block_expert.astype(jnp.int32), block_valid.astype(jnp.int32), row_tok.astype(jnp.int32),
            row_dst.astype(jnp.int32))


def _combine_kernel(x2_ref, route_ref, g_ref, *rest, final):
    y_refs, o_ref = rest[:TOP_K], rest[TOP_K]
    route = route_ref[...]
    gates = [route[:, kk:kk + 1] for kk in range(TOP_K)]
    slabs = []
    for j in range(ROW_TILE):
        acc = x2_ref[:, j * LANES:(j + 1) * LANES]
        for kk in range(TOP_K):
            acc = acc + gates[kk] * y_refs[kk][pl.ds(j, x2_ref.shape[0], stride=ROW_TILE), :]
        slabs.append(acc)
    out = jnp.concatenate(slabs, axis=1)
    o_ref[...] = _rms(out, g_ref[...]) if final else out


def _combine(x2, route, y, g, final):
    t, d = x2.shape
    tiles = t // MID_ROWS
    slot_spec = lambda kk: pl.BlockSpec((MID_ROWS * ROW_TILE, LANES), lambda i: (kk * tiles + i, 0))
    return pl.pallas_call(
        functools.partial(_combine_kernel, final=final),
        grid=(tiles,),
        in_specs=[pl.BlockSpec((MID_ROWS, d), lambda i: (i, 0)), pl.BlockSpec((MID_ROWS, LANES), lambda i: (i, 0)),
                  pl.BlockSpec((1, d), lambda i: (0, 0))] + [slot_spec(kk) for kk in range(TOP_K)],
        out_specs=pl.BlockSpec((MID_ROWS, d), lambda i: (i, 0)),
        out_shape=jax.ShapeDtypeStruct((t, d), F32),
        compiler_params=_params("arbitrary"),
        name="combine",
    )(x2, route, g.reshape(1, d), *([y] * TOP_K))


def kernel(x, mem, norm_mix_g, w_in, gdn_conv_w, gdn_A_log, gdn_dt_bias, gdn_norm_g, moba_norm_g, w_out,
           norm_xattn_g, norm_mem_g, xattn_w_q, xattn_w_kv, xattn_w_o, norm_ffn_g, router_w, router_b,
           w_gate_up, b_gate_up, w_down, b_down, final_norm_g):
    b, s, d = x.shape
    t = b * s
    mem_len = mem.shape[1]
    mw = moba_norm_g.shape[1]
    gw = GDN_HEADS * GDN_HEAD_DIM
    xcur = x.reshape(t, d)
    for l in range(w_in.shape[0]):
        qt, mk, mv, gqkv, gz, gba = _in_proj(xcur, norm_mix_g[l], w_in[l], gdn_conv_w[l], mw, gw, s)
        o_moba = _moba(qt, mk, mv, b, s)
        o_gdn = _gdn(gqkv.reshape(b, s, 3 * gw), gz.reshape(b, s, gw), gba.reshape(b, s, LANES),
                     gdn_A_log[l], gdn_dt_bias[l], gdn_norm_g[l])
        kv = _mem_kv(mem.reshape(b * mem_len, d), norm_mem_g[l], xattn_w_kv[l], mem_len)
        x2, h3, route = _mid(xcur, o_moba.reshape(t, mw), o_gdn.reshape(t, gw), moba_norm_g[l], w_out[l],
                             norm_xattn_g[l], xattn_w_q[l], kv, xattn_w_o[l], norm_ffn_g[l], router_w[l],
                             router_b[l], s, mem_len)
        expert = route[:, TOP_K:2 * TOP_K].astype(jnp.int32)
        plan = _route_plan(expert, t)
        y = _moe(h3, *plan, w_gate_up[l], b_gate_up[l], w_down[l], b_down[l])
        xcur = _combine(x2, route, y, final_norm_g, l == w_in.shape[0] - 1)
    return xcur.reshape(b, s, d)
```

```python
import functools

import jax
import jax.numpy as jnp
from jax import lax
from jax.experimental import pallas as pl
from jax.experimental.pallas import tpu as pltpu

F32 = jnp.float32
BF16 = jnp.bfloat16

RMS_EPS = 1e-6
MOBA_HEAD_DIM = 64
MOBA_BLOCK = 256
MOBA_TOPK = 3
GDN_HEAD_DIM = 128
GDN_HEADS = 4
GDN_CONV = 4
GDN_CHUNK = 64
XATTN_HEAD_DIM = 128
N_EXPERTS = 32
TOP_K = 4
SWIGLU_LIMIT = 7.0
SWIGLU_ALPHA = 1.702

LANES = 128
ROW_TILE = 8
VMEM_LIMIT = 56 * 1024 * 1024

IN_ROWS = 512
MID_ROWS = 512
MOE_ROWS = 512
MOE_CHUNK = 128
GDN_GROUP = 32
PREP_ROWS = 256
GDN_HALO = 8
SCAN_BATCH = 2
SCAN_ROWS = 1024
NEG_INF = float("-inf")
MOBA_Q_SCALE = 1.4426950408889634 / MOBA_HEAD_DIM ** 0.5
MOBA_MASKED = -1e30


def _params(*sem):
    return pltpu.CompilerParams(dimension_semantics=sem, vmem_limit_bytes=VMEM_LIMIT)


def _rms(x, g):
    return x * lax.rsqrt(jnp.mean(x * x, axis=-1, keepdims=True) + RMS_EPS) * g


def _dot(a, b):
    return jnp.dot(a, b, preferred_element_type=F32)


def _dot_nt(a, b):
    return lax.dot_general(a, b, (((1,), (1,)), ((), ())), preferred_element_type=F32)


def _sigmoid(x):
    return 1.0 / (1.0 + jnp.exp(-x))


def _in_proj_kernel(x_ref, g_ref, w_ref, wt_ref, cw_ref, qt_ref, mk_ref, mv_ref, qkv_ref, gz_ref, gba_ref, *bufs,
                    mw, gw, tiles_per_seq):
    hn = _rms(x_ref[...], g_ref[...]).astype(BF16)
    mm = lambda lo, hi: _dot(hn, w_ref[:, lo:hi])
    rows = x_ref.shape[0]
    halo = GDN_HALO

    @pl.when(pl.program_id(0) % tiles_per_seq == 0)
    def _():
        for buf in bufs:
            buf[:halo, :] = jnp.zeros((halo, LANES), F32)

    @pl.when(pl.program_id(0) % tiles_per_seq != 0)
    def _():
        for buf in bufs:
            buf[:halo, :] = buf[rows:rows + halo, :]

    for grp in range(3):
        raw = mm(2 * mw + grp * gw, 2 * mw + (grp + 1) * gw)
        for h in range(GDN_HEADS):
            bufs[grp * GDN_HEADS + h][halo:, :] = raw[:, h * LANES:(h + 1) * LANES]
    qt_ref[...] = (_dot_nt(wt_ref[...], hn) * MOBA_Q_SCALE).astype(BF16)
    mk_ref[...] = mm(0, mw).astype(BF16)
    mv_ref[...] = mm(mw, 2 * mw).astype(BF16)
    gz_ref[...] = mm(2 * mw + 3 * gw, 2 * mw + 4 * gw)
    gba_ref[...] = mm(2 * mw + 4 * gw, 2 * mw + 4 * gw + LANES)

    for cb in range(3 * gw // LANES):
        cols = slice(cb * LANES, (cb + 1) * LANES)
        taps = [cw_ref[j:j + 1, cols] for j in range(GDN_CONV)]
        buf = bufs[cb]
        for r0 in range(0, rows, PREP_ROWS):
            y = taps[GDN_CONV - 1] * buf[halo + r0:halo + r0 + PREP_ROWS, :]
            for sft in range(1, GDN_CONV):
                y = y + taps[GDN_CONV - 1 - sft] * buf[halo + r0 - sft:halo + r0 - sft + PREP_ROWS, :]
            y = y * _sigmoid(y)
            if cb < 2 * GDN_HEADS:
                scale = GDN_HEAD_DIM ** -0.5 if cb < GDN_HEADS else 1.0
                y = y * (lax.rsqrt(jnp.sum(y * y, axis=-1, keepdims=True) + RMS_EPS) * scale)
            qkv_ref[r0:r0 + PREP_ROWS, cols] = y.astype(BF16)


def _in_proj(x2d, g, w_in, conv_w, mw, gw, seq):
    t, d = x2d.shape
    n_real = w_in.shape[1] - mw
    n_pad = 2 * mw + 4 * gw + LANES
    w = jnp.pad(w_in[:, mw:], ((0, 0), (0, n_pad - n_real))).astype(BF16)
    w_t = w_in[:, :mw].T.astype(BF16)
    row = lambda n: pl.BlockSpec((IN_ROWS, n), lambda i: (i, 0))
    const = lambda shape: pl.BlockSpec(shape, lambda i: (0, 0))
    return pl.pallas_call(
        functools.partial(_in_proj_kernel, mw=mw, gw=gw, tiles_per_seq=seq // IN_ROWS),
        grid=(t // IN_ROWS,),
        in_specs=[row(d), const((1, d)), const((d, n_pad)), const((mw, d)), const((GDN_CONV, 3 * gw))],
        out_specs=[pl.BlockSpec((mw, IN_ROWS), lambda i: (0, i)), row(mw), row(mw), row(3 * gw), row(gw),
                   row(LANES)],
        out_shape=[jax.ShapeDtypeStruct((mw, t), BF16), jax.ShapeDtypeStruct((t, mw), BF16),
                   jax.ShapeDtypeStruct((t, mw), BF16),
                   jax.ShapeDtypeStruct((t, 3 * gw), BF16), jax.ShapeDtypeStruct((t, gw), F32),
                   jax.ShapeDtypeStruct((t, LANES), F32)],
        scratch_shapes=[pltpu.VMEM((GDN_HALO + IN_ROWS, LANES), F32)] * (3 * gw // LANES),
        compiler_params=_params("arbitrary"),
        name="in_proj",
    )(x2d, g.reshape(1, d), w, w_t, conv_w)


def _moba_select(g_t, i):
    nb = g_t.shape[0]
    row = lax.broadcasted_iota(jnp.int32, g_t.shape, 0)
    valid = row < i
    sel = jnp.zeros_like(g_t)
    for j in range(nb):
        gj = g_t[j:j + 1, :]
        beats = valid & ((g_t > gj) | ((g_t == gj) & (row < j)))
        rank = jnp.sum(jnp.where(beats, 1.0, 0.0), axis=0, keepdims=True)
        sel = jnp.where(row == j, jnp.where(rank < MOBA_TOPK, 1.0, 0.0), sel)
    return jnp.where(valid, sel, 0.0)


def _dot_tn(a, b):
    return lax.dot_general(a, b, (((0,), (0,)), ((), ())), preferred_element_type=F32)


def _moba_kernel(qt_ref, qt_all_ref, k_ref, v_ref, o_ref, kaug_ref, bias_ref, *, nb):
    i = pl.program_id(2)
    bs = MOBA_BLOCK
    hd = MOBA_HEAD_DIM
    lane = lax.broadcasted_iota(jnp.int32, (1, LANES), 1)
    low = lax.broadcasted_iota(jnp.int32, (LANES, 1), 0) < hd

    @pl.when(i == 0)
    def _():
        kmean = []
        for j in range(nb):
            rows = slice(j * bs, (j + 1) * bs)
            kb = k_ref[0, rows, :]
            kmean.append(jnp.mean(kb.astype(F32), axis=0, keepdims=True))
            kaug_ref[0, rows, :] = jnp.where(lane < hd, kb, jnp.where(lane == hd + j, 1.0, 0.0).astype(BF16))
            kaug_ref[1, rows, :] = jnp.where(lane >= hd, kb, jnp.where(lane == j, 1.0, 0.0).astype(BF16))
        kmean = jnp.concatenate(kmean, axis=0)
        q_all = qt_all_ref[...].astype(F32)
        for h, qh in enumerate((jnp.where(low, q_all, 0.0), jnp.where(low, 0.0, q_all))):
            gate = jnp.dot(kmean, qh, preferred_element_type=F32, precision=lax.Precision.HIGHEST)
            for jq in range(nb):
                sel = _moba_select(gate[:, jq * bs:(jq + 1) * bs], jq)
                bias_ref[h, jq] = jnp.where(sel > 0.5, 0.0, MOBA_MASKED)

    qt = qt_ref[...]
    qtf = qt.astype(F32)
    bias = [bias_ref[h, i] for h in (0, 1)]
    pad = jnp.zeros((hd - nb, bs), F32)
    q_past = (jnp.concatenate([qtf[:hd], bias[0], pad], axis=0).astype(BF16),
              jnp.concatenate([bias[1], pad, qtf[hd:]], axis=0).astype(BF16))
    zero = jnp.zeros_like(qt)
    q_own = (jnp.where(low, qt, zero), jnp.where(low, zero, qt))

    key_ix = lax.broadcasted_iota(jnp.int32, (bs, bs), 0)
    qry_ix = lax.broadcasted_iota(jnp.int32, (bs, bs), 1)
    causal_bias = jnp.where(key_ix <= qry_ix, 0.0, NEG_INF)
    own = pl.ds(pl.multiple_of(i * bs, bs), bs)
    k_own = k_ref[0, own, :]
    v_own = v_ref[0, own, :]

    def attend(width):
        heads = (0, 1)
        parts = [slice(lo * bs, (lo + 1) * bs) for lo in range(width)]
        n = len(parts) + 1
        past_scores = lambda k: [_dot(kaug_ref[h, parts[k - 1], :], q_past[h]) for h in heads]
        scores = {0: [_dot(k_own, q_own[h]) + causal_bias for h in heads]}
        if n > 1:
            scores[1] = past_scores(1)
        m = l = acc = None
        for k in range(n):
            s_k = scores.pop(k)
            m_new = [s.max(axis=0, keepdims=True) for s in s_k]
            if m is not None:
                m_new = [jnp.maximum(m[h], m_new[h]) for h in heads]
            p = [jnp.exp2(s_k[h] - m_new[h]) for h in heads]
            p_sum = [x.sum(axis=0, keepdims=True) for x in p]
            if k + 2 < n:
                scores[k + 2] = past_scores(k + 2)
            values = v_own if k == 0 else v_ref[0, parts[k - 1], :]
            pv = [_dot_tn(values, p[h].astype(BF16)) for h in heads]
            if m is None:
                l, acc = p_sum, pv
            else:
                alpha = [jnp.exp2(m[h] - m_new[h]) for h in heads]
                l = [alpha[h] * l[h] + p_sum[h] for h in heads]
                acc = [alpha[h] * acc[h] + pv[h] for h in heads]
            m = m_new
        o_ref[0] = jnp.where(low, acc[0] / l[0], acc[1] / l[1]).T

    for width in range(nb):
        pl.when(i == width)(functools.partial(attend, width))


def _moba(qt, mk, mv, b, s):
    mw = mk.shape[-1]
    nb = s // MOBA_BLOCK
    seq_blk = pl.BlockSpec((1, s, LANES), lambda bi, hp, i: (bi, 0, hp))
    return pl.pallas_call(
        functools.partial(_moba_kernel, nb=nb),
        grid=(b, mw // LANES, nb),
        in_specs=[pl.BlockSpec((LANES, MOBA_BLOCK), lambda bi, hp, i: (hp, bi * nb + i)),
                  pl.BlockSpec((LANES, s), lambda bi, hp, i: (hp, bi)), seq_blk, seq_blk],
        out_specs=pl.BlockSpec((1, MOBA_BLOCK, LANES), lambda bi, hp, i: (bi, i, hp)),
        out_shape=jax.ShapeDtypeStruct((b, s, mw), F32),
        scratch_shapes=[pltpu.VMEM((2, s, LANES), BF16), pltpu.VMEM((2, nb, nb, MOBA_BLOCK), F32)],
        compiler_params=_params("arbitrary", "arbitrary", "arbitrary"),
        name="moba",
    )(qt, qt, mk.reshape(b, s, mw), mv.reshape(b, s, mw))


def _gdn_gate_kernel(ba_ref, alog_ref, dtb_ref, bg_ref, *, seq):
    x = ba_ref[0]
    lane = lax.broadcasted_iota(jnp.int32, (1, LANES), 1)
    xa = x + dtb_ref[...]
    softplus = jnp.maximum(xa, 0.0) + jnp.log(1.0 + jnp.exp(-jnp.abs(xa)))
    g = jnp.where((lane >= GDN_HEADS) & (lane < 2 * GDN_HEADS), -jnp.exp(alog_ref[...]) * softplus, 0.0)
    pos = lax.broadcasted_iota(jnp.int32, (seq, 1), 0) % GDN_CHUNK
    sft = 1
    while sft < GDN_CHUNK:
        g = g + jnp.where(pos >= sft, pltpu.roll(g, sft, 0), 0.0)
        sft *= 2
    bg_ref[0] = jnp.where(lane < GDN_HEADS, _sigmoid(x), g)


def _unit_lower_inverses(mats):
    c = mats[0].shape[0]
    r = lax.broadcasted_iota(jnp.int32, (c, c), 0)
    cc = lax.broadcasted_iota(jnp.int32, (c, c), 1)
    eye = jnp.where(r == cc, 1.0, 0.0)
    pair = (r // 2) == (cc // 2)
    invs = [eye - jnp.where(pair, a, 0.0) for a in mats]
    size = 4
    while size <= c:
        level = ((r // size) == (cc // size)) & ((r // (size // 2)) != (cc // (size // 2)))
        inv_b = [inv.astype(BF16) for inv in invs]
        left = [_dot(ib, jnp.where(level, a, 0.0).astype(BF16)).astype(BF16) for ib, a in zip(inv_b, mats)]
        invs = [inv - _dot(lf, ib) for inv, lf, ib in zip(invs, left, inv_b)]
        size *= 2
    return invs


def _gdn_intra_kernel(q_ref, k_ref, v_ref, bg_ref, u_ref, w_ref, qd_ref, kd_ref, qk_ref):
    h = pl.program_id(1)
    c = GDN_CHUNK
    chunks = range(GDN_GROUP)
    lane = lax.broadcasted_iota(jnp.int32, (1, LANES), 1)
    r_ix = lax.broadcasted_iota(jnp.int32, (c, c), 0)
    c_ix = lax.broadcasted_iota(jnp.int32, (c, c), 1)
    rows = [slice(gi * c, (gi + 1) * c) for gi in chunks]
    k_b16 = [k_ref[0, rw, :] for rw in rows]
    q_b16 = [q_ref[0, rw, :] for rw in rows]
    bgs = [bg_ref[0, rw, :] for rw in rows]
    beta = [jnp.sum(jnp.where(lane == h, bg, 0.0), axis=1, keepdims=True) for bg in bgs]
    gam = [jnp.sum(jnp.where(lane == GDN_HEADS + h, bg, 0.0), axis=1, keepdims=True) for bg in bgs]
    kb = [kk.astype(F32) * bt for kk, bt in zip(k_b16, beta)]
    kk_raw = [_dot_nt(x.astype(BF16), kk) for x, kk in zip(kb, k_b16)]
    qk_raw = [_dot_nt(qq, kk) for qq, kk in zip(q_b16, k_b16)]
    decay = []
    for gm in gam:
        gam_r = jnp.sum(jnp.where(r_ix == c_ix, gm, 0.0), axis=0, keepdims=True)
        decay.append(jnp.exp(jnp.where(c_ix <= r_ix, gm - gam_r, NEG_INF)))
    t_inv = _unit_lower_inverses([jnp.where(c_ix < r_ix, x * dc, 0.0) for x, dc in zip(kk_raw, decay)])
    eg = [jnp.exp(gm) for gm in gam]
    rhs = [jnp.concatenate([v_ref[0, rw, :].astype(F32) * bt, x * e], axis=1).astype(BF16)
           for rw, bt, x, e in zip(rows, beta, kb, eg)]
    uw = [_dot(ti.astype(BF16), rh) for ti, rh in zip(t_inv, rhs)]
    for gi in chunks:
        rw = rows[gi]
        u_ref[0, rw, :] = uw[gi][:, :LANES].astype(BF16)
        w_ref[0, rw, :] = uw[gi][:, LANES:].astype(BF16)
        qk_ref[0, 0, rw, :] = (qk_raw[gi] * decay[gi]).astype(BF16)
        qd_ref[0, rw, :] = (q_b16[gi].astype(F32) * eg[gi]).astype(BF16)
        kd_ref[0, rw, :] = (k_b16[gi].astype(F32) * jnp.exp(gam[gi][c - 1:c, :] - gam[gi])).astype(BF16)


def _gdn_scan_kernel(u_ref, w_ref, qd_ref, kd_ref, qk_ref, z_ref, bg_ref, ng_ref, o_ref, state_ref, *, rows_per_step):
    c = GDN_CHUNK
    ng = ng_ref[...]
    chains = [(bi, h) for bi in range(SCAN_BATCH) for h in range(GDN_HEADS)]
    cols = [slice(h * GDN_HEAD_DIM, (h + 1) * GDN_HEAD_DIM) for h in range(GDN_HEADS)]

    @pl.when(pl.program_id(1) == 0)
    def _():
        state_ref[...] = jnp.zeros(state_ref.shape, F32)

    def step(n, states):
        r0 = pl.multiple_of(n * c, c)
        rows = pl.ds(r0, c)
        tails = [bg_ref[bi, pl.ds(r0 + c - 8, 8), :] for bi in range(SCAN_BATCH)]
        s_b = [st.astype(BF16) for st in states]
        ws = [_dot(w_ref[bi, rows, cols[h]], s_b[k]) for k, (bi, h) in enumerate(chains)]
        qs = [_dot(qd_ref[bi, rows, cols[h]], s_b[k]) for k, (bi, h) in enumerate(chains)]
        v_b = [(u_ref[bi, rows, cols[h]].astype(F32) - ws[k]).astype(BF16) for k, (bi, h) in enumerate(chains)]
        kd_v = [_dot_tn(kd_ref[bi, rows, cols[h]], v_b[k]) for k, (bi, h) in enumerate(chains)]
        qkv = [_dot(qk_ref[bi, h, rows, :], v_b[k]) for k, (bi, h) in enumerate(chains)]
        new_states = []
        for k, (bi, h) in enumerate(chains):
            g_last = tails[bi][7:8, GDN_HEADS + h:GDN_HEADS + h + 1]
            new_states.append(states[k] * jnp.exp(g_last) + kd_v[k])
            z = z_ref[bi, rows, cols[h]]
            o_ref[bi, rows, cols[h]] = _rms(qs[k] + qkv[k], ng) * (z * _sigmoid(z))
        return tuple(new_states)

    init = tuple(state_ref[k] for k in range(len(chains)))
    final = lax.fori_loop(0, rows_per_step // c, step, init)
    for k, st in enumerate(final):
        state_ref[k] = st


def _gdn(qkv, gz, gba, a_log, dt_bias, norm_g):
    b, s, w3 = qkv.shape
    nh = GDN_HEADS
    hw = nh * GDN_HEAD_DIM
    c = GDN_CHUNK
    lane_pad = lambda v: jnp.pad(v.reshape(1, -1), ((0, 0), (nh, LANES - 2 * nh)))
    seq_blk = pl.BlockSpec((1, s, LANES), lambda bi: (bi, 0, 0))
    const = pl.BlockSpec((1, LANES), lambda bi: (0, 0))
    bg = pl.pallas_call(
        functools.partial(_gdn_gate_kernel, seq=s),
        grid=(b,),
        in_specs=[seq_blk, const, const],
        out_specs=seq_blk,
        out_shape=jax.ShapeDtypeStruct((b, s, LANES), F32),
        compiler_params=_params("arbitrary"),
        name="gdn_gate",
    )(gba, lane_pad(a_log), lane_pad(dt_bias))

    grp = GDN_GROUP * c
    head_blk = lambda off: pl.BlockSpec((1, grp, LANES), lambda bi, h, n: (bi, n, off + h))
    head_shape = jax.ShapeDtypeStruct((b, s, hw), BF16)
    u, w, qd, kd, qk = pl.pallas_call(
        _gdn_intra_kernel,
        grid=(b, nh, s // grp),
        in_specs=[head_blk(0), head_blk(nh), head_blk(2 * nh),
                  pl.BlockSpec((1, grp, LANES), lambda bi, h, n: (bi, n, 0))],
        out_specs=[head_blk(0)] * 4 + [pl.BlockSpec((1, 1, grp, c), lambda bi, h, n: (bi, h, n, 0))],
        out_shape=[head_shape] * 4 + [jax.ShapeDtypeStruct((b, nh, s, c), BF16)],
        compiler_params=_params("arbitrary", "arbitrary", "arbitrary"),
        name="gdn_intra",
    )(qkv, qkv, qkv, bg)

    tile = lambda width: pl.BlockSpec((SCAN_BATCH, SCAN_ROWS, width), lambda bi, ti: (bi, ti, 0))
    return pl.pallas_call(
        functools.partial(_gdn_scan_kernel, rows_per_step=SCAN_ROWS),
        grid=(b // SCAN_BATCH, s // SCAN_ROWS),
        in_specs=[tile(hw), tile(hw), tile(hw), tile(hw),
                  pl.BlockSpec((SCAN_BATCH, nh, SCAN_ROWS, c), lambda bi, ti: (bi, 0, ti, 0)), tile(hw), tile(LANES),
                  pl.BlockSpec((1, LANES), lambda bi, ti: (0, 0))],
        out_specs=tile(hw),
        out_shape=jax.ShapeDtypeStruct((b, s, hw), F32),
        scratch_shapes=[pltpu.VMEM((SCAN_BATCH * nh, GDN_HEAD_DIM, GDN_HEAD_DIM), F32)],
        compiler_params=_params("arbitrary", "arbitrary"),
        name="gdn_scan",
    )(u, w, qd, kd, qk, gz, bg, norm_g.reshape(1, -1))


def _mem_kv_kernel(m_ref, g_ref, w_ref, kv_ref):
    kv_ref[...] = _dot(_rms(m_ref[...], g_ref[...]).astype(BF16), w_ref[...]).astype(BF16)


def _mem_kv(mem2d, g, w_kv, rows):
    t, d = mem2d.shape
    n = w_kv.shape[1]
    return pl.pallas_call(
        _mem_kv_kernel,
        grid=(t // rows,),
        in_specs=[pl.BlockSpec((rows, d), lambda i: (i, 0)), pl.BlockSpec((1, d), lambda i: (0, 0)),
                  pl.BlockSpec((d, n), lambda i: (0, 0))],
        out_specs=pl.BlockSpec((rows, n), lambda i: (i, 0)),
        out_shape=jax.ShapeDtypeStruct((t, n), BF16),
        compiler_params=_params("arbitrary"),
        name="mem_kv",
    )(mem2d, g.reshape(1, d), w_kv.astype(BF16))


def _mid_kernel(x_ref, om_ref, og_ref, mg_ref, wout_ref, xg_ref, wq_ref, kv_ref, wo_ref, fg_ref, rw_ref, rb_ref,
                x2_ref, h3_ref, route_ref, *, mw, xw):
    mo = _rms(om_ref[...], mg_ref[...]).astype(BF16)
    x1 = x_ref[...] + _dot(mo, wout_ref[:mw, :]) + _dot(og_ref[...].astype(BF16), wout_ref[mw:, :])

    h2 = _rms(x1, xg_ref[...]).astype(BF16)
    q = (_dot(h2, wq_ref[...]) * (XATTN_HEAD_DIM ** -0.5)).astype(BF16)
    head_cols = [slice(h * XATTN_HEAD_DIM, (h + 1) * XATTN_HEAD_DIM) for h in range(xw // XATTN_HEAD_DIM)]
    scores = [_dot_nt(q[:, sl], kv_ref[:, sl]) for sl in head_cols]
    probs = [jnp.exp(s - jnp.max(s, axis=1, keepdims=True)) for s in scores]
    heads = [_dot(p.astype(BF16), kv_ref[:, xw + sl.start:xw + sl.stop]) / jnp.sum(p, axis=1, keepdims=True)
             for p, sl in zip(probs, head_cols)]
    x2 = x1 + _dot(jnp.concatenate(heads, axis=1).astype(BF16), wo_ref[...])
    x2_ref[...] = x2

    h3 = _rms(x2, fg_ref[...])
    slabs = h3.shape[1] // LANES
    for j in range(slabs):
        h3_ref[pl.ds(j, h3.shape[0], stride=slabs), :] = h3[:, j * LANES:(j + 1) * LANES]
    ne = rb_ref.shape[0]
    h_hi = h3.astype(BF16)
    h_lo = (h3 - h_hi.astype(F32)).astype(BF16)
    by_hi = _dot_nt(rw_ref[...], h_hi)
    logits = by_hi[:ne, :] + by_hi[ne:, :] + _dot_nt(rw_ref[:ne, :], h_lo) + rb_ref[...]
    row = lax.broadcasted_iota(jnp.int32, logits.shape, 0)
    weights, picks, top = [], [], None
    for kk in range(TOP_K):
        m = jnp.max(logits, axis=0, keepdims=True)
        idx = jnp.min(jnp.where(logits == m, row, ne), axis=0, keepdims=True)
        logits = jnp.where(row == idx, NEG_INF, logits)
        top = m if top is None else top
        weights.append(jnp.exp(m - top))
        picks.append(idx.astype(F32))
    denom = sum(weights[1:], weights[0])
    rows = [wk / denom for wk in weights] + picks
    rows.append(jnp.zeros((LANES - len(rows), logits.shape[1]), F32))
    route_ref[...] = jnp.concatenate(rows, axis=0).T


def _mid(x2d, om, og, moba_g, w_out, xattn_g, w_q, kv, w_o, ffn_g, router_w, router_b, seq, mem_len):
    t, d = x2d.shape
    mw, gw, xw, ne = om.shape[1], og.shape[1], w_q.shape[1], router_w.shape[1]
    tiles_per_seq = seq // MID_ROWS
    row = lambda n: pl.BlockSpec((MID_ROWS, n), lambda i: (i, 0))
    const = lambda r, c: pl.BlockSpec((r, c), lambda i: (0, 0))
    rw_hi = router_w.T.astype(BF16)
    rw_lo = (router_w.T - rw_hi.astype(F32)).astype(BF16)
    return pl.pallas_call(
        functools.partial(_mid_kernel, mw=mw, xw=xw),
        grid=(t // MID_ROWS,),
        in_specs=[row(d), row(mw), row(gw), const(1, mw), const(mw + gw, d), const(1, d), const(d, xw),
                  pl.BlockSpec((mem_len, 2 * xw), lambda i: (i // tiles_per_seq, 0)),
                  const(xw, d), const(1, d), const(2 * ne, d), const(ne, 1)],
        out_specs=[row(d), pl.BlockSpec((MID_ROWS * (d // LANES), LANES), lambda i: (i, 0)), row(LANES)],
        out_shape=[jax.ShapeDtypeStruct((t, d), F32), jax.ShapeDtypeStruct((t * (d // LANES), LANES), F32),
                   jax.ShapeDtypeStruct((t, LANES), F32)],
        compiler_params=_params("arbitrary"),
        name="mid",
    )(x2d, om, og, moba_g.reshape(1, mw), w_out.astype(BF16), xattn_g.reshape(1, d), w_q.astype(BF16), kv,
      w_o.astype(BF16), ffn_g.reshape(1, d), jnp.concatenate([rw_hi, rw_lo], axis=0), router_b.reshape(ne, 1))


def _moe_kernel(be_ref, bv_ref, tok_ref, tokn_ref, dst_ref, h_hbm, wgu_ref, bgu_ref, wd_ref, bd_ref, y_hbm,
                xbuf, ybuf, wgu_b16, wd_b16, sem_in, sem_out, *, dff):
    blk = pl.program_id(0)
    nv = bv_ref[blk]
    nv_next = bv_ref[blk + 1]
    slot = blk % 2
    rows = MOE_ROWS
    chunk = MOE_CHUNK
    n_chunks = rows // chunk
    rt = ROW_TILE

    def row_in(s, r, t):
        return pltpu.make_async_copy(h_hbm.at[pl.ds(pl.multiple_of(t, rt), rt), :],
                                     xbuf.at[s, pl.ds(r * rt, rt), :], sem_in.at[s])

    def row_out(s, r, d):
        return pltpu.make_async_copy(ybuf.at[s, pl.ds(r * rt, rt), :],
                                     y_hbm.at[pl.ds(pl.multiple_of(d, rt), rt), :], sem_out.at[s])

    def wait_gather(s):
        pltpu.make_async_copy(h_hbm.at[pl.ds(0, rows * rt), :], xbuf.at[s], sem_in.at[s]).wait()

    def wait_scatter(s):
        pltpu.make_async_copy(ybuf.at[s], y_hbm.at[pl.ds(0, rows * rt), :], sem_out.at[s]).wait()

    @pl.when(blk == 0)
    def _():
        ybuf[1] = jnp.zeros(ybuf.shape[1:], F32)
        spare = pltpu.make_async_copy(ybuf.at[1], y_hbm.at[pl.ds(y_hbm.shape[0] - rows * rt, rows * rt), :],
                                      sem_out.at[1])
        spare.start()
        spare.wait()

    @pl.when((blk == 0) & (nv > 0))
    def _():
        for r in range(rows):
            row_in(0, r, tok_ref[0, 0, r]).start(priority=r % 2)

    @pl.when(nv > 0)
    def _():
        @pl.when((blk == 0) | (be_ref[blk] != be_ref[jnp.maximum(blk - 1, 0)]))
        def _():
            wgu_b16[...] = wgu_ref[0].astype(BF16)
            wd_b16[...] = wd_ref[0].astype(BF16)

        wait_gather(slot)

        @pl.when(blk >= 2)
        def _():
            wait_scatter(slot)

        slab = lambda c, j: pl.ds(c * chunk * rt + j, chunk, stride=rt)
        load_x = lambda c: jnp.concatenate([xbuf[slot, slab(c, j), :] for j in range(rt)], axis=1).astype(BF16)
        per_chunk_in = rows // (n_chunks // 2)
        x_next = load_x(0)
        for c in range(n_chunks):
            x = x_next
            if c < n_chunks // 2:
                for r in range(c * per_chunk_in, (c + 1) * per_chunk_in):
                    row_in(1 - slot, r, tokn_ref[0, 0, r]).start(priority=r % 2)
            if c:
                for r in range((c - 1) * chunk, c * chunk):
                    row_out(slot, r, dst_ref[0, 0, r]).start(priority=r % 2)
            gu = _dot(x, wgu_b16[...]) + bgu_ref[0]
            gate = jnp.minimum(gu[:, :dff], SWIGLU_LIMIT)
            up = jnp.clip(gu[:, dff:], -SWIGLU_LIMIT, SWIGLU_LIMIT)
            act = (up + 1.0) * gate * _sigmoid(SWIGLU_ALPHA * gate)
            y = _dot(act.astype(BF16), wd_b16[...]) + bd_ref[0]
            if c + 1 < n_chunks:
                x_next = load_x(c + 1)
            for j in range(rt):
                ybuf[slot, slab(c, j), :] = y[:, j * LANES:(j + 1) * LANES]
        for r in range(rows - chunk, rows):
            row_out(slot, r, dst_ref[0, 0, r]).start(priority=r % 2)

        @pl.when(nv_next == 0)
        def _():
            wait_gather(1 - slot)

            @pl.when(blk >= 1)
            def _():
                wait_scatter(1 - slot)
            wait_scatter(slot)


def _moe(h3_tiles, block_expert, block_valid, row_tok, row_dst, w_gu, b_gu, w_d, b_d):
    ne, d, n2 = w_gu.shape
    assert d == ROW_TILE * LANES
    t = h3_tiles.shape[0] // ROW_TILE
    dff = n2 // 2
    nblk = block_expert.shape[0]
    idx_blk = pl.BlockSpec((1, 1, MOE_ROWS), lambda i, be, bv: (i, 0, 0), memory_space=pltpu.SMEM)
    idx_next = pl.BlockSpec((1, 1, MOE_ROWS), lambda i, be, bv: (jnp.minimum(i + 1, nblk - 1), 0, 0),
                            memory_space=pltpu.SMEM)
    grid_spec = pltpu.PrefetchScalarGridSpec(
        num_scalar_prefetch=2,
        grid=(nblk,),
        in_specs=[idx_blk, idx_next, idx_blk, pl.BlockSpec(memory_space=pl.ANY),
                  pl.BlockSpec((1, d, n2), lambda i, be, bv: (be[i], 0, 0)),
                  pl.BlockSpec((1, 1, n2), lambda i, be, bv: (be[i], 0, 0)),
                  pl.BlockSpec((1, dff, d), lambda i, be, bv: (be[i], 0, 0)),
                  pl.BlockSpec((1, 1, d), lambda i, be, bv: (be[i], 0, 0))],
        out_specs=pl.BlockSpec(memory_space=pl.ANY),
        scratch_shapes=[pltpu.VMEM((2, MOE_ROWS * ROW_TILE, LANES), F32),
                        pltpu.VMEM((2, MOE_ROWS * ROW_TILE, LANES), F32),
                        pltpu.VMEM((d, n2), BF16), pltpu.VMEM((dff, d), BF16),
                        pltpu.SemaphoreType.DMA((2,)), pltpu.SemaphoreType.DMA((2,))],
    )
    tok3 = (row_tok * ROW_TILE).reshape(nblk, 1, MOE_ROWS)
    valid_ext = jnp.concatenate([block_valid, jnp.zeros((1,), jnp.int32)])
    return pl.pallas_call(
        functools.partial(_moe_kernel, dff=dff),
        grid_spec=grid_spec,
        out_shape=jax.ShapeDtypeStruct(((TOP_K * t + MOE_ROWS) * ROW_TILE, LANES), F32),
        compiler_params=_params("arbitrary"),
        name="moe",
    )(block_expert, valid_ext, tok3, tok3, (row_dst * ROW_TILE).reshape(nblk, 1, MOE_ROWS), h3_tiles,
      w_gu, b_gu.reshape(ne, 1, n2), w_d, b_d.reshape(ne, 1, d))


def _route_plan(expert, t):
    n_pairs = t * TOP_K
    e_flat = expert.reshape(-1)
    order = jnp.argsort(e_flat, stable=True).astype(jnp.int32)
    experts = jnp.arange(N_EXPERTS, dtype=jnp.int32)
    counts = jnp.sum((e_flat[:, None] == experts[None, :]).astype(jnp.int32), axis=0)
    padded = (counts + MOE_ROWS - 1) // MOE_ROWS * MOE_ROWS
    pend = jnp.cumsum(padded)
    pstart = pend - padded
    gstart = jnp.cumsum(counts) - counts
    nblk = n_pairs // MOE_ROWS + N_EXPERTS
    blk_row0 = jnp.arange(nblk, dtype=jnp.int32) * MOE_ROWS
    block_expert = jnp.minimum(jnp.sum((pend[None, :] <= blk_row0[:, None]).astype(jnp.int32), axis=1), N_EXPERTS - 1)
    pick = lambda table: jnp.sum(jnp.where(block_expert[:, None] == experts[None, :], table[None, :], 0), axis=1)
    in_group = blk_row0 - pick(pstart)
    block_valid = jnp.where(blk_row0 < pend[-1], jnp.clip(pick(counts) - in_group, 0, MOE_ROWS), 0)
    first_sorted = pick(gstart) + in_group
    local = jnp.arange(MOE_ROWS, dtype=jnp.int32)
    srt = first_sorted[:, None] + local[None, :]
    pair = order[jnp.clip(srt, 0, n_pairs - 1)]
    row_tok = pair // TOP_K
    row_dst = jnp.where(local[None, :] < block_valid[:, None], (pair % TOP_K) * t + row_tok, n_pairs + local[None, :])
    return (block_expert.astype(jnp.int32), block_valid.astype(jnp.int32), row_tok.astype(jnp.int32),
            row_dst.astype(jnp.int32))


def _combine_kernel(x2_ref, route_ref, g_ref, *rest, final):
    y_refs, o_ref = rest[:TOP_K], rest[TOP_K]
    route = route_ref[...]
    gates = [route[:, kk:kk + 1] for kk in range(TOP_K)]
    slabs = []
    for j in range(ROW_TILE):
        acc = x2_ref[:, j * LANES:(j + 1) * LANES]
        for kk in range(TOP_K):
            acc = acc + gates[kk] * y_refs[kk][pl.ds(j, x2_ref.shape[0], stride=ROW_TILE), :]
        slabs.append(acc)
    out = jnp.concatenate(slabs, axis=1)
    o_ref[...] = _rms(out, g_ref[...]) if final else out


def _combine(x2, route, y, g, final):
    t, d = x2.shape
    tiles = t // MID_ROWS
    slot_spec = lambda kk: pl.BlockSpec((MID_ROWS * ROW_TILE, LANES), lambda i: (kk * tiles + i, 0))
    return pl.pallas_call(
        functools.partial(_combine_kernel, final=final),
        grid=(tiles,),
        in_specs=[pl.BlockSpec((MID_ROWS, d), lambda i: (i, 0)), pl.BlockSpec((MID_ROWS, LANES), lambda i: (i, 0)),
                  pl.BlockSpec((1, d), lambda i: (0, 0))] + [slot_spec(kk) for kk in range(TOP_K)],
        out_specs=pl.BlockSpec((MID_ROWS, d), lambda i: (i, 0)),
        out_shape=jax.ShapeDtypeStruct((t, d), F32),
        compiler_params=_params("arbitrary"),
        name="combine",
    )(x2, route, g.reshape(1, d), *([y] * TOP_K))


def kernel(x, mem, norm_mix_g, w_in, gdn_conv_w, gdn_A_log, gdn_dt_bias, gdn_norm_g, moba_norm_g, w_out,
           norm_xattn_g, norm_mem_g, xattn_w_q, xattn_w_kv, xattn_w_o, norm_ffn_g, router_w, router_b,
           w_gate_up, b_gate_up, w_down, b_down, final_norm_g):
    b, s, d = x.shape
    t = b * s
    mem_len = mem.shape[1]
    mw = moba_norm_g.shape[1]
    gw = GDN_HEADS * GDN_HEAD_DIM
    xcur = x.reshape(t, d)
    for l in range(w_in.shape[0]):
        qt, mk, mv, gqkv, gz, gba = _in_proj(xcur, norm_mix_g[l], w_in[l], gdn_conv_w[l], mw, gw, s)
        o_moba = _moba(qt, mk, mv, b, s)
        o_gdn = _gdn(gqkv.reshape(b, s, 3 * gw), gz.reshape(b, s, gw), gba.reshape(b, s, LANES),
                     gdn_A_log[l], gdn_dt_bias[l], gdn_norm_g[l])
        kv = _mem_kv(mem.reshape(b * mem_len, d), norm_mem_g[l], xattn_w_kv[l], mem_len)
        x2, h3, route = _mid(xcur, o_moba.reshape(t, mw), o_gdn.reshape(t, gw), moba_norm_g[l], w_out[l],
                             norm_xattn_g[l], xattn_w_q[l], kv, xattn_w_o[l], norm_ffn_g[l], router_w[l],
                             router_b[l], s, mem_len)
        expert = route[:, TOP_K:2 * TOP_K].astype(jnp.int32)
        plan = _route_plan(expert, t)
        y = _moe(h3, *plan, w_gate_up[l], b_gate_up[l], w_down[l], b_down[l])
        xcur = _combine(x2, route, y, final_norm_g, l == w_in.shape[0] - 1)
    return xcur.reshape(b, s, d)
```

```python
import functools

import jax
import jax.numpy as jnp
from jax import lax
from jax.experimental import pallas as pl
from jax.experimental.pallas import tpu as pltpu

F32 = jnp.float32
BF16 = jnp.bfloat16

RMS_EPS = 1e-6
MOBA_HEAD_DIM = 64
MOBA_BLOCK = 256
MOBA_TOPK = 3
GDN_HEAD_DIM = 128
GDN_HEADS = 4
GDN_CONV = 4
GDN_CHUNK = 64
XATTN_HEAD_DIM = 128
N_EXPERTS = 32
TOP_K = 4
SWIGLU_LIMIT = 7.0
SWIGLU_ALPHA = 1.702

LANES = 128
ROW_TILE = 8
VMEM_LIMIT = 56 * 1024 * 1024

IN_ROWS = 512
MID_ROWS = 512
MOE_ROWS = 512
MOE_CHUNK = 128
GDN_GROUP = 32
PREP_ROWS = 256
GDN_HALO = ROW_TILE
SCAN_BATCH = 2
SCAN_ROWS = 1024
NEG_INF = float("-inf")
MOBA_Q_SCALE = 1.4426950408889634 / MOBA_HEAD_DIM ** 0.5
MOBA_MASKED = -1e30


def _params(*sem):
    return pltpu.CompilerParams(dimension_semantics=sem, vmem_limit_bytes=VMEM_LIMIT)


def _rms(x, g):
    return x * lax.rsqrt(jnp.mean(x * x, axis=-1, keepdims=True) + RMS_EPS) * g


def _dot(a, b):
    return jnp.dot(a, b, preferred_element_type=F32)


def _dot_nt(a, b):
    return lax.dot_general(a, b, (((1,), (1,)), ((), ())), preferred_element_type=F32)


def _sigmoid(x):
    return 1.0 / (1.0 + jnp.exp(-x))


def _gdn_gates(x, a_log, dt_bias):
    lane = lax.broadcasted_iota(jnp.int32, (1, LANES), 1)
    xa = x + dt_bias
    softplus = jnp.maximum(xa, 0.0) + jnp.log(1.0 + jnp.exp(-jnp.abs(xa)))
    g = jnp.where((lane >= GDN_HEADS) & (lane < 2 * GDN_HEADS), -jnp.exp(a_log) * softplus, 0.0)
    pos = lax.broadcasted_iota(jnp.int32, (x.shape[0], 1), 0) % GDN_CHUNK
    sft = 1
    while sft < GDN_CHUNK:
        g = g + jnp.where(pos >= sft, pltpu.roll(g, sft, 0), 0.0)
        sft *= 2
    return jnp.where(lane < GDN_HEADS, _sigmoid(x), g)


def _in_proj_kernel(x_ref, g_ref, w_ref, wt_ref, cw_ref, alog_ref, dtb_ref, qt_ref, mk_ref, mv_ref, qkv_ref, gz_ref,
                    bg_ref, *bufs,
                    mw, gw, tiles_per_seq):
    hn = _rms(x_ref[...], g_ref[...]).astype(BF16)
    mm = lambda lo, hi: _dot(hn, w_ref[:, lo:hi])
    rows = x_ref.shape[0]
    halo = GDN_HALO

    @pl.when(pl.program_id(0) % tiles_per_seq == 0)
    def _():
        for buf in bufs:
            buf[:halo, :] = jnp.zeros((halo, LANES), F32)

    @pl.when(pl.program_id(0) % tiles_per_seq != 0)
    def _():
        for buf in bufs:
            buf[:halo, :] = buf[rows:rows + halo, :]

    for grp in range(3):
        raw = mm(2 * mw + grp * gw, 2 * mw + (grp + 1) * gw)
        for h in range(GDN_HEADS):
            bufs[grp * GDN_HEADS + h][halo:, :] = raw[:, h * LANES:(h + 1) * LANES]
    qt_ref[...] = (_dot_nt(wt_ref[...], hn) * MOBA_Q_SCALE).astype(BF16)
    mk_ref[...] = mm(0, mw).astype(BF16)
    mv_ref[...] = mm(mw, 2 * mw).astype(BF16)
    gz_ref[...] = mm(2 * mw + 3 * gw, 2 * mw + 4 * gw)
    bg_ref[...] = _gdn_gates(mm(2 * mw + 4 * gw, 2 * mw + 4 * gw + LANES), alog_ref[...], dtb_ref[...])

    for cb in range(3 * gw // LANES):
        cols = slice(cb * LANES, (cb + 1) * LANES)
        taps = [cw_ref[j:j + 1, cols] for j in range(GDN_CONV)]
        buf = bufs[cb]
        for r0 in range(0, rows, PREP_ROWS):
            y = taps[GDN_CONV - 1] * buf[halo + r0:halo + r0 + PREP_ROWS, :]
            for sft in range(1, GDN_CONV):
                y = y + taps[GDN_CONV - 1 - sft] * buf[halo + r0 - sft:halo + r0 - sft + PREP_ROWS, :]
            y = y * _sigmoid(y)
            if cb < 2 * GDN_HEADS:
                scale = GDN_HEAD_DIM ** -0.5 if cb < GDN_HEADS else 1.0
                y = y * (lax.rsqrt(jnp.sum(y * y, axis=-1, keepdims=True) + RMS_EPS) * scale)
            qkv_ref[r0:r0 + PREP_ROWS, cols] = y.astype(BF16)


def _in_proj(x2d, g, w_in, conv_w, a_log, dt_bias, mw, gw, seq):
    t, d = x2d.shape
    n_real = w_in.shape[1] - mw
    n_pad = 2 * mw + 4 * gw + LANES
    w = jnp.pad(w_in[:, mw:], ((0, 0), (0, n_pad - n_real))).astype(BF16)
    w_t = w_in[:, :mw].T.astype(BF16)
    row = lambda n: pl.BlockSpec((IN_ROWS, n), lambda i: (i, 0))
    const = lambda shape: pl.BlockSpec(shape, lambda i: (0, 0))
    nh = GDN_HEADS
    lane_pad = lambda v: jnp.pad(v.reshape(1, -1), ((0, 0), (nh, LANES - 2 * nh)))
    return pl.pallas_call(
        functools.partial(_in_proj_kernel, mw=mw, gw=gw, tiles_per_seq=seq // IN_ROWS),
        grid=(t // IN_ROWS,),
        in_specs=[row(d), const((1, d)), const((d, n_pad)), const((mw, d)), const((GDN_CONV, 3 * gw)),
                  const((1, LANES)), const((1, LANES))],
        out_specs=[pl.BlockSpec((mw, IN_ROWS), lambda i: (0, i)), row(mw), row(mw), row(3 * gw), row(gw),
                   row(LANES)],
        out_shape=[jax.ShapeDtypeStruct((mw, t), BF16), jax.ShapeDtypeStruct((t, mw), BF16),
                   jax.ShapeDtypeStruct((t, mw), BF16),
                   jax.ShapeDtypeStruct((t, 3 * gw), BF16), jax.ShapeDtypeStruct((t, gw), F32),
                   jax.ShapeDtypeStruct((t, LANES), F32)],
        scratch_shapes=[pltpu.VMEM((GDN_HALO + IN_ROWS, LANES), F32)] * (3 * gw // LANES),
        compiler_params=_params("arbitrary"),
        name="in_proj",
    )(x2d, g.reshape(1, d), w, w_t, conv_w, lane_pad(a_log), lane_pad(dt_bias))


def _moba_select(g_t, i):
    nb = g_t.shape[0]
    row = lax.broadcasted_iota(jnp.int32, g_t.shape, 0)
    valid = row < i
    sel = jnp.zeros_like(g_t)
    for j in range(nb):
        gj = g_t[j:j + 1, :]
        beats = valid & ((g_t > gj) | ((g_t == gj) & (row < j)))
        rank = jnp.sum(jnp.where(beats, 1.0, 0.0), axis=0, keepdims=True)
        sel = jnp.where(row == j, jnp.where(rank < MOBA_TOPK, 1.0, 0.0), sel)
    return jnp.where(valid, sel, 0.0)


def _dot_tn(a, b):
    return lax.dot_general(a, b, (((0,), (0,)), ((), ())), preferred_element_type=F32)


def _moba_kernel(qt_ref, qt_all_ref, k_ref, v_ref, o_ref, kaug_ref, bias_ref, *, nb):
    i = pl.program_id(2)
    bs = MOBA_BLOCK
    hd = MOBA_HEAD_DIM
    lane = lax.broadcasted_iota(jnp.int32, (1, LANES), 1)
    low = lax.broadcasted_iota(jnp.int32, (LANES, 1), 0) < hd

    @pl.when(i == 0)
    def _():
        kmean = []
        for j in range(nb):
            rows = slice(j * bs, (j + 1) * bs)
            kb = k_ref[0, rows, :]
            kmean.append(jnp.mean(kb.astype(F32), axis=0, keepdims=True))
            kaug_ref[0, rows, :] = jnp.where(lane < hd, kb, jnp.where(lane == hd + j, 1.0, 0.0).astype(BF16))
            kaug_ref[1, rows, :] = jnp.where(lane >= hd, kb, jnp.where(lane == j, 1.0, 0.0).astype(BF16))
        kmean = jnp.concatenate(kmean, axis=0)
        q_all = qt_all_ref[...].astype(F32)
        for h, qh in enumerate((jnp.where(low, q_all, 0.0), jnp.where(low, 0.0, q_all))):
            gate = jnp.dot(kmean, qh, preferred_element_type=F32, precision=lax.Precision.HIGHEST)
            for jq in range(nb):
                sel = _moba_select(gate[:, jq * bs:(jq + 1) * bs], jq)
                bias_ref[h, jq] = jnp.where(sel > 0.5, 0.0, MOBA_MASKED)

    qt = qt_ref[...]
    qtf = qt.astype(F32)
    bias = [bias_ref[h, i] for h in (0, 1)]
    pad = jnp.zeros((hd - nb, bs), F32)
    q_past = (jnp.concatenate([qtf[:hd], bias[0], pad], axis=0).astype(BF16),
              jnp.concatenate([bias[1], pad, qtf[hd:]], axis=0).astype(BF16))
    zero = jnp.zeros_like(qt)
    q_own = (jnp.where(low, qt, zero), jnp.where(low, zero, qt))

    key_ix = lax.broadcasted_iota(jnp.int32, (bs, bs), 0)
    qry_ix = lax.broadcasted_iota(jnp.int32, (bs, bs), 1)
    causal_bias = jnp.where(key_ix <= qry_ix, 0.0, NEG_INF)
    own = pl.ds(pl.multiple_of(i * bs, bs), bs)
    k_own = k_ref[0, own, :]
    v_own = v_ref[0, own, :]

    def attend(width):
        heads = (0, 1)
        parts = [slice(lo * bs, (lo + 1) * bs) for lo in range(width)]
        n = len(parts) + 1
        past_scores = lambda k: [_dot(kaug_ref[h, parts[k - 1], :], q_past[h]) for h in heads]
        scores = {0: [_dot(k_own, q_own[h]) + causal_bias for h in heads]}
        if n > 1:
            scores[1] = past_scores(1)
        m = l = acc = None
        for k in range(n):
            s_k = scores.pop(k)
            m_new = [s.max(axis=0, keepdims=True) for s in s_k]
            if m is not None:
                m_new = [jnp.maximum(m[h], m_new[h]) for h in heads]
            p = [jnp.exp2(s_k[h] - m_new[h]) for h in heads]
            p_sum = [x.sum(axis=0, keepdims=True) for x in p]
            if k + 2 < n:
                scores[k + 2] = past_scores(k + 2)
            values = v_own if k == 0 else v_ref[0, parts[k - 1], :]
            pv = [_dot_tn(values, p[h].astype(BF16)) for h in heads]
            if m is None:
                l, acc = p_sum, pv
            else:
                alpha = [jnp.exp2(m[h] - m_new[h]) for h in heads]
                l = [alpha[h] * l[h] + p_sum[h] for h in heads]
                acc = [alpha[h] * acc[h] + pv[h] for h in heads]
            m = m_new
        o_ref[0] = jnp.where(low, acc[0] / l[0], acc[1] / l[1]).T

    for width in range(nb):
        pl.when(i == width)(functools.partial(attend, width))


def _moba(qt, mk, mv, b, s):
    mw = mk.shape[-1]
    nb = s // MOBA_BLOCK
    seq_blk = pl.BlockSpec((1, s, LANES), lambda bi, hp, i: (bi, 0, hp))
    return pl.pallas_call(
        functools.partial(_moba_kernel, nb=nb),
        grid=(b, mw // LANES, nb),
        in_specs=[pl.BlockSpec((LANES, MOBA_BLOCK), lambda bi, hp, i: (hp, bi * nb + i)),
                  pl.BlockSpec((LANES, s), lambda bi, hp, i: (hp, bi)), seq_blk, seq_blk],
        out_specs=pl.BlockSpec((1, MOBA_BLOCK, LANES), lambda bi, hp, i: (bi, i, hp)),
        out_shape=jax.ShapeDtypeStruct((b, s, mw), F32),
        scratch_shapes=[pltpu.VMEM((2, s, LANES), BF16), pltpu.VMEM((2, nb, nb, MOBA_BLOCK), F32)],
        compiler_params=_params("arbitrary", "arbitrary", "arbitrary"),
        name="moba",
    )(qt, qt, mk.reshape(b, s, mw), mv.reshape(b, s, mw))


def _unit_lower_inverses(mats):
    c = mats[0].shape[0]
    r = lax.broadcasted_iota(jnp.int32, (c, c), 0)
    cc = lax.broadcasted_iota(jnp.int32, (c, c), 1)
    eye = jnp.where(r == cc, 1.0, 0.0)
    pair = (r // 2) == (cc // 2)
    invs = [eye - jnp.where(pair, a, 0.0) for a in mats]
    size = 4
    while size <= c:
        level = ((r // size) == (cc // size)) & ((r // (size // 2)) != (cc // (size // 2)))
        inv_b = [inv.astype(BF16) for inv in invs]
        left = [_dot(ib, jnp.where(level, a, 0.0).astype(BF16)).astype(BF16) for ib, a in zip(inv_b, mats)]
        invs = [inv - _dot(lf, ib) for inv, lf, ib in zip(invs, left, inv_b)]
        size *= 2
    return invs


def _gdn_intra_kernel(q_ref, k_ref, v_ref, bg_ref, u_ref, w_ref, qd_ref, kd_ref, qk_ref):
    h = pl.program_id(1)
    c = GDN_CHUNK
    chunks = range(GDN_GROUP)
    lane = lax.broadcasted_iota(jnp.int32, (1, LANES), 1)
    r_ix = lax.broadcasted_iota(jnp.int32, (c, c), 0)
    c_ix = lax.broadcasted_iota(jnp.int32, (c, c), 1)
    rows = [slice(gi * c, (gi + 1) * c) for gi in chunks]
    k_b16 = [k_ref[0, rw, :] for rw in rows]
    q_b16 = [q_ref[0, rw, :] for rw in rows]
    bgs = [bg_ref[0, rw, :] for rw in rows]
    beta = [jnp.sum(jnp.where(lane == h, bg, 0.0), axis=1, keepdims=True) for bg in bgs]
    gam = [jnp.sum(jnp.where(lane == GDN_HEADS + h, bg, 0.0), axis=1, keepdims=True) for bg in bgs]
    kb = [kk.astype(F32) * bt for kk, bt in zip(k_b16, beta)]
    kk_raw = [_dot_nt(x.astype(BF16), kk) for x, kk in zip(kb, k_b16)]
    qk_raw = [_dot_nt(qq, kk) for qq, kk in zip(q_b16, k_b16)]
    decay = []
    for gm in gam:
        gam_r = jnp.sum(jnp.where(r_ix == c_ix, gm, 0.0), axis=0, keepdims=True)
        decay.append(jnp.exp(jnp.where(c_ix <= r_ix, gm - gam_r, NEG_INF)))
    t_inv = _unit_lower_inverses([jnp.where(c_ix < r_ix, x * dc, 0.0) for x, dc in zip(kk_raw, decay)])
    eg = [jnp.exp(gm) for gm in gam]
    rhs = [jnp.concatenate([v_ref[0, rw, :].astype(F32) * bt, x * e], axis=1).astype(BF16)
           for rw, bt, x, e in zip(rows, beta, kb, eg)]
    uw = [_dot(ti.astype(BF16), rh) for ti, rh in zip(t_inv, rhs)]
    for gi in chunks:
        rw = rows[gi]
        u_ref[0, rw, :] = uw[gi][:, :LANES].astype(BF16)
        w_ref[0, rw, :] = uw[gi][:, LANES:].astype(BF16)
        qk_ref[0, 0, rw, :] = (qk_raw[gi] * decay[gi]).astype(BF16)
        qd_ref[0, rw, :] = (q_b16[gi].astype(F32) * eg[gi]).astype(BF16)
        kd_ref[0, rw, :] = (k_b16[gi].astype(F32) * jnp.exp(gam[gi][c - 1:c, :] - gam[gi])).astype(BF16)


def _gdn_scan_kernel(u_ref, w_ref, qd_ref, kd_ref, qk_ref, z_ref, bg_ref, ng_ref, o_ref, state_ref, *, rows_per_step):
    c = GDN_CHUNK
    ng = ng_ref[...]
    chains = [(bi, h) for bi in range(SCAN_BATCH) for h in range(GDN_HEADS)]
    cols = [slice(h * GDN_HEAD_DIM, (h + 1) * GDN_HEAD_DIM) for h in range(GDN_HEADS)]

    @pl.when(pl.program_id(1) == 0)
    def _():
        state_ref[...] = jnp.zeros(state_ref.shape, F32)

    def step(n, states):
        r0 = pl.multiple_of(n * c, c)
        rows = pl.ds(r0, c)
        tails = [bg_ref[bi, pl.ds(r0 + c - 8, 8), :] for bi in range(SCAN_BATCH)]
        s_b = [st.astype(BF16) for st in states]
        ws = [_dot(w_ref[bi, rows, cols[h]], s_b[k]) for k, (bi, h) in enumerate(chains)]
        qs = [_dot(qd_ref[bi, rows, cols[h]], s_b[k]) for k, (bi, h) in enumerate(chains)]
        v_b = [(u_ref[bi, rows, cols[h]].astype(F32) - ws[k]).astype(BF16) for k, (bi, h) in enumerate(chains)]
        kd_v = [_dot_tn(kd_ref[bi, rows, cols[h]], v_b[k]) for k, (bi, h) in enumerate(chains)]
        qkv = [_dot(qk_ref[bi, h, rows, :], v_b[k]) for k, (bi, h) in enumerate(chains)]
        new_states = []
        for k, (bi, h) in enumerate(chains):
            g_last = tails[bi][7:8, GDN_HEADS + h:GDN_HEADS + h + 1]
            new_states.append(states[k] * jnp.exp(g_last) + kd_v[k])
            z = z_ref[bi, rows, cols[h]]
            o_ref[bi, rows, cols[h]] = _rms(qs[k] + qkv[k], ng) * (z * _sigmoid(z))
        return tuple(new_states)

    init = tuple(state_ref[k] for k in range(len(chains)))
    final = lax.fori_loop(0, rows_per_step // c, step, init)
    for k, st in enumerate(final):
        state_ref[k] = st


def _gdn(qkv, gz, bg, norm_g):
    b, s, w3 = qkv.shape
    nh = GDN_HEADS
    hw = nh * GDN_HEAD_DIM
    c = GDN_CHUNK
    grp = GDN_GROUP * c
    head_blk = lambda off: pl.BlockSpec((1, grp, LANES), lambda bi, h, n: (bi, n, off + h))
    head_shape = jax.ShapeDtypeStruct((b, s, hw), BF16)
    u, w, qd, kd, qk = pl.pallas_call(
        _gdn_intra_kernel,
        grid=(b, nh, s // grp),
        in_specs=[head_blk(0), head_blk(nh), head_blk(2 * nh),
                  pl.BlockSpec((1, grp, LANES), lambda bi, h, n: (bi, n, 0))],
        out_specs=[head_blk(0)] * 4 + [pl.BlockSpec((1, 1, grp, c), lambda bi, h, n: (bi, h, n, 0))],
        out_shape=[head_shape] * 4 + [jax.ShapeDtypeStruct((b, nh, s, c), BF16)],
        compiler_params=_params("arbitrary", "arbitrary", "arbitrary"),
        name="gdn_intra",
    )(qkv, qkv, qkv, bg)

    tile = lambda width: pl.BlockSpec((SCAN_BATCH, SCAN_ROWS, width), lambda bi, ti: (bi, ti, 0))
    return pl.pallas_call(
        functools.partial(_gdn_scan_kernel, rows_per_step=SCAN_ROWS),
        grid=(b // SCAN_BATCH, s // SCAN_ROWS),
        in_specs=[tile(hw), tile(hw), tile(hw), tile(hw),
                  pl.BlockSpec((SCAN_BATCH, nh, SCAN_ROWS, c), lambda bi, ti: (bi, 0, ti, 0)), tile(hw), tile(LANES),
                  pl.BlockSpec((1, LANES), lambda bi, ti: (0, 0))],
        out_specs=tile(hw),
        out_shape=jax.ShapeDtypeStruct((b, s, hw), F32),
        scratch_shapes=[pltpu.VMEM((SCAN_BATCH * nh, GDN_HEAD_DIM, GDN_HEAD_DIM), F32)],
        compiler_params=_params("arbitrary", "arbitrary"),
        name="gdn_scan",
    )(u, w, qd, kd, qk, gz, bg, norm_g.reshape(1, -1))


def _mem_kv_kernel(m_ref, g_ref, w_ref, kv_ref):
    kv_ref[...] = _dot(_rms(m_ref[...], g_ref[...]).astype(BF16), w_ref[...]).astype(BF16)


def _mem_kv(mem2d, g, w_kv, rows):
    t, d = mem2d.shape
    n = w_kv.shape[1]
    return pl.pallas_call(
        _mem_kv_kernel,
        grid=(t // rows,),
        in_specs=[pl.BlockSpec((rows, d), lambda i: (i, 0)), pl.BlockSpec((1, d), lambda i: (0, 0)),
                  pl.BlockSpec((d, n), lambda i: (0, 0))],
        out_specs=pl.BlockSpec((rows, n), lambda i: (i, 0)),
        out_shape=jax.ShapeDtypeStruct((t, n), BF16),
        compiler_params=_params("arbitrary"),
        name="mem_kv",
    )(mem2d, g.reshape(1, d), w_kv.astype(BF16))


def _mid_kernel(x_ref, om_ref, og_ref, mg_ref, wout_ref, xg_ref, wq_ref, kv_ref, wo_ref, fg_ref, rw_ref, rb_ref,
                x2_ref, h3_ref, route_ref, *, mw, xw):
    mo = _rms(om_ref[...], mg_ref[...]).astype(BF16)
    x1 = x_ref[...] + _dot(mo, wout_ref[:mw, :]) + _dot(og_ref[...].astype(BF16), wout_ref[mw:, :])

    h2 = _rms(x1, xg_ref[...]).astype(BF16)
    q = (_dot(h2, wq_ref[...]) * (XATTN_HEAD_DIM ** -0.5)).astype(BF16)
    head_cols = [slice(h * XATTN_HEAD_DIM, (h + 1) * XATTN_HEAD_DIM) for h in range(xw // XATTN_HEAD_DIM)]
    scores = [_dot_nt(q[:, sl], kv_ref[:, sl]) for sl in head_cols]
    probs = [jnp.exp(s - jnp.max(s, axis=1, keepdims=True)) for s in scores]
    heads = [_dot(p.astype(BF16), kv_ref[:, xw + sl.start:xw + sl.stop]) / jnp.sum(p, axis=1, keepdims=True)
             for p, sl in zip(probs, head_cols)]
    x2 = x1 + _dot(jnp.concatenate(heads, axis=1).astype(BF16), wo_ref[...])
    x2_ref[...] = x2

    h3 = _rms(x2, fg_ref[...])
    slabs = h3.shape[1] // LANES
    for j in range(slabs):
        h3_ref[pl.ds(j, h3.shape[0], stride=slabs), :] = h3[:, j * LANES:(j + 1) * LANES]
    ne = rb_ref.shape[0]
    h_hi = h3.astype(BF16)
    h_lo = (h3 - h_hi.astype(F32)).astype(BF16)
    by_hi = _dot_nt(rw_ref[...], h_hi)
    logits = by_hi[:ne, :] + by_hi[ne:, :] + _dot_nt(rw_ref[:ne, :], h_lo) + rb_ref[...]
    row = lax.broadcasted_iota(jnp.int32, logits.shape, 0)
    weights, picks, top = [], [], None
    for kk in range(TOP_K):
        m = jnp.max(logits, axis=0, keepdims=True)
        idx = jnp.min(jnp.where(logits == m, row, ne), axis=0, keepdims=True)
        logits = jnp.where(row == idx, NEG_INF, logits)
        top = m if top is None else top
        weights.append(jnp.exp(m - top))
        picks.append(idx.astype(F32))
    denom = sum(weights[1:], weights[0])
    rows = [wk / denom for wk in weights] + picks
    rows.append(jnp.zeros((LANES - len(rows), logits.shape[1]), F32))
    route_ref[...] = jnp.concatenate(rows, axis=0).T


def _mid(x2d, om, og, moba_g, w_out, xattn_g, w_q, kv, w_o, ffn_g, router_w, router_b, seq, mem_len):
    t, d = x2d.shape
    mw, gw, xw, ne = om.shape[1], og.shape[1], w_q.shape[1], router_w.shape[1]
    tiles_per_seq = seq // MID_ROWS
    row = lambda n: pl.BlockSpec((MID_ROWS, n), lambda i: (i, 0))
    const = lambda r, c: pl.BlockSpec((r, c), lambda i: (0, 0))
    rw_hi = router_w.T.astype(BF16)
    rw_lo = (router_w.T - rw_hi.astype(F32)).astype(BF16)
    return pl.pallas_call(
        functools.partial(_mid_kernel, mw=mw, xw=xw),
        grid=(t // MID_ROWS,),
        in_specs=[row(d), row(mw), row(gw), const(1, mw), const(mw + gw, d), const(1, d), const(d, xw),
                  pl.BlockSpec((mem_len, 2 * xw), lambda i: (i // tiles_per_seq, 0)),
                  const(xw, d), const(1, d), const(2 * ne, d), const(ne, 1)],
        out_specs=[row(d), pl.BlockSpec((MID_ROWS * (d // LANES), LANES), lambda i: (i, 0)), row(LANES)],
        out_shape=[jax.ShapeDtypeStruct((t, d), F32), jax.ShapeDtypeStruct((t * (d // LANES), LANES), F32),
                   jax.ShapeDtypeStruct((t, LANES), F32)],
        compiler_params=_params("arbitrary"),
        name="mid",
    )(x2d, om, og, moba_g.reshape(1, mw), w_out.astype(BF16), xattn_g.reshape(1, d), w_q.astype(BF16), kv,
      w_o.astype(BF16), ffn_g.reshape(1, d), jnp.concatenate([rw_hi, rw_lo], axis=0), router_b.reshape(ne, 1))


def _moe_kernel(be_ref, bv_ref, tok_ref, tokn_ref, dst_ref, h_hbm, wgu_ref, bgu_ref, wd_ref, bd_ref, y_hbm,
                xbuf, ybuf, wgu_b16, wd_b16, sem_in, sem_out, *, dff):
    blk = pl.program_id(0)
    nv = bv_ref[blk]
    nv_next = bv_ref[blk + 1]
    slot = blk % 2
    rows = MOE_ROWS
    chunk = MOE_CHUNK
    n_chunks = rows // chunk
    rt = ROW_TILE

    def row_in(s, r, t):
        return pltpu.make_async_copy(h_hbm.at[pl.ds(pl.multiple_of(t, rt), rt), :],
                                     xbuf.at[s, pl.ds(r * rt, rt), :], sem_in.at[s])

    def row_out(s, r, d):
        return pltpu.make_async_copy(ybuf.at[s, pl.ds(r * rt, rt), :],
                                     y_hbm.at[pl.ds(pl.multiple_of(d, rt), rt), :], sem_out.at[s])

    def wait_gather(s):
        pltpu.make_async_copy(h_hbm.at[pl.ds(0, rows * rt), :], xbuf.at[s], sem_in.at[s]).wait()

    def wait_scatter(s):
        pltpu.make_async_copy(ybuf.at[s], y_hbm.at[pl.ds(0, rows * rt), :], sem_out.at[s]).wait()

    @pl.when(blk == 0)
    def _():
        ybuf[1] = jnp.zeros(ybuf.shape[1:], F32)
        spare = pltpu.make_async_copy(ybuf.at[1], y_hbm.at[pl.ds(y_hbm.shape[0] - rows * rt, rows * rt), :],
                                      sem_out.at[1])
        spare.start()
        spare.wait()

    @pl.when((blk == 0) & (nv > 0))
    def _():
        for r in range(rows):
            row_in(0, r, tok_ref[0, 0, r]).start(priority=r % 2)

    @pl.when(nv > 0)
    def _():
        @pl.when((blk == 0) | (be_ref[blk] != be_ref[jnp.maximum(blk - 1, 0)]))
        def _():
            wgu_b16[...] = wgu_ref[0].astype(BF16)
            wd_b16[...] = wd_ref[0].astype(BF16)

        wait_gather(slot)

        @pl.when(blk >= 2)
        def _():
            wait_scatter(slot)

        slab = lambda c, j: pl.ds(c * chunk * rt + j, chunk, stride=rt)
        load_x = lambda c: jnp.concatenate([xbuf[slot, slab(c, j), :] for j in range(rt)], axis=1).astype(BF16)
        per_chunk_in = rows // (n_chunks // 2)
        x_next = load_x(0)
        for c in range(n_chunks):
            x = x_next
            if c < n_chunks // 2:
                for r in range(c * per_chunk_in, (c + 1) * per_chunk_in):
                    row_in(1 - slot, r, tokn_ref[0, 0, r]).start(priority=r % 2)
            if c:
                for r in range((c - 1) * chunk, c * chunk):
                    row_out(slot, r, dst_ref[0, 0, r]).start(priority=r % 2)
            gu = _dot(x, wgu_b16[...]) + bgu_ref[0]
            gate = jnp.minimum(gu[:, :dff], SWIGLU_LIMIT)
            up = jnp.clip(gu[:, dff:], -SWIGLU_LIMIT, SWIGLU_LIMIT)
            act = (up + 1.0) * gate * _sigmoid(SWIGLU_ALPHA * gate)
            y = _dot(act.astype(BF16), wd_b16[...]) + bd_ref[0]
            if c + 1 < n_chunks:
                x_next = load_x(c + 1)
            for j in range(rt):
                ybuf[slot, slab(c, j), :] = y[:, j * LANES:(j + 1) * LANES]
        for r in range(rows - chunk, rows):
            row_out(slot, r, dst_ref[0, 0, r]).start(priority=r % 2)

        @pl.when(nv_next == 0)
        def _():
            wait_gather(1 - slot)

            @pl.when(blk >= 1)
            def _():
                wait_scatter(1 - slot)
            wait_scatter(slot)


def _moe(h3_tiles, block_expert, block_valid, row_tok, row_dst, w_gu, b_gu, w_d, b_d):
    ne, d, n2 = w_gu.shape
    assert d == ROW_TILE * LANES
    t = h3_tiles.shape[0] // ROW_TILE
    dff = n2 // 2
    nblk = block_expert.shape[0]
    idx_blk = pl.BlockSpec((1, 1, MOE_ROWS), lambda i, be, bv: (i, 0, 0), memory_space=pltpu.SMEM)
    idx_next = pl.BlockSpec((1, 1, MOE_ROWS), lambda i, be, bv: (jnp.minimum(i + 1, nblk - 1), 0, 0),
                            memory_space=pltpu.SMEM)
    grid_spec = pltpu.PrefetchScalarGridSpec(
        num_scalar_prefetch=2,
        grid=(nblk,),
        in_specs=[idx_blk, idx_next, idx_blk, pl.BlockSpec(memory_space=pl.ANY),
                  pl.BlockSpec((1, d, n2), lambda i, be, bv: (be[i], 0, 0)),
                  pl.BlockSpec((1, 1, n2), lambda i, be, bv: (be[i], 0, 0)),
                  pl.BlockSpec((1, dff, d), lambda i, be, bv: (be[i], 0, 0)),
                  pl.BlockSpec((1, 1, d), lambda i, be, bv: (be[i], 0, 0))],
        out_specs=pl.BlockSpec(memory_space=pl.ANY),
        scratch_shapes=[pltpu.VMEM((2, MOE_ROWS * ROW_TILE, LANES), F32),
                        pltpu.VMEM((2, MOE_ROWS * ROW_TILE, LANES), F32),
                        pltpu.VMEM((d, n2), BF16), pltpu.VMEM((dff, d), BF16),
                        pltpu.SemaphoreType.DMA((2,)), pltpu.SemaphoreType.DMA((2,))],
    )
    tok3 = (row_tok * ROW_TILE).reshape(nblk, 1, MOE_ROWS)
    valid_ext = jnp.concatenate([block_valid, jnp.zeros((1,), jnp.int32)])
    return pl.pallas_call(
        functools.partial(_moe_kernel, dff=dff),
        grid_spec=grid_spec,
        out_shape=jax.ShapeDtypeStruct(((TOP_K * t + MOE_ROWS) * ROW_TILE, LANES), F32),
        compiler_params=_params("arbitrary"),
        name="moe",
    )(block_expert, valid_ext, tok3, tok3, (row_dst * ROW_TILE).reshape(nblk, 1, MOE_ROWS), h3_tiles,
      w_gu, b_gu.reshape(ne, 1, n2), w_d, b_d.reshape(ne, 1, d))


def _route_plan(expert, t):
    n_pairs = t * TOP_K
    e_flat = expert.reshape(-1)
    order = jnp.argsort(e_flat, stable=True).astype(jnp.int32)
    experts = jnp.arange(N_EXPERTS, dtype=jnp.int32)
    counts = jnp.sum((e_flat[:, None] == experts[None, :]).astype(jnp.int32), axis=0)
    padded = (counts + MOE_ROWS - 1) // MOE_ROWS * MOE_ROWS
    pend = jnp.cumsum(padded)
    pstart = pend - padded
    gstart = jnp.cumsum(counts) - counts
    nblk = n_pairs // MOE_ROWS + N_EXPERTS
    blk_row0 = jnp.arange(nblk, dtype=jnp.int32) * MOE_ROWS
    block_expert = jnp.minimum(jnp.sum((pend[None, :] <= blk_row0[:, None]).astype(jnp.int32), axis=1), N_EXPERTS - 1)
    pick = lambda table: jnp.sum(jnp.where(block_expert[:, None] == experts[None, :], table[None, :], 0), axis=1)
    in_group = blk_row0 - pick(pstart)
    block_valid = jnp.where(blk_row0 < pend[-1], jnp.clip(pick(counts) - in_group, 0, MOE_ROWS), 0)
    first_sorted = pick(gstart) + in_group
    local = jnp.arange(MOE_ROWS, dtype=jnp.int32)
    srt = first_sorted[:, None] + local[None, :]
    pair = order[jnp.clip(srt, 0, n_pairs - 1)]
    row_tok = pair // TOP_K
    row_dst = jnp.where(local[None, :] < block_valid[:, None], (pair % TOP_K) * t + row_tok, n_pairs + local[None, :])
    return (block_expert.astype(jnp.int32), block_valid.astype(jnp.int32), row_tok.astype(jnp.int32),
            row_dst.astype(jnp.int32))


def _combine_kernel(x2_ref, route_ref, g_ref, *rest, final):
    y_refs, o_ref = rest[:TOP_K], rest[TOP_K]
    route = route_ref[...]
    gates = [route[:, kk:kk + 1] for kk in range(TOP_K)]
    slabs = []
    for j in range(ROW_TILE):
        acc = x2_ref[:, j * LANES:(j + 1) * LANES]
        for kk in range(TOP_K):
            acc = acc + gates[kk] * y_refs[kk][pl.ds(j, x2_ref.shape[0], stride=ROW_TILE), :]
        slabs.append(acc)
    out = jnp.concatenate(slabs, axis=1)
    o_ref[...] = _rms(out, g_ref[...]) if final else out


def _combine(x2, route, y, g, final):
    t, d = x2.shape
    tiles = t // MID_ROWS
    slot_spec = lambda kk: pl.BlockSpec((MID_ROWS * ROW_TILE, LANES), lambda i: (kk * tiles + i, 0))
    return pl.pallas_call(
        functools.partial(_combine_kernel, final=final),
        grid=(tiles,),
        in_specs=[pl.BlockSpec((MID_ROWS, d), lambda i: (i, 0)), pl.BlockSpec((MID_ROWS, LANES), lambda i: (i, 0)),
                  pl.BlockSpec((1, d), lambda i: (0, 0))] + [slot_spec(kk) for kk in range(TOP_K)],
        out_specs=pl.BlockSpec((MID_ROWS, d), lambda i: (i, 0)),
        out_shape=jax.ShapeDtypeStruct((t, d), F32),
        compiler_params=_params("arbitrary"),
        name="combine",
    )(x2, route, g.reshape(1, d), *([y] * TOP_K))


def kernel(x, mem, norm_mix_g, w_in, gdn_conv_w, gdn_A_log, gdn_dt_bias, gdn_norm_g, moba_norm_g, w_out,
           norm_xattn_g, norm_mem_g, xattn_w_q, xattn_w_kv, xattn_w_o, norm_ffn_g, router_w, router_b,
           w_gate_up, b_gate_up, w_down, b_down, final_norm_g):
    b, s, d = x.shape
    t = b * s
    mem_len = mem.shape[1]
    mw = moba_norm_g.shape[1]
    gw = GDN_HEADS * GDN_HEAD_DIM
    xcur = x.reshape(t, d)
    for l in range(w_in.shape[0]):
        qt, mk, mv, gqkv, gz, bg = _in_proj(xcur, norm_mix_g[l], w_in[l], gdn_conv_w[l], gdn_A_log[l], gdn_dt_bias[l],
                                            mw, gw, s)
        o_moba = _moba(qt, mk, mv, b, s)
        o_gdn = _gdn(gqkv.reshape(b, s, 3 * gw), gz.reshape(b, s, gw), bg.reshape(b, s, LANES), gdn_norm_g[l])
        kv = _mem_kv(mem.reshape(b * mem_len, d), norm_mem_g[l], xattn_w_kv[l], mem_len)
        x2, h3, route = _mid(xcur, o_moba.reshape(t, mw), o_gdn.reshape(t, gw), moba_norm_g[l], w_out[l],
                             norm_xattn_g[l], xattn_w_q[l], kv, xattn_w_o[l], norm_ffn_g[l], router_w[l],
                             router_b[l], s, mem_len)
        expert = route[:, TOP_K:2 * TOP_K].astype(jnp.int32)
        plan = _route_plan(expert, t)
        y = _moe(h3, *plan, w_gate_up[l], b_gate_up[l], w_down[l], b_down[l])
        xcur = _combine(x2, route, y, final_norm_g, l == w_in.shape[0] - 1)
    return xcur.reshape(b, s, d)
```

```python
import functools

import jax
import jax.numpy as jnp
from jax import lax
from jax.experimental import pallas as pl
from jax.experimental.pallas import tpu as pltpu

F32 = jnp.float32
BF16 = jnp.bfloat16

RMS_EPS = 1e-6
MOBA_HEAD_DIM = 64
MOBA_BLOCK = 256
MOBA_TOPK = 3
GDN_HEAD_DIM = 128
GDN_HEADS = 4
GDN_CONV = 4
GDN_CHUNK = 64
XATTN_HEAD_DIM = 128
N_EXPERTS = 32
TOP_K = 4
SWIGLU_LIMIT = 7.0
SWIGLU_ALPHA = 1.702

LANES = 128
ROW_TILE = 8
VMEM_LIMIT = 56 * 1024 * 1024

IN_ROWS = 512
MID_ROWS = 1024
COMBINE_ROWS = 512
MOE_ROWS = 512
MOE_CHUNK = 128
GDN_GROUP = 32
PREP_ROWS = 256
GDN_HALO = ROW_TILE
SCAN_BATCH = 2
SCAN_ROWS = 1024
NEG_INF = float("-inf")
MOBA_Q_SCALE = 1.4426950408889634 / MOBA_HEAD_DIM ** 0.5
MOBA_MASKED = -1e30


def _params(*sem):
    return pltpu.CompilerParams(dimension_semantics=sem, vmem_limit_bytes=VMEM_LIMIT)


def _rms(x, g):
    return x * lax.rsqrt(jnp.mean(x * x, axis=-1, keepdims=True) + RMS_EPS) * g


def _dot(a, b):
    return jnp.dot(a, b, preferred_element_type=F32)


def _dot_nt(a, b):
    return lax.dot_general(a, b, (((1,), (1,)), ((), ())), preferred_element_type=F32)


def _sigmoid(x):
    return 1.0 / (1.0 + jnp.exp(-x))


def _gdn_gates(x, a_log, dt_bias):
    lane = lax.broadcasted_iota(jnp.int32, (1, LANES), 1)
    xa = x + dt_bias
    softplus = jnp.maximum(xa, 0.0) + jnp.log(1.0 + jnp.exp(-jnp.abs(xa)))
    g = jnp.where((lane >= GDN_HEADS) & (lane < 2 * GDN_HEADS), -jnp.exp(a_log) * softplus, 0.0)
    pos = lax.broadcasted_iota(jnp.int32, (x.shape[0], 1), 0) % GDN_CHUNK
    sft = 1
    while sft < GDN_CHUNK:
        g = g + jnp.where(pos >= sft, pltpu.roll(g, sft, 0), 0.0)
        sft *= 2
    return jnp.where(lane < GDN_HEADS, _sigmoid(x), g)


def _in_proj_kernel(x_ref, g_ref, w_ref, wt_ref, cw_ref, alog_ref, dtb_ref, qt_ref, mk_ref, mv_ref, qkv_ref, gz_ref,
                    bg_ref, *bufs,
                    mw, gw, tiles_per_seq):
    hn = _rms(x_ref[...], g_ref[...]).astype(BF16)
    mm = lambda lo, hi: _dot(hn, w_ref[:, lo:hi])
    rows = x_ref.shape[0]
    halo = GDN_HALO

    @pl.when(pl.program_id(0) % tiles_per_seq == 0)
    def _():
        for buf in bufs:
            buf[:halo, :] = jnp.zeros((halo, LANES), F32)

    @pl.when(pl.program_id(0) % tiles_per_seq != 0)
    def _():
        for buf in bufs:
            buf[:halo, :] = buf[rows:rows + halo, :]

    for grp in range(3):
        raw = mm(2 * mw + grp * gw, 2 * mw + (grp + 1) * gw)
        for h in range(GDN_HEADS):
            bufs[grp * GDN_HEADS + h][halo:, :] = raw[:, h * LANES:(h + 1) * LANES]
    qt_ref[...] = (_dot_nt(wt_ref[...], hn) * MOBA_Q_SCALE).astype(BF16)
    mk_ref[...] = mm(0, mw).astype(BF16)
    mv_ref[...] = mm(mw, 2 * mw).astype(BF16)
    gz_ref[...] = mm(2 * mw + 3 * gw, 2 * mw + 4 * gw)
    bg_ref[...] = _gdn_gates(mm(2 * mw + 4 * gw, 2 * mw + 4 * gw + LANES), alog_ref[...], dtb_ref[...])

    for cb in range(3 * gw // LANES):
        cols = slice(cb * LANES, (cb + 1) * LANES)
        taps = [cw_ref[j:j + 1, cols] for j in range(GDN_CONV)]
        buf = bufs[cb]
        for r0 in range(0, rows, PREP_ROWS):
            y = taps[GDN_CONV - 1] * buf[halo + r0:halo + r0 + PREP_ROWS, :]
            for sft in range(1, GDN_CONV):
                y = y + taps[GDN_CONV - 1 - sft] * buf[halo + r0 - sft:halo + r0 - sft + PREP_ROWS, :]
            y = y * _sigmoid(y)
            if cb < 2 * GDN_HEADS:
                scale = GDN_HEAD_DIM ** -0.5 if cb < GDN_HEADS else 1.0
                y = y * (lax.rsqrt(jnp.sum(y * y, axis=-1, keepdims=True) + RMS_EPS) * scale)
            qkv_ref[r0:r0 + PREP_ROWS, cols] = y.astype(BF16)


def _in_proj(x2d, g, w_in, conv_w, a_log, dt_bias, mw, gw, seq):
    t, d = x2d.shape
    n_real = w_in.shape[1] - mw
    n_pad = 2 * mw + 4 * gw + LANES
    w = jnp.pad(w_in[:, mw:], ((0, 0), (0, n_pad - n_real))).astype(BF16)
    w_t = w_in[:, :mw].T.astype(BF16)
    row = lambda n: pl.BlockSpec((IN_ROWS, n), lambda i: (i, 0))
    const = lambda shape: pl.BlockSpec(shape, lambda i: (0, 0))
    nh = GDN_HEADS
    lane_pad = lambda v: jnp.pad(v.reshape(1, -1), ((0, 0), (nh, LANES - 2 * nh)))
    return pl.pallas_call(
        functools.partial(_in_proj_kernel, mw=mw, gw=gw, tiles_per_seq=seq // IN_ROWS),
        grid=(t // IN_ROWS,),
        in_specs=[row(d), const((1, d)), const((d, n_pad)), const((mw, d)), const((GDN_CONV, 3 * gw)),
                  const((1, LANES)), const((1, LANES))],
        out_specs=[pl.BlockSpec((mw, IN_ROWS), lambda i: (0, i)), row(mw), row(mw), row(3 * gw), row(gw),
                   row(LANES)],
        out_shape=[jax.ShapeDtypeStruct((mw, t), BF16), jax.ShapeDtypeStruct((t, mw), BF16),
                   jax.ShapeDtypeStruct((t, mw), BF16),
                   jax.ShapeDtypeStruct((t, 3 * gw), BF16), jax.ShapeDtypeStruct((t, gw), F32),
                   jax.ShapeDtypeStruct((t, LANES), F32)],
        scratch_shapes=[pltpu.VMEM((GDN_HALO + IN_ROWS, LANES), F32)] * (3 * gw // LANES),
        compiler_params=_params("arbitrary"),
        name="in_proj",
    )(x2d, g.reshape(1, d), w, w_t, conv_w, lane_pad(a_log), lane_pad(dt_bias))


def _moba_select(g_t, i):
    nb = g_t.shape[0]
    row = lax.broadcasted_iota(jnp.int32, g_t.shape, 0)
    valid = row < i
    sel = jnp.zeros_like(g_t)
    for j in range(nb):
        gj = g_t[j:j + 1, :]
        beats = valid & ((g_t > gj) | ((g_t == gj) & (row < j)))
        rank = jnp.sum(jnp.where(beats, 1.0, 0.0), axis=0, keepdims=True)
        sel = jnp.where(row == j, jnp.where(rank < MOBA_TOPK, 1.0, 0.0), sel)
    return jnp.where(valid, sel, 0.0)


def _dot_tn(a, b):
    return lax.dot_general(a, b, (((0,), (0,)), ((), ())), preferred_element_type=F32)


def _moba_kernel(qt_ref, qt_all_ref, k_ref, v_ref, o_ref, kaug_ref, bias_ref, *, nb):
    i = pl.program_id(2)
    bs = MOBA_BLOCK
    hd = MOBA_HEAD_DIM
    lane = lax.broadcasted_iota(jnp.int32, (1, LANES), 1)
    low = lax.broadcasted_iota(jnp.int32, (LANES, 1), 0) < hd

    @pl.when(i == 0)
    def _():
        kmean = []
        for j in range(nb):
            rows = slice(j * bs, (j + 1) * bs)
            kb = k_ref[0, rows, :]
            kmean.append(jnp.mean(kb.astype(F32), axis=0, keepdims=True))
            kaug_ref[0, rows, :] = jnp.where(lane < hd, kb, jnp.where(lane == hd + j, 1.0, 0.0).astype(BF16))
            kaug_ref[1, rows, :] = jnp.where(lane >= hd, kb, jnp.where(lane == j, 1.0, 0.0).astype(BF16))
        kmean = jnp.concatenate(kmean, axis=0)
        q_all = qt_all_ref[...].astype(F32)
        for h, qh in enumerate((jnp.where(low, q_all, 0.0), jnp.where(low, 0.0, q_all))):
            gate = jnp.dot(kmean, qh, preferred_element_type=F32, precision=lax.Precision.HIGHEST)
            for jq in range(nb):
                sel = _moba_select(gate[:, jq * bs:(jq + 1) * bs], jq)
                bias_ref[h, jq] = jnp.where(sel > 0.5, 0.0, MOBA_MASKED)

    qt = qt_ref[...]
    qtf = qt.astype(F32)
    bias = [bias_ref[h, i] for h in (0, 1)]
    pad = jnp.zeros((hd - nb, bs), F32)
    q_past = (jnp.concatenate([qtf[:hd], bias[0], pad], axis=0).astype(BF16),
              jnp.concatenate([bias[1], pad, qtf[hd:]], axis=0).astype(BF16))
    zero = jnp.zeros_like(qt)
    q_own = (jnp.where(low, qt, zero), jnp.where(low, zero, qt))

    key_ix = lax.broadcasted_iota(jnp.int32, (bs, bs), 0)
    qry_ix = lax.broadcasted_iota(jnp.int32, (bs, bs), 1)
    causal_bias = jnp.where(key_ix <= qry_ix, 0.0, NEG_INF)
    own = pl.ds(pl.multiple_of(i * bs, bs), bs)
    k_own = k_ref[0, own, :]
    v_own = v_ref[0, own, :]

    def attend(width):
        heads = (0, 1)
        parts = [slice(lo * bs, (lo + 1) * bs) for lo in range(width)]
        n = len(parts) + 1
        past_scores = lambda k: [_dot(kaug_ref[h, parts[k - 1], :], q_past[h]) for h in heads]
        scores = {0: [_dot(k_own, q_own[h]) + causal_bias for h in heads]}
        if n > 1:
            scores[1] = past_scores(1)
        m = l = acc = None
        for k in range(n):
            s_k = scores.pop(k)
            m_new = [s.max(axis=0, keepdims=True) for s in s_k]
            if m is not None:
                m_new = [jnp.maximum(m[h], m_new[h]) for h in heads]
            p = [jnp.exp2(s_k[h] - m_new[h]) for h in heads]
            p_sum = [x.sum(axis=0, keepdims=True) for x in p]
            if k + 2 < n:
                scores[k + 2] = past_scores(k + 2)
            values = v_own if k == 0 else v_ref[0, parts[k - 1], :]
            pv = [_dot_tn(values, p[h].astype(BF16)) for h in heads]
            if m is None:
                l, acc = p_sum, pv
            else:
                alpha = [jnp.exp2(m[h] - m_new[h]) for h in heads]
                l = [alpha[h] * l[h] + p_sum[h] for h in heads]
                acc = [alpha[h] * acc[h] + pv[h] for h in heads]
            m = m_new
        o_ref[0] = jnp.where(low, acc[0] / l[0], acc[1] / l[1]).T

    for width in range(nb):
        pl.when(i == width)(functools.partial(attend, width))


def _moba(qt, mk, mv, b, s):
    mw = mk.shape[-1]
    nb = s // MOBA_BLOCK
    seq_blk = pl.BlockSpec((1, s, LANES), lambda bi, hp, i: (bi, 0, hp))
    return pl.pallas_call(
        functools.partial(_moba_kernel, nb=nb),
        grid=(b, mw // LANES, nb),
        in_specs=[pl.BlockSpec((LANES, MOBA_BLOCK), lambda bi, hp, i: (hp, bi * nb + i)),
                  pl.BlockSpec((LANES, s), lambda bi, hp, i: (hp, bi)), seq_blk, seq_blk],
        out_specs=pl.BlockSpec((1, MOBA_BLOCK, LANES), lambda bi, hp, i: (bi, i, hp)),
        out_shape=jax.ShapeDtypeStruct((b, s, mw), F32),
        scratch_shapes=[pltpu.VMEM((2, s, LANES), BF16), pltpu.VMEM((2, nb, nb, MOBA_BLOCK), F32)],
        compiler_params=_params("arbitrary", "arbitrary", "arbitrary"),
        name="moba",
    )(qt, qt, mk.reshape(b, s, mw), mv.reshape(b, s, mw))


def _unit_lower_inverses(mats):
    c = mats[0].shape[0]
    r = lax.broadcasted_iota(jnp.int32, (c, c), 0)
    cc = lax.broadcasted_iota(jnp.int32, (c, c), 1)
    eye = jnp.where(r == cc, 1.0, 0.0)
    pair = (r // 2) == (cc // 2)
    invs = [eye - jnp.where(pair, a, 0.0) for a in mats]
    size = 4
    while size <= c:
        level = ((r // size) == (cc // size)) & ((r // (size // 2)) != (cc // (size // 2)))
        inv_b = [inv.astype(BF16) for inv in invs]
        left = [_dot(ib, jnp.where(level, a, 0.0).astype(BF16)).astype(BF16) for ib, a in zip(inv_b, mats)]
        invs = [inv - _dot(lf, ib) for inv, lf, ib in zip(invs, left, inv_b)]
        size *= 2
    return invs


def _gdn_intra_kernel(q_ref, k_ref, v_ref, bg_ref, u_ref, w_ref, qd_ref, kd_ref, qk_ref):
    h = pl.program_id(1)
    c = GDN_CHUNK
    chunks = range(GDN_GROUP)
    lane = lax.broadcasted_iota(jnp.int32, (1, LANES), 1)
    r_ix = lax.broadcasted_iota(jnp.int32, (c, c), 0)
    c_ix = lax.broadcasted_iota(jnp.int32, (c, c), 1)
    rows = [slice(gi * c, (gi + 1) * c) for gi in chunks]
    k_b16 = [k_ref[0, rw, :] for rw in rows]
    q_b16 = [q_ref[0, rw, :] for rw in rows]
    bgs = [bg_ref[0, rw, :] for rw in rows]
    beta = [jnp.sum(jnp.where(lane == h, bg, 0.0), axis=1, keepdims=True) for bg in bgs]
    gam = [jnp.sum(jnp.where(lane == GDN_HEADS + h, bg, 0.0), axis=1, keepdims=True) for bg in bgs]
    kb = [kk.astype(F32) * bt for kk, bt in zip(k_b16, beta)]
    kk_raw = [_dot_nt(x.astype(BF16), kk) for x, kk in zip(kb, k_b16)]
    qk_raw = [_dot_nt(qq, kk) for qq, kk in zip(q_b16, k_b16)]
    decay = []
    for gm in gam:
        gam_r = jnp.sum(jnp.where(r_ix == c_ix, gm, 0.0), axis=0, keepdims=True)
        decay.append(jnp.exp(jnp.where(c_ix <= r_ix, gm - gam_r, NEG_INF)))
    t_inv = _unit_lower_inverses([jnp.where(c_ix < r_ix, x * dc, 0.0) for x, dc in zip(kk_raw, decay)])
    eg = [jnp.exp(gm) for gm in gam]
    rhs = [jnp.concatenate([v_ref[0, rw, :].astype(F32) * bt, x * e], axis=1).astype(BF16)
           for rw, bt, x, e in zip(rows, beta, kb, eg)]
    uw = [_dot(ti.astype(BF16), rh) for ti, rh in zip(t_inv, rhs)]
    for gi in chunks:
        rw = rows[gi]
        u_ref[0, rw, :] = uw[gi][:, :LANES].astype(BF16)
        w_ref[0, rw, :] = uw[gi][:, LANES:].astype(BF16)
        qk_ref[0, 0, rw, :] = (qk_raw[gi] * decay[gi]).astype(BF16)
        qd_ref[0, rw, :] = (q_b16[gi].astype(F32) * eg[gi]).astype(BF16)
        kd_ref[0, rw, :] = (k_b16[gi].astype(F32) * jnp.exp(gam[gi][c - 1:c, :] - gam[gi])).astype(BF16)


def _gdn_scan_kernel(u_ref, w_ref, qd_ref, kd_ref, qk_ref, z_ref, bg_ref, ng_ref, o_ref, state_ref, *, rows_per_step):
    c = GDN_CHUNK
    ng = ng_ref[...]
    chains = [(bi, h) for bi in range(SCAN_BATCH) for h in range(GDN_HEADS)]
    cols = [slice(h * GDN_HEAD_DIM, (h + 1) * GDN_HEAD_DIM) for h in range(GDN_HEADS)]

    @pl.when(pl.program_id(1) == 0)
    def _():
        state_ref[...] = jnp.zeros(state_ref.shape, F32)

    def step(n, states):
        r0 = pl.multiple_of(n * c, c)
        rows = pl.ds(r0, c)
        tails = [bg_ref[bi, pl.ds(r0 + c - 8, 8), :] for bi in range(SCAN_BATCH)]
        s_b = [st.astype(BF16) for st in states]
        ws = [_dot(w_ref[bi, rows, cols[h]], s_b[k]) for k, (bi, h) in enumerate(chains)]
        qs = [_dot(qd_ref[bi, rows, cols[h]], s_b[k]) for k, (bi, h) in enumerate(chains)]
        v_b = [(u_ref[bi, rows, cols[h]].astype(F32) - ws[k]).astype(BF16) for k, (bi, h) in enumerate(chains)]
        kd_v = [_dot_tn(kd_ref[bi, rows, cols[h]], v_b[k]) for k, (bi, h) in enumerate(chains)]
        qkv = [_dot(qk_ref[bi, h, rows, :], v_b[k]) for k, (bi, h) in enumerate(chains)]
        new_states = []
        for k, (bi, h) in enumerate(chains):
            g_last = tails[bi][7:8, GDN_HEADS + h:GDN_HEADS + h + 1]
            new_states.append(states[k] * jnp.exp(g_last) + kd_v[k])
            z = z_ref[bi, rows, cols[h]]
            o_ref[bi, rows, cols[h]] = _rms(qs[k] + qkv[k], ng) * (z * _sigmoid(z))
        return tuple(new_states)

    init = tuple(state_ref[k] for k in range(len(chains)))
    final = lax.fori_loop(0, rows_per_step // c, step, init)
    for k, st in enumerate(final):
        state_ref[k] = st


def _gdn(qkv, gz, bg, norm_g):
    b, s, w3 = qkv.shape
    nh = GDN_HEADS
    hw = nh * GDN_HEAD_DIM
    c = GDN_CHUNK
    grp = GDN_GROUP * c
    head_blk = lambda off: pl.BlockSpec((1, grp, LANES), lambda bi, h, n: (bi, n, off + h))
    head_shape = jax.ShapeDtypeStruct((b, s, hw), BF16)
    u, w, qd, kd, qk = pl.pallas_call(
        _gdn_intra_kernel,
        grid=(b, nh, s // grp),
        in_specs=[head_blk(0), head_blk(nh), head_blk(2 * nh),
                  pl.BlockSpec((1, grp, LANES), lambda bi, h, n: (bi, n, 0))],
        out_specs=[head_blk(0)] * 4 + [pl.BlockSpec((1, 1, grp, c), lambda bi, h, n: (bi, h, n, 0))],
        out_shape=[head_shape] * 4 + [jax.ShapeDtypeStruct((b, nh, s, c), BF16)],
        compiler_params=_params("arbitrary", "arbitrary", "arbitrary"),
        name="gdn_intra",
    )(qkv, qkv, qkv, bg)

    tile = lambda width: pl.BlockSpec((SCAN_BATCH, SCAN_ROWS, width), lambda bi, ti: (bi, ti, 0))
    return pl.pallas_call(
        functools.partial(_gdn_scan_kernel, rows_per_step=SCAN_ROWS),
        grid=(b // SCAN_BATCH, s // SCAN_ROWS),
        in_specs=[tile(hw), tile(hw), tile(hw), tile(hw),
                  pl.BlockSpec((SCAN_BATCH, nh, SCAN_ROWS, c), lambda bi, ti: (bi, 0, ti, 0)), tile(hw), tile(LANES),
                  pl.BlockSpec((1, LANES), lambda bi, ti: (0, 0))],
        out_specs=tile(hw),
        out_shape=jax.ShapeDtypeStruct((b, s, hw), F32),
        scratch_shapes=[pltpu.VMEM((SCAN_BATCH * nh, GDN_HEAD_DIM, GDN_HEAD_DIM), F32)],
        compiler_params=_params("arbitrary", "arbitrary"),
        name="gdn_scan",
    )(u, w, qd, kd, qk, gz, bg, norm_g.reshape(1, -1))


def _mem_kv_kernel(m_ref, g_ref, w_ref, kv_ref):
    kv_ref[...] = _dot(_rms(m_ref[...], g_ref[...]).astype(BF16), w_ref[...]).astype(BF16)


def _mem_kv(mem2d, g, w_kv, rows):
    t, d = mem2d.shape
    n = w_kv.shape[1]
    return pl.pallas_call(
        _mem_kv_kernel,
        grid=(t // rows,),
        in_specs=[pl.BlockSpec((rows, d), lambda i: (i, 0)), pl.BlockSpec((1, d), lambda i: (0, 0)),
                  pl.BlockSpec((d, n), lambda i: (0, 0))],
        out_specs=pl.BlockSpec((rows, n), lambda i: (i, 0)),
        out_shape=jax.ShapeDtypeStruct((t, n), BF16),
        compiler_params=_params("arbitrary"),
        name="mem_kv",
    )(mem2d, g.reshape(1, d), w_kv.astype(BF16))


def _mid_kernel(x_ref, om_ref, og_ref, mg_ref, wout_ref, xg_ref, wq_ref, kv_ref, wo_ref, fg_ref, rw_ref, rb_ref,
                x2_ref, h3_ref, route_ref, *, mw, xw):
    mo = _rms(om_ref[...], mg_ref[...]).astype(BF16)
    x1 = x_ref[...] + _dot(mo, wout_ref[:mw, :]) + _dot(og_ref[...].astype(BF16), wout_ref[mw:, :])

    h2 = _rms(x1, xg_ref[...]).astype(BF16)
    q = (_dot(h2, wq_ref[...]) * (XATTN_HEAD_DIM ** -0.5)).astype(BF16)
    head_cols = [slice(h * XATTN_HEAD_DIM, (h + 1) * XATTN_HEAD_DIM) for h in range(xw // XATTN_HEAD_DIM)]
    scores = [_dot_nt(q[:, sl], kv_ref[:, sl]) for sl in head_cols]
    probs = [jnp.exp(s - jnp.max(s, axis=1, keepdims=True)) for s in scores]
    heads = [_dot(p.astype(BF16), kv_ref[:, xw + sl.start:xw + sl.stop]) / jnp.sum(p, axis=1, keepdims=True)
             for p, sl in zip(probs, head_cols)]
    x2 = x1 + _dot(jnp.concatenate(heads, axis=1).astype(BF16), wo_ref[...])
    x2_ref[...] = x2

    h3 = _rms(x2, fg_ref[...])
    slabs = h3.shape[1] // LANES
    for j in range(slabs):
        h3_ref[pl.ds(j, h3.shape[0], stride=slabs), :] = h3[:, j * LANES:(j + 1) * LANES]
    ne = rb_ref.shape[0]
    h_hi = h3.astype(BF16)
    h_lo = (h3 - h_hi.astype(F32)).astype(BF16)
    by_hi = _dot_nt(rw_ref[...], h_hi)
    logits = by_hi[:ne, :] + by_hi[ne:, :] + _dot_nt(rw_ref[:ne, :], h_lo) + rb_ref[...]
    row = lax.broadcasted_iota(jnp.int32, logits.shape, 0)
    weights, picks, top = [], [], None
    for kk in range(TOP_K):
        m = jnp.max(logits, axis=0, keepdims=True)
        idx = jnp.min(jnp.where(logits == m, row, ne), axis=0, keepdims=True)
        logits = jnp.where(row == idx, NEG_INF, logits)
        top = m if top is None else top
        weights.append(jnp.exp(m - top))
        picks.append(idx.astype(F32))
    denom = sum(weights[1:], weights[0])
    rows = [wk / denom for wk in weights] + picks
    rows.append(jnp.zeros((LANES - len(rows), logits.shape[1]), F32))
    route_ref[...] = jnp.concatenate(rows, axis=0).T


def _mid(x2d, om, og, moba_g, w_out, xattn_g, w_q, kv, w_o, ffn_g, router_w, router_b, seq, mem_len):
    t, d = x2d.shape
    mw, gw, xw, ne = om.shape[1], og.shape[1], w_q.shape[1], router_w.shape[1]
    tiles_per_seq = seq // MID_ROWS
    row = lambda n: pl.BlockSpec((MID_ROWS, n), lambda i: (i, 0))
    const = lambda r, c: pl.BlockSpec((r, c), lambda i: (0, 0))
    rw_hi = router_w.T.astype(BF16)
    rw_lo = (router_w.T - rw_hi.astype(F32)).astype(BF16)
    return pl.pallas_call(
        functools.partial(_mid_kernel, mw=mw, xw=xw),
        grid=(t // MID_ROWS,),
        in_specs=[row(d), row(mw), row(gw), const(1, mw), const(mw + gw, d), const(1, d), const(d, xw),
                  pl.BlockSpec((mem_len, 2 * xw), lambda i: (i // tiles_per_seq, 0)),
                  const(xw, d), const(1, d), const(2 * ne, d), const(ne, 1)],
        out_specs=[row(d), pl.BlockSpec((MID_ROWS * (d // LANES), LANES), lambda i: (i, 0)), row(LANES)],
        out_shape=[jax.ShapeDtypeStruct((t, d), F32), jax.ShapeDtypeStruct((t * (d // LANES), LANES), F32),
                   jax.ShapeDtypeStruct((t, LANES), F32)],
        compiler_params=_params("arbitrary"),
        name="mid",
    )(x2d, om, og, moba_g.reshape(1, mw), w_out.astype(BF16), xattn_g.reshape(1, d), w_q.astype(BF16), kv,
      w_o.astype(BF16), ffn_g.reshape(1, d), jnp.concatenate([rw_hi, rw_lo], axis=0), router_b.reshape(ne, 1))


def _moe_kernel(be_ref, bv_ref, tok_ref, tokn_ref, dst_ref, h_hbm, wgu_ref, bgu_ref, wd_ref, bd_ref, y_hbm,
                xbuf, ybuf, wgu_b16, wd_b16, sem_in, sem_out, *, dff):
    blk = pl.program_id(0)
    nv = bv_ref[blk]
    nv_next = bv_ref[blk + 1]
    slot = blk % 2
    rows = MOE_ROWS
    chunk = MOE_CHUNK
    n_chunks = rows // chunk
    rt = ROW_TILE

    def row_in(s, r, t):
        return pltpu.make_async_copy(h_hbm.at[pl.ds(pl.multiple_of(t, rt), rt), :],
                                     xbuf.at[s, pl.ds(r * rt, rt), :], sem_in.at[s])

    def row_out(s, r, d):
        return pltpu.make_async_copy(ybuf.at[s, pl.ds(r * rt, rt), :],
                                     y_hbm.at[pl.ds(pl.multiple_of(d, rt), rt), :], sem_out.at[s])

    def wait_gather(s):
        pltpu.make_async_copy(h_hbm.at[pl.ds(0, rows * rt), :], xbuf.at[s], sem_in.at[s]).wait()

    def wait_scatter(s):
        pltpu.make_async_copy(ybuf.at[s], y_hbm.at[pl.ds(0, rows * rt), :], sem_out.at[s]).wait()

    @pl.when(blk == 0)
    def _():
        ybuf[1] = jnp.zeros(ybuf.shape[1:], F32)
        spare = pltpu.make_async_copy(ybuf.at[1], y_hbm.at[pl.ds(y_hbm.shape[0] - rows * rt, rows * rt), :],
                                      sem_out.at[1])
        spare.start()
        spare.wait()

    @pl.when((blk == 0) & (nv > 0))
    def _():
        for r in range(rows):
            row_in(0, r, tok_ref[0, 0, r]).start(priority=r % 2)

    @pl.when(nv > 0)
    def _():
        @pl.when((blk == 0) | (be_ref[blk] != be_ref[jnp.maximum(blk - 1, 0)]))
        def _():
            wgu_b16[...] = wgu_ref[0].astype(BF16)
            wd_b16[...] = wd_ref[0].astype(BF16)

        wait_gather(slot)

        @pl.when(blk >= 2)
        def _():
            wait_scatter(slot)

        slab = lambda c, j: pl.ds(c * chunk * rt + j, chunk, stride=rt)
        load_x = lambda c: jnp.concatenate([xbuf[slot, slab(c, j), :] for j in range(rt)], axis=1).astype(BF16)
        per_chunk_in = rows // (n_chunks // 2)
        x_next = load_x(0)
        for c in range(n_chunks):
            x = x_next
            if c < n_chunks // 2:
                for r in range(c * per_chunk_in, (c + 1) * per_chunk_in):
                    row_in(1 - slot, r, tokn_ref[0, 0, r]).start(priority=r % 2)
            if c:
                for r in range((c - 1) * chunk, c * chunk):
                    row_out(slot, r, dst_ref[0, 0, r]).start(priority=r % 2)
            gu = _dot(x, wgu_b16[...]) + bgu_ref[0]
            gate = jnp.minimum(gu[:, :dff], SWIGLU_LIMIT)
            up = jnp.clip(gu[:, dff:], -SWIGLU_LIMIT, SWIGLU_LIMIT)
            act = (up + 1.0) * gate * _sigmoid(SWIGLU_ALPHA * gate)
            y = _dot(act.astype(BF16), wd_b16[...]) + bd_ref[0]
            if c + 1 < n_chunks:
                x_next = load_x(c + 1)
            for j in range(rt):
                ybuf[slot, slab(c, j), :] = y[:, j * LANES:(j + 1) * LANES]
        for r in range(rows - chunk, rows):
            row_out(slot, r, dst_ref[0, 0, r]).start(priority=r % 2)

        @pl.when(nv_next == 0)
        def _():
            wait_gather(1 - slot)

            @pl.when(blk >= 1)
            def _():
                wait_scatter(1 - slot)
            wait_scatter(slot)


def _moe(h3_tiles, block_expert, block_valid, row_tok, row_dst, w_gu, b_gu, w_d, b_d):
    ne, d, n2 = w_gu.shape
    assert d == ROW_TILE * LANES
    t = h3_tiles.shape[0] // ROW_TILE
    dff = n2 // 2
    nblk = block_expert.shape[0]
    idx_blk = pl.BlockSpec((1, 1, MOE_ROWS), lambda i, be, bv: (i, 0, 0), memory_space=pltpu.SMEM)
    idx_next = pl.BlockSpec((1, 1, MOE_ROWS), lambda i, be, bv: (jnp.minimum(i + 1, nblk - 1), 0, 0),
                            memory_space=pltpu.SMEM)
    grid_spec = pltpu.PrefetchScalarGridSpec(
        num_scalar_prefetch=2,
        grid=(nblk,),
        in_specs=[idx_blk, idx_next, idx_blk, pl.BlockSpec(memory_space=pl.ANY),
                  pl.BlockSpec((1, d, n2), lambda i, be, bv: (be[i], 0, 0)),
                  pl.BlockSpec((1, 1, n2), lambda i, be, bv: (be[i], 0, 0)),
                  pl.BlockSpec((1, dff, d), lambda i, be, bv: (be[i], 0, 0)),
                  pl.BlockSpec((1, 1, d), lambda i, be, bv: (be[i], 0, 0))],
        out_specs=pl.BlockSpec(memory_space=pl.ANY),
        scratch_shapes=[pltpu.VMEM((2, MOE_ROWS * ROW_TILE, LANES), F32),
                        pltpu.VMEM((2, MOE_ROWS * ROW_TILE, LANES), F32),
                        pltpu.VMEM((d, n2), BF16), pltpu.VMEM((dff, d), BF16),
                        pltpu.SemaphoreType.DMA((2,)), pltpu.SemaphoreType.DMA((2,))],
    )
    tok3 = (row_tok * ROW_TILE).reshape(nblk, 1, MOE_ROWS)
    valid_ext = jnp.concatenate([block_valid, jnp.zeros((1,), jnp.int32)])
    return pl.pallas_call(
        functools.partial(_moe_kernel, dff=dff),
        grid_spec=grid_spec,
        out_shape=jax.ShapeDtypeStruct(((TOP_K * t + MOE_ROWS) * ROW_TILE, LANES), F32),
        compiler_params=_params("arbitrary"),
        name="moe",
    )(block_expert, valid_ext, tok3, tok3, (row_dst * ROW_TILE).reshape(nblk, 1, MOE_ROWS), h3_tiles,
      w_gu, b_gu.reshape(ne, 1, n2), w_d, b_d.reshape(ne, 1, d))


def _route_plan(expert, t):
    n_pairs = t * TOP_K
    e_flat = expert.reshape(-1)
    order = jnp.argsort(e_flat, stable=True).astype(jnp.int32)
    experts = jnp.arange(N_EXPERTS, dtype=jnp.int32)
    counts = jnp.sum((e_flat[:, None] == experts[None, :]).astype(jnp.int32), axis=0)
    padded = (counts + MOE_ROWS - 1) // MOE_ROWS * MOE_ROWS
    pend = jnp.cumsum(padded)
    pstart = pend - padded
    gstart = jnp.cumsum(counts) - counts
    nblk = n_pairs // MOE_ROWS + N_EXPERTS
    blk_row0 = jnp.arange(nblk, dtype=jnp.int32) * MOE_ROWS
    block_expert = jnp.minimum(jnp.sum((pend[None, :] <= blk_row0[:, None]).astype(jnp.int32), axis=1), N_EXPERTS - 1)
    pick = lambda table: jnp.sum(jnp.where(block_expert[:, None] == experts[None, :], table[None, :], 0), axis=1)
    in_group = blk_row0 - pick(pstart)
    block_valid = jnp.where(blk_row0 < pend[-1], jnp.clip(pick(counts) - in_group, 0, MOE_ROWS), 0)
    first_sorted = pick(gstart) + in_group
    local = jnp.arange(MOE_ROWS, dtype=jnp.int32)
    srt = first_sorted[:, None] + local[None, :]
    pair = order[jnp.clip(srt, 0, n_pairs - 1)]
    row_tok = pair // TOP_K
    row_dst = jnp.where(local[None, :] < block_valid[:, None], (pair % TOP_K) * t + row_tok, n_pairs + local[None, :])
    return (block_expert.astype(jnp.int32), block_valid.astype(jnp.int32), row_tok.astype(jnp.int32),
            row_dst.astype(jnp.int32))


def _combine_kernel(x2_ref, route_ref, g_ref, *rest, final):
    y_refs, o_ref = rest[:TOP_K], rest[TOP_K]
    route = route_ref[...]
    gates = [route[:, kk:kk + 1] for kk in range(TOP_K)]
    slabs = []
    for j in range(ROW_TILE):
        acc = x2_ref[:, j * LANES:(j + 1) * LANES]
        for kk in range(TOP_K):
            acc = acc + gates[kk] * y_refs[kk][pl.ds(j, x2_ref.shape[0], stride=ROW_TILE), :]
        slabs.append(acc)
    out = jnp.concatenate(slabs, axis=1)
    o_ref[...] = _rms(out, g_ref[...]) if final else out


def _combine(x2, route, y, g, final):
    t, d = x2.shape
    rows = COMBINE_ROWS
    tiles = t // rows
    slot_spec = lambda kk: pl.BlockSpec((rows * ROW_TILE, LANES), lambda i: (kk * tiles + i, 0))
    return pl.pallas_call(
        functools.partial(_combine_kernel, final=final),
        grid=(tiles,),
        in_specs=[pl.BlockSpec((rows, d), lambda i: (i, 0)), pl.BlockSpec((rows, LANES), lambda i: (i, 0)),
                  pl.BlockSpec((1, d), lambda i: (0, 0))] + [slot_spec(kk) for kk in range(TOP_K)],
        out_specs=pl.BlockSpec((rows, d), lambda i: (i, 0)),
        out_shape=jax.ShapeDtypeStruct((t, d), F32),
        compiler_params=_params("arbitrary"),
        name="combine",
    )(x2, route, g.reshape(1, d), *([y] * TOP_K))


def kernel(x, mem, norm_mix_g, w_in, gdn_conv_w, gdn_A_log, gdn_dt_bias, gdn_norm_g, moba_norm_g, w_out,
           norm_xattn_g, norm_mem_g, xattn_w_q, xattn_w_kv, xattn_w_o, norm_ffn_g, router_w, router_b,
           w_gate_up, b_gate_up, w_down, b_down, final_norm_g):
    b, s, d = x.shape
    t = b * s
    mem_len = mem.shape[1]
    mw = moba_norm_g.shape[1]
    gw = GDN_HEADS * GDN_HEAD_DIM
    xcur = x.reshape(t, d)
    for l in range(w_in.shape[0]):
        qt, mk, mv, gqkv, gz, bg = _in_proj(xcur, norm_mix_g[l], w_in[l], gdn_conv_w[l], gdn_A_log[l], gdn_dt_bias[l],
                                            mw, gw, s)
        o_moba = _moba(qt, mk, mv, b, s)
        o_gdn = _gdn(gqkv.reshape(b, s, 3 * gw), gz.reshape(b, s, gw), bg.reshape(b, s, LANES), gdn_norm_g[l])
        kv = _mem_kv(mem.reshape(b * mem_len, d), norm_mem_g[l], xattn_w_kv[l], mem_len)
        x2, h3, route = _mid(xcur, o_moba.reshape(t, mw), o_gdn.reshape(t, gw), moba_norm_g[l], w_out[l],
                             norm_xattn_g[l], xattn_w_q[l], kv, xattn_w_o[l], norm_ffn_g[l], router_w[l],
                             router_b[l], s, mem_len)
        expert = route[:, TOP_K:2 * TOP_K].astype(jnp.int32)
        plan = _route_plan(expert, t)
        y = _moe(h3, *plan, w_gate_up[l], b_gate_up[l], w_down[l], b_down[l])
        xcur = _combine(x2, route, y, final_norm_g, l == w_in.shape[0] - 1)
    return xcur.reshape(b, s, d)
```

```python
import functools

import jax
import jax.numpy as jnp
from jax import lax
from jax.experimental import pallas as pl
from jax.experimental.pallas import tpu as pltpu

F32 = jnp.float32
BF16 = jnp.bfloat16

RMS_EPS = 1e-6
MOBA_HEAD_DIM = 64
MOBA_BLOCK = 256
MOBA_TOPK = 3
GDN_HEAD_DIM = 128
GDN_HEADS = 4
GDN_CONV = 4
GDN_CHUNK = 64
XATTN_HEAD_DIM = 128
N_EXPERTS = 32
TOP_K = 4
SWIGLU_LIMIT = 7.0
SWIGLU_ALPHA = 1.702

LANES = 128
ROW_TILE = 8
VMEM_LIMIT = 56 * 1024 * 1024

IN_ROWS = 512
MID_ROWS = 1024
COMBINE_ROWS = 512
MOE_ROWS = 512
MOE_CHUNK = 128
GDN_GROUP = 32
PREP_ROWS = 256
GDN_HALO = ROW_TILE
SCAN_BATCH = 2
SCAN_ROWS = 1024
NEG_INF = float("-inf")
MOBA_Q_SCALE = 1.4426950408889634 / MOBA_HEAD_DIM ** 0.5
MOBA_MASKED = -1e30


def _params(*sem):
    return pltpu.CompilerParams(dimension_semantics=sem, vmem_limit_bytes=VMEM_LIMIT)


def _rms(x, g):
    return x * lax.rsqrt(jnp.mean(x * x, axis=-1, keepdims=True) + RMS_EPS) * g


def _dot(a, b):
    return jnp.dot(a, b, preferred_element_type=F32)


def _dot_nt(a, b):
    return lax.dot_general(a, b, (((1,), (1,)), ((), ())), preferred_element_type=F32)


def _sigmoid(x):
    return 1.0 / (1.0 + jnp.exp(-x))


def _gdn_gates(x, a_log, dt_bias):
    lane = lax.broadcasted_iota(jnp.int32, (1, LANES), 1)
    xa = x + dt_bias
    softplus = jnp.maximum(xa, 0.0) + jnp.log(1.0 + jnp.exp(-jnp.abs(xa)))
    g = jnp.where((lane >= GDN_HEADS) & (lane < 2 * GDN_HEADS), -jnp.exp(a_log) * softplus, 0.0)
    pos = lax.broadcasted_iota(jnp.int32, (x.shape[0], 1), 0) % GDN_CHUNK
    sft = 1
    while sft < GDN_CHUNK:
        g = g + jnp.where(pos >= sft, pltpu.roll(g, sft, 0), 0.0)
        sft *= 2
    return jnp.where(lane < GDN_HEADS, _sigmoid(x), g)


def _in_proj_kernel(x_ref, g_ref, w_ref, wt_ref, cw_ref, alog_ref, dtb_ref, qt_ref, mk_ref, mv_ref, qkv_ref, gz_ref,
                    bg_ref, *bufs,
                    mw, gw, tiles_per_seq):
    hn = _rms(x_ref[...], g_ref[...]).astype(BF16)
    mm = lambda lo, hi: _dot(hn, w_ref[:, lo:hi])
    rows = x_ref.shape[0]
    halo = GDN_HALO

    @pl.when(pl.program_id(0) % tiles_per_seq == 0)
    def _():
        for buf in bufs:
            buf[:halo, :] = jnp.zeros((halo, LANES), F32)

    @pl.when(pl.program_id(0) % tiles_per_seq != 0)
    def _():
        for buf in bufs:
            buf[:halo, :] = buf[rows:rows + halo, :]

    for grp in range(3):
        raw = mm(2 * mw + grp * gw, 2 * mw + (grp + 1) * gw)
        for h in range(GDN_HEADS):
            bufs[grp * GDN_HEADS + h][halo:, :] = raw[:, h * LANES:(h + 1) * LANES]
    qt_ref[...] = (_dot_nt(wt_ref[...], hn) * MOBA_Q_SCALE).astype(BF16)
    mk_ref[...] = mm(0, mw).astype(BF16)
    mv_ref[...] = mm(mw, 2 * mw).astype(BF16)
    gz_ref[...] = mm(2 * mw + 3 * gw, 2 * mw + 4 * gw)
    bg_ref[...] = _gdn_gates(mm(2 * mw + 4 * gw, 2 * mw + 4 * gw + LANES), alog_ref[...], dtb_ref[...])

    for cb in range(3 * gw // LANES):
        cols = slice(cb * LANES, (cb + 1) * LANES)
        taps = [cw_ref[j:j + 1, cols] for j in range(GDN_CONV)]
        buf = bufs[cb]
        for r0 in range(0, rows, PREP_ROWS):
            y = taps[GDN_CONV - 1] * buf[halo + r0:halo + r0 + PREP_ROWS, :]
            for sft in range(1, GDN_CONV):
                y = y + taps[GDN_CONV - 1 - sft] * buf[halo + r0 - sft:halo + r0 - sft + PREP_ROWS, :]
            y = y * _sigmoid(y)
            if cb < 2 * GDN_HEADS:
                scale = GDN_HEAD_DIM ** -0.5 if cb < GDN_HEADS else 1.0
                y = y * (lax.rsqrt(jnp.sum(y * y, axis=-1, keepdims=True) + RMS_EPS) * scale)
            qkv_ref[r0:r0 + PREP_ROWS, cols] = y.astype(BF16)


def _in_proj(x2d, g, w_in, conv_w, a_log, dt_bias, mw, gw, seq):
    t, d = x2d.shape
    n_real = w_in.shape[1] - mw
    n_pad = 2 * mw + 4 * gw + LANES
    w = jnp.pad(w_in[:, mw:], ((0, 0), (0, n_pad - n_real))).astype(BF16)
    w_t = w_in[:, :mw].T.astype(BF16)
    row = lambda n: pl.BlockSpec((IN_ROWS, n), lambda i: (i, 0))
    const = lambda shape: pl.BlockSpec(shape, lambda i: (0, 0))
    nh = GDN_HEADS
    lane_pad = lambda v: jnp.pad(v.reshape(1, -1), ((0, 0), (nh, LANES - 2 * nh)))
    return pl.pallas_call(
        functools.partial(_in_proj_kernel, mw=mw, gw=gw, tiles_per_seq=seq // IN_ROWS),
        grid=(t // IN_ROWS,),
        in_specs=[row(d), const((1, d)), const((d, n_pad)), const((mw, d)), const((GDN_CONV, 3 * gw)),
                  const((1, LANES)), const((1, LANES))],
        out_specs=[pl.BlockSpec((mw, IN_ROWS), lambda i: (0, i)), row(mw), row(mw), row(3 * gw), row(gw),
                   row(LANES)],
        out_shape=[jax.ShapeDtypeStruct((mw, t), BF16), jax.ShapeDtypeStruct((t, mw), BF16),
                   jax.ShapeDtypeStruct((t, mw), BF16),
                   jax.ShapeDtypeStruct((t, 3 * gw), BF16), jax.ShapeDtypeStruct((t, gw), F32),
                   jax.ShapeDtypeStruct((t, LANES), F32)],
        scratch_shapes=[pltpu.VMEM((GDN_HALO + IN_ROWS, LANES), F32)] * (3 * gw // LANES),
        compiler_params=_params("arbitrary"),
        name="in_proj",
    )(x2d, g.reshape(1, d), w, w_t, conv_w, lane_pad(a_log), lane_pad(dt_bias))


def _moba_select(g_t, i):
    nb = g_t.shape[0]
    row = lax.broadcasted_iota(jnp.int32, g_t.shape, 0)
    valid = row < i
    sel = jnp.zeros_like(g_t)
    for j in range(nb):
        gj = g_t[j:j + 1, :]
        beats = valid & ((g_t > gj) | ((g_t == gj) & (row < j)))
        rank = jnp.sum(jnp.where(beats, 1.0, 0.0), axis=0, keepdims=True)
        sel = jnp.where(row == j, jnp.where(rank < MOBA_TOPK, 1.0, 0.0), sel)
    return jnp.where(valid, sel, 0.0)


def _dot_tn(a, b):
    return lax.dot_general(a, b, (((0,), (0,)), ((), ())), preferred_element_type=F32)


def _moba_kernel(qt_ref, qt_all_ref, k_ref, v_ref, o_ref, kaug_ref, bias_ref, *, nb):
    i = pl.program_id(2)
    bs = MOBA_BLOCK
    hd = MOBA_HEAD_DIM
    lane = lax.broadcasted_iota(jnp.int32, (1, LANES), 1)
    low = lax.broadcasted_iota(jnp.int32, (LANES, 1), 0) < hd

    @pl.when(i == 0)
    def _():
        kmean = []
        for j in range(nb):
            rows = slice(j * bs, (j + 1) * bs)
            kb = k_ref[0, rows, :]
            kmean.append(jnp.mean(kb.astype(F32), axis=0, keepdims=True))
            kaug_ref[0, rows, :] = jnp.where(lane < hd, kb, jnp.where(lane == hd + j, 1.0, 0.0).astype(BF16))
            kaug_ref[1, rows, :] = jnp.where(lane >= hd, kb, jnp.where(lane == j, 1.0, 0.0).astype(BF16))
        kmean = jnp.concatenate(kmean, axis=0)
        q_all = qt_all_ref[...].astype(F32)
        for h, qh in enumerate((jnp.where(low, q_all, 0.0), jnp.where(low, 0.0, q_all))):
            gate = jnp.dot(kmean, qh, preferred_element_type=F32, precision=lax.Precision.HIGHEST)
            for jq in range(nb):
                sel = _moba_select(gate[:, jq * bs:(jq + 1) * bs], jq)
                bias_ref[h, jq] = jnp.where(sel > 0.5, 0.0, MOBA_MASKED)

    qt = qt_ref[...]
    qtf = qt.astype(F32)
    bias = [bias_ref[h, i] for h in (0, 1)]
    pad = jnp.zeros((hd - nb, bs), F32)
    q_past = (jnp.concatenate([qtf[:hd], bias[0], pad], axis=0).astype(BF16),
              jnp.concatenate([bias[1], pad, qtf[hd:]], axis=0).astype(BF16))
    zero = jnp.zeros_like(qt)
    q_own = (jnp.where(low, qt, zero), jnp.where(low, zero, qt))

    key_ix = lax.broadcasted_iota(jnp.int32, (bs, bs), 0)
    qry_ix = lax.broadcasted_iota(jnp.int32, (bs, bs), 1)
    causal_bias = jnp.where(key_ix <= qry_ix, 0.0, NEG_INF)
    own = pl.ds(pl.multiple_of(i * bs, bs), bs)
    k_own = k_ref[0, own, :]
    v_own = v_ref[0, own, :]

    def attend(width):
        heads = (0, 1)
        parts = [slice(lo * bs, (lo + 1) * bs) for lo in range(width)]
        n = len(parts) + 1
        past_scores = lambda k: [_dot(kaug_ref[h, parts[k - 1], :], q_past[h]) for h in heads]
        scores = {0: [_dot(k_own, q_own[h]) + causal_bias for h in heads]}
        if n > 1:
            scores[1] = past_scores(1)
        m = l = acc = None
        for k in range(n):
            s_k = scores.pop(k)
            m_new = [s.max(axis=0, keepdims=True) for s in s_k]
            if m is not None:
                m_new = [jnp.maximum(m[h], m_new[h]) for h in heads]
            p = [jnp.exp2(s_k[h] - m_new[h]) for h in heads]
            p_sum = [x.sum(axis=0, keepdims=True) for x in p]
            if k + 2 < n:
                scores[k + 2] = past_scores(k + 2)
            values = v_own if k == 0 else v_ref[0, parts[k - 1], :]
            pv = [_dot_tn(values, p[h].astype(BF16)) for h in heads]
            if m is None:
                l, acc = p_sum, pv
            else:
                alpha = [jnp.exp2(m[h] - m_new[h]) for h in heads]
                l = [alpha[h] * l[h] + p_sum[h] for h in heads]
                acc = [alpha[h] * acc[h] + pv[h] for h in heads]
            m = m_new
        o_ref[0] = jnp.where(low, acc[0] / l[0], acc[1] / l[1]).T

    for width in range(nb):
        pl.when(i == width)(functools.partial(attend, width))


def _moba(qt, mk, mv, b, s):
    mw = mk.shape[-1]
    nb = s // MOBA_BLOCK
    seq_blk = pl.BlockSpec((1, s, LANES), lambda bi, hp, i: (bi, 0, hp))
    return pl.pallas_call(
        functools.partial(_moba_kernel, nb=nb),
        grid=(b, mw // LANES, nb),
        in_specs=[pl.BlockSpec((LANES, MOBA_BLOCK), lambda bi, hp, i: (hp, bi * nb + i)),
                  pl.BlockSpec((LANES, s), lambda bi, hp, i: (hp, bi)), seq_blk, seq_blk],
        out_specs=pl.BlockSpec((1, MOBA_BLOCK, LANES), lambda bi, hp, i: (bi, i, hp)),
        out_shape=jax.ShapeDtypeStruct((b, s, mw), F32),
        scratch_shapes=[pltpu.VMEM((2, s, LANES), BF16), pltpu.VMEM((2, nb, nb, MOBA_BLOCK), F32)],
        compiler_params=_params("arbitrary", "arbitrary", "arbitrary"),
        name="moba",
    )(qt, qt, mk.reshape(b, s, mw), mv.reshape(b, s, mw))


def _unit_lower_inverses(mats):
    c = mats[0].shape[0]
    r = lax.broadcasted_iota(jnp.int32, (c, c), 0)
    cc = lax.broadcasted_iota(jnp.int32, (c, c), 1)
    eye = jnp.where(r == cc, 1.0, 0.0)
    pair = (r // 2) == (cc // 2)
    invs = [eye - jnp.where(pair, a, 0.0) for a in mats]
    size = 4
    while size <= c:
        level = ((r // size) == (cc // size)) & ((r // (size // 2)) != (cc // (size // 2)))
        inv_b = [inv.astype(BF16) for inv in invs]
        left = [_dot(ib, jnp.where(level, a, 0.0).astype(BF16)).astype(BF16) for ib, a in zip(inv_b, mats)]
        invs = [inv - _dot(lf, ib) for inv, lf, ib in zip(invs, left, inv_b)]
        size *= 2
    return invs


def _gdn_intra_kernel(q_ref, k_ref, v_ref, bg_ref, u_ref, w_ref, qd_ref, kd_ref, qk_ref):
    h = pl.program_id(1)
    c = GDN_CHUNK
    chunks = range(GDN_GROUP)
    lane = lax.broadcasted_iota(jnp.int32, (1, LANES), 1)
    r_ix = lax.broadcasted_iota(jnp.int32, (c, c), 0)
    c_ix = lax.broadcasted_iota(jnp.int32, (c, c), 1)
    rows = [slice(gi * c, (gi + 1) * c) for gi in chunks]
    k_b16 = [k_ref[0, rw, :] for rw in rows]
    q_b16 = [q_ref[0, rw, :] for rw in rows]
    bgs = [bg_ref[0, rw, :] for rw in rows]
    beta = [jnp.sum(jnp.where(lane == h, bg, 0.0), axis=1, keepdims=True) for bg in bgs]
    gam = [jnp.sum(jnp.where(lane == GDN_HEADS + h, bg, 0.0), axis=1, keepdims=True) for bg in bgs]
    kb = [kk.astype(F32) * bt for kk, bt in zip(k_b16, beta)]
    kk_raw = [_dot_nt(x.astype(BF16), kk) for x, kk in zip(kb, k_b16)]
    qk_raw = [_dot_nt(qq, kk) for qq, kk in zip(q_b16, k_b16)]
    decay = []
    for gm in gam:
        gam_r = jnp.sum(jnp.where(r_ix == c_ix, gm, 0.0), axis=0, keepdims=True)
        decay.append(jnp.exp(jnp.where(c_ix <= r_ix, gm - gam_r, NEG_INF)))
    t_inv = _unit_lower_inverses([jnp.where(c_ix < r_ix, x * dc, 0.0) for x, dc in zip(kk_raw, decay)])
    eg = [jnp.exp(gm) for gm in gam]
    rhs = [jnp.concatenate([v_ref[0, rw, :].astype(F32) * bt, x * e], axis=1).astype(BF16)
           for rw, bt, x, e in zip(rows, beta, kb, eg)]
    uw = [_dot(ti.astype(BF16), rh) for ti, rh in zip(t_inv, rhs)]
    for gi in chunks:
        rw = rows[gi]
        u_ref[0, rw, :] = uw[gi][:, :LANES].astype(BF16)
        w_ref[0, rw, :] = uw[gi][:, LANES:].astype(BF16)
        qk_ref[0, 0, rw, :] = (qk_raw[gi] * decay[gi]).astype(BF16)
        qd_ref[0, rw, :] = (q_b16[gi].astype(F32) * eg[gi]).astype(BF16)
        kd_ref[0, rw, :] = (k_b16[gi].astype(F32) * jnp.exp(gam[gi][c - 1:c, :] - gam[gi])).astype(BF16)


def _gdn_scan_kernel(u_ref, w_ref, qd_ref, kd_ref, qk_ref, z_ref, bg_ref, ng_ref, o_ref, state_ref, *, rows_per_step):
    c = GDN_CHUNK
    ng = ng_ref[...]
    chains = [(bi, h) for bi in range(SCAN_BATCH) for h in range(GDN_HEADS)]
    cols = [slice(h * GDN_HEAD_DIM, (h + 1) * GDN_HEAD_DIM) for h in range(GDN_HEADS)]

    @pl.when(pl.program_id(1) == 0)
    def _():
        state_ref[...] = jnp.zeros(state_ref.shape, F32)

    def step(n, states):
        r0 = pl.multiple_of(n * c, c)
        rows = pl.ds(r0, c)
        tails = [bg_ref[bi, pl.ds(r0 + c - 8, 8), :] for bi in range(SCAN_BATCH)]
        s_b = [st.astype(BF16) for st in states]
        ws = [_dot(w_ref[bi, rows, cols[h]], s_b[k]) for k, (bi, h) in enumerate(chains)]
        qs = [_dot(qd_ref[bi, rows, cols[h]], s_b[k]) for k, (bi, h) in enumerate(chains)]
        v_b = [(u_ref[bi, rows, cols[h]].astype(F32) - ws[k]).astype(BF16) for k, (bi, h) in enumerate(chains)]
        kd_v = [_dot_tn(kd_ref[bi, rows, cols[h]], v_b[k]) for k, (bi, h) in enumerate(chains)]
        qkv = [_dot(qk_ref[bi, h, rows, :], v_b[k]) for k, (bi, h) in enumerate(chains)]
        new_states = []
        for k, (bi, h) in enumerate(chains):
            g_last = tails[bi][7:8, GDN_HEADS + h:GDN_HEADS + h + 1]
            new_states.append(states[k] * jnp.exp(g_last) + kd_v[k])
            z = z_ref[bi, rows, cols[h]]
            o_ref[bi, rows, cols[h]] = _rms(qs[k] + qkv[k], ng) * (z * _sigmoid(z))
        return tuple(new_states)

    init = tuple(state_ref[k] for k in range(len(chains)))
    final = lax.fori_loop(0, rows_per_step // c, step, init)
    for k, st in enumerate(final):
        state_ref[k] = st


def _gdn(qkv, gz, bg, norm_g):
    b, s, w3 = qkv.shape
    nh = GDN_HEADS
    hw = nh * GDN_HEAD_DIM
    c = GDN_CHUNK
    grp = GDN_GROUP * c
    head_blk = lambda off: pl.BlockSpec((1, grp, LANES), lambda bi, h, n: (bi, n, off + h))
    head_shape = jax.ShapeDtypeStruct((b, s, hw), BF16)
    u, w, qd, kd, qk = pl.pallas_call(
        _gdn_intra_kernel,
        grid=(b, nh, s // grp),
        in_specs=[head_blk(0), head_blk(nh), head_blk(2 * nh),
                  pl.BlockSpec((1, grp, LANES), lambda bi, h, n: (bi, n, 0))],
        out_specs=[head_blk(0)] * 4 + [pl.BlockSpec((1, 1, grp, c), lambda bi, h, n: (bi, h, n, 0))],
        out_shape=[head_shape] * 4 + [jax.ShapeDtypeStruct((b, nh, s, c), BF16)],
        compiler_params=_params("arbitrary", "arbitrary", "arbitrary"),
        name="gdn_intra",
    )(qkv, qkv, qkv, bg)

    tile = lambda width: pl.BlockSpec((SCAN_BATCH, SCAN_ROWS, width), lambda bi, ti: (bi, ti, 0))
    return pl.pallas_call(
        functools.partial(_gdn_scan_kernel, rows_per_step=SCAN_ROWS),
        grid=(b // SCAN_BATCH, s // SCAN_ROWS),
        in_specs=[tile(hw), tile(hw), tile(hw), tile(hw),
                  pl.BlockSpec((SCAN_BATCH, nh, SCAN_ROWS, c), lambda bi, ti: (bi, 0, ti, 0)), tile(hw), tile(LANES),
                  pl.BlockSpec((1, LANES), lambda bi, ti: (0, 0))],
        out_specs=tile(hw),
        out_shape=jax.ShapeDtypeStruct((b, s, hw), F32),
        scratch_shapes=[pltpu.VMEM((SCAN_BATCH * nh, GDN_HEAD_DIM, GDN_HEAD_DIM), F32)],
        compiler_params=_params("arbitrary", "arbitrary"),
        name="gdn_scan",
    )(u, w, qd, kd, qk, gz, bg, norm_g.reshape(1, -1))


def _mem_kv_kernel(m_ref, g_ref, w_ref, kv_ref):
    kv_ref[...] = _dot(_rms(m_ref[...], g_ref[...]).astype(BF16), w_ref[...]).astype(BF16)


def _mem_kv(mem2d, g, w_kv, rows):
    t, d = mem2d.shape
    n = w_kv.shape[1]
    return pl.pallas_call(
        _mem_kv_kernel,
        grid=(t // rows,),
        in_specs=[pl.BlockSpec((rows, d), lambda i: (i, 0)), pl.BlockSpec((1, d), lambda i: (0, 0)),
                  pl.BlockSpec((d, n), lambda i: (0, 0))],
        out_specs=pl.BlockSpec((rows, n), lambda i: (i, 0)),
        out_shape=jax.ShapeDtypeStruct((t, n), BF16),
        compiler_params=_params("arbitrary"),
        name="mem_kv",
    )(mem2d, g.reshape(1, d), w_kv.astype(BF16))


def _mid_kernel(x_ref, om_ref, og_ref, mg_ref, wout_ref, xg_ref, wq_ref, kv_ref, wo_ref, fg_ref, rw_ref, rb_ref,
                x2_ref, h3_ref, route_ref, *, mw, xw):
    mo = _rms(om_ref[...], mg_ref[...]).astype(BF16)
    x1 = x_ref[...] + _dot(mo, wout_ref[:mw, :]) + _dot(og_ref[...].astype(BF16), wout_ref[mw:, :])

    h2 = _rms(x1, xg_ref[...]).astype(BF16)
    q = (_dot(h2, wq_ref[...]) * (XATTN_HEAD_DIM ** -0.5)).astype(BF16)
    head_cols = [slice(h * XATTN_HEAD_DIM, (h + 1) * XATTN_HEAD_DIM) for h in range(xw // XATTN_HEAD_DIM)]
    scores = [_dot_nt(q[:, sl], kv_ref[:, sl]) for sl in head_cols]
    probs = [jnp.exp(s - jnp.max(s, axis=1, keepdims=True)) for s in scores]
    heads = [_dot(p.astype(BF16), kv_ref[:, xw + sl.start:xw + sl.stop]) / jnp.sum(p, axis=1, keepdims=True)
             for p, sl in zip(probs, head_cols)]
    x2 = x1 + _dot(jnp.concatenate(heads, axis=1).astype(BF16), wo_ref[...])
    x2_ref[...] = x2

    h3 = _rms(x2, fg_ref[...])
    slabs = h3.shape[1] // LANES
    for j in range(slabs):
        h3_ref[pl.ds(j, h3.shape[0], stride=slabs), :] = h3[:, j * LANES:(j + 1) * LANES]
    ne = rb_ref.shape[0]
    h_hi = h3.astype(BF16)
    h_lo = (h3 - h_hi.astype(F32)).astype(BF16)
    by_hi = _dot_nt(rw_ref[...], h_hi)
    logits = by_hi[:ne, :] + by_hi[ne:, :] + _dot_nt(rw_ref[:ne, :], h_lo) + rb_ref[...]
    row = lax.broadcasted_iota(jnp.int32, logits.shape, 0)
    weights, picks, top = [], [], None
    for kk in range(TOP_K):
        m = jnp.max(logits, axis=0, keepdims=True)
        idx = jnp.min(jnp.where(logits == m, row, ne), axis=0, keepdims=True)
        logits = jnp.where(row == idx, NEG_INF, logits)
        top = m if top is None else top
        weights.append(jnp.exp(m - top))
        picks.append(idx.astype(F32))
    denom = sum(weights[1:], weights[0])
    rows = [wk / denom for wk in weights] + picks
    rows.append(jnp.zeros((LANES - len(rows), logits.shape[1]), F32))
    route_ref[...] = jnp.concatenate(rows, axis=0).T


def _mid(x2d, om, og, moba_g, w_out, xattn_g, w_q, kv, w_o, ffn_g, router_w, router_b, seq, mem_len):
    t, d = x2d.shape
    mw, gw, xw, ne = om.shape[1], og.shape[1], w_q.shape[1], router_w.shape[1]
    tiles_per_seq = seq // MID_ROWS
    row = lambda n: pl.BlockSpec((MID_ROWS, n), lambda i: (i, 0))
    const = lambda r, c: pl.BlockSpec((r, c), lambda i: (0, 0))
    rw_hi = router_w.T.astype(BF16)
    rw_lo = (router_w.T - rw_hi.astype(F32)).astype(BF16)
    return pl.pallas_call(
        functools.partial(_mid_kernel, mw=mw, xw=xw),
        grid=(t // MID_ROWS,),
        in_specs=[row(d), row(mw), row(gw), const(1, mw), const(mw + gw, d), const(1, d), const(d, xw),
                  pl.BlockSpec((mem_len, 2 * xw), lambda i: (i // tiles_per_seq, 0)),
                  const(xw, d), const(1, d), const(2 * ne, d), const(ne, 1)],
        out_specs=[row(d), pl.BlockSpec((MID_ROWS * (d // LANES), LANES), lambda i: (i, 0)), row(LANES)],
        out_shape=[jax.ShapeDtypeStruct((t, d), F32), jax.ShapeDtypeStruct((t * (d // LANES), LANES), F32),
                   jax.ShapeDtypeStruct((t, LANES), F32)],
        compiler_params=_params("arbitrary"),
        name="mid",
    )(x2d, om, og, moba_g.reshape(1, mw), w_out.astype(BF16), xattn_g.reshape(1, d), w_q.astype(BF16), kv,
      w_o.astype(BF16), ffn_g.reshape(1, d), jnp.concatenate([rw_hi, rw_lo], axis=0), router_b.reshape(ne, 1))


def _moe_kernel(be_ref, bv_ref, tok_ref, tokn_ref, dst_ref, h_hbm, wgu_ref, bgu_ref, wd_ref, bd_ref, y_hbm,
                xbuf, ybuf, wgu_b16, wd_b16, sem_in, sem_out, *, dff):
    blk = pl.program_id(0)
    nv = bv_ref[blk]
    nv_next = bv_ref[blk + 1]
    slot = blk % 2
    rows = MOE_ROWS
    chunk = MOE_CHUNK
    n_chunks = rows // chunk
    rt = ROW_TILE

    def row_in(s, r, t):
        return pltpu.make_async_copy(h_hbm.at[pl.ds(pl.multiple_of(t, rt), rt), :],
                                     xbuf.at[s, pl.ds(r * rt, rt), :], sem_in.at[s])

    def row_out(s, r, d):
        return pltpu.make_async_copy(ybuf.at[s, pl.ds(r * rt, rt), :],
                                     y_hbm.at[pl.ds(pl.multiple_of(d, rt), rt), :], sem_out.at[s])

    def wait_gather(s):
        pltpu.make_async_copy(h_hbm.at[pl.ds(0, rows * rt), :], xbuf.at[s], sem_in.at[s]).wait()

    def wait_scatter(s):
        pltpu.make_async_copy(ybuf.at[s], y_hbm.at[pl.ds(0, rows * rt), :], sem_out.at[s]).wait()

    @pl.when(blk == 0)
    def _():
        ybuf[1] = jnp.zeros(ybuf.shape[1:], F32)
        spare = pltpu.make_async_copy(ybuf.at[1], y_hbm.at[pl.ds(y_hbm.shape[0] - rows * rt, rows * rt), :],
                                      sem_out.at[1])
        spare.start()
        spare.wait()

    @pl.when((blk == 0) & (nv > 0))
    def _():
        for r in range(rows):
            row_in(0, r, tok_ref[0, 0, r]).start(priority=r % 2)

    @pl.when(nv > 0)
    def _():
        @pl.when((blk == 0) | (be_ref[blk] != be_ref[jnp.maximum(blk - 1, 0)]))
        def _():
            wgu_b16[...] = wgu_ref[0].astype(BF16)
            wd_b16[...] = wd_ref[0].astype(BF16)

        wait_gather(slot)

        @pl.when(blk >= 2)
        def _():
            wait_scatter(slot)

        slab = lambda c, j: pl.ds(c * chunk * rt + j, chunk, stride=rt)
        load_x = lambda c: jnp.concatenate([xbuf[slot, slab(c, j), :] for j in range(rt)], axis=1).astype(BF16)
        in_bounds = [rows * k // (n_chunks - 1) for k in range(n_chunks)]
        x_next = load_x(0)
        for c in range(n_chunks):
            x = x_next
            if c < n_chunks - 1:
                for r in range(in_bounds[c], in_bounds[c + 1]):
                    row_in(1 - slot, r, tokn_ref[0, 0, r]).start(priority=r % 2)
            if c:
                for r in range((c - 1) * chunk, c * chunk):
                    row_out(slot, r, dst_ref[0, 0, r]).start(priority=r % 2)
            gu = _dot(x, wgu_b16[...]) + bgu_ref[0]
            gate = jnp.minimum(gu[:, :dff], SWIGLU_LIMIT)
            up = jnp.clip(gu[:, dff:], -SWIGLU_LIMIT, SWIGLU_LIMIT)
            act = (up + 1.0) * gate * _sigmoid(SWIGLU_ALPHA * gate)
            y = _dot(act.astype(BF16), wd_b16[...]) + bd_ref[0]
            if c + 1 < n_chunks:
                x_next = load_x(c + 1)
            for j in range(rt):
                ybuf[slot, slab(c, j), :] = y[:, j * LANES:(j + 1) * LANES]
        for r in range(rows - chunk, rows):
            row_out(slot, r, dst_ref[0, 0, r]).start(priority=r % 2)

        @pl.when(nv_next == 0)
        def _():
            wait_gather(1 - slot)

            @pl.when(blk >= 1)
            def _():
                wait_scatter(1 - slot)
            wait_scatter(slot)


def _moe(h3_tiles, block_expert, block_valid, row_tok, row_dst, w_gu, b_gu, w_d, b_d):
    ne, d, n2 = w_gu.shape
    assert d == ROW_TILE * LANES
    t = h3_tiles.shape[0] // ROW_TILE
    dff = n2 // 2
    nblk = block_expert.shape[0]
    idx_blk = pl.BlockSpec((1, 1, MOE_ROWS), lambda i, be, bv: (i, 0, 0), memory_space=pltpu.SMEM)
    idx_next = pl.BlockSpec((1, 1, MOE_ROWS), lambda i, be, bv: (jnp.minimum(i + 1, nblk - 1), 0, 0),
                            memory_space=pltpu.SMEM)
    grid_spec = pltpu.PrefetchScalarGridSpec(
        num_scalar_prefetch=2,
        grid=(nblk,),
        in_specs=[idx_blk, idx_next, idx_blk, pl.BlockSpec(memory_space=pl.ANY),
                  pl.BlockSpec((1, d, n2), lambda i, be, bv: (be[i], 0, 0)),
                  pl.BlockSpec((1, 1, n2), lambda i, be, bv: (be[i], 0, 0)),
                  pl.BlockSpec((1, dff, d), lambda i, be, bv: (be[i], 0, 0)),
                  pl.BlockSpec((1, 1, d), lambda i, be, bv: (be[i], 0, 0))],
        out_specs=pl.BlockSpec(memory_space=pl.ANY),
        scratch_shapes=[pltpu.VMEM((2, MOE_ROWS * ROW_TILE, LANES), F32),
                        pltpu.VMEM((2, MOE_ROWS * ROW_TILE, LANES), F32),
                        pltpu.VMEM((d, n2), BF16), pltpu.VMEM((dff, d), BF16),
                        pltpu.SemaphoreType.DMA((2,)), pltpu.SemaphoreType.DMA((2,))],
    )
    tok3 = (row_tok * ROW_TILE).reshape(nblk, 1, MOE_ROWS)
    valid_ext = jnp.concatenate([block_valid, jnp.zeros((1,), jnp.int32)])
    return pl.pallas_call(
        functools.partial(_moe_kernel, dff=dff),
        grid_spec=grid_spec,
        out_shape=jax.ShapeDtypeStruct(((TOP_K * t + MOE_ROWS) * ROW_TILE, LANES), F32),
        compiler_params=_params("arbitrary"),
        name="moe",
    )(block_expert, valid_ext, tok3, tok3, (row_dst * ROW_TILE).reshape(nblk, 1, MOE_ROWS), h3_tiles,
      w_gu, b_gu.reshape(ne, 1, n2), w_d, b_d.reshape(ne, 1, d))


def _route_plan(expert, t):
    n_pairs = t * TOP_K
    e_flat = expert.reshape(-1)
    order = jnp.argsort(e_flat, stable=True).astype(jnp.int32)
    experts = jnp.arange(N_EXPERTS, dtype=jnp.int32)
    counts = jnp.sum((e_flat[:, None] == experts[None, :]).astype(jnp.int32), axis=0)
    padded = (counts + MOE_ROWS - 1) // MOE_ROWS * MOE_ROWS
    pend = jnp.cumsum(padded)
    pstart = pend - padded
    gstart = jnp.cumsum(counts) - counts
    nblk = n_pairs // MOE_ROWS + N_EXPERTS
    blk_row0 = jnp.arange(nblk, dtype=jnp.int32) * MOE_ROWS
    block_expert = jnp.minimum(jnp.sum((pend[None, :] <= blk_row0[:, None]).astype(jnp.int32), axis=1), N_EXPERTS - 1)
    pick = lambda table: jnp.sum(jnp.where(block_expert[:, None] == experts[None, :], table[None, :], 0), axis=1)
    in_group = blk_row0 - pick(pstart)
    block_valid = jnp.where(blk_row0 < pend[-1], jnp.clip(pick(counts) - in_group, 0, MOE_ROWS), 0)
    first_sorted = pick(gstart) + in_group
    local = jnp.arange(MOE_ROWS, dtype=jnp.int32)
    srt = first_sorted[:, None] + local[None, :]
    pair = order[jnp.clip(srt, 0, n_pairs - 1)]
    row_tok = pair // TOP_K
    row_dst = jnp.where(local[None, :] < block_valid[:, None], (pair % TOP_K) * t + row_tok, n_pairs + local[None, :])
    return (block_expert.astype(jnp.int32), block_valid.astype(jnp.int32), row_tok.astype(jnp.int32),
            row_dst.astype(jnp.int32))


def _combine_kernel(x2_ref, route_ref, g_ref, *rest, final):
    y_refs, o_ref = rest[:TOP_K], rest[TOP_K]
    route = route_ref[...]
    gates = [route[:, kk:kk + 1] for kk in range(TOP_K)]
    slabs = []
    for j in range(ROW_TILE):
        acc = x2_ref[:, j * LANES:(j + 1) * LANES]
        for kk in range(TOP_K):
            acc = acc + gates[kk] * y_refs[kk][pl.ds(j, x2_ref.shape[0], stride=ROW_TILE), :]
        slabs.append(acc)
    out = jnp.concatenate(slabs, axis=1)
    o_ref[...] = _rms(out, g_ref[...]) if final else out


def _combine(x2, route, y, g, final):
    t, d = x2.shape
    rows = COMBINE_ROWS
    tiles = t // rows
    slot_spec = lambda kk: pl.BlockSpec((rows * ROW_TILE, LANES), lambda i: (kk * tiles + i, 0))
    return pl.pallas_call(
        functools.partial(_combine_kernel, final=final),
        grid=(tiles,),
        in_specs=[pl.BlockSpec((rows, d), lambda i: (i, 0)), pl.BlockSpec((rows, LANES), lambda i: (i, 0)),
                  pl.BlockSpec((1, d), lambda i: (0, 0))] + [slot_spec(kk) for kk in range(TOP_K)],
        out_specs=pl.BlockSpec((rows, d), lambda i: (i, 0)),
        out_shape=jax.ShapeDtypeStruct((t, d), F32),
        compiler_params=_params("arbitrary"),
        name="combine",
    )(x2, route, g.reshape(1, d), *([y] * TOP_K))


def kernel(x, mem, norm_mix_g, w_in, gdn_conv_w, gdn_A_log, gdn_dt_bias, gdn_norm_g, moba_norm_g, w_out,
           norm_xattn_g, norm_mem_g, xattn_w_q, xattn_w_kv, xattn_w_o, norm_ffn_g, router_w, router_b,
           w_gate_up, b_gate_up, w_down, b_down, final_norm_g):
    b, s, d = x.shape
    t = b * s
    mem_len = mem.shape[1]
    mw = moba_norm_g.shape[1]
    gw = GDN_HEADS * GDN_HEAD_DIM
    xcur = x.reshape(t, d)
    for l in range(w_in.shape[0]):
        qt, mk, mv, gqkv, gz, bg = _in_proj(xcur, norm_mix_g[l], w_in[l], gdn_conv_w[l], gdn_A_log[l], gdn_dt_bias[l],
                                            mw, gw, s)
        o_moba = _moba(qt, mk, mv, b, s)
        o_gdn = _gdn(gqkv.reshape(b, s, 3 * gw), gz.reshape(b, s, gw), bg.reshape(b, s, LANES), gdn_norm_g[l])
        kv = _mem_kv(mem.reshape(b * mem_len, d), norm_mem_g[l], xattn_w_kv[l], mem_len)
        x2, h3, route = _mid(xcur, o_moba.reshape(t, mw), o_gdn.reshape(t, gw), moba_norm_g[l], w_out[l],
                             norm_xattn_g[l], xattn_w_q[l], kv, xattn_w_o[l], norm_ffn_g[l], router_w[l],
                             router_b[l], s, mem_len)
        expert = route[:, TOP_K:2 * TOP_K].astype(jnp.int32)
        plan = _route_plan(expert, t)
        y = _moe(h3, *plan, w_gate_up[l], b_gate_up[l], w_down[l], b_down[l])
        xcur = _combine(x2, route, y, final_norm_g, l == w_in.shape[0] - 1)
    return xcur.reshape(b, s, d)
```

```python
import functools

import jax
import jax.numpy as jnp
from jax import lax
from jax.experimental import pallas as pl
from jax.experimental.pallas import tpu as pltpu

F32 = jnp.float32
BF16 = jnp.bfloat16

RMS_EPS = 1e-6
MOBA_HEAD_DIM = 64
MOBA_BLOCK = 256
MOBA_TOPK = 3
GDN_HEAD_DIM = 128
GDN_HEADS = 4
GDN_CONV = 4
GDN_CHUNK = 64
XATTN_HEAD_DIM = 128
N_EXPERTS = 32
TOP_K = 4
SWIGLU_LIMIT = 7.0
SWIGLU_ALPHA = 1.702

LANES = 128
ROW_TILE = 8
VMEM_LIMIT = 56 * 1024 * 1024

IN_ROWS = 512
MID_ROWS = 1024
COMBINE_ROWS = 512
MOE_ROWS = 512
MOE_CHUNK = 128
GDN_GROUP = 32
PREP_ROWS = 256
GDN_HALO = ROW_TILE
SCAN_BATCH = 2
SCAN_ROWS = 1024
NEG_INF = float("-inf")
MOBA_Q_SCALE = 1.4426950408889634 / MOBA_HEAD_DIM ** 0.5
MOBA_MASKED = -1e30


def _params(*sem):
    return pltpu.CompilerParams(dimension_semantics=sem, vmem_limit_bytes=VMEM_LIMIT)


def _rms(x, g):
    return x * lax.rsqrt(jnp.mean(x * x, axis=-1, keepdims=True) + RMS_EPS) * g


def _dot(a, b):
    return jnp.dot(a, b, preferred_element_type=F32)


def _dot_nt(a, b):
    return lax.dot_general(a, b, (((1,), (1,)), ((), ())), preferred_element_type=F32)


def _sigmoid(x):
    return 1.0 / (1.0 + jnp.exp(-x))


def _gdn_gates(x, a_log, dt_bias):
    lane = lax.broadcasted_iota(jnp.int32, (1, LANES), 1)
    xa = x + dt_bias
    softplus = jnp.maximum(xa, 0.0) + jnp.log(1.0 + jnp.exp(-jnp.abs(xa)))
    g = jnp.where((lane >= GDN_HEADS) & (lane < 2 * GDN_HEADS), -jnp.exp(a_log) * softplus, 0.0)
    pos = lax.broadcasted_iota(jnp.int32, (x.shape[0], 1), 0) % GDN_CHUNK
    sft = 1
    while sft < GDN_CHUNK:
        g = g + jnp.where(pos >= sft, pltpu.roll(g, sft, 0), 0.0)
        sft *= 2
    return jnp.where(lane < GDN_HEADS, _sigmoid(x), g)


def _in_proj_kernel(x_ref, g_ref, w_ref, wt_ref, cw_ref, alog_ref, dtb_ref, qt_ref, mk_ref, mv_ref, qkv_ref, gz_ref,
                    bg_ref, *bufs,
                    mw, gw, tiles_per_seq):
    hn = _rms(x_ref[...], g_ref[...]).astype(BF16)
    mm = lambda lo, hi: _dot(hn, w_ref[:, lo:hi])
    rows = x_ref.shape[0]
    halo = GDN_HALO

    @pl.when(pl.program_id(0) % tiles_per_seq == 0)
    def _():
        for buf in bufs:
            buf[:halo, :] = jnp.zeros((halo, LANES), F32)

    @pl.when(pl.program_id(0) % tiles_per_seq != 0)
    def _():
        for buf in bufs:
            buf[:halo, :] = buf[rows:rows + halo, :]

    for grp in range(3):
        raw = mm(2 * mw + grp * gw, 2 * mw + (grp + 1) * gw)
        for h in range(GDN_HEADS):
            bufs[grp * GDN_HEADS + h][halo:, :] = raw[:, h * LANES:(h + 1) * LANES]
    qt_ref[...] = (_dot_nt(wt_ref[...], hn) * MOBA_Q_SCALE).astype(BF16)
    mk_ref[...] = mm(0, mw).astype(BF16)
    mv_ref[...] = mm(mw, 2 * mw).astype(BF16)
    gz_ref[...] = mm(2 * mw + 3 * gw, 2 * mw + 4 * gw)
    bg_ref[...] = _gdn_gates(mm(2 * mw + 4 * gw, 2 * mw + 4 * gw + LANES), alog_ref[...], dtb_ref[...])

    for cb in range(3 * gw // LANES):
        cols = slice(cb * LANES, (cb + 1) * LANES)
        taps = [cw_ref[j:j + 1, cols] for j in range(GDN_CONV)]
        buf = bufs[cb]
        for r0 in range(0, rows, PREP_ROWS):
            y = taps[GDN_CONV - 1] * buf[halo + r0:halo + r0 + PREP_ROWS, :]
            for sft in range(1, GDN_CONV):
                y = y + taps[GDN_CONV - 1 - sft] * buf[halo + r0 - sft:halo + r0 - sft + PREP_ROWS, :]
            y = y * _sigmoid(y)
            if cb < 2 * GDN_HEADS:
                scale = GDN_HEAD_DIM ** -0.5 if cb < GDN_HEADS else 1.0
                y = y * (lax.rsqrt(jnp.sum(y * y, axis=-1, keepdims=True) + RMS_EPS) * scale)
            qkv_ref[r0:r0 + PREP_ROWS, cols] = y.astype(BF16)


def _in_proj(x2d, g, w_in, conv_w, a_log, dt_bias, mw, gw, seq):
    t, d = x2d.shape
    n_real = w_in.shape[1] - mw
    n_pad = 2 * mw + 4 * gw + LANES
    w = jnp.pad(w_in[:, mw:], ((0, 0), (0, n_pad - n_real))).astype(BF16)
    w_t = w_in[:, :mw].T.astype(BF16)
    row = lambda n: pl.BlockSpec((IN_ROWS, n), lambda i: (i, 0))
    const = lambda shape: pl.BlockSpec(shape, lambda i: (0, 0))
    nh = GDN_HEADS
    lane_pad = lambda v: jnp.pad(v.reshape(1, -1), ((0, 0), (nh, LANES - 2 * nh)))
    return pl.pallas_call(
        functools.partial(_in_proj_kernel, mw=mw, gw=gw, tiles_per_seq=seq // IN_ROWS),
        grid=(t // IN_ROWS,),
        in_specs=[row(d), const((1, d)), const((d, n_pad)), const((mw, d)), const((GDN_CONV, 3 * gw)),
                  const((1, LANES)), const((1, LANES))],
        out_specs=[pl.BlockSpec((mw, IN_ROWS), lambda i: (0, i)), row(mw), row(mw), row(3 * gw), row(gw),
                   row(LANES)],
        out_shape=[jax.ShapeDtypeStruct((mw, t), BF16), jax.ShapeDtypeStruct((t, mw), BF16),
                   jax.ShapeDtypeStruct((t, mw), BF16),
                   jax.ShapeDtypeStruct((t, 3 * gw), BF16), jax.ShapeDtypeStruct((t, gw), F32),
                   jax.ShapeDtypeStruct((t, LANES), F32)],
        scratch_shapes=[pltpu.VMEM((GDN_HALO + IN_ROWS, LANES), F32)] * (3 * gw // LANES),
        compiler_params=_params("arbitrary"),
        name="in_proj",
    )(x2d, g.reshape(1, d), w, w_t, conv_w, lane_pad(a_log), lane_pad(dt_bias))


def _moba_select(g_t, i):
    nb = g_t.shape[0]
    row = lax.broadcasted_iota(jnp.int32, g_t.shape, 0)
    valid = row < i
    sel = jnp.zeros_like(g_t)
    for j in range(nb):
        gj = g_t[j:j + 1, :]
        beats = valid & ((g_t > gj) | ((g_t == gj) & (row < j)))
        rank = jnp.sum(jnp.where(beats, 1.0, 0.0), axis=0, keepdims=True)
        sel = jnp.where(row == j, jnp.where(rank < MOBA_TOPK, 1.0, 0.0), sel)
    return jnp.where(valid, sel, 0.0)


def _dot_tn(a, b):
    return lax.dot_general(a, b, (((0,), (0,)), ((), ())), preferred_element_type=F32)


def _moba_kernel(qt_ref, qt_all_ref, k_ref, v_ref, o_ref, kaug_ref, bias_ref, *, nb):
    i = pl.program_id(2)
    bs = MOBA_BLOCK
    hd = MOBA_HEAD_DIM
    lane = lax.broadcasted_iota(jnp.int32, (1, LANES), 1)
    low = lax.broadcasted_iota(jnp.int32, (LANES, 1), 0) < hd

    @pl.when(i == 0)
    def _():
        kmean = []
        for j in range(nb):
            rows = slice(j * bs, (j + 1) * bs)
            kb = k_ref[0, rows, :]
            kmean.append(jnp.mean(kb.astype(F32), axis=0, keepdims=True))
            kaug_ref[0, rows, :] = jnp.where(lane < hd, kb, jnp.where(lane == hd + j, 1.0, 0.0).astype(BF16))
            kaug_ref[1, rows, :] = jnp.where(lane >= hd, kb, jnp.where(lane == j, 1.0, 0.0).astype(BF16))
        kmean = jnp.concatenate(kmean, axis=0)
        q_all = qt_all_ref[...].astype(F32)
        for h, qh in enumerate((jnp.where(low, q_all, 0.0), jnp.where(low, 0.0, q_all))):
            gate = jnp.dot(kmean, qh, preferred_element_type=F32, precision=lax.Precision.HIGHEST)
            for jq in range(nb):
                sel = _moba_select(gate[:, jq * bs:(jq + 1) * bs], jq)
                bias_ref[h, jq] = jnp.where(sel > 0.5, 0.0, MOBA_MASKED)

    qt = qt_ref[...]
    qtf = qt.astype(F32)
    bias = [bias_ref[h, i] for h in (0, 1)]
    pad = jnp.zeros((hd - nb, bs), F32)
    q_past = (jnp.concatenate([qtf[:hd], bias[0], pad], axis=0).astype(BF16),
              jnp.concatenate([bias[1], pad, qtf[hd:]], axis=0).astype(BF16))
    zero = jnp.zeros_like(qt)
    q_own = (jnp.where(low, qt, zero), jnp.where(low, zero, qt))

    key_ix = lax.broadcasted_iota(jnp.int32, (bs, bs), 0)
    qry_ix = lax.broadcasted_iota(jnp.int32, (bs, bs), 1)
    causal_bias = jnp.where(key_ix <= qry_ix, 0.0, NEG_INF)
    own = pl.ds(pl.multiple_of(i * bs, bs), bs)
    k_own = k_ref[0, own, :]
    v_own = v_ref[0, own, :]

    def attend(width):
        heads = (0, 1)
        parts = [slice(lo * bs, (lo + 1) * bs) for lo in range(width)]
        n = len(parts) + 1
        past_scores = lambda k: [_dot(kaug_ref[h, parts[k - 1], :], q_past[h]) for h in heads]
        scores = {0: [_dot(k_own, q_own[h]) + causal_bias for h in heads]}
        if n > 1:
            scores[1] = past_scores(1)
        m = l = acc = None
        for k in range(n):
            s_k = scores.pop(k)
            m_new = [s.max(axis=0, keepdims=True) for s in s_k]
            if m is not None:
                m_new = [jnp.maximum(m[h], m_new[h]) for h in heads]
            p = [jnp.exp2(s_k[h] - m_new[h]) for h in heads]
            p_sum = [x.sum(axis=0, keepdims=True) for x in p]
            if k + 2 < n:
                scores[k + 2] = past_scores(k + 2)
            values = v_own if k == 0 else v_ref[0, parts[k - 1], :]
            pv = [_dot_tn(values, p[h].astype(BF16)) for h in heads]
            if m is None:
                l, acc = p_sum, pv
            else:
                alpha = [jnp.exp2(m[h] - m_new[h]) for h in heads]
                l = [alpha[h] * l[h] + p_sum[h] for h in heads]
                acc = [alpha[h] * acc[h] + pv[h] for h in heads]
            m = m_new
        o_ref[0] = jnp.where(low, acc[0] / l[0], acc[1] / l[1]).T

    for width in range(nb):
        pl.when(i == width)(functools.partial(attend, width))


def _moba(qt, mk, mv, b, s):
    mw = mk.shape[-1]
    nb = s // MOBA_BLOCK
    seq_blk = pl.BlockSpec((1, s, LANES), lambda bi, hp, i: (bi, 0, hp))
    return pl.pallas_call(
        functools.partial(_moba_kernel, nb=nb),
        grid=(b, mw // LANES, nb),
        in_specs=[pl.BlockSpec((LANES, MOBA_BLOCK), lambda bi, hp, i: (hp, bi * nb + i)),
                  pl.BlockSpec((LANES, s), lambda bi, hp, i: (hp, bi)), seq_blk, seq_blk],
        out_specs=pl.BlockSpec((1, MOBA_BLOCK, LANES), lambda bi, hp, i: (bi, i, hp)),
        out_shape=jax.ShapeDtypeStruct((b, s, mw), F32),
        scratch_shapes=[pltpu.VMEM((2, s, LANES), BF16), pltpu.VMEM((2, nb, nb, MOBA_BLOCK), F32)],
        compiler_params=_params("arbitrary", "arbitrary", "arbitrary"),
        name="moba",
    )(qt, qt, mk.reshape(b, s, mw), mv.reshape(b, s, mw))


def _unit_lower_inverses(mats):
    c = mats[0].shape[0]
    r = lax.broadcasted_iota(jnp.int32, (c, c), 0)
    cc = lax.broadcasted_iota(jnp.int32, (c, c), 1)
    eye = jnp.where(r == cc, 1.0, 0.0)
    pair = (r // 2) == (cc // 2)
    invs = [eye - jnp.where(pair, a, 0.0) for a in mats]
    size = 4
    while size <= c:
        level = ((r // size) == (cc // size)) & ((r // (size // 2)) != (cc // (size // 2)))
        inv_b = [inv.astype(BF16) for inv in invs]
        left = [_dot(ib, jnp.where(level, a, 0.0).astype(BF16)).astype(BF16) for ib, a in zip(inv_b, mats)]
        invs = [inv - _dot(lf, ib) for inv, lf, ib in zip(invs, left, inv_b)]
        size *= 2
    return invs


def _gdn_intra_kernel(q_ref, k_ref, v_ref, bg_ref, u_ref, w_ref, qd_ref, kd_ref, qk_ref):
    h = pl.program_id(1)
    c = GDN_CHUNK
    chunks = range(GDN_GROUP)
    lane = lax.broadcasted_iota(jnp.int32, (1, LANES), 1)
    r_ix = lax.broadcasted_iota(jnp.int32, (c, c), 0)
    c_ix = lax.broadcasted_iota(jnp.int32, (c, c), 1)
    rows = [slice(gi * c, (gi + 1) * c) for gi in chunks]
    k_b16 = [k_ref[0, rw, :] for rw in rows]
    q_b16 = [q_ref[0, rw, :] for rw in rows]
    bgs = [bg_ref[0, rw, :] for rw in rows]
    beta = [jnp.sum(jnp.where(lane == h, bg, 0.0), axis=1, keepdims=True) for bg in bgs]
    gam = [jnp.sum(jnp.where(lane == GDN_HEADS + h, bg, 0.0), axis=1, keepdims=True) for bg in bgs]
    kb = [kk.astype(F32) * bt for kk, bt in zip(k_b16, beta)]
    kk_raw = [_dot_nt(x.astype(BF16), kk) for x, kk in zip(kb, k_b16)]
    qk_raw = [_dot_nt(qq, kk) for qq, kk in zip(q_b16, k_b16)]
    decay = []
    for gm in gam:
        gam_r = jnp.sum(jnp.where(r_ix == c_ix, gm, 0.0), axis=0, keepdims=True)
        decay.append(jnp.exp(jnp.where(c_ix <= r_ix, gm - gam_r, NEG_INF)))
    t_inv = _unit_lower_inverses([jnp.where(c_ix < r_ix, x * dc, 0.0) for x, dc in zip(kk_raw, decay)])
    eg = [jnp.exp(gm) for gm in gam]
    rhs = [jnp.concatenate([v_ref[0, rw, :].astype(F32) * bt, x * e], axis=1).astype(BF16)
           for rw, bt, x, e in zip(rows, beta, kb, eg)]
    uw = [_dot(ti.astype(BF16), rh) for ti, rh in zip(t_inv, rhs)]
    for gi in chunks:
        rw = rows[gi]
        u_ref[0, rw, :] = uw[gi][:, :LANES].astype(BF16)
        w_ref[0, rw, :] = uw[gi][:, LANES:].astype(BF16)
        qk_ref[0, 0, rw, :] = (qk_raw[gi] * decay[gi]).astype(BF16)
        qd_ref[0, rw, :] = (q_b16[gi].astype(F32) * eg[gi]).astype(BF16)
        kd_ref[0, rw, :] = (k_b16[gi].astype(F32) * jnp.exp(gam[gi][c - 1:c, :] - gam[gi])).astype(BF16)


def _gdn_scan_kernel(u_ref, w_ref, qd_ref, kd_ref, qk_ref, z_ref, bg_ref, ng_ref, o_ref, state_ref, *, rows_per_step):
    c = GDN_CHUNK
    ng = ng_ref[...]
    chains = [(bi, h) for bi in range(SCAN_BATCH) for h in range(GDN_HEADS)]
    cols = [slice(h * GDN_HEAD_DIM, (h + 1) * GDN_HEAD_DIM) for h in range(GDN_HEADS)]

    @pl.when(pl.program_id(1) == 0)
    def _():
        state_ref[...] = jnp.zeros(state_ref.shape, F32)

    def step(n, states):
        r0 = pl.multiple_of(n * c, c)
        rows = pl.ds(r0, c)
        tails = [bg_ref[bi, pl.ds(r0 + c - 8, 8), :] for bi in range(SCAN_BATCH)]
        s_b = [st.astype(BF16) for st in states]
        ws = [_dot(w_ref[bi, rows, cols[h]], s_b[k]) for k, (bi, h) in enumerate(chains)]
        qs = [_dot(qd_ref[bi, rows, cols[h]], s_b[k]) for k, (bi, h) in enumerate(chains)]
        v_b = [(u_ref[bi, rows, cols[h]].astype(F32) - ws[k]).astype(BF16) for k, (bi, h) in enumerate(chains)]
        kd_v = [_dot_tn(kd_ref[bi, rows, cols[h]], v_b[k]) for k, (bi, h) in enumerate(chains)]
        qkv = [_dot(qk_ref[bi, h, rows, :], v_b[k]) for k, (bi, h) in enumerate(chains)]
        new_states = []
        for k, (bi, h) in enumerate(chains):
            g_last = tails[bi][7:8, GDN_HEADS + h:GDN_HEADS + h + 1]
            new_states.append(states[k] * jnp.exp(g_last) + kd_v[k])
            z = z_ref[bi, rows, cols[h]]
            o_ref[bi, rows, cols[h]] = _rms(qs[k] + qkv[k], ng) * (z * _sigmoid(z))
        return tuple(new_states)

    init = tuple(state_ref[k] for k in range(len(chains)))
    final = lax.fori_loop(0, rows_per_step // c, step, init)
    for k, st in enumerate(final):
        state_ref[k] = st


def _gdn(qkv, gz, bg, norm_g):
    b, s, w3 = qkv.shape
    nh = GDN_HEADS
    hw = nh * GDN_HEAD_DIM
    c = GDN_CHUNK
    grp = GDN_GROUP * c
    head_blk = lambda off: pl.BlockSpec((1, grp, LANES), lambda bi, h, n: (bi, n, off + h))
    head_shape = jax.ShapeDtypeStruct((b, s, hw), BF16)
    u, w, qd, kd, qk = pl.pallas_call(
        _gdn_intra_kernel,
        grid=(b, nh, s // grp),
        in_specs=[head_blk(0), head_blk(nh), head_blk(2 * nh),
                  pl.BlockSpec((1, grp, LANES), lambda bi, h, n: (bi, n, 0))],
        out_specs=[head_blk(0)] * 4 + [pl.BlockSpec((1, 1, grp, c), lambda bi, h, n: (bi, h, n, 0))],
        out_shape=[head_shape] * 4 + [jax.ShapeDtypeStruct((b, nh, s, c), BF16)],
        compiler_params=_params("arbitrary", "arbitrary", "arbitrary"),
        name="gdn_intra",
    )(qkv, qkv, qkv, bg)

    tile = lambda width: pl.BlockSpec((SCAN_BATCH, SCAN_ROWS, width), lambda bi, ti: (bi, ti, 0))
    return pl.pallas_call(
        functools.partial(_gdn_scan_kernel, rows_per_step=SCAN_ROWS),
        grid=(b // SCAN_BATCH, s // SCAN_ROWS),
        in_specs=[tile(hw), tile(hw), tile(hw), tile(hw),
                  pl.BlockSpec((SCAN_BATCH, nh, SCAN_ROWS, c), lambda bi, ti: (bi, 0, ti, 0)), tile(hw), tile(LANES),
                  pl.BlockSpec((1, LANES), lambda bi, ti: (0, 0))],
        out_specs=tile(hw),
        out_shape=jax.ShapeDtypeStruct((b, s, hw), F32),
        scratch_shapes=[pltpu.VMEM((SCAN_BATCH * nh, GDN_HEAD_DIM, GDN_HEAD_DIM), F32)],
        compiler_params=_params("arbitrary", "arbitrary"),
        name="gdn_scan",
    )(u, w, qd, kd, qk, gz, bg, norm_g.reshape(1, -1))


def _mem_kv_kernel(m_ref, g_ref, w_ref, kv_ref):
    kv_ref[...] = _dot(_rms(m_ref[...], g_ref[...]).astype(BF16), w_ref[...]).astype(BF16)


def _mem_kv(mem2d, g, w_kv, rows):
    t, d = mem2d.shape
    n = w_kv.shape[1]
    return pl.pallas_call(
        _mem_kv_kernel,
        grid=(t // rows,),
        in_specs=[pl.BlockSpec((rows, d), lambda i: (i, 0)), pl.BlockSpec((1, d), lambda i: (0, 0)),
                  pl.BlockSpec((d, n), lambda i: (0, 0))],
        out_specs=pl.BlockSpec((rows, n), lambda i: (i, 0)),
        out_shape=jax.ShapeDtypeStruct((t, n), BF16),
        compiler_params=_params("arbitrary"),
        name="mem_kv",
    )(mem2d, g.reshape(1, d), w_kv.astype(BF16))


def _mid_kernel(x_ref, om_ref, og_ref, mg_ref, wout_ref, xg_ref, wq_ref, kv_ref, wo_ref, fg_ref, rw_ref, rb_ref,
                x2_ref, h3_ref, route_ref, *, mw, xw):
    mo = _rms(om_ref[...], mg_ref[...]).astype(BF16)
    x1 = x_ref[...] + _dot(mo, wout_ref[:mw, :]) + _dot(og_ref[...].astype(BF16), wout_ref[mw:, :])

    h2 = _rms(x1, xg_ref[...]).astype(BF16)
    q = (_dot(h2, wq_ref[...]) * (XATTN_HEAD_DIM ** -0.5)).astype(BF16)
    head_cols = [slice(h * XATTN_HEAD_DIM, (h + 1) * XATTN_HEAD_DIM) for h in range(xw // XATTN_HEAD_DIM)]
    scores = [_dot_nt(q[:, sl], kv_ref[:, sl]) for sl in head_cols]
    probs = [jnp.exp(s - jnp.max(s, axis=1, keepdims=True)) for s in scores]
    heads = [_dot(p.astype(BF16), kv_ref[:, xw + sl.start:xw + sl.stop]) / jnp.sum(p, axis=1, keepdims=True)
             for p, sl in zip(probs, head_cols)]
    x2 = x1 + _dot(jnp.concatenate(heads, axis=1).astype(BF16), wo_ref[...])
    x2_ref[...] = x2

    h3 = _rms(x2, fg_ref[...])
    slabs = h3.shape[1] // LANES
    for j in range(slabs):
        h3_ref[pl.ds(j, h3.shape[0], stride=slabs), :] = h3[:, j * LANES:(j + 1) * LANES]
    ne = rb_ref.shape[0]
    h_hi = h3.astype(BF16)
    h_lo = (h3 - h_hi.astype(F32)).astype(BF16)
    by_hi = _dot_nt(rw_ref[...], h_hi)
    logits = by_hi[:ne, :] + by_hi[ne:, :] + _dot_nt(rw_ref[:ne, :], h_lo) + rb_ref[...]
    row = lax.broadcasted_iota(jnp.int32, logits.shape, 0)
    weights, picks, top = [], [], None
    for kk in range(TOP_K):
        m = jnp.max(logits, axis=0, keepdims=True)
        idx = jnp.min(jnp.where(logits == m, row, ne), axis=0, keepdims=True)
        logits = jnp.where(row == idx, NEG_INF, logits)
        top = m if top is None else top
        weights.append(jnp.exp(m - top))
        picks.append(idx.astype(F32))
    denom = sum(weights[1:], weights[0])
    rows = [wk / denom for wk in weights] + picks
    rows.append(jnp.zeros((LANES - len(rows), logits.shape[1]), F32))
    route_ref[...] = jnp.concatenate(rows, axis=0).T


def _mid(x2d, om, og, moba_g, w_out, xattn_g, w_q, kv, w_o, ffn_g, router_w, router_b, seq, mem_len):
    t, d = x2d.shape
    mw, gw, xw, ne = om.shape[1], og.shape[1], w_q.shape[1], router_w.shape[1]
    tiles_per_seq = seq // MID_ROWS
    row = lambda n: pl.BlockSpec((MID_ROWS, n), lambda i: (i, 0))
    const = lambda r, c: pl.BlockSpec((r, c), lambda i: (0, 0))
    rw_hi = router_w.T.astype(BF16)
    rw_lo = (router_w.T - rw_hi.astype(F32)).astype(BF16)
    return pl.pallas_call(
        functools.partial(_mid_kernel, mw=mw, xw=xw),
        grid=(t // MID_ROWS,),
        in_specs=[row(d), row(mw), row(gw), const(1, mw), const(mw + gw, d), const(1, d), const(d, xw),
                  pl.BlockSpec((mem_len, 2 * xw), lambda i: (i // tiles_per_seq, 0)),
                  const(xw, d), const(1, d), const(2 * ne, d), const(ne, 1)],
        out_specs=[row(d), pl.BlockSpec((MID_ROWS * (d // LANES), LANES), lambda i: (i, 0)), row(LANES)],
        out_shape=[jax.ShapeDtypeStruct((t, d), F32), jax.ShapeDtypeStruct((t * (d // LANES), LANES), F32),
                   jax.ShapeDtypeStruct((t, LANES), F32)],
        compiler_params=_params("arbitrary"),
        name="mid",
    )(x2d, om, og, moba_g.reshape(1, mw), w_out.astype(BF16), xattn_g.reshape(1, d), w_q.astype(BF16), kv,
      w_o.astype(BF16), ffn_g.reshape(1, d), jnp.concatenate([rw_hi, rw_lo], axis=0), router_b.reshape(ne, 1))


def _moe_kernel(be_ref, bv_ref, tok_ref, tokn_ref, dst_ref, h_hbm, wgu_ref, bgu_ref, wd_ref, bd_ref, y_hbm,
                xbuf, ybuf, wgu_b16, wd_b16, sem_in, sem_out, *, dff):
    blk = pl.program_id(0)
    nv = bv_ref[blk]
    nv_next = bv_ref[blk + 1]
    slot = blk % 2
    rows = MOE_ROWS
    chunk = MOE_CHUNK
    n_chunks = rows // chunk
    rt = ROW_TILE

    def row_in(s, r, t):
        return pltpu.make_async_copy(h_hbm.at[pl.ds(pl.multiple_of(t, rt), rt), :],
                                     xbuf.at[s, pl.ds(r * rt, rt), :], sem_in.at[s])

    def row_out(s, r, d):
        return pltpu.make_async_copy(ybuf.at[s, pl.ds(r * rt, rt), :],
                                     y_hbm.at[pl.ds(pl.multiple_of(d, rt), rt), :], sem_out.at[s])

    def wait_gather(s):
        pltpu.make_async_copy(h_hbm.at[pl.ds(0, rows * rt), :], xbuf.at[s], sem_in.at[s]).wait()

    def wait_scatter(s):
        pltpu.make_async_copy(ybuf.at[s], y_hbm.at[pl.ds(0, rows * rt), :], sem_out.at[s]).wait()

    @pl.when(blk == 0)
    def _():
        ybuf[1] = jnp.zeros(ybuf.shape[1:], F32)
        spare = pltpu.make_async_copy(ybuf.at[1], y_hbm.at[pl.ds(y_hbm.shape[0] - rows * rt, rows * rt), :],
                                      sem_out.at[1])
        spare.start()
        spare.wait()

    @pl.when((blk == 0) & (nv > 0))
    def _():
        for r in range(rows):
            row_in(0, r, tok_ref[0, 0, r]).start(priority=r % 2)

    @pl.when(nv > 0)
    def _():
        @pl.when((blk == 0) | (be_ref[blk] != be_ref[jnp.maximum(blk - 1, 0)]))
        def _():
            wgu_b16[...] = wgu_ref[0].astype(BF16)
            wd_b16[...] = wd_ref[0].astype(BF16)

        wait_gather(slot)

        @pl.when(blk >= 2)
        def _():
            wait_scatter(slot)

        slab = lambda c, j: pl.ds(c * chunk * rt + j, chunk, stride=rt)
        load_x = lambda c: jnp.concatenate([xbuf[slot, slab(c, j), :] for j in range(rt)], axis=1).astype(BF16)
        x_next = load_x(0)
        for c in range(n_chunks):
            x = x_next
            if c == 0:
                for r in range(rows):
                    row_in(1 - slot, r, tokn_ref[0, 0, r]).start(priority=r % 2)
            if c:
                for r in range((c - 1) * chunk, c * chunk):
                    row_out(slot, r, dst_ref[0, 0, r]).start(priority=r % 2)
            gu = _dot(x, wgu_b16[...]) + bgu_ref[0]
            gate = jnp.minimum(gu[:, :dff], SWIGLU_LIMIT)
            up = jnp.clip(gu[:, dff:], -SWIGLU_LIMIT, SWIGLU_LIMIT)
            act = (up + 1.0) * gate * _sigmoid(SWIGLU_ALPHA * gate)
            y = _dot(act.astype(BF16), wd_b16[...]) + bd_ref[0]
            if c + 1 < n_chunks:
                x_next = load_x(c + 1)
            for j in range(rt):
                ybuf[slot, slab(c, j), :] = y[:, j * LANES:(j + 1) * LANES]
        for r in range(rows - chunk, rows):
            row_out(slot, r, dst_ref[0, 0, r]).start(priority=r % 2)

        @pl.when(nv_next == 0)
        def _():
            wait_gather(1 - slot)

            @pl.when(blk >= 1)
            def _():
                wait_scatter(1 - slot)
            wait_scatter(slot)


def _moe(h3_tiles, block_expert, block_valid, row_tok, row_dst, w_gu, b_gu, w_d, b_d):
    ne, d, n2 = w_gu.shape
    assert d == ROW_TILE * LANES
    t = h3_tiles.shape[0] // ROW_TILE
    dff = n2 // 2
    nblk = block_expert.shape[0]
    idx_blk = pl.BlockSpec((1, 1, MOE_ROWS), lambda i, be, bv: (i, 0, 0), memory_space=pltpu.SMEM)
    idx_next = pl.BlockSpec((1, 1, MOE_ROWS), lambda i, be, bv: (jnp.minimum(i + 1, nblk - 1), 0, 0),
                            memory_space=pltpu.SMEM)
    grid_spec = pltpu.PrefetchScalarGridSpec(
        num_scalar_prefetch=2,
        grid=(nblk,),
        in_specs=[idx_blk, idx_next, idx_blk, pl.BlockSpec(memory_space=pl.ANY),
                  pl.BlockSpec((1, d, n2), lambda i, be, bv: (be[i], 0, 0)),
                  pl.BlockSpec((1, 1, n2), lambda i, be, bv: (be[i], 0, 0)),
                  pl.BlockSpec((1, dff, d), lambda i, be, bv: (be[i], 0, 0)),
                  pl.BlockSpec((1, 1, d), lambda i, be, bv: (be[i], 0, 0))],
        out_specs=pl.BlockSpec(memory_space=pl.ANY),
        scratch_shapes=[pltpu.VMEM((2, MOE_ROWS * ROW_TILE, LANES), F32),
                        pltpu.VMEM((2, MOE_ROWS * ROW_TILE, LANES), F32),
                        pltpu.VMEM((d, n2), BF16), pltpu.VMEM((dff, d), BF16),
                        pltpu.SemaphoreType.DMA((2,)), pltpu.SemaphoreType.DMA((2,))],
    )
    tok3 = (row_tok * ROW_TILE).reshape(nblk, 1, MOE_ROWS)
    valid_ext = jnp.concatenate([block_valid, jnp.zeros((1,), jnp.int32)])
    return pl.pallas_call(
        functools.partial(_moe_kernel, dff=dff),
        grid_spec=grid_spec,
        out_shape=jax.ShapeDtypeStruct(((TOP_K * t + MOE_ROWS) * ROW_TILE, LANES), F32),
        compiler_params=_params("arbitrary"),
        name="moe",
    )(block_expert, valid_ext, tok3, tok3, (row_dst * ROW_TILE).reshape(nblk, 1, MOE_ROWS), h3_tiles,
      w_gu, b_gu.reshape(ne, 1, n2), w_d, b_d.reshape(ne, 1, d))


def _route_plan(expert, t):
    n_pairs = t * TOP_K
    e_flat = expert.reshape(-1)
    order = jnp.argsort(e_flat, stable=True).astype(jnp.int32)
    experts = jnp.arange(N_EXPERTS, dtype=jnp.int32)
    counts = jnp.sum((e_flat[:, None] == experts[None, :]).astype(jnp.int32), axis=0)
    padded = (counts + MOE_ROWS - 1) // MOE_ROWS * MOE_ROWS
    pend = jnp.cumsum(padded)
    pstart = pend - padded
    gstart = jnp.cumsum(counts) - counts
    nblk = n_pairs // MOE_ROWS + N_EXPERTS
    blk_row0 = jnp.arange(nblk, dtype=jnp.int32) * MOE_ROWS
    block_expert = jnp.minimum(jnp.sum((pend[None, :] <= blk_row0[:, None]).astype(jnp.int32), axis=1), N_EXPERTS - 1)
    pick = lambda table: jnp.sum(jnp.where(block_expert[:, None] == experts[None, :], table[None, :], 0), axis=1)
    in_group = blk_row0 - pick(pstart)
    block_valid = jnp.where(blk_row0 < pend[-1], jnp.clip(pick(counts) - in_group, 0, MOE_ROWS), 0)
    first_sorted = pick(gstart) + in_group
    local = jnp.arange(MOE_ROWS, dtype=jnp.int32)
    srt = first_sorted[:, None] + local[None, :]
    pair = order[jnp.clip(srt, 0, n_pairs - 1)]
    row_tok = pair // TOP_K
    row_dst = jnp.where(local[None, :] < block_valid[:, None], (pair % TOP_K) * t + row_tok, n_pairs + local[None, :])
    return (block_expert.astype(jnp.int32), block_valid.astype(jnp.int32), row_tok.astype(jnp.int32),
            row_dst.astype(jnp.int32))


def _combine_kernel(x2_ref, route_ref, g_ref, *rest, final):
    y_refs, o_ref = rest[:TOP_K], rest[TOP_K]
    route = route_ref[...]
    gates = [route[:, kk:kk + 1] for kk in range(TOP_K)]
    slabs = []
    for j in range(ROW_TILE):
        acc = x2_ref[:, j * LANES:(j + 1) * LANES]
        for kk in range(TOP_K):
            acc = acc + gates[kk] * y_refs[kk][pl.ds(j, x2_ref.shape[0], stride=ROW_TILE), :]
        slabs.append(acc)
    out = jnp.concatenate(slabs, axis=1)
    o_ref[...] = _rms(out, g_ref[...]) if final else out


def _combine(x2, route, y, g, final):
    t, d = x2.shape
    rows = COMBINE_ROWS
    tiles = t // rows
    slot_spec = lambda kk: pl.BlockSpec((rows * ROW_TILE, LANES), lambda i: (kk * tiles + i, 0))
    return pl.pallas_call(
        functools.partial(_combine_kernel, final=final),
        grid=(tiles,),
        in_specs=[pl.BlockSpec((rows, d), lambda i: (i, 0)), pl.BlockSpec((rows, LANES), lambda i: (i, 0)),
                  pl.BlockSpec((1, d), lambda i: (0, 0))] + [slot_spec(kk) for kk in range(TOP_K)],
        out_specs=pl.BlockSpec((rows, d), lambda i: (i, 0)),
        out_shape=jax.ShapeDtypeStruct((t, d), F32),
        compiler_params=_params("arbitrary"),
        name="combine",
    )(x2, route, g.reshape(1, d), *([y] * TOP_K))


def kernel(x, mem, norm_mix_g, w_in, gdn_conv_w, gdn_A_log, gdn_dt_bias, gdn_norm_g, moba_norm_g, w_out,
           norm_xattn_g, norm_mem_g, xattn_w_q, xattn_w_kv, xattn_w_o, norm_ffn_g, router_w, router_b,
           w_gate_up, b_gate_up, w_down, b_down, final_norm_g):
    b, s, d = x.shape
    t = b * s
    mem_len = mem.shape[1]
    mw = moba_norm_g.shape[1]
    gw = GDN_HEADS * GDN_HEAD_DIM
    xcur = x.reshape(t, d)
    for l in range(w_in.shape[0]):
        qt, mk, mv, gqkv, gz, bg = _in_proj(xcur, norm_mix_g[l], w_in[l], gdn_conv_w[l], gdn_A_log[l], gdn_dt_bias[l],
                                            mw, gw, s)
        o_moba = _moba(qt, mk, mv, b, s)
        o_gdn = _gdn(gqkv.reshape(b, s, 3 * gw), gz.reshape(b, s, gw), bg.reshape(b, s, LANES), gdn_norm_g[l])
        kv = _mem_kv(mem.reshape(b * mem_len, d), norm_mem_g[l], xattn_w_kv[l], mem_len)
        x2, h3, route = _mid(xcur, o_moba.reshape(t, mw), o_gdn.reshape(t, gw), moba_norm_g[l], w_out[l],
                             norm_xattn_g[l], xattn_w_q[l], kv, xattn_w_o[l], norm_ffn_g[l], router_w[l],
                             router_b[l], s, mem_len)
        expert = route[:, TOP_K:2 * TOP_K].astype(jnp.int32)
        plan = _route_plan(expert, t)
        y = _moe(h3, *plan, w_gate_up[l], b_gate_up[l], w_down[l], b_down[l])
        xcur = _combine(x2, route, y, final_norm_g, l == w_in.shape[0] - 1)
    return xcur.reshape(b, s, d)
```

```python
import functools

import jax
import jax.numpy as jnp
from jax import lax
from jax.experimental import pallas as pl
from jax.experimental.pallas import tpu as pltpu

F32 = jnp.float32
BF16 = jnp.bfloat16

RMS_EPS = 1e-6
MOBA_HEAD_DIM = 64
MOBA_BLOCK = 256
MOBA_TOPK = 3
GDN_HEAD_DIM = 128
GDN_HEADS = 4
GDN_CONV = 4
GDN_CHUNK = 64
XATTN_HEAD_DIM = 128
N_EXPERTS = 32
TOP_K = 4
SWIGLU_LIMIT = 7.0
SWIGLU_ALPHA = 1.702

LANES = 128
ROW_TILE = 8
VMEM_LIMIT = 56 * 1024 * 1024

IN_ROWS = 512
MID_ROWS = 1024
COMBINE_ROWS = 512
MOE_ROWS = 512
MOE_CHUNK = 128
GDN_GROUP = 32
PREP_ROWS = 128
GDN_HALO = ROW_TILE
SCAN_BATCH = 2
SCAN_ROWS = 1024
NEG_INF = float("-inf")
MOBA_Q_SCALE = 1.4426950408889634 / MOBA_HEAD_DIM ** 0.5
MOBA_MASKED = -1e30


def _params(*sem):
    return pltpu.CompilerParams(dimension_semantics=sem, vmem_limit_bytes=VMEM_LIMIT)


def _rms(x, g):
    return x * lax.rsqrt(jnp.mean(x * x, axis=-1, keepdims=True) + RMS_EPS) * g


def _dot(a, b):
    return jnp.dot(a, b, preferred_element_type=F32)


def _dot_nt(a, b):
    return lax.dot_general(a, b, (((1,), (1,)), ((), ())), preferred_element_type=F32)


def _sigmoid(x):
    return 1.0 / (1.0 + jnp.exp(-x))


def _gdn_gates(x, a_log, dt_bias):
    lane = lax.broadcasted_iota(jnp.int32, (1, LANES), 1)
    xa = x + dt_bias
    softplus = jnp.maximum(xa, 0.0) + jnp.log(1.0 + jnp.exp(-jnp.abs(xa)))
    g = jnp.where((lane >= GDN_HEADS) & (lane < 2 * GDN_HEADS), -jnp.exp(a_log) * softplus, 0.0)
    pos = lax.broadcasted_iota(jnp.int32, (x.shape[0], 1), 0) % GDN_CHUNK
    sft = 1
    while sft < GDN_CHUNK:
        g = g + jnp.where(pos >= sft, pltpu.roll(g, sft, 0), 0.0)
        sft *= 2
    return jnp.where(lane < GDN_HEADS, _sigmoid(x), g)


def _in_proj_kernel(x_ref, g_ref, w_ref, wt_ref, cw_ref, alog_ref, dtb_ref, qt_ref, mk_ref, mv_ref, qkv_ref, gz_ref,
                    bg_ref, *bufs,
                    mw, gw, tiles_per_seq):
    hn = _rms(x_ref[...], g_ref[...]).astype(BF16)
    mm = lambda lo, hi: _dot(hn, w_ref[:, lo:hi])
    rows = x_ref.shape[0]
    halo = GDN_HALO

    @pl.when(pl.program_id(0) % tiles_per_seq == 0)
    def _():
        for buf in bufs:
            buf[:halo, :] = jnp.zeros((halo, LANES), F32)

    @pl.when(pl.program_id(0) % tiles_per_seq != 0)
    def _():
        for buf in bufs:
            buf[:halo, :] = buf[rows:rows + halo, :]

    for grp in range(3):
        raw = mm(2 * mw + grp * gw, 2 * mw + (grp + 1) * gw)
        for h in range(GDN_HEADS):
            bufs[grp * GDN_HEADS + h][halo:, :] = raw[:, h * LANES:(h + 1) * LANES]
    qt_ref[...] = (_dot_nt(wt_ref[...], hn) * MOBA_Q_SCALE).astype(BF16)
    mk_ref[...] = mm(0, mw).astype(BF16)
    mv_ref[...] = mm(mw, 2 * mw).astype(BF16)
    gz_ref[...] = mm(2 * mw + 3 * gw, 2 * mw + 4 * gw)
    bg_ref[...] = _gdn_gates(mm(2 * mw + 4 * gw, 2 * mw + 4 * gw + LANES), alog_ref[...], dtb_ref[...])

    for cb in range(3 * gw // LANES):
        cols = slice(cb * LANES, (cb + 1) * LANES)
        taps = [cw_ref[j:j + 1, cols] for j in range(GDN_CONV)]
        buf = bufs[cb]
        for r0 in range(0, rows, PREP_ROWS):
            y = taps[GDN_CONV - 1] * buf[halo + r0:halo + r0 + PREP_ROWS, :]
            for sft in range(1, GDN_CONV):
                y = y + taps[GDN_CONV - 1 - sft] * buf[halo + r0 - sft:halo + r0 - sft + PREP_ROWS, :]
            y = y * _sigmoid(y)
            if cb < 2 * GDN_HEADS:
                scale = GDN_HEAD_DIM ** -0.5 if cb < GDN_HEADS else 1.0
                y = y * (lax.rsqrt(jnp.sum(y * y, axis=-1, keepdims=True) + RMS_EPS) * scale)
            qkv_ref[r0:r0 + PREP_ROWS, cols] = y.astype(BF16)


def _in_proj(x2d, g, w_in, conv_w, a_log, dt_bias, mw, gw, seq):
    t, d = x2d.shape
    n_real = w_in.shape[1] - mw
    n_pad = 2 * mw + 4 * gw + LANES
    w = jnp.pad(w_in[:, mw:], ((0, 0), (0, n_pad - n_real))).astype(BF16)
    w_t = w_in[:, :mw].T.astype(BF16)
    row = lambda n: pl.BlockSpec((IN_ROWS, n), lambda i: (i, 0))
    const = lambda shape: pl.BlockSpec(shape, lambda i: (0, 0))
    nh = GDN_HEADS
    lane_pad = lambda v: jnp.pad(v.reshape(1, -1), ((0, 0), (nh, LANES - 2 * nh)))
    return pl.pallas_call(
        functools.partial(_in_proj_kernel, mw=mw, gw=gw, tiles_per_seq=seq // IN_ROWS),
        grid=(t // IN_ROWS,),
        in_specs=[row(d), const((1, d)), const((d, n_pad)), const((mw, d)), const((GDN_CONV, 3 * gw)),
                  const((1, LANES)), const((1, LANES))],
        out_specs=[pl.BlockSpec((mw, IN_ROWS), lambda i: (0, i)), row(mw), row(mw), row(3 * gw), row(gw),
                   row(LANES)],
        out_shape=[jax.ShapeDtypeStruct((mw, t), BF16), jax.ShapeDtypeStruct((t, mw), BF16),
                   jax.ShapeDtypeStruct((t, mw), BF16),
                   jax.ShapeDtypeStruct((t, 3 * gw), BF16), jax.ShapeDtypeStruct((t, gw), F32),
                   jax.ShapeDtypeStruct((t, LANES), F32)],
        scratch_shapes=[pltpu.VMEM((GDN_HALO + IN_ROWS, LANES), F32)] * (3 * gw // LANES),
        compiler_params=_params("arbitrary"),
        name="in_proj",
    )(x2d, g.reshape(1, d), w, w_t, conv_w, lane_pad(a_log), lane_pad(dt_bias))


def _moba_select(g_t, i):
    nb = g_t.shape[0]
    row = lax.broadcasted_iota(jnp.int32, g_t.shape, 0)
    valid = row < i
    sel = jnp.zeros_like(g_t)
    for j in range(nb):
        gj = g_t[j:j + 1, :]
        beats = valid & ((g_t > gj) | ((g_t == gj) & (row < j)))
        rank = jnp.sum(jnp.where(beats, 1.0, 0.0), axis=0, keepdims=True)
        sel = jnp.where(row == j, jnp.where(rank < MOBA_TOPK, 1.0, 0.0), sel)
    return jnp.where(valid, sel, 0.0)


def _dot_tn(a, b):
    return lax.dot_general(a, b, (((0,), (0,)), ((), ())), preferred_element_type=F32)


def _moba_kernel(qt_ref, qt_all_ref, k_ref, v_ref, o_ref, kaug_ref, bias_ref, *, nb):
    i = pl.program_id(2)
    bs = MOBA_BLOCK
    hd = MOBA_HEAD_DIM
    lane = lax.broadcasted_iota(jnp.int32, (1, LANES), 1)
    low = lax.broadcasted_iota(jnp.int32, (LANES, 1), 0) < hd

    @pl.when(i == 0)
    def _():
        kmean = []
        for j in range(nb):
            rows = slice(j * bs, (j + 1) * bs)
            kb = k_ref[0, rows, :]
            kmean.append(jnp.mean(kb.astype(F32), axis=0, keepdims=True))
            kaug_ref[0, rows, :] = jnp.where(lane < hd, kb, jnp.where(lane == hd + j, 1.0, 0.0).astype(BF16))
            kaug_ref[1, rows, :] = jnp.where(lane >= hd, kb, jnp.where(lane == j, 1.0, 0.0).astype(BF16))
        kmean = jnp.concatenate(kmean, axis=0)
        q_all = qt_all_ref[...].astype(F32)
        for h, qh in enumerate((jnp.where(low, q_all, 0.0), jnp.where(low, 0.0, q_all))):
            gate = jnp.dot(kmean, qh, preferred_element_type=F32, precision=lax.Precision.HIGHEST)
            for jq in range(nb):
                sel = _moba_select(gate[:, jq * bs:(jq + 1) * bs], jq)
                bias_ref[h, jq] = jnp.where(sel > 0.5, 0.0, MOBA_MASKED)

    qt = qt_ref[...]
    qtf = qt.astype(F32)
    bias = [bias_ref[h, i] for h in (0, 1)]
    pad = jnp.zeros((hd - nb, bs), F32)
    q_past = (jnp.concatenate([qtf[:hd], bias[0], pad], axis=0).astype(BF16),
              jnp.concatenate([bias[1], pad, qtf[hd:]], axis=0).astype(BF16))
    zero = jnp.zeros_like(qt)
    q_own = (jnp.where(low, qt, zero), jnp.where(low, zero, qt))

    key_ix = lax.broadcasted_iota(jnp.int32, (bs, bs), 0)
    qry_ix = lax.broadcasted_iota(jnp.int32, (bs, bs), 1)
    causal_bias = jnp.where(key_ix <= qry_ix, 0.0, NEG_INF)
    own = pl.ds(pl.multiple_of(i * bs, bs), bs)
    k_own = k_ref[0, own, :]
    v_own = v_ref[0, own, :]

    def attend(width):
        heads = (0, 1)
        parts = [slice(lo * bs, (lo + 1) * bs) for lo in range(width)]
        n = len(parts) + 1
        past_scores = lambda k: [_dot(kaug_ref[h, parts[k - 1], :], q_past[h]) for h in heads]
        scores = {0: [_dot(k_own, q_own[h]) + causal_bias for h in heads]}
        if n > 1:
            scores[1] = past_scores(1)
        m = l = acc = None
        for k in range(n):
            s_k = scores.pop(k)
            m_new = [s.max(axis=0, keepdims=True) for s in s_k]
            if m is not None:
                m_new = [jnp.maximum(m[h], m_new[h]) for h in heads]
            p = [jnp.exp2(s_k[h] - m_new[h]) for h in heads]
            p_sum = [x.sum(axis=0, keepdims=True) for x in p]
            if k + 2 < n:
                scores[k + 2] = past_scores(k + 2)
            values = v_own if k == 0 else v_ref[0, parts[k - 1], :]
            pv = [_dot_tn(values, p[h].astype(BF16)) for h in heads]
            if m is None:
                l, acc = p_sum, pv
            else:
                alpha = [jnp.exp2(m[h] - m_new[h]) for h in heads]
                l = [alpha[h] * l[h] + p_sum[h] for h in heads]
                acc = [alpha[h] * acc[h] + pv[h] for h in heads]
            m = m_new
        o_ref[0] = jnp.where(low, acc[0] / l[0], acc[1] / l[1]).T

    for width in range(nb):
        pl.when(i == width)(functools.partial(attend, width))


def _moba(qt, mk, mv, b, s):
    mw = mk.shape[-1]
    nb = s // MOBA_BLOCK
    seq_blk = pl.BlockSpec((1, s, LANES), lambda bi, hp, i: (bi, 0, hp))
    return pl.pallas_call(
        functools.partial(_moba_kernel, nb=nb),
        grid=(b, mw // LANES, nb),
        in_specs=[pl.BlockSpec((LANES, MOBA_BLOCK), lambda bi, hp, i: (hp, bi * nb + i)),
                  pl.BlockSpec((LANES, s), lambda bi, hp, i: (hp, bi)), seq_blk, seq_blk],
        out_specs=pl.BlockSpec((1, MOBA_BLOCK, LANES), lambda bi, hp, i: (bi, i, hp)),
        out_shape=jax.ShapeDtypeStruct((b, s, mw), F32),
        scratch_shapes=[pltpu.VMEM((2, s, LANES), BF16), pltpu.VMEM((2, nb, nb, MOBA_BLOCK), F32)],
        compiler_params=_params("arbitrary", "arbitrary", "arbitrary"),
        name="moba",
    )(qt, qt, mk.reshape(b, s, mw), mv.reshape(b, s, mw))


def _unit_lower_inverses(mats):
    c = mats[0].shape[0]
    r = lax.broadcasted_iota(jnp.int32, (c, c), 0)
    cc = lax.broadcasted_iota(jnp.int32, (c, c), 1)
    eye = jnp.where(r == cc, 1.0, 0.0)
    pair = (r // 2) == (cc // 2)
    invs = [eye - jnp.where(pair, a, 0.0) for a in mats]
    size = 4
    while size <= c:
        level = ((r // size) == (cc // size)) & ((r // (size // 2)) != (cc // (size // 2)))
        inv_b = [inv.astype(BF16) for inv in invs]
        left = [_dot(ib, jnp.where(level, a, 0.0).astype(BF16)).astype(BF16) for ib, a in zip(inv_b, mats)]
        invs = [inv - _dot(lf, ib) for inv, lf, ib in zip(invs, left, inv_b)]
        size *= 2
    return invs


def _gdn_intra_kernel(q_ref, k_ref, v_ref, bg_ref, u_ref, w_ref, qd_ref, kd_ref, qk_ref):
    h = pl.program_id(1)
    c = GDN_CHUNK
    chunks = range(GDN_GROUP)
    lane = lax.broadcasted_iota(jnp.int32, (1, LANES), 1)
    r_ix = lax.broadcasted_iota(jnp.int32, (c, c), 0)
    c_ix = lax.broadcasted_iota(jnp.int32, (c, c), 1)
    rows = [slice(gi * c, (gi + 1) * c) for gi in chunks]
    k_b16 = [k_ref[0, rw, :] for rw in rows]
    q_b16 = [q_ref[0, rw, :] for rw in rows]
    bgs = [bg_ref[0, rw, :] for rw in rows]
    beta = [jnp.sum(jnp.where(lane == h, bg, 0.0), axis=1, keepdims=True) for bg in bgs]
    gam = [jnp.sum(jnp.where(lane == GDN_HEADS + h, bg, 0.0), axis=1, keepdims=True) for bg in bgs]
    kb = [kk.astype(F32) * bt for kk, bt in zip(k_b16, beta)]
    kk_raw = [_dot_nt(x.astype(BF16), kk) for x, kk in zip(kb, k_b16)]
    qk_raw = [_dot_nt(qq, kk) for qq, kk in zip(q_b16, k_b16)]
    decay = []
    for gm in gam:
        gam_r = jnp.sum(jnp.where(r_ix == c_ix, gm, 0.0), axis=0, keepdims=True)
        decay.append(jnp.exp(jnp.where(c_ix <= r_ix, gm - gam_r, NEG_INF)))
    t_inv = _unit_lower_inverses([jnp.where(c_ix < r_ix, x * dc, 0.0) for x, dc in zip(kk_raw, decay)])
    eg = [jnp.exp(gm) for gm in gam]
    rhs = [jnp.concatenate([v_ref[0, rw, :].astype(F32) * bt, x * e], axis=1).astype(BF16)
           for rw, bt, x, e in zip(rows, beta, kb, eg)]
    uw = [_dot(ti.astype(BF16), rh) for ti, rh in zip(t_inv, rhs)]
    for gi in chunks:
        rw = rows[gi]
        u_ref[0, rw, :] = uw[gi][:, :LANES].astype(BF16)
        w_ref[0, rw, :] = uw[gi][:, LANES:].astype(BF16)
        qk_ref[0, 0, rw, :] = (qk_raw[gi] * decay[gi]).astype(BF16)
        qd_ref[0, rw, :] = (q_b16[gi].astype(F32) * eg[gi]).astype(BF16)
        kd_ref[0, rw, :] = (k_b16[gi].astype(F32) * jnp.exp(gam[gi][c - 1:c, :] - gam[gi])).astype(BF16)


def _gdn_scan_kernel(u_ref, w_ref, qd_ref, kd_ref, qk_ref, z_ref, bg_ref, ng_ref, o_ref, state_ref, *, rows_per_step):
    c = GDN_CHUNK
    ng = ng_ref[...]
    chains = [(bi, h) for bi in range(SCAN_BATCH) for h in range(GDN_HEADS)]
    cols = [slice(h * GDN_HEAD_DIM, (h + 1) * GDN_HEAD_DIM) for h in range(GDN_HEADS)]

    @pl.when(pl.program_id(1) == 0)
    def _():
        state_ref[...] = jnp.zeros(state_ref.shape, F32)

    def step(n, states):
        r0 = pl.multiple_of(n * c, c)
        rows = pl.ds(r0, c)
        tails = [bg_ref[bi, pl.ds(r0 + c - 8, 8), :] for bi in range(SCAN_BATCH)]
        s_b = [st.astype(BF16) for st in states]
        ws = [_dot(w_ref[bi, rows, cols[h]], s_b[k]) for k, (bi, h) in enumerate(chains)]
        qs = [_dot(qd_ref[bi, rows, cols[h]], s_b[k]) for k, (bi, h) in enumerate(chains)]
        v_b = [(u_ref[bi, rows, cols[h]].astype(F32) - ws[k]).astype(BF16) for k, (bi, h) in enumerate(chains)]
        kd_v = [_dot_tn(kd_ref[bi, rows, cols[h]], v_b[k]) for k, (bi, h) in enumerate(chains)]
        qkv = [_dot(qk_ref[bi, h, rows, :], v_b[k]) for k, (bi, h) in enumerate(chains)]
        new_states = []
        for k, (bi, h) in enumerate(chains):
            g_last = tails[bi][7:8, GDN_HEADS + h:GDN_HEADS + h + 1]
            new_states.append(states[k] * jnp.exp(g_last) + kd_v[k])
            z = z_ref[bi, rows, cols[h]]
            o_ref[bi, rows, cols[h]] = _rms(qs[k] + qkv[k], ng) * (z * _sigmoid(z))
        return tuple(new_states)

    init = tuple(state_ref[k] for k in range(len(chains)))
    final = lax.fori_loop(0, rows_per_step // c, step, init)
    for k, st in enumerate(final):
        state_ref[k] = st


def _gdn(qkv, gz, bg, norm_g):
    b, s, w3 = qkv.shape
    nh = GDN_HEADS
    hw = nh * GDN_HEAD_DIM
    c = GDN_CHUNK
    grp = GDN_GROUP * c
    head_blk = lambda off: pl.BlockSpec((1, grp, LANES), lambda bi, h, n: (bi, n, off + h))
    head_shape = jax.ShapeDtypeStruct((b, s, hw), BF16)
    u, w, qd, kd, qk = pl.pallas_call(
        _gdn_intra_kernel,
        grid=(b, nh, s // grp),
        in_specs=[head_blk(0), head_blk(nh), head_blk(2 * nh),
                  pl.BlockSpec((1, grp, LANES), lambda bi, h, n: (bi, n, 0))],
        out_specs=[head_blk(0)] * 4 + [pl.BlockSpec((1, 1, grp, c), lambda bi, h, n: (bi, h, n, 0))],
        out_shape=[head_shape] * 4 + [jax.ShapeDtypeStruct((b, nh, s, c), BF16)],
        compiler_params=_params("arbitrary", "arbitrary", "arbitrary"),
        name="gdn_intra",
    )(qkv, qkv, qkv, bg)

    tile = lambda width: pl.BlockSpec((SCAN_BATCH, SCAN_ROWS, width), lambda bi, ti: (bi, ti, 0))
    return pl.pallas_call(
        functools.partial(_gdn_scan_kernel, rows_per_step=SCAN_ROWS),
        grid=(b // SCAN_BATCH, s // SCAN_ROWS),
        in_specs=[tile(hw), tile(hw), tile(hw), tile(hw),
                  pl.BlockSpec((SCAN_BATCH, nh, SCAN_ROWS, c), lambda bi, ti: (bi, 0, ti, 0)), tile(hw), tile(LANES),
                  pl.BlockSpec((1, LANES), lambda bi, ti: (0, 0))],
        out_specs=tile(hw),
        out_shape=jax.ShapeDtypeStruct((b, s, hw), F32),
        scratch_shapes=[pltpu.VMEM((SCAN_BATCH * nh, GDN_HEAD_DIM, GDN_HEAD_DIM), F32)],
        compiler_params=_params("arbitrary", "arbitrary"),
        name="gdn_scan",
    )(u, w, qd, kd, qk, gz, bg, norm_g.reshape(1, -1))


def _mem_kv_kernel(m_ref, g_ref, w_ref, kv_ref):
    kv_ref[...] = _dot(_rms(m_ref[...], g_ref[...]).astype(BF16), w_ref[...]).astype(BF16)


def _mem_kv(mem2d, g, w_kv, rows):
    t, d = mem2d.shape
    n = w_kv.shape[1]
    return pl.pallas_call(
        _mem_kv_kernel,
        grid=(t // rows,),
        in_specs=[pl.BlockSpec((rows, d), lambda i: (i, 0)), pl.BlockSpec((1, d), lambda i: (0, 0)),
                  pl.BlockSpec((d, n), lambda i: (0, 0))],
        out_specs=pl.BlockSpec((rows, n), lambda i: (i, 0)),
        out_shape=jax.ShapeDtypeStruct((t, n), BF16),
        compiler_params=_params("arbitrary"),
        name="mem_kv",
    )(mem2d, g.reshape(1, d), w_kv.astype(BF16))


def _mid_kernel(x_ref, om_ref, og_ref, mg_ref, wout_ref, xg_ref, wq_ref, kv_ref, wo_ref, fg_ref, rw_ref, rb_ref,
                x2_ref, h3_ref, route_ref, *, mw, xw):
    mo = _rms(om_ref[...], mg_ref[...]).astype(BF16)
    x1 = x_ref[...] + _dot(mo, wout_ref[:mw, :]) + _dot(og_ref[...].astype(BF16), wout_ref[mw:, :])

    h2 = _rms(x1, xg_ref[...]).astype(BF16)
    q = (_dot(h2, wq_ref[...]) * (XATTN_HEAD_DIM ** -0.5)).astype(BF16)
    head_cols = [slice(h * XATTN_HEAD_DIM, (h + 1) * XATTN_HEAD_DIM) for h in range(xw // XATTN_HEAD_DIM)]
    scores = [_dot_nt(q[:, sl], kv_ref[:, sl]) for sl in head_cols]
    probs = [jnp.exp(s - jnp.max(s, axis=1, keepdims=True)) for s in scores]
    heads = [_dot(p.astype(BF16), kv_ref[:, xw + sl.start:xw + sl.stop]) / jnp.sum(p, axis=1, keepdims=True)
             for p, sl in zip(probs, head_cols)]
    x2 = x1 + _dot(jnp.concatenate(heads, axis=1).astype(BF16), wo_ref[...])
    x2_ref[...] = x2

    h3 = _rms(x2, fg_ref[...])
    slabs = h3.shape[1] // LANES
    for j in range(slabs):
        h3_ref[pl.ds(j, h3.shape[0], stride=slabs), :] = h3[:, j * LANES:(j + 1) * LANES]
    ne = rb_ref.shape[0]
    h_hi = h3.astype(BF16)
    h_lo = (h3 - h_hi.astype(F32)).astype(BF16)
    by_hi = _dot_nt(rw_ref[...], h_hi)
    logits = by_hi[:ne, :] + by_hi[ne:, :] + _dot_nt(rw_ref[:ne, :], h_lo) + rb_ref[...]
    row = lax.broadcasted_iota(jnp.int32, logits.shape, 0)
    weights, picks, top = [], [], None
    for kk in range(TOP_K):
        m = jnp.max(logits, axis=0, keepdims=True)
        idx = jnp.min(jnp.where(logits == m, row, ne), axis=0, keepdims=True)
        logits = jnp.where(row == idx, NEG_INF, logits)
        top = m if top is None else top
        weights.append(jnp.exp(m - top))
        picks.append(idx.astype(F32))
    denom = sum(weights[1:], weights[0])
    rows = [wk / denom for wk in weights] + picks
    rows.append(jnp.zeros((LANES - len(rows), logits.shape[1]), F32))
    route_ref[...] = jnp.concatenate(rows, axis=0).T


def _mid(x2d, om, og, moba_g, w_out, xattn_g, w_q, kv, w_o, ffn_g, router_w, router_b, seq, mem_len):
    t, d = x2d.shape
    mw, gw, xw, ne = om.shape[1], og.shape[1], w_q.shape[1], router_w.shape[1]
    tiles_per_seq = seq // MID_ROWS
    row = lambda n: pl.BlockSpec((MID_ROWS, n), lambda i: (i, 0))
    const = lambda r, c: pl.BlockSpec((r, c), lambda i: (0, 0))
    rw_hi = router_w.T.astype(BF16)
    rw_lo = (router_w.T - rw_hi.astype(F32)).astype(BF16)
    return pl.pallas_call(
        functools.partial(_mid_kernel, mw=mw, xw=xw),
        grid=(t // MID_ROWS,),
        in_specs=[row(d), row(mw), row(gw), const(1, mw), const(mw + gw, d), const(1, d), const(d, xw),
                  pl.BlockSpec((mem_len, 2 * xw), lambda i: (i // tiles_per_seq, 0)),
                  const(xw, d), const(1, d), const(2 * ne, d), const(ne, 1)],
        out_specs=[row(d), pl.BlockSpec((MID_ROWS * (d // LANES), LANES), lambda i: (i, 0)), row(LANES)],
        out_shape=[jax.ShapeDtypeStruct((t, d), F32), jax.ShapeDtypeStruct((t * (d // LANES), LANES), F32),
                   jax.ShapeDtypeStruct((t, LANES), F32)],
        compiler_params=_params("arbitrary"),
        name="mid",
    )(x2d, om, og, moba_g.reshape(1, mw), w_out.astype(BF16), xattn_g.reshape(1, d), w_q.astype(BF16), kv,
      w_o.astype(BF16), ffn_g.reshape(1, d), jnp.concatenate([rw_hi, rw_lo], axis=0), router_b.reshape(ne, 1))


def _moe_kernel(be_ref, bv_ref, tok_ref, tokn_ref, dst_ref, h_hbm, wgu_ref, bgu_ref, wd_ref, bd_ref, y_hbm,
                xbuf, ybuf, wgu_b16, wd_b16, sem_in, sem_out, *, dff):
    blk = pl.program_id(0)
    nv = bv_ref[blk]
    nv_next = bv_ref[blk + 1]
    slot = blk % 2
    rows = MOE_ROWS
    chunk = MOE_CHUNK
    n_chunks = rows // chunk
    rt = ROW_TILE

    def row_in(s, r, t):
        return pltpu.make_async_copy(h_hbm.at[pl.ds(pl.multiple_of(t, rt), rt), :],
                                     xbuf.at[s, pl.ds(r * rt, rt), :], sem_in.at[s])

    def row_out(s, r, d):
        return pltpu.make_async_copy(ybuf.at[s, pl.ds(r * rt, rt), :],
                                     y_hbm.at[pl.ds(pl.multiple_of(d, rt), rt), :], sem_out.at[s])

    def wait_gather(s):
        pltpu.make_async_copy(h_hbm.at[pl.ds(0, rows * rt), :], xbuf.at[s], sem_in.at[s]).wait()

    def wait_scatter(s):
        pltpu.make_async_copy(ybuf.at[s], y_hbm.at[pl.ds(0, rows * rt), :], sem_out.at[s]).wait()

    @pl.when(blk == 0)
    def _():
        ybuf[1] = jnp.zeros(ybuf.shape[1:], F32)
        spare = pltpu.make_async_copy(ybuf.at[1], y_hbm.at[pl.ds(y_hbm.shape[0] - rows * rt, rows * rt), :],
                                      sem_out.at[1])
        spare.start()
        spare.wait()

    @pl.when((blk == 0) & (nv > 0))
    def _():
        for r in range(rows):
            row_in(0, r, tok_ref[0, 0, r]).start(priority=r % 2)

    @pl.when(nv > 0)
    def _():
        @pl.when((blk == 0) | (be_ref[blk] != be_ref[jnp.maximum(blk - 1, 0)]))
        def _():
            wgu_b16[...] = wgu_ref[0].astype(BF16)
            wd_b16[...] = wd_ref[0].astype(BF16)

        wait_gather(slot)

        @pl.when(blk >= 2)
        def _():
            wait_scatter(slot)

        slab = lambda c, j: pl.ds(c * chunk * rt + j, chunk, stride=rt)
        load_x = lambda c: jnp.concatenate([xbuf[slot, slab(c, j), :] for j in range(rt)], axis=1).astype(BF16)
        x_next = load_x(0)
        for c in range(n_chunks):
            x = x_next
            if c == 0:
                for r in range(rows):
                    row_in(1 - slot, r, tokn_ref[0, 0, r]).start(priority=r % 2)
            if c:
                for r in range((c - 1) * chunk, c * chunk):
                    row_out(slot, r, dst_ref[0, 0, r]).start(priority=r % 2)
            gu = _dot(x, wgu_b16[...]) + bgu_ref[0]
            gate = jnp.minimum(gu[:, :dff], SWIGLU_LIMIT)
            up = jnp.clip(gu[:, dff:], -SWIGLU_LIMIT, SWIGLU_LIMIT)
            act = (up + 1.0) * gate * _sigmoid(SWIGLU_ALPHA * gate)
            y = _dot(act.astype(BF16), wd_b16[...]) + bd_ref[0]
            if c + 1 < n_chunks:
                x_next = load_x(c + 1)
            for j in range(rt):
                ybuf[slot, slab(c, j), :] = y[:, j * LANES:(j + 1) * LANES]
        for r in range(rows - chunk, rows):
            row_out(slot, r, dst_ref[0, 0, r]).start(priority=r % 2)

        @pl.when(nv_next == 0)
        def _():
            wait_gather(1 - slot)

            @pl.when(blk >= 1)
            def _():
                wait_scatter(1 - slot)
            wait_scatter(slot)


def _moe(h3_tiles, block_expert, block_valid, row_tok, row_dst, w_gu, b_gu, w_d, b_d):
    ne, d, n2 = w_gu.shape
    assert d == ROW_TILE * LANES
    t = h3_tiles.shape[0] // ROW_TILE
    dff = n2 // 2
    nblk = block_expert.shape[0]
    idx_blk = pl.BlockSpec((1, 1, MOE_ROWS), lambda i, be, bv: (i, 0, 0), memory_space=pltpu.SMEM)
    idx_next = pl.BlockSpec((1, 1, MOE_ROWS), lambda i, be, bv: (jnp.minimum(i + 1, nblk - 1), 0, 0),
                            memory_space=pltpu.SMEM)
    grid_spec = pltpu.PrefetchScalarGridSpec(
        num_scalar_prefetch=2,
        grid=(nblk,),
        in_specs=[idx_blk, idx_next, idx_blk, pl.BlockSpec(memory_space=pl.ANY),
                  pl.BlockSpec((1, d, n2), lambda i, be, bv: (be[i], 0, 0)),
                  pl.BlockSpec((1, 1, n2), lambda i, be, bv: (be[i], 0, 0)),
                  pl.BlockSpec((1, dff, d), lambda i, be, bv: (be[i], 0, 0)),
                  pl.BlockSpec((1, 1, d), lambda i, be, bv: (be[i], 0, 0))],
        out_specs=pl.BlockSpec(memory_space=pl.ANY),
        scratch_shapes=[pltpu.VMEM((2, MOE_ROWS * ROW_TILE, LANES), F32),
                        pltpu.VMEM((2, MOE_ROWS * ROW_TILE, LANES), F32),
                        pltpu.VMEM((d, n2), BF16), pltpu.VMEM((dff, d), BF16),
                        pltpu.SemaphoreType.DMA((2,)), pltpu.SemaphoreType.DMA((2,))],
    )
    tok3 = (row_tok * ROW_TILE).reshape(nblk, 1, MOE_ROWS)
    valid_ext = jnp.concatenate([block_valid, jnp.zeros((1,), jnp.int32)])
    return pl.pallas_call(
        functools.partial(_moe_kernel, dff=dff),
        grid_spec=grid_spec,
        out_shape=jax.ShapeDtypeStruct(((TOP_K * t + MOE_ROWS) * ROW_TILE, LANES), F32),
        compiler_params=_params("arbitrary"),
        name="moe",
    )(block_expert, valid_ext, tok3, tok3, (row_dst * ROW_TILE).reshape(nblk, 1, MOE_ROWS), h3_tiles,
      w_gu, b_gu.reshape(ne, 1, n2), w_d, b_d.reshape(ne, 1, d))


def _route_plan(expert, t):
    n_pairs = t * TOP_K
    e_flat = expert.reshape(-1)
    order = jnp.argsort(e_flat, stable=True).astype(jnp.int32)
    experts = jnp.arange(N_EXPERTS, dtype=jnp.int32)
    counts = jnp.sum((e_flat[:, None] == experts[None, :]).astype(jnp.int32), axis=0)
    padded = (counts + MOE_ROWS - 1) // MOE_ROWS * MOE_ROWS
    pend = jnp.cumsum(padded)
    pstart = pend - padded
    gstart = jnp.cumsum(counts) - counts
    nblk = n_pairs // MOE_ROWS + N_EXPERTS
    blk_row0 = jnp.arange(nblk, dtype=jnp.int32) * MOE_ROWS
    block_expert = jnp.minimum(jnp.sum((pend[None, :] <= blk_row0[:, None]).astype(jnp.int32), axis=1), N_EXPERTS - 1)
    pick = lambda table: jnp.sum(jnp.where(block_expert[:, None] == experts[None, :], table[None, :], 0), axis=1)
    in_group = blk_row0 - pick(pstart)
    block_valid = jnp.where(blk_row0 < pend[-1], jnp.clip(pick(counts) - in_group, 0, MOE_ROWS), 0)
    first_sorted = pick(gstart) + in_group
    local = jnp.arange(MOE_ROWS, dtype=jnp.int32)
    srt = first_sorted[:, None] + local[None, :]
    pair = order[jnp.clip(srt, 0, n_pairs - 1)]
    row_tok = pair // TOP_K
    row_dst = jnp.where(local[None, :] < block_valid[:, None], (pair % TOP_K) * t + row_tok, n_pairs + local[None, :])
    return (block_expert.astype(jnp.int32), block_valid.astype(jnp.int32), row_tok.astype(jnp.int32),
            row_dst.astype(jnp.int32))


def _combine_kernel(x2_ref, route_ref, g_ref, *rest, final):
    y_refs, o_ref = rest[:TOP_K], rest[TOP_K]
    route = route_ref[...]
    gates = [route[:, kk:kk + 1] for kk in range(TOP_K)]
    slabs = []
    for j in range(ROW_TILE):
        acc = x2_ref[:, j * LANES:(j + 1) * LANES]
        for kk in range(TOP_K):
            acc = acc + gates[kk] * y_refs[kk][pl.ds(j, x2_ref.shape[0], stride=ROW_TILE), :]
        slabs.append(acc)
    out = jnp.concatenate(slabs, axis=1)
    o_ref[...] = _rms(out, g_ref[...]) if final else out


def _combine(x2, route, y, g, final):
    t, d = x2.shape
    rows = COMBINE_ROWS
    tiles = t // rows
    slot_spec = lambda kk: pl.BlockSpec((rows * ROW_TILE, LANES), lambda i: (kk * tiles + i, 0))
    return pl.pallas_call(
        functools.partial(_combine_kernel, final=final),
        grid=(tiles,),
        in_specs=[pl.BlockSpec((rows, d), lambda i: (i, 0)), pl.BlockSpec((rows, LANES), lambda i: (i, 0)),
                  pl.BlockSpec((1, d), lambda i: (0, 0))] + [slot_spec(kk) for kk in range(TOP_K)],
        out_specs=pl.BlockSpec((rows, d), lambda i: (i, 0)),
        out_shape=jax.ShapeDtypeStruct((t, d), F32),
        compiler_params=_params("arbitrary"),
        name="combine",
    )(x2, route, g.reshape(1, d), *([y] * TOP_K))


def kernel(x, mem, norm_mix_g, w_in, gdn_conv_w, gdn_A_log, gdn_dt_bias, gdn_norm_g, moba_norm_g, w_out,
           norm_xattn_g, norm_mem_g, xattn_w_q, xattn_w_kv, xattn_w_o, norm_ffn_g, router_w, router_b,
           w_gate_up, b_gate_up, w_down, b_down, final_norm_g):
    b, s, d = x.shape
    t = b * s
    mem_len = mem.shape[1]
    mw = moba_norm_g.shape[1]
    gw = GDN_HEADS * GDN_HEAD_DIM
    xcur = x.reshape(t, d)
    for l in range(w_in.shape[0]):
        qt, mk, mv, gqkv, gz, bg = _in_proj(xcur, norm_mix_g[l], w_in[l], gdn_conv_w[l], gdn_A_log[l], gdn_dt_bias[l],
                                            mw, gw, s)
        o_moba = _moba(qt, mk, mv, b, s)
        o_gdn = _gdn(gqkv.reshape(b, s, 3 * gw), gz.reshape(b, s, gw), bg.reshape(b, s, LANES), gdn_norm_g[l])
        kv = _mem_kv(mem.reshape(b * mem_len, d), norm_mem_g[l], xattn_w_kv[l], mem_len)
        x2, h3, route = _mid(xcur, o_moba.reshape(t, mw), o_gdn.reshape(t, gw), moba_norm_g[l], w_out[l],
                             norm_xattn_g[l], xattn_w_q[l], kv, xattn_w_o[l], norm_ffn_g[l], router_w[l],
                             router_b[l], s, mem_len)
        expert = route[:, TOP_K:2 * TOP_K].astype(jnp.int32)
        plan = _route_plan(expert, t)
        y = _moe(h3, *plan, w_gate_up[l], b_gate_up[l], w_down[l], b_down[l])
        xcur = _combine(x2, route, y, final_norm_g, l == w_in.shape[0] - 1)
    return xcur.reshape(b, s, d)
```
